```python
import math
import numpy as np
import jax
import jax.numpy as jnp
from jax import lax

D_MODEL = 2048
BATCH = 8
SEQ = 2048
DEPTH = 2

D_MIX = D_MODEL
N_MIXERS = 4
W_GROUP = D_MIX // N_MIXERS

ROPE_THETA = 500000.0
Q_BLOCK = 128
NEG_INF = -1.0e30
FORCE_SCORE = 1.0e4

MLA_HEADS = 4
MLA_QK = 128
ROPE_DIM = MLA_QK // 4
MLA_ROPE = ROPE_DIM
MLA_NOPE = MLA_QK - MLA_ROPE
MLA_V = W_GROUP // MLA_HEADS
MLA_Q_LORA = 448
MLA_KV_LORA = 128

NSA_HEADS = 4
NSA_KV_GROUPS = 1
NSA_HPG = NSA_HEADS // NSA_KV_GROUPS
NSA_HD = W_GROUP // NSA_HEADS
CMP_BLOCK = 32
CMP_STRIDE = 16
SEL_BLOCK = 64
N_SEL = 8
N_LOCAL = 2
WINDOW = 512

POOL_SIZES = (2, 4, 8, 16)
POOL_GROUPS = len(POOL_SIZES)
POOL_CH = W_GROUP // POOL_GROUPS

S5_CH = 16
S5_GROUPS = W_GROUP // S5_CH
S5_STATE = 64

MOE_GROUPS = 4
MOE_EPG = 8
MOE_EXPERTS = MOE_GROUPS * MOE_EPG
MOE_TOPK = 2
MOE_HIDDEN = 512

COLS_A = MLA_Q_LORA + MLA_KV_LORA + MLA_ROPE
COLS_B = NSA_HEADS * NSA_HD + 6 * NSA_KV_GROUPS * NSA_HD + 3 * NSA_HEADS
COLS_C = W_GROUP
COLS_D = W_GROUP
N_IN = COLS_A + COLS_B + COLS_C + COLS_D

kernel_name = 'hybrid_mla_nsa_pool_s5_hmoe'


def rmsnorm(x, g, eps=1e-6):
    xf = x.astype(jnp.float32)
    y = xf * lax.rsqrt(jnp.mean(xf * xf, axis=-1, keepdims=True) + eps)
    return (y * g.astype(jnp.float32)).astype(x.dtype)


def rope_tables(seq):
    inv = ROPE_THETA ** (-jnp.arange(0, ROPE_DIM, 2, dtype=jnp.float32) / ROPE_DIM)
    ang = jnp.arange(seq, dtype=jnp.float32)[:, None] * inv[None, :]
    return jnp.cos(ang), jnp.sin(ang)


def apply_rope(x, cos, sin):
    half = cos.shape[-1]
    c = cos[None, :, None, :]
    s = sin[None, :, None, :]
    xf = x.astype(jnp.float32)
    x1, x2, rest = xf[..., :half], xf[..., half:2 * half], xf[..., 2 * half:]
    return jnp.concatenate([x1 * c - x2 * s, x2 * c + x1 * s, rest], axis=-1).astype(x.dtype)


def causal_block_attention(q, k, v):
    B, S, H, Dk = q.shape
    nb = S // Q_BLOCK
    scale = 1.0 / math.sqrt(Dk)
    k_pos = jnp.arange(S)
    qb = q.reshape(B, nb, Q_BLOCK, H, Dk).transpose(1, 0, 2, 3, 4)

    def one_block(args):
        q_blk, blk = args
        q_pos = blk * Q_BLOCK + jnp.arange(Q_BLOCK)
        s = jnp.einsum('bqhd,bkhd->bhqk', q_blk, k, preferred_element_type=jnp.float32) * scale
        s = jnp.where(k_pos[None, :] <= q_pos[:, None], s, NEG_INF)
        p = jax.nn.softmax(s, axis=-1)
        return jnp.einsum('bhqk,bkhd->bqhd', p.astype(v.dtype), v)

    o = lax.map(one_block, (qb, jnp.arange(nb)))
    return o.transpose(1, 0, 2, 3, 4).reshape(B, S, H, v.shape[-1])


def mla_mixer(cols, q_norm_g, kv_norm_g, w_uq, w_ukv, cos, sin):
    B, S, _ = cols.shape
    c_q, c_kv, k_rope = jnp.split(cols, [MLA_Q_LORA, MLA_Q_LORA + MLA_KV_LORA], axis=-1)
    q = (rmsnorm(c_q, q_norm_g) @ w_uq).reshape(B, S, MLA_HEADS, MLA_QK)
    kv = (rmsnorm(c_kv, kv_norm_g) @ w_ukv).reshape(B, S, MLA_HEADS, MLA_NOPE + MLA_V)
    k_nope, v = kv[..., :MLA_NOPE], kv[..., MLA_NOPE:]
    q = jnp.concatenate([q[..., :MLA_NOPE], apply_rope(q[..., MLA_NOPE:], cos, sin)], axis=-1)
    k_rope = apply_rope(k_rope[:, :, None, :], cos, sin)
    k = jnp.concatenate([k_nope, jnp.broadcast_to(k_rope, (B, S, MLA_HEADS, MLA_ROPE))], axis=-1)
    o = causal_block_attention(q, k, v)
    return o.reshape(B, S, MLA_HEADS * MLA_V)


def nsa_compressed(q, kc, vc, cmp_pe, cmp_w1, cmp_w2, scale):
    B, S, G, J, Dh = q.shape
    n_cmp = (S - CMP_BLOCK) // CMP_STRIDE + 1
    idx = np.arange(n_cmp)[:, None] * CMP_STRIDE + np.arange(CMP_BLOCK)[None, :]

    def compress(t, pe, w1, w2):
        blk = t[:, idx] + pe[None, None, :, None, :]
        blk = blk.transpose(0, 1, 3, 2, 4).reshape(B, n_cmp, G, CMP_BLOCK * Dh)
        return jax.nn.gelu(blk @ w1) @ w2

    k_cmp = compress(kc, cmp_pe[0], cmp_w1[0], cmp_w2[0])
    v_cmp = compress(vc, cmp_pe[1], cmp_w1[1], cmp_w2[1])
    s = jnp.einsum('bsgjd,bngd->bgjsn', q, k_cmp, preferred_element_type=jnp.float32) * scale
    blk_end = np.arange(n_cmp) * CMP_STRIDE + CMP_BLOCK - 1
    valid = blk_end[None, :] <= np.arange(S)[:, None]
    s = jnp.where(valid, s, NEG_INF)
    p = jnp.where(valid, jax.nn.softmax(s, axis=-1), 0.0)
    o = jnp.einsum('bgjsn,bngd->bsgjd', p.astype(v_cmp.dtype), v_cmp)
    return o, p


def nsa_select_blocks(p_cmp, S):
    n_cmp = p_cmp.shape[-1]
    n_blk = S // SEL_BLOCK
    c_start = np.arange(n_cmp) * CMP_STRIDE
    b_start = np.arange(n_blk) * SEL_BLOCK
    overlap = ((c_start[:, None] < b_start[None, :] + SEL_BLOCK)
               & (c_start[:, None] + CMP_BLOCK > b_start[None, :])).astype(np.float32)
    imp = jnp.einsum('bgjsn,nk->bgsk', p_cmp, jnp.asarray(overlap))
    cur = (np.arange(S) // SEL_BLOCK)[:, None]
    blk = np.arange(n_blk)[None, :]
    forced = (blk == 0) | ((blk <= cur) & (blk > cur - N_LOCAL))
    valid = blk <= cur
    score = jnp.where(forced, FORCE_SCORE, jnp.where(valid, imp, -1.0))
    _, idx = lax.top_k(score, min(N_SEL, n_blk))
    return idx


def nsa_selected(q, ks, vs, sel_idx, scale):
    B, S, G, J, Dh = q.shape
    nb = S // Q_BLOCK
    n_blk = S // SEL_BLOCK
    n_pick = sel_idx.shape[-1]
    kb = ks.reshape(B, n_blk, SEL_BLOCK, G, Dh).transpose(0, 3, 1, 2, 4)
    vb = vs.reshape(B, n_blk, SEL_BLOCK, G, Dh).transpose(0, 3, 1, 2, 4)
    qb = q.reshape(B, nb, Q_BLOCK, G, J, Dh).transpose(1, 0, 2, 3, 4, 5)
    ib = sel_idx.reshape(B, G, nb, Q_BLOCK, n_pick).transpose(2, 0, 1, 3, 4)
    bi = jnp.arange(B)[:, None, None, None]
    gi = jnp.arange(G)[None, :, None, None]
    tok = jnp.arange(SEL_BLOCK)

    def one_block(args):
        q_blk, i_blk, blk = args
        q_pos = blk * Q_BLOCK + jnp.arange(Q_BLOCK)
        k_sel = kb[bi, gi, i_blk]
        v_sel = vb[bi, gi, i_blk]
        s = jnp.einsum('bqgjd,bgqnld->bgjqnl', q_blk, k_sel, preferred_element_type=jnp.float32) * scale
        k_pos = i_blk[..., None] * SEL_BLOCK + tok
        mask = k_pos <= q_pos[None, None, :, None, None]
        s = jnp.where(mask[:, :, None], s, NEG_INF).reshape(B, G, J, Q_BLOCK, n_pick * SEL_BLOCK)
        p = jax.nn.softmax(s, axis=-1).reshape(B, G, J, Q_BLOCK, n_pick, SEL_BLOCK)
        return jnp.einsum('bgjqnl,bgqnld->bqgjd', p.astype(v_sel.dtype), v_sel)

    o = lax.map(one_block, (qb, ib, jnp.arange(nb)))
    return o.transpose(1, 0, 2, 3, 4, 5).reshape(B, S, G, J, Dh)


def nsa_window(q, kw, vw, scale):
    B, S, G, J, Dh = q.shape
    nb = S // Q_BLOCK
    pad = ((0, 0), (WINDOW, 0), (0, 0), (0, 0))
    kp, vp = jnp.pad(kw, pad), jnp.pad(vw, pad)
    idx = np.arange(nb)[:, None] * Q_BLOCK + np.arange(Q_BLOCK + WINDOW)[None, :]
    k_band, v_band = kp[:, idx], vp[:, idx]
    qb = q.reshape(B, nb, Q_BLOCK, G, J, Dh)
    s = jnp.einsum('bnqgjd,bnkgd->bngjqk', qb, k_band, preferred_element_type=jnp.float32) * scale
    q_pos = np.arange(nb)[:, None] * Q_BLOCK + np.arange(Q_BLOCK)[None, :]
    k_pos = idx - WINDOW
    kq, qq = k_pos[:, None, :], q_pos[:, :, None]
    mask = (kq <= qq) & (kq > qq - WINDOW) & (kq >= 0)
    s = jnp.where(mask[None, :, None, None], s, NEG_INF)
    p = jax.nn.softmax(s, axis=-1)
    o = jnp.einsum('bngjqk,bnkgd->bnqgjd', p.astype(v_band.dtype), v_band)
    return o.reshape(B, S, G, J, Dh)


def nsa_mixer(cols, cmp_pe, cmp_w1, cmp_w2, cos, sin):
    B, S, _ = cols.shape
    G, J, Dh = NSA_KV_GROUPS, NSA_HPG, NSA_HD
    kvw = G * Dh
    splits = [NSA_HEADS * Dh + i * kvw for i in range(7)]
    q, kc, vc, ks, vs, kw, vw, gl = jnp.split(cols, splits, axis=-1)
    q = apply_rope(q.reshape(B, S, NSA_HEADS, Dh), cos, sin).reshape(B, S, G, J, Dh)
    kc, ks, kw = [apply_rope(t.reshape(B, S, G, Dh), cos, sin) for t in (kc, ks, kw)]
    vc, vs, vw = [t.reshape(B, S, G, Dh) for t in (vc, vs, vw)]
    gates = jax.nn.sigmoid(gl.astype(jnp.float32)).reshape(B, S, G, J, 3)
    scale = 1.0 / math.sqrt(Dh)
    o_c, p_cmp = nsa_compressed(q, kc, vc, cmp_pe, cmp_w1, cmp_w2, scale)
    sel_idx = nsa_select_blocks(p_cmp, S)
    o_s = nsa_selected(q, ks, vs, sel_idx, scale)
    o_w = nsa_window(q, kw, vw, scale)
    o = gates[..., 0:1] * o_c + gates[..., 1:2] * o_s + gates[..., 2:3] * o_w
    return o.reshape(B, S, NSA_HEADS * Dh).astype(cols.dtype)


def pool_mixer(u, pool_w, pool_b, pool_scale):
    B, S, _ = u.shape
    uf = u.astype(jnp.float32).reshape(B, S, POOL_GROUPS, POOL_CH)
    cs = jnp.pad(jnp.cumsum(uf, axis=1), ((0, 0), (1, 0), (0, 0), (0, 0)))
    t = np.arange(S)[:, None]
    w = np.array(POOL_SIZES)[None, :]
    lo = np.maximum(t + 1 - w, 0)
    cnt = np.minimum(t + 1, w).astype(np.float32)
    cs_lo = cs[:, lo, np.arange(POOL_GROUPS)[None, :]]
    pooled = (cs[:, 1:] - cs_lo) / cnt[None, :, :, None]
    y = jnp.einsum('bsgc,gcd->bsgd', pooled - uf, pool_w.astype(jnp.float32))
    y = (y.reshape(B, S, W_GROUP) + pool_b.astype(jnp.float32)) * pool_scale.astype(jnp.float32)
    return y.astype(u.dtype)


def s5_combine(e1, e2):
    a1r, a1i, b1r, b1i = e1
    a2r, a2i, b2r, b2i = e2
    return (a2r * a1r - a2i * a1i,
            a2r * a1i + a2i * a1r,
            a2r * b1r - a2i * b1i + b2r,
            a2r * b1i + a2i * b1r + b2i)


def s5_mixer(u, a_re, a_im, log_dt, b_re, b_im, c_re, c_im, d_skip, glu_w, glu_b):
    B, S, _ = u.shape
    uf = u.astype(jnp.float32).reshape(B, S, S5_GROUPS, S5_CH)
    dt = jnp.exp(log_dt.astype(jnp.float32))[:, None]
    lr, li = a_re.astype(jnp.float32), a_im.astype(jnp.float32)
    mag = jnp.exp(lr * dt)
    abar_r, abar_i = mag * jnp.cos(li * dt), mag * jnp.sin(li * dt)
    den = lr * lr + li * li
    nr, ni = abar_r - 1.0, abar_i
    coef_r = (nr * lr + ni * li) / den
    coef_i = (ni * lr - nr * li) / den
    br, bi = b_re.astype(jnp.float32), b_im.astype(jnp.float32)
    bbar_r = coef_r[..., None] * br - coef_i[..., None] * bi
    bbar_i = coef_r[..., None] * bi + coef_i[..., None] * br
    bu_r = jnp.einsum('gpc,bsgc->bsgp', bbar_r, uf)
    bu_i = jnp.einsum('gpc,bsgc->bsgp', bbar_i, uf)
    a_r = jnp.broadcast_to(abar_r, bu_r.shape)
    a_i = jnp.broadcast_to(abar_i, bu_r.shape)
    _, _, x_r, x_i = lax.associative_scan(s5_combine, (a_r, a_i, bu_r, bu_i), axis=1)
    y = (jnp.einsum('gcp,bsgp->bsgc', c_re.astype(jnp.float32), x_r)
         - jnp.einsum('gcp,bsgp->bsgc', c_im.astype(jnp.float32), x_i))
    y = y.reshape(B, S, W_GROUP) + d_skip.astype(jnp.float32) * uf.reshape(B, S, W_GROUP)
    y = jax.nn.gelu(y)
    y = y * jax.nn.sigmoid(y @ glu_w.astype(jnp.float32) + glu_b.astype(jnp.float32))
    return y.astype(u.dtype)


def hmoe(h, w_group, b_group, w_expert, b_expert, w_gate, w_up, w_down):
    B, S, D = h.shape
    t = h.reshape(B * S, D)
    p_group = jax.nn.softmax((t @ w_group).astype(jnp.float32) + b_group.astype(jnp.float32), axis=-1)
    p_top, g_top = lax.top_k(p_group, 1)
    g_onehot = jax.nn.one_hot(g_top[:, 0], MOE_GROUPS, dtype=jnp.float32)
    e_logits = ((t @ w_expert).astype(jnp.float32) + b_expert.astype(jnp.float32)).reshape(-1, MOE_GROUPS, MOE_EPG)
    e_logits = jnp.einsum('tge,tg->te', e_logits, g_onehot)
    p_top2, e_top = lax.top_k(jax.nn.softmax(e_logits, axis=-1), MOE_TOPK)
    p_top2 = p_top2 / jnp.sum(p_top2, axis=-1, keepdims=True)
    w_e = jnp.einsum('tke,tk->te', jax.nn.one_hot(e_top, MOE_EPG, dtype=jnp.float32), p_top2)
    comb = (g_onehot[:, :, None] * (p_top * w_e)[:, None, :]).astype(t.dtype)
    out = jnp.zeros_like(t)
    for g in range(MOE_GROUPS):
        sl = slice(g * MOE_EPG, (g + 1) * MOE_EPG)
        a = jnp.einsum('td,edf->tef', t, w_gate[sl])
        u = jnp.einsum('td,edf->tef', t, w_up[sl])
        hid = jax.nn.silu(a) * u * comb[:, g, :, None]
        out = out + jnp.einsum('tef,efd->td', hid, w_down[sl])
    return out.reshape(B, S, D)


def setup_inputs(seed: int = 0) -> dict:
    key = jax.random.key(seed)
    keys = jax.random.split(key, 40)
    it = iter(range(40))

    def nrm(shape, scale):
        return jax.random.normal(keys[next(it)], shape, jnp.float32) * scale

    def gain(shape):
        return 1.0 + nrm(shape, 0.02)

    L, D = DEPTH, D_MODEL
    x = nrm((BATCH, SEQ, D), 1.0)
    attn_norm_g = gain((L, D))
    w_in = nrm((L, D, N_IN), D ** -0.5)
    mla_q_norm_g = gain((L, MLA_Q_LORA))
    mla_kv_norm_g = gain((L, MLA_KV_LORA))
    mla_w_uq = nrm((L, MLA_Q_LORA, MLA_HEADS * MLA_QK), MLA_Q_LORA ** -0.5)
    mla_w_ukv = nrm((L, MLA_KV_LORA, MLA_HEADS * (MLA_NOPE + MLA_V)), MLA_KV_LORA ** -0.5)
    nsa_cmp_pe = nrm((L, 2, CMP_BLOCK, NSA_HD), 0.1)
    nsa_cmp_w1 = nrm((L, 2, CMP_BLOCK * NSA_HD, NSA_HD), (CMP_BLOCK * NSA_HD) ** -0.5)
    nsa_cmp_w2 = nrm((L, 2, NSA_HD, NSA_HD), NSA_HD ** -0.5)
    pool_w = nrm((L, POOL_GROUPS, POOL_CH, POOL_CH), POOL_CH ** -0.5)
    pool_b = nrm((L, W_GROUP), 0.01)
    pool_scale = gain((L, W_GROUP))
    s5_a_re = -0.5 + nrm((L, S5_GROUPS, S5_STATE), 0.01)
    s5_a_im = jnp.pi * jnp.arange(S5_STATE, dtype=jnp.float32) + nrm((L, S5_GROUPS, S5_STATE), 0.01)
    s5_log_dt = jax.random.uniform(keys[next(it)], (L, S5_GROUPS), jnp.float32,
                                   minval=math.log(1e-3), maxval=math.log(1e-1))
    s5_b_re = nrm((L, S5_GROUPS, S5_STATE, S5_CH), (2 * S5_CH) ** -0.5)
    s5_b_im = nrm((L, S5_GROUPS, S5_STATE, S5_CH), (2 * S5_CH) ** -0.5)
    s5_c_re = nrm((L, S5_GROUPS, S5_CH, S5_STATE), S5_STATE ** -0.5)
    s5_c_im = nrm((L, S5_GROUPS, S5_CH, S5_STATE), S5_STATE ** -0.5)
    s5_d = nrm((L, W_GROUP), 0.5)
    s5_glu_w = nrm((L, W_GROUP, W_GROUP), W_GROUP ** -0.5)
    s5_glu_b = nrm((L, W_GROUP), 0.01)
    mix_norm_g = gain((L, D_MIX))
    w_out = nrm((L, D_MIX, D), D_MIX ** -0.5)
    ffn_norm_g = gain((L, D))
    moe_w_group = nrm((L, D, MOE_GROUPS), D ** -0.5)
    moe_b_group = nrm((L, MOE_GROUPS), 0.01)
    moe_w_expert = nrm((L, D, MOE_EXPERTS), D ** -0.5)
    moe_b_expert = nrm((L, MOE_EXPERTS), 0.01)
    moe_w_gate = nrm((L, MOE_EXPERTS, D, MOE_HIDDEN), D ** -0.5)
    moe_w_up = nrm((L, MOE_EXPERTS, D, MOE_HIDDEN), D ** -0.5)
    moe_w_down = nrm((L, MOE_EXPERTS, MOE_HIDDEN, D), MOE_HIDDEN ** -0.5)
    final_norm_g = gain((D,))
    return {'x': x, 'attn_norm_g': attn_norm_g, 'w_in': w_in,
            'mla_q_norm_g': mla_q_norm_g, 'mla_kv_norm_g': mla_kv_norm_g,
            'mla_w_uq': mla_w_uq, 'mla_w_ukv': mla_w_ukv,
            'nsa_cmp_pe': nsa_cmp_pe, 'nsa_cmp_w1': nsa_cmp_w1, 'nsa_cmp_w2': nsa_cmp_w2,
            'pool_w': pool_w, 'pool_b': pool_b, 'pool_scale': pool_scale,
            's5_a_re': s5_a_re, 's5_a_im': s5_a_im, 's5_log_dt': s5_log_dt,
            's5_b_re': s5_b_re, 's5_b_im': s5_b_im, 's5_c_re': s5_c_re, 's5_c_im': s5_c_im,
            's5_d': s5_d, 's5_glu_w': s5_glu_w, 's5_glu_b': s5_glu_b,
            'mix_norm_g': mix_norm_g, 'w_out': w_out, 'ffn_norm_g': ffn_norm_g,
            'moe_w_group': moe_w_group, 'moe_b_group': moe_b_group,
            'moe_w_expert': moe_w_expert, 'moe_b_expert': moe_b_expert,
            'moe_w_gate': moe_w_gate, 'moe_w_up': moe_w_up, 'moe_w_down': moe_w_down,
            'final_norm_g': final_norm_g}


def reference(x, attn_norm_g, w_in, mla_q_norm_g, mla_kv_norm_g, mla_w_uq, mla_w_ukv,
              nsa_cmp_pe, nsa_cmp_w1, nsa_cmp_w2, pool_w, pool_b, pool_scale,
              s5_a_re, s5_a_im, s5_log_dt, s5_b_re, s5_b_im, s5_c_re, s5_c_im,
              s5_d, s5_glu_w, s5_glu_b, mix_norm_g, w_out, ffn_norm_g,
              moe_w_group, moe_b_group, moe_w_expert, moe_b_expert,
              moe_w_gate, moe_w_up, moe_w_down, final_norm_g):
    B, S, _ = x.shape
    cos, sin = rope_tables(S)
    split_at = [COLS_A, COLS_A + COLS_B, COLS_A + COLS_B + COLS_C]
    for l in range(DEPTH):
        h = rmsnorm(x, attn_norm_g[l])
        c_a, c_b, c_c, c_d = jnp.split(h @ w_in[l], split_at, axis=-1)
        y_a = mla_mixer(c_a, mla_q_norm_g[l], mla_kv_norm_g[l], mla_w_uq[l], mla_w_ukv[l], cos, sin)
        y_b = nsa_mixer(c_b, nsa_cmp_pe[l], nsa_cmp_w1[l], nsa_cmp_w2[l], cos, sin)
        y_c = pool_mixer(c_c, pool_w[l], pool_b[l], pool_scale[l])
        y_d = s5_mixer(c_d, s5_a_re[l], s5_a_im[l], s5_log_dt[l], s5_b_re[l], s5_b_im[l],
                       s5_c_re[l], s5_c_im[l], s5_d[l], s5_glu_w[l], s5_glu_b[l])
        y = jnp.stack([y_a, y_b, y_c, y_d], axis=2)
        y = rmsnorm(y, mix_norm_g[l].reshape(N_MIXERS, W_GROUP)).reshape(B, S, D_MIX)
        x = x + y @ w_out[l]
        x = x + hmoe(rmsnorm(x, ffn_norm_g[l]), moe_w_group[l], moe_b_group[l],
                     moe_w_expert[l], moe_b_expert[l], moe_w_gate[l], moe_w_up[l], moe_w_down[l])
    return rmsnorm(x, final_norm_g)
```

```python
import functools
import math

import numpy as np
import jax
import jax.numpy as jnp
from jax import lax
from jax.experimental import pallas as pl
from jax.experimental.pallas import tpu as pltpu

F32 = jnp.float32
BF16 = jnp.bfloat16

D_MODEL = 2048
SEQ = 2048
W_GROUP = 512
LANES = 128
ROW_CHUNKS = D_MODEL // LANES

ROPE_THETA = 500000.0
ROPE_HALF = 16
NEG_INF = -1.0e30
FORCE_SCORE = 1.0e4
EPS = 1e-6

MLA_HEADS = 4
MLA_Q_LORA = 448
MLA_NOPE = 96
MLA_V = 128
HEAD_D = 128

CMP_BLOCK = 32
CMP_STRIDE = 16
SEL_BLOCK = 64
N_SEL = 8
N_LOCAL = 2
WINDOW = 512
N_CMP_PAD = SEQ // CMP_STRIDE
N_BLK = SEQ // SEL_BLOCK

POOL_SIZES = (2, 4, 8, 16)
S5_GROUPS = 32
S5_CH = 16
S5_STATE = 64
S5_WIDTH = S5_GROUPS * S5_STATE

MOE_GROUPS = 4
MOE_EPG = 8
MOE_EXPERTS = 32
MOE_HIDDEN = 512

COL_CQ, COL_NQ, COL_POOL, COL_S5 = 0, 512, 1024, 1536
COL_CKV, COL_KR, COL_KV6, COL_GL = 2048, 2176, 2304, 3072
N_IN_PAD = 3200

VMEM_LIMIT = 56 * 1024 * 1024


def _cparams(sem, vmem=VMEM_LIMIT):
    return pltpu.CompilerParams(dimension_semantics=sem, vmem_limit_bytes=vmem)


def _rms(x, n=None):
    n = x.shape[-1] if n is None else n
    return x * lax.rsqrt(jnp.sum(x * x, axis=-1, keepdims=True) / n + EPS)


def _dot(a, b):
    return jnp.dot(a, b, preferred_element_type=F32)


def _dot_nt(a, b, precision=None):
    return lax.dot_general(a, b, (((1,), (1,)), ((), ())), preferred_element_type=F32,
                           precision=precision)


def _rope(x, c, sa, sb):
    return x * c + pltpu.roll(x, LANES - ROPE_HALF, 1) * sa + pltpu.roll(x, ROPE_HALF, 1) * sb


def _in_proj_kernel(x_ref, g_ref, w_ref, o_ref, h_ref):
    @pl.when(pl.program_id(1) == 0)
    def _():
        h_ref[...] = (_rms(x_ref[...]) * g_ref[...]).astype(BF16)

    o_ref[...] = _dot(h_ref[...], w_ref[...])


def _in_proj(x, g, w):
    t = x.shape[0]
    tm, tn = 1024, 640
    return pl.pallas_call(
        _in_proj_kernel,
        grid=(t // tm, N_IN_PAD // tn),
        in_specs=[pl.BlockSpec((tm, D_MODEL), lambda i, j: (i, 0)),
                  pl.BlockSpec((1, D_MODEL), lambda i, j: (0, 0)),
                  pl.BlockSpec((D_MODEL, tn), lambda i, j: (0, j))],
        out_specs=pl.BlockSpec((tm, tn), lambda i, j: (i, j)),
        out_shape=jax.ShapeDtypeStruct((t, N_IN_PAD), F32),
        scratch_shapes=[pltpu.VMEM((tm, D_MODEL), BF16)],
        compiler_params=_cparams(("parallel", "arbitrary")),
        name="in_proj",
    )(x, g, w)


def _mla_prep_kernel(cq_ref, ckv_ref, kr_ref, qg_ref, kvg_ref, wuq_ref, wk_ref, wv_ref,
                     qc_ref, qsa_ref, qsb_ref, kc_ref, ksa_ref, ksb_ref,
                     q_out, k_out, v_out):
    scale = 1.0 / math.sqrt(HEAD_D)
    qn = (_rms(cq_ref[...], MLA_Q_LORA) * qg_ref[...]).astype(BF16)
    q = _dot(qn, wuq_ref[...])
    kvn = (_rms(ckv_ref[...]) * kvg_ref[...]).astype(BF16)
    kn = _dot(kvn, wk_ref[...])
    v_out[...] = _dot(kvn, wv_ref[...]).astype(BF16)
    kr = _rope(kr_ref[...], kc_ref[...], ksa_ref[...], ksb_ref[...])
    kr = pltpu.roll(kr, MLA_NOPE, 1)
    for h in range(MLA_HEADS):
        sl = slice(h * HEAD_D, (h + 1) * HEAD_D)
        qh = _rope(q[:, sl], qc_ref[...], qsa_ref[...], qsb_ref[...])
        q_out[:, sl] = (qh * scale).astype(BF16)
        k_out[:, sl] = (kn[:, sl] + kr).astype(BF16)


def _mla_prep(cols, qg, kvg, wuq, wk, wv, qtabs, ktabs):
    t = cols.shape[0]
    tm = 512
    nsb = SEQ // tm
    tab = pl.BlockSpec((tm, LANES), lambda i: (i % nsb, 0))
    full = lambda shape: pl.BlockSpec(shape, lambda i: (0, 0))
    out = pl.BlockSpec((tm, W_GROUP), lambda i: (i, 0))
    return pl.pallas_call(
        _mla_prep_kernel,
        grid=(t // tm,),
        in_specs=[pl.BlockSpec((tm, 512), lambda i: (i, COL_CQ // 512)),
                  pl.BlockSpec((tm, LANES), lambda i: (i, COL_CKV // LANES)),
                  pl.BlockSpec((tm, LANES), lambda i: (i, COL_KR // LANES)),
                  full((1, 512)), full((1, LANES)),
                  full((512, 512)), full((LANES, 512)), full((LANES, 512)),
                  tab, tab, tab, tab, tab, tab],
        out_specs=[out, out, out],
        out_shape=[jax.ShapeDtypeStruct((t, W_GROUP), BF16)] * 3,
        compiler_params=_cparams(("parallel",)),
        name="mla_prep",
    )(cols, cols, cols, qg, kvg, wuq, wk, wv, *qtabs, *ktabs)


def _mla_attn_kernel(q_ref, k_ref, v_ref, o_ref, *, tq):
    i = pl.program_id(1)
    row = lax.broadcasted_iota(jnp.int32, (tq, tq), 0)
    col = lax.broadcasted_iota(jnp.int32, (tq, tq), 1)
    for h in range(MLA_HEADS):
        sl = slice(h * HEAD_D, (h + 1) * HEAD_D)
        q = q_ref[:, sl]

        def body(j, carry, sl=sl, q=q):
            m, l, acc = carry
            r0 = pl.multiple_of(j * tq, tq)
            s = _dot_nt(q, k_ref[pl.ds(r0, tq), sl])
            s = jnp.where(col + j * tq <= row + i * tq, s, NEG_INF)
            m_new = jnp.maximum(m, jnp.max(s, axis=-1, keepdims=True))
            p = jnp.exp(s - m_new)
            alpha = jnp.exp(m - m_new)
            l = alpha * l + jnp.sum(p, axis=-1, keepdims=True)
            acc = alpha * acc + _dot(p.astype(BF16), v_ref[pl.ds(r0, tq), sl])
            return m_new, l, acc

        init = (jnp.full((tq, 1), NEG_INF, F32), jnp.zeros((tq, 1), F32),
                jnp.zeros((tq, HEAD_D), F32))
        _, l, acc = lax.fori_loop(0, i + 1, body, init)
        o_ref[:, sl] = acc / l


def _mla_attn(q, k, v, batch):
    tq = 256
    nq = SEQ // tq
    return pl.pallas_call(
        functools.partial(_mla_attn_kernel, tq=tq),
        grid=(batch, nq),
        in_specs=[pl.BlockSpec((tq, W_GROUP), lambda b, i: (b * nq + i, 0)),
                  pl.BlockSpec((SEQ, W_GROUP), lambda b, i: (b, 0)),
                  pl.BlockSpec((SEQ, W_GROUP), lambda b, i: (b, 0))],
        out_specs=pl.BlockSpec((tq, W_GROUP), lambda b, i: (b * nq + i, 0)),
        out_shape=jax.ShapeDtypeStruct((batch * SEQ, W_GROUP), F32),
        compiler_params=_cparams(("parallel", "arbitrary")),
        name="mla_attn",
    )(q, k, v)


def _nsa_prep_kernel(q_ref, kv_ref, gl_ref, c_ref, sa_ref, sb_ref,
                     q_out, kc_out, vc_out, kv_out, g_out):
    scale = 1.0 / math.sqrt(HEAD_D)
    c, sa, sb = c_ref[...], sa_ref[...], sb_ref[...]
    for h in range(4):
        sl = slice(h * HEAD_D, (h + 1) * HEAD_D)
        q_out[:, sl] = (_rope(q_ref[:, sl], c, sa, sb) * scale).astype(BF16)
    kv = lambda n: kv_ref[:, n * HEAD_D:(n + 1) * HEAD_D]
    kc_out[...] = _rope(kv(0), c, sa, sb).astype(BF16)
    vc_out[...] = kv(1).astype(BF16)
    kv_out[:, 0:128] = _rope(kv(2), c, sa, sb).astype(BF16)
    kv_out[:, 128:256] = kv(3).astype(BF16)
    kv_out[:, 256:384] = _rope(kv(4), c, sa, sb).astype(BF16)
    kv_out[:, 384:512] = kv(5).astype(BF16)
    g_out[...] = jax.nn.sigmoid(gl_ref[...])


def _nsa_prep(cols, tabs):
    t = cols.shape[0]
    tm = 512
    nsb = SEQ // tm
    tab = pl.BlockSpec((tm, LANES), lambda i: (i % nsb, 0))
    return pl.pallas_call(
        _nsa_prep_kernel,
        grid=(t // tm,),
        in_specs=[pl.BlockSpec((tm, 512), lambda i: (i, COL_NQ // 512)),
                  pl.BlockSpec((tm, 768), lambda i: (i, COL_KV6 // 768)),
                  pl.BlockSpec((tm, LANES), lambda i: (i, COL_GL // LANES)),
                  tab, tab, tab],
        out_specs=[pl.BlockSpec((tm, 512), lambda i: (i, 0)),
                   pl.BlockSpec((tm, LANES), lambda i: (i, 0)),
                   pl.BlockSpec((tm, LANES), lambda i: (i, 0)),
                   pl.BlockSpec((tm, 512), lambda i: (i, 0)),
                   pl.BlockSpec((tm, LANES), lambda i: (i, 0))],
        out_shape=[jax.ShapeDtypeStruct((t, 512), BF16),
                   jax.ShapeDtypeStruct((t, LANES), BF16),
                   jax.ShapeDtypeStruct((t, LANES), BF16),
                   jax.ShapeDtypeStruct((t, 512), BF16),
                   jax.ShapeDtypeStruct((t, LANES), F32)],
        compiler_params=_cparams(("parallel",)),
        name="nsa_prep",
    )(cols, cols, cols, *tabs)


def _nsa_compress_kernel(xk_ref, xv_ref, w1_ref, w2_ref, pe_ref, k_out, v_out):
    half = CMP_STRIDE * HEAD_D
    for c, (x_ref, o_ref) in enumerate(((xk_ref, k_out), (xv_ref, v_out))):
        x = x_ref[...]
        a = _dot(x, w1_ref[c, 0:half, :])
        b = _dot(x, w1_ref[c, half:2 * half, :])
        b = pltpu.roll(b, N_CMP_PAD - 1, 0)
        pe = _dot(pe_ref[c], w1_ref[c])[0:1, :]
        hid = jax.nn.gelu(a + b + pe)
        o_ref[0] = _dot(hid.astype(BF16), w2_ref[c]).astype(BF16)


def _nsa_compress(xk, xv, w1, w2, pe, batch):
    xspec = pl.BlockSpec((N_CMP_PAD, CMP_STRIDE * HEAD_D), lambda b: (b, 0))
    ospec = pl.BlockSpec((1, N_CMP_PAD, HEAD_D), lambda b: (b, 0, 0))
    return pl.pallas_call(
        _nsa_compress_kernel,
        grid=(batch,),
        in_specs=[xspec, xspec,
                  pl.BlockSpec((2, CMP_BLOCK * HEAD_D, HEAD_D), lambda b: (0, 0, 0)),
                  pl.BlockSpec((2, HEAD_D, HEAD_D), lambda b: (0, 0, 0)),
                  pl.BlockSpec((2, 8, CMP_BLOCK * HEAD_D), lambda b: (0, 0, 0))],
        out_specs=[ospec, ospec],
        out_shape=[jax.ShapeDtypeStruct((batch, N_CMP_PAD, HEAD_D), BF16)] * 2,
        compiler_params=_cparams(("parallel",)),
        name="nsa_compress",
    )(xk, xv, w1, w2, pe)


def _softmax_rows(s):
    m = jnp.max(s, axis=-1, keepdims=True)
    p = jnp.exp(s - m)
    return p / jnp.sum(p, axis=-1, keepdims=True)


def _nsa_attn_kernel(q_ref, kc_ref, vc_ref, kv_ref, g_ref, ovt_ref, e_ref, o_ref,
                     m_ref, l_ref, acc_ref, *, tq, ck):
    i = pl.program_id(1)
    q0 = i * tq
    nh = 4
    qs = jnp.concatenate([q_ref[:, h * HEAD_D:(h + 1) * HEAD_D] for h in range(nh)], axis=0)
    qpos = q0 + lax.broadcasted_iota(jnp.int32, (tq, 1), 0)
    stack = lambda a: jnp.concatenate([a] * nh, axis=0)

    sc = _dot_nt(qs, kc_ref[0])
    n_idx = lax.broadcasted_iota(jnp.int32, (tq, N_CMP_PAD), 1)
    valid_c = n_idx * CMP_STRIDE + (CMP_BLOCK - 1) <= qpos
    valid_c4 = stack(valid_c)
    pc = _softmax_rows(jnp.where(valid_c4, sc, NEG_INF))
    pc = jnp.where(valid_c4, pc, 0.0)
    o_c = _dot(pc.astype(BF16), vc_ref[0])
    psum = pc[0:tq] + pc[tq:2 * tq] + pc[2 * tq:3 * tq] + pc[3 * tq:4 * tq]

    imp_t = _dot_nt(ovt_ref[...], psum, precision=lax.Precision.HIGHEST)
    kblk = lax.broadcasted_iota(jnp.int32, (N_BLK, tq), 0)
    cur = (q0 + lax.broadcasted_iota(jnp.int32, (N_BLK, tq), 1)) // SEL_BLOCK
    forced = (kblk == 0) | ((kblk <= cur) & (kblk > cur - N_LOCAL))
    score = jnp.where(forced, FORCE_SCORE, jnp.where(kblk <= cur, imp_t, -1.0))
    cnt = jnp.zeros((N_BLK, tq), F32)
    for j in range(N_BLK):
        sj = score[j:j + 1, :]
        beats = (sj > score) | ((sj == score) & (kblk > j))
        cnt = cnt + beats.astype(F32)
    sel = jnp.transpose((cnt < N_SEL).astype(F32)).astype(BF16)

    m_ref[...] = jnp.full(m_ref.shape, NEG_INF, F32)
    l_ref[...] = jnp.zeros(l_ref.shape, F32)
    acc_ref[...] = jnp.zeros(acc_ref.shape, F32)

    def body(c, carry):
        r0 = pl.multiple_of(c * ck, ck)
        s = _dot_nt(qs, kv_ref[pl.ds(r0, ck), 0:128])
        chosen = _dot(sel, e_ref[c])
        kpos = r0 + lax.broadcasted_iota(jnp.int32, (tq, ck), 1)
        ok = (chosen > 0.5) & (kpos <= qpos)
        s = s + stack(jnp.where(ok, 0.0, NEG_INF))
        m_old = m_ref[...]
        m_new = jnp.maximum(m_old, jnp.max(s, axis=-1, keepdims=True))
        p = jnp.exp(s - m_new)
        alpha = jnp.exp(m_old - m_new)
        l_ref[...] = alpha * l_ref[...] + jnp.sum(p, axis=-1, keepdims=True)
        acc_ref[...] = alpha * acc_ref[...] + _dot(p.astype(BF16), kv_ref[pl.ds(r0, ck), 128:256])
        m_ref[...] = m_new
        return carry

    lax.fori_loop(0, q0 // ck + 1, body, 0)
    o_s = acc_ref[...] / l_ref[...]

    wlen = WINDOW + tq
    w0 = pl.multiple_of(jnp.maximum(q0 - WINDOW, 0), tq)
    sw = _dot_nt(qs, kv_ref[pl.ds(w0, wlen), 256:384])
    kpos = w0 + lax.broadcasted_iota(jnp.int32, (tq, wlen), 1)
    ok = (kpos <= qpos) & (kpos > qpos - WINDOW)
    pw = _softmax_rows(sw + stack(jnp.where(ok, 0.0, NEG_INF)))
    o_w = _dot(pw.astype(BF16), kv_ref[pl.ds(w0, wlen), 384:512])

    g = g_ref[...]
    for h in range(nh):
        rs = slice(h * tq, (h + 1) * tq)
        o_ref[:, h * HEAD_D:(h + 1) * HEAD_D] = (
            g[:, 3 * h:3 * h + 1] * o_c[rs] + g[:, 3 * h + 1:3 * h + 2] * o_s[rs]
            + g[:, 3 * h + 2:3 * h + 3] * o_w[rs])


def _nsa_attn(q, kcmp, vcmp, kv, gates, ovt, emat, batch):
    tq, ck = 128, 512
    nq = SEQ // tq
    return pl.pallas_call(
        functools.partial(_nsa_attn_kernel, tq=tq, ck=ck),
        grid=(batch, nq),
        in_specs=[pl.BlockSpec((tq, 512), lambda b, i: (b * nq + i, 0)),
                  pl.BlockSpec((1, N_CMP_PAD, HEAD_D), lambda b, i: (b, 0, 0)),
                  pl.BlockSpec((1, N_CMP_PAD, HEAD_D), lambda b, i: (b, 0, 0)),
                  pl.BlockSpec((SEQ, 512), lambda b, i: (b, 0)),
                  pl.BlockSpec((tq, LANES), lambda b, i: (b * nq + i, 0)),
                  pl.BlockSpec((N_BLK, N_CMP_PAD), lambda b, i: (0, 0)),
                  pl.BlockSpec((SEQ // ck, N_BLK, ck), lambda b, i: (0, 0, 0))],
        out_specs=pl.BlockSpec((tq, 512), lambda b, i: (b * nq + i, 0)),
        out_shape=jax.ShapeDtypeStruct((batch * SEQ, 512), F32),
        scratch_shapes=[pltpu.VMEM((4 * tq, 1), F32), pltpu.VMEM((4 * tq, 1), F32),
                        pltpu.VMEM((4 * tq, HEAD_D), F32)],
        compiler_params=_cparams(("parallel", "arbitrary")),
        name="nsa_attn",
    )(q, kcmp, vcmp, kv, gates, ovt, emat)


def _pool_kernel(u_ref, w_ref, b_ref, s_ref, o_ref, pad_ref):
    maxw = POOL_SIZES[-1]
    pad_ref[0:maxw, :] = jnp.zeros((maxw, W_GROUP), F32)
    pad_ref[maxw:maxw + SEQ, :] = u_ref[...]
    rc = 512
    for g, w in enumerate(POOL_SIZES):
        sl = slice(g * LANES, (g + 1) * LANES)
        for r in range(SEQ // rc):
            acc = pad_ref[maxw + r * rc:maxw + (r + 1) * rc, sl]
            tok = acc
            for j in range(1, w):
                acc = acc + pad_ref[maxw - j + r * rc:maxw - j + (r + 1) * rc, sl]
            t = r * rc + lax.broadcasted_iota(jnp.int32, (rc, 1), 0)
            cnt = jnp.minimum(t + 1, w).astype(F32)
            d = acc / cnt - tok
            y = _dot(d.astype(BF16), w_ref[g])
            o_ref[r * rc:(r + 1) * rc, sl] = (y + b_ref[:, sl]) * s_ref[:, sl]


def _pool(cols, w, b, s, batch):
    return pl.pallas_call(
        _pool_kernel,
        grid=(batch,),
        in_specs=[pl.BlockSpec((SEQ, W_GROUP), lambda i: (i, COL_POOL // W_GROUP)),
                  pl.BlockSpec((4, LANES, LANES), lambda i: (0, 0, 0)),
                  pl.BlockSpec((1, W_GROUP), lambda i: (0, 0)),
                  pl.BlockSpec((1, W_GROUP), lambda i: (0, 0))],
        out_specs=pl.BlockSpec((SEQ, W_GROUP), lambda i: (i, 0)),
        out_shape=jax.ShapeDtypeStruct((batch * SEQ, W_GROUP), F32),
        scratch_shapes=[pltpu.VMEM((SEQ + POOL_SIZES[-1], W_GROUP), F32)],
        compiler_params=_cparams(("parallel",)),
        name="pool",
    )(cols, w, b, s)


def _s5_kernel(u_ref, wb_ref, ar_ref, ai_ref, wc_ref, d_ref, gw_ref, gb_ref, o_ref,
               bu_ref, st_ref, *, batch, tc):
    @pl.when(pl.program_id(0) == 0)
    def _():
        st_ref[...] = jnp.zeros(st_ref.shape, F32)

    u = u_ref[...]
    bu_ref[...] = _dot(u.astype(BF16), wb_ref[...])
    lc = 512
    unroll = 8
    for c in range(S5_WIDTH // lc):
        re = slice(c * lc, (c + 1) * lc)
        im = slice(S5_WIDTH + c * lc, S5_WIDTH + (c + 1) * lc)
        ar, ai = ar_ref[:, re], ai_ref[:, re]

        def body(tb, carry, re=re, im=im, ar=ar, ai=ai):
            xr, xi = carry
            for k in range(unroll):
                r0 = pl.multiple_of((tb * unroll + k) * batch, batch)
                nxr = ar * xr - ai * xi + bu_ref[pl.ds(r0, batch), re]
                nxi = ar * xi + ai * xr + bu_ref[pl.ds(r0, batch), im]
                bu_ref[pl.ds(r0, batch), re] = nxr
                bu_ref[pl.ds(r0, batch), im] = nxi
                xr, xi = nxr, nxi
            return xr, xi

        xr, xi = lax.fori_loop(0, tc // unroll, body, (st_ref[:, re], st_ref[:, im]))
        st_ref[:, re] = xr
        st_ref[:, im] = xi

    y = _dot(bu_ref[...].astype(BF16), wc_ref[...]) + d_ref[...] * u
    y = jax.nn.gelu(y)
    z = _dot(y.astype(BF16), gw_ref[...]) + gb_ref[...]
    o_ref[...] = y * jax.nn.sigmoid(z)


def _s5(u_t, wb, ar, ai, wc, d, gw, gb, batch):
    tc = 64
    rows = tc * batch
    full = lambda shape: pl.BlockSpec(shape, lambda i: (0, 0))
    return pl.pallas_call(
        functools.partial(_s5_kernel, batch=batch, tc=tc),
        grid=(SEQ // tc,),
        in_specs=[pl.BlockSpec((rows, W_GROUP), lambda i: (i, 0)),
                  full((W_GROUP, 2 * S5_WIDTH)), full((batch, S5_WIDTH)), full((batch, S5_WIDTH)),
                  full((2 * S5_WIDTH, W_GROUP)), full((1, W_GROUP)),
                  full((W_GROUP, W_GROUP)), full((1, W_GROUP))],
        out_specs=pl.BlockSpec((rows, W_GROUP), lambda i: (i, 0)),
        out_shape=jax.ShapeDtypeStruct((SEQ * batch, W_GROUP), F32),
        scratch_shapes=[pltpu.VMEM((rows, 2 * S5_WIDTH), F32),
                        pltpu.VMEM((batch, 2 * S5_WIDTH), F32)],
        compiler_params=_cparams(("arbitrary",)),
        name="s5",
    )(u_t, wb, ar, ai, wc, d, gw, gb)


def _out_proj_kernel(ya_ref, yb_ref, yc_ref, yd_ref, mg_ref, wo_ref, x_ref, fg_ref,
                     wr_ref, br_ref, x1_ref, h2_ref, lg_ref, *, tm):
    acc = x_ref[...]
    for gi, y_ref in enumerate((ya_ref, yb_ref, yc_ref, yd_ref)):
        sl = slice(gi * W_GROUP, (gi + 1) * W_GROUP)
        n = (_rms(y_ref[...]) * mg_ref[:, sl]).astype(BF16)
        acc = acc + _dot(n, wo_ref[sl, :])
    x1_ref[...] = acc
    h2 = _rms(acc) * fg_ref[...]
    lg_ref[...] = jnp.dot(h2, wr_ref[...], preferred_element_type=F32,
                          precision=lax.Precision.HIGHEST) + br_ref[...]
    for s in range(ROW_CHUNKS):
        h2_ref[pl.ds(s, tm, stride=ROW_CHUNKS), :] = h2[:, s * LANES:(s + 1) * LANES]


def _out_proj(ya, yb, yc, yd, mg, wo, x, fg, wr, br):
    t = x.shape[0]
    tm = 256
    yspec = pl.BlockSpec((tm, W_GROUP), lambda i: (i, 0))
    full = lambda shape: pl.BlockSpec(shape, lambda i: (0, 0))
    return pl.pallas_call(
        functools.partial(_out_proj_kernel, tm=tm),
        grid=(t // tm,),
        in_specs=[yspec, yspec, yspec, yspec, full((1, D_MODEL)), full((D_MODEL, D_MODEL)),
                  pl.BlockSpec((tm, D_MODEL), lambda i: (i, 0)), full((1, D_MODEL)),
                  full((D_MODEL, LANES)), full((1, LANES))],
        out_specs=[pl.BlockSpec((tm, D_MODEL), lambda i: (i, 0)),
                   pl.BlockSpec((tm * ROW_CHUNKS, LANES), lambda i: (i, 0)),
                   pl.BlockSpec((tm, LANES), lambda i: (i, 0))],
        out_shape=[jax.ShapeDtypeStruct((t, D_MODEL), F32),
                   jax.ShapeDtypeStruct((t * ROW_CHUNKS, LANES), F32),
                   jax.ShapeDtypeStruct((t, LANES), F32)],
        compiler_params=_cparams(("parallel",)),
        name="out_proj",
    )(ya, yb, yc, yd, mg, wo, x, fg, wr, br)


def _router_kernel(lg_ref, meta_ref, cnt_ref, carry_ref, *, tm):
    @pl.when(pl.program_id(0) == 0)
    def _():
        carry_ref[...] = jnp.zeros(carry_ref.shape, F32)

    lg = lg_ref[...]
    lane = lax.broadcasted_iota(jnp.int32, (tm, LANES), 1)
    big = jnp.int32(1 << 20)
    rmax = lambda a: jnp.max(a, axis=-1, keepdims=True)
    rmin = lambda a: jnp.min(a, axis=-1, keepdims=True)
    rsum = lambda a: jnp.sum(a, axis=-1, keepdims=True)

    is_g = lane < MOE_GROUPS
    gl = jnp.where(is_g, lg, NEG_INF)
    gm = rmax(gl)
    p_top = 1.0 / rsum(jnp.where(is_g, jnp.exp(gl - gm), 0.0))
    g_top = rmin(jnp.where(is_g & (gl == gm), lane, big))

    is_e = (lane >= MOE_GROUPS) & (lane < MOE_GROUPS + MOE_EXPERTS) \
        & (((lane - MOE_GROUPS) // MOE_EPG) == g_top)
    el = jnp.where(is_e, lg, NEG_INF)
    ee = jnp.where(is_e, jnp.exp(el - rmax(el)), 0.0)
    p = jnp.where(is_e, ee / rsum(ee), -1.0)
    p1 = rmax(p)
    i1 = rmin(jnp.where(p == p1, lane, big))
    p_rest = jnp.where(lane == i1, -1.0, p)
    p2 = rmax(p_rest)
    i2 = rmin(jnp.where((p_rest == p2) & is_e & (lane != i1), lane, big))
    den = p1 + p2
    w1 = p_top * (p1 / den)
    w2 = p_top * (p2 / den)

    hit1, hit2 = lane == i1, lane == i2
    oh = (hit1 | hit2).astype(F32)
    r = lax.broadcasted_iota(jnp.int32, (tm, tm), 0)
    c = lax.broadcasted_iota(jnp.int32, (tm, tm), 1)
    before = _dot((c < r).astype(BF16), oh.astype(BF16)) + carry_ref[0:1, :]
    r1 = rsum(jnp.where(hit1, before, 0.0))
    r2 = rsum(jnp.where(hit2, before, 0.0))
    carry_ref[...] = carry_ref[...] + jnp.sum(oh, axis=0, keepdims=True)
    cnt_ref[...] = carry_ref[...]

    e1 = (i1 - MOE_GROUPS).astype(F32)
    e2 = (i2 - MOE_GROUPS).astype(F32)
    vals = (e1, e2, w1, w2, r1, r2)
    meta = jnp.zeros((tm, LANES), F32)
    for k, v in enumerate(vals):
        meta = jnp.where(lane == k, v, meta)
    meta_ref[...] = meta


def _router(logits):
    t = logits.shape[0]
    tm = 512
    return pl.pallas_call(
        functools.partial(_router_kernel, tm=tm),
        grid=(t // tm,),
        in_specs=[pl.BlockSpec((tm, LANES), lambda i: (i, 0))],
        out_specs=[pl.BlockSpec((tm, LANES), lambda i: (i, 0)),
                   pl.BlockSpec((8, LANES), lambda i: (i, 0))],
        out_shape=[jax.ShapeDtypeStruct((t, LANES), F32),
                   jax.ShapeDtypeStruct((t // tm * 8, LANES), F32)],
        scratch_shapes=[pltpu.VMEM((8, LANES), F32)],
        compiler_params=_cparams(("arbitrary",)),
        name="router",
    )(logits)


def _dispatch_kernel(d1_ref, d2_ref, h_ref, xs_ref, sem, *, td):
    base = pl.program_id(0) * td

    def copies(r):
        src = h_ref.at[pl.ds((base + r) * ROW_CHUNKS, ROW_CHUNKS)]
        return (pltpu.make_async_copy(src, xs_ref.at[pl.ds(d1_ref[base + r] * ROW_CHUNKS, ROW_CHUNKS)], sem),
                pltpu.make_async_copy(src, xs_ref.at[pl.ds(d2_ref[base + r] * ROW_CHUNKS, ROW_CHUNKS)], sem))

    def start(r, carry):
        for cp in copies(r):
            cp.start()
        return carry

    def wait(r, carry):
        for cp in copies(r):
            cp.wait()
        return carry

    lax.fori_loop(0, td, start, 0)
    lax.fori_loop(0, td, wait, 0)


def _dispatch(dest1, dest2, h2):
    t = dest1.shape[0]
    td = 512
    return pl.pallas_call(
        functools.partial(_dispatch_kernel, td=td),
        grid_spec=pltpu.PrefetchScalarGridSpec(
            num_scalar_prefetch=2,
            grid=(t // td,),
            in_specs=[pl.BlockSpec(memory_space=pl.ANY)],
            out_specs=pl.BlockSpec(memory_space=pl.ANY),
            scratch_shapes=[pltpu.SemaphoreType.DMA(())]),
        out_shape=jax.ShapeDtypeStruct((2 * t * ROW_CHUNKS, LANES), F32),
        compiler_params=pltpu.CompilerParams(dimension_semantics=("arbitrary",),
                                             has_side_effects=True),
        name="dispatch",
    )(dest1, dest2, h2)


def _expert_kernel(vt_ref, ve_ref, vlo_ref, vhi_ref, vfirst_ref, vvalid_ref,
                   xs_ref, wg_ref, wu_ref, wd_ref, ys_ref, wgb_ref, wub_ref, wdb_ref, *, tmx):
    v = pl.program_id(0)
    prev = jnp.maximum(v - 1, 0)

    @pl.when((v == 0) | (ve_ref[v] != ve_ref[prev]))
    def _():
        wgb_ref[...] = wg_ref[0].astype(BF16)
        wub_ref[...] = wu_ref[0].astype(BF16)
        wdb_ref[...] = wd_ref[0].astype(BF16)

    @pl.when(vvalid_ref[v] == 1)
    def _():
        x = jnp.concatenate([xs_ref[pl.ds(s, tmx, stride=ROW_CHUNKS), :] for s in range(ROW_CHUNKS)],
                            axis=1).astype(BF16)
        a = _dot(x, wgb_ref[...])
        u = _dot(x, wub_ref[...])
        hid = (jax.nn.silu(a) * u).astype(BF16)
        y = _dot(hid, wdb_ref[...])
        rows = lax.broadcasted_iota(jnp.int32, (tmx, 1), 0)
        mine = (rows >= vlo_ref[v]) & (rows < vhi_ref[v])

        @pl.when(vfirst_ref[v] == 1)
        def _():
            for s in range(ROW_CHUNKS):
                ys_ref[pl.ds(s, tmx, stride=ROW_CHUNKS), :] = jnp.where(
                    mine, y[:, s * LANES:(s + 1) * LANES], 0.0)

        @pl.when(vfirst_ref[v] == 0)
        def _():
            for s in range(ROW_CHUNKS):
                old = ys_ref[pl.ds(s, tmx, stride=ROW_CHUNKS), :]
                ys_ref[pl.ds(s, tmx, stride=ROW_CHUNKS), :] = jnp.where(
                    mine, y[:, s * LANES:(s + 1) * LANES], old)


def _experts(sched, xs, wg, wu, wd, tmx):
    nvis = sched[0].shape[0]
    rows = xs.shape[0]
    xspec = pl.BlockSpec((tmx * ROW_CHUNKS, LANES), lambda v, vt, *_: (vt[v], 0))
    return pl.pallas_call(
        functools.partial(_expert_kernel, tmx=tmx),
        grid_spec=pltpu.PrefetchScalarGridSpec(
            num_scalar_prefetch=6,
            grid=(nvis,),
            in_specs=[xspec,
                      pl.BlockSpec((1, D_MODEL, MOE_HIDDEN), lambda v, vt, ve, *_: (ve[v], 0, 0)),
                      pl.BlockSpec((1, D_MODEL, MOE_HIDDEN), lambda v, vt, ve, *_: (ve[v], 0, 0)),
                      pl.BlockSpec((1, MOE_HIDDEN, D_MODEL), lambda v, vt, ve, *_: (ve[v], 0, 0))],
            out_specs=xspec,
            scratch_shapes=[pltpu.VMEM((D_MODEL, MOE_HIDDEN), BF16),
                            pltpu.VMEM((D_MODEL, MOE_HIDDEN), BF16),
                            pltpu.VMEM((MOE_HIDDEN, D_MODEL), BF16)]),
        out_shape=jax.ShapeDtypeStruct((rows, LANES), F32),
        compiler_params=_cparams(("arbitrary",)),
        name="experts",
    )(*sched, xs, wg, wu, wd)


def _combine_kernel(d1_ref, d2_ref, ys_ref, meta_ref, x1_ref, fg_ref, o_ref,
                    b1_ref, b2_ref, sem, *, tc, final):
    base = pl.program_id(0) * tc

    def copies(r):
        dst = pl.ds(r * ROW_CHUNKS, ROW_CHUNKS)
        return (pltpu.make_async_copy(ys_ref.at[pl.ds(d1_ref[base + r] * ROW_CHUNKS, ROW_CHUNKS)],
                                      b1_ref.at[dst], sem),
                pltpu.make_async_copy(ys_ref.at[pl.ds(d2_ref[base + r] * ROW_CHUNKS, ROW_CHUNKS)],
                                      b2_ref.at[dst], sem))

    def start(r, carry):
        for cp in copies(r):
            cp.start()
        return carry

    def wait(r, carry):
        for cp in copies(r):
            cp.wait()
        return carry

    lax.fori_loop(0, tc, start, 0)
    lax.fori_loop(0, tc, wait, 0)
    w1 = meta_ref[:, 2:3]
    w2 = meta_ref[:, 3:4]
    parts = []
    for s in range(ROW_CHUNKS):
        y = (w1 * b1_ref[pl.ds(s, tc, stride=ROW_CHUNKS), :]
             + w2 * b2_ref[pl.ds(s, tc, stride=ROW_CHUNKS), :])
        parts.append(x1_ref[:, s * LANES:(s + 1) * LANES] + y)
    x2 = jnp.concatenate(parts, axis=1)
    if final:
        x2 = _rms(x2) * fg_ref[...]
    o_ref[...] = x2


def _combine(dest1, dest2, ys, meta, x1, fg, final):
    t = x1.shape[0]
    tc = 256
    return pl.pallas_call(
        functools.partial(_combine_kernel, tc=tc, final=final),
        grid_spec=pltpu.PrefetchScalarGridSpec(
            num_scalar_prefetch=2,
            grid=(t // tc,),
            in_specs=[pl.BlockSpec(memory_space=pl.ANY),
                      pl.BlockSpec((tc, LANES), lambda i, *_: (i, 0)),
                      pl.BlockSpec((tc, D_MODEL), lambda i, *_: (i, 0)),
                      pl.BlockSpec((1, D_MODEL), lambda i, *_: (0, 0))],
            out_specs=pl.BlockSpec((tc, D_MODEL), lambda i, *_: (i, 0)),
            scratch_shapes=[pltpu.VMEM((tc * ROW_CHUNKS, LANES), F32),
                            pltpu.VMEM((tc * ROW_CHUNKS, LANES), F32),
                            pltpu.SemaphoreType.DMA(())]),
        out_shape=jax.ShapeDtypeStruct((t, D_MODEL), F32),
        compiler_params=_cparams(("arbitrary",)),
        name="combine",
    )(dest1, dest2, ys, meta, x1, fg)


def _rope_tables(r0):
    inv = ROPE_THETA ** (-jnp.arange(0, 2 * ROPE_HALF, 2, dtype=F32) / (2 * ROPE_HALF))
    ang = jnp.arange(SEQ, dtype=F32)[:, None] * inv[None, :]
    cos, sin = jnp.cos(ang), jnp.sin(ang)
    c = jnp.ones((SEQ, LANES), F32).at[:, r0:r0 + ROPE_HALF].set(cos)
    c = c.at[:, r0 + ROPE_HALF:r0 + 2 * ROPE_HALF].set(cos)
    sa = jnp.zeros((SEQ, LANES), F32).at[:, r0:r0 + ROPE_HALF].set(-sin)
    sb = jnp.zeros((SEQ, LANES), F32).at[:, r0 + ROPE_HALF:r0 + 2 * ROPE_HALF].set(sin)
    return c, sa, sb


def _pad_cols(w, width):
    return jnp.pad(w, ((0, 0), (0, width - w.shape[1])))


def _w_in_layout(w):
    a0 = 0
    b0 = 608
    c0 = b0 + 1292
    d0 = c0 + 512
    segs = [
        _pad_cols(w[:, a0:a0 + 448], 512),
        w[:, b0:b0 + 512],
        w[:, c0:c0 + 512],
        w[:, d0:d0 + 512],
        w[:, a0 + 448:a0 + 576],
        _pad_cols(w[:, a0 + 576:a0 + 608], 128),
        w[:, b0 + 512:b0 + 1280],
        _pad_cols(w[:, b0 + 1280:b0 + 1292], 128),
    ]
    return jnp.concatenate(segs, axis=1).astype(BF16)


def _s5_params(a_re, a_im, log_dt, b_re, b_im, c_re, c_im, batch):
    dt = jnp.exp(log_dt)[:, None]
    mag = jnp.exp(a_re * dt)
    abar_r, abar_i = mag * jnp.cos(a_im * dt), mag * jnp.sin(a_im * dt)
    den = a_re * a_re + a_im * a_im
    nr, ni = abar_r - 1.0, abar_i
    coef_r = (nr * a_re + ni * a_im) / den
    coef_i = (ni * a_re - nr * a_im) / den
    bbar_r = coef_r[..., None] * b_re - coef_i[..., None] * b_im
    bbar_i = coef_r[..., None] * b_im + coef_i[..., None] * b_re
    eye = jnp.eye(S5_GROUPS, dtype=F32)
    blk_b = lambda m: jnp.einsum('gpc,gh->gchp', m, eye).reshape(W_GROUP, S5_WIDTH)
    wb = jnp.concatenate([blk_b(bbar_r), blk_b(bbar_i)], axis=1).astype(BF16)
    blk_c = lambda m: jnp.einsum('gcp,gh->gphc', m, eye).reshape(S5_WIDTH, W_GROUP)
    wc = jnp.concatenate([blk_c(c_re), blk_c(-c_im)], axis=0).astype(BF16)
    ar = jnp.broadcast_to(abar_r.reshape(1, S5_WIDTH), (batch, S5_WIDTH))
    ai = jnp.broadcast_to(abar_i.reshape(1, S5_WIDTH), (batch, S5_WIDTH))
    return wb, ar, ai, wc


def _nsa_consts(ck):
    c_start = np.arange(N_CMP_PAD) * CMP_STRIDE
    b_start = np.arange(N_BLK) * SEL_BLOCK
    ov = ((c_start[None, :] < b_start[:, None] + SEL_BLOCK)
          & (c_start[None, :] + CMP_BLOCK > b_start[:, None])).astype(np.float32)
    key_blk = np.arange(SEQ) // SEL_BLOCK
    e = (key_blk[None, :] == np.arange(N_BLK)[:, None]).astype(np.float32)
    e = e.reshape(N_BLK, SEQ // ck, ck).transpose(1, 0, 2)
    return jnp.asarray(ov), jnp.asarray(e, dtype=BF16)


def _moe_schedule(meta, counts, tmx):
    e1 = meta[:, 0].astype(jnp.int32)
    e2 = meta[:, 1].astype(jnp.int32)
    r1 = meta[:, 4].astype(jnp.int32)
    r2 = meta[:, 5].astype(jnp.int32)
    cnt = counts.astype(jnp.int32)
    ends = jnp.cumsum(cnt)
    offs = ends - cnt
    dest1 = offs[e1] + r1
    dest2 = offs[e2] + r2
    n_tiles = 2 * meta.shape[0] // tmx
    nvis_max = n_tiles + MOE_EXPERTS
    first_tile = offs // tmx
    last_tile = jnp.maximum(ends - 1, 0) // tmx
    nvis = jnp.where(cnt > 0, last_tile - first_tile + 1, 0)
    cumv = jnp.cumsum(nvis)
    total = cumv[-1]
    v = jnp.arange(nvis_max, dtype=jnp.int32)
    vc = jnp.minimum(v, total - 1)
    ve = jnp.searchsorted(cumv, vc, side='right').astype(jnp.int32)
    vt = first_tile[ve] + vc - (cumv[ve] - nvis[ve])
    vlo = jnp.clip(offs[ve] - vt * tmx, 0, tmx)
    vhi = jnp.clip(ends[ve] - vt * tmx, 0, tmx)
    vvalid = (v < total).astype(jnp.int32)
    vfirst = jnp.concatenate([jnp.ones((1,), jnp.int32), (vt[1:] != vt[:-1]).astype(jnp.int32)])
    sched = tuple(a.astype(jnp.int32) for a in (vt, ve, vlo, vhi, vfirst, vvalid))
    return dest1, dest2, sched


def kernel(x, attn_norm_g, w_in, mla_q_norm_g, mla_kv_norm_g, mla_w_uq, mla_w_ukv, nsa_cmp_pe, nsa_cmp_w1, nsa_cmp_w2, pool_w, pool_b, pool_scale, s5_a_re, s5_a_im, s5_log_dt, s5_b_re, s5_b_im, s5_c_re, s5_c_im, s5_d, s5_glu_w, s5_glu_b, mix_norm_g, w_out, ffn_norm_g, moe_w_group, moe_b_group, moe_w_expert, moe_b_expert, moe_w_gate, moe_w_up, moe_w_down, final_norm_g):
    batch, seq, d = x.shape
    depth = w_in.shape[0]
    t = batch * seq
    xf = x.reshape(t, d)
    qtabs = _rope_tables(MLA_NOPE)
    ktabs = _rope_tables(0)
    ovt, emat = _nsa_consts(512)
    row = lambda v: v.reshape(1, -1)
    tmx = 256

    for l in range(depth):
        cols = _in_proj(xf, row(attn_norm_g[l]), _w_in_layout(w_in[l]))

        wuq = jnp.pad(mla_w_uq[l], ((0, 512 - MLA_Q_LORA), (0, 0))).astype(BF16)
        ukv = mla_w_ukv[l].reshape(HEAD_D, MLA_HEADS, MLA_NOPE + MLA_V)
        wk = jnp.pad(ukv[:, :, :MLA_NOPE], ((0, 0), (0, 0), (0, HEAD_D - MLA_NOPE)))
        wk = wk.reshape(HEAD_D, 512).astype(BF16)
        wv = ukv[:, :, MLA_NOPE:].reshape(HEAD_D, 512).astype(BF16)
        qg = jnp.pad(mla_q_norm_g[l], (0, 512 - MLA_Q_LORA)).reshape(1, 512)
        q_a, k_a, v_a = _mla_prep(cols, qg, row(mla_kv_norm_g[l]), wuq, wk, wv, qtabs, ktabs)
        y_a = _mla_attn(q_a, k_a, v_a, batch)

        q_b, kc, vc, kv_b, gates = _nsa_prep(cols, ktabs)
        chunk = CMP_STRIDE * HEAD_D
        pe = jnp.broadcast_to(nsa_cmp_pe[l].reshape(2, 1, CMP_BLOCK * HEAD_D),
                              (2, 8, CMP_BLOCK * HEAD_D)).astype(BF16)
        kcmp, vcmp = _nsa_compress(kc.reshape(t // CMP_STRIDE, chunk), vc.reshape(t // CMP_STRIDE, chunk),
                                   nsa_cmp_w1[l].astype(BF16), nsa_cmp_w2[l].astype(BF16), pe, batch)
        y_b = _nsa_attn(q_b, kcmp, vcmp, kv_b, gates, ovt, emat, batch)

        y_c = _pool(cols, pool_w[l].astype(BF16), row(pool_b[l]), row(pool_scale[l]), batch)

        wb, ar, ai, wc = _s5_params(s5_a_re[l], s5_a_im[l], s5_log_dt[l], s5_b_re[l], s5_b_im[l],
                                    s5_c_re[l], s5_c_im[l], batch)
        u_t = cols[:, COL_S5:COL_S5 + W_GROUP].reshape(batch, seq, W_GROUP)
        u_t = u_t.transpose(1, 0, 2).reshape(t, W_GROUP)
        y_d = _s5(u_t, wb, ar, ai, wc, row(s5_d[l]), s5_glu_w[l].astype(BF16), row(s5_glu_b[l]), batch)
        y_d = y_d.reshape(seq, batch, W_GROUP).transpose(1, 0, 2).reshape(t, W_GROUP)

        wr = jnp.concatenate([moe_w_group[l], moe_w_expert[l]], axis=1)
        wr = _pad_cols(wr, LANES)
        br = jnp.pad(jnp.concatenate([moe_b_group[l], moe_b_expert[l]]), (0, LANES - 36)).reshape(1, LANES)
        x1, h2, logits = _out_proj(y_a, y_b, y_c, y_d, row(mix_norm_g[l]), w_out[l].astype(BF16),
                                   xf, row(ffn_norm_g[l]), wr, br)

        meta, counts = _router(logits)
        dest1, dest2, sched = _moe_schedule(meta, counts[-1, MOE_GROUPS:MOE_GROUPS + MOE_EXPERTS], tmx)
        xs = _dispatch(dest1, dest2, h2)
        ys = _experts(sched, xs, moe_w_gate[l], moe_w_up[l], moe_w_down[l], tmx)
        xf = _combine(dest1, dest2, ys, meta, x1, row(final_norm_g), final=(l == depth - 1))

    return xf.reshape(batch, seq, d)
```

```python
import functools
import math

import numpy as np
import jax
import jax.numpy as jnp
from jax import lax
from jax.experimental import pallas as pl
from jax.experimental.pallas import tpu as pltpu

F32 = jnp.float32
BF16 = jnp.bfloat16

D_MODEL = 2048
SEQ = 2048
W_GROUP = 512
LANES = 128
ROW_CHUNKS = D_MODEL // LANES

ROPE_THETA = 500000.0
ROPE_HALF = 16
NEG_INF = -1.0e30
FORCE_SCORE = 1.0e4
EPS = 1e-6

MLA_HEADS = 4
MLA_Q_LORA = 448
MLA_NOPE = 96
MLA_V = 128
HEAD_D = 128

CMP_BLOCK = 32
CMP_STRIDE = 16
SEL_BLOCK = 64
N_SEL = 8
N_LOCAL = 2
WINDOW = 512
N_CMP_PAD = SEQ // CMP_STRIDE
N_BLK = SEQ // SEL_BLOCK

POOL_SIZES = (2, 4, 8, 16)
S5_GROUPS = 32
S5_CH = 16
S5_STATE = 64
S5_WIDTH = S5_GROUPS * S5_STATE

MOE_GROUPS = 4
MOE_EPG = 8
MOE_EXPERTS = 32
MOE_HIDDEN = 512

COL_CQ, COL_NQ, COL_POOL, COL_S5 = 0, 512, 1024, 1536
COL_CKV, COL_KR, COL_KV6, COL_GL = 2048, 2176, 2304, 3072
N_IN_PAD = 3200

VMEM_LIMIT = 56 * 1024 * 1024


def _cparams(sem, vmem=VMEM_LIMIT):
    return pltpu.CompilerParams(dimension_semantics=sem, vmem_limit_bytes=vmem)


def _rms(x, n=None):
    n = x.shape[-1] if n is None else n
    return x * lax.rsqrt(jnp.sum(x * x, axis=-1, keepdims=True) / n + EPS)


def _dot(a, b):
    return jnp.dot(a, b, preferred_element_type=F32)


def _dot_nt(a, b, precision=None):
    return lax.dot_general(a, b, (((1,), (1,)), ((), ())), preferred_element_type=F32,
                           precision=precision)


def _rope(x, c, sa, sb):
    return x * c + pltpu.roll(x, LANES - ROPE_HALF, 1) * sa + pltpu.roll(x, ROPE_HALF, 1) * sb


def _in_proj_kernel(x_ref, g_ref, w_ref, o_ref, h_ref):
    @pl.when(pl.program_id(1) == 0)
    def _():
        h_ref[...] = (_rms(x_ref[...]) * g_ref[...]).astype(BF16)

    o_ref[...] = _dot(h_ref[...], w_ref[...])


def _in_proj(x, g, w):
    t = x.shape[0]
    tm, tn = 1024, 640
    return pl.pallas_call(
        _in_proj_kernel,
        grid=(t // tm, N_IN_PAD // tn),
        in_specs=[pl.BlockSpec((tm, D_MODEL), lambda i, j: (i, 0)),
                  pl.BlockSpec((1, D_MODEL), lambda i, j: (0, 0)),
                  pl.BlockSpec((D_MODEL, tn), lambda i, j: (0, j))],
        out_specs=pl.BlockSpec((tm, tn), lambda i, j: (i, j)),
        out_shape=jax.ShapeDtypeStruct((t, N_IN_PAD), F32),
        scratch_shapes=[pltpu.VMEM((tm, D_MODEL), BF16)],
        compiler_params=_cparams(("parallel", "arbitrary")),
        name="in_proj",
    )(x, g, w)


def _mla_prep_kernel(cq_ref, ckv_ref, kr_ref, qg_ref, kvg_ref, wuq_ref, wk_ref, wv_ref,
                     qc_ref, qsa_ref, qsb_ref, kc_ref, ksa_ref, ksb_ref,
                     q_out, k_out, v_out):
    scale = 1.0 / math.sqrt(HEAD_D)
    qn = (_rms(cq_ref[...], MLA_Q_LORA) * qg_ref[...]).astype(BF16)
    q = _dot(qn, wuq_ref[...])
    kvn = (_rms(ckv_ref[...]) * kvg_ref[...]).astype(BF16)
    kn = _dot(kvn, wk_ref[...])
    v_out[...] = _dot(kvn, wv_ref[...]).astype(BF16)
    kr = _rope(kr_ref[...], kc_ref[...], ksa_ref[...], ksb_ref[...])
    kr = pltpu.roll(kr, MLA_NOPE, 1)
    for h in range(MLA_HEADS):
        sl = slice(h * HEAD_D, (h + 1) * HEAD_D)
        qh = _rope(q[:, sl], qc_ref[...], qsa_ref[...], qsb_ref[...])
        q_out[:, sl] = (qh * scale).astype(BF16)
        k_out[:, sl] = (kn[:, sl] + kr).astype(BF16)


def _mla_prep(cols, qg, kvg, wuq, wk, wv, qtabs, ktabs):
    t = cols.shape[0]
    tm = 512
    nsb = SEQ // tm
    tab = pl.BlockSpec((tm, LANES), lambda i: (i % nsb, 0))
    full = lambda shape: pl.BlockSpec(shape, lambda i: (0, 0))
    out = pl.BlockSpec((tm, W_GROUP), lambda i: (i, 0))
    return pl.pallas_call(
        _mla_prep_kernel,
        grid=(t // tm,),
        in_specs=[pl.BlockSpec((tm, 512), lambda i: (i, COL_CQ // 512)),
                  pl.BlockSpec((tm, LANES), lambda i: (i, COL_CKV // LANES)),
                  pl.BlockSpec((tm, LANES), lambda i: (i, COL_KR // LANES)),
                  full((1, 512)), full((1, LANES)),
                  full((512, 512)), full((LANES, 512)), full((LANES, 512)),
                  tab, tab, tab, tab, tab, tab],
        out_specs=[out, out, out],
        out_shape=[jax.ShapeDtypeStruct((t, W_GROUP), BF16)] * 3,
        compiler_params=_cparams(("parallel",)),
        name="mla_prep",
    )(cols, cols, cols, qg, kvg, wuq, wk, wv, *qtabs, *ktabs)


def _mla_attn_kernel(q_ref, k_ref, v_ref, o_ref, *, tq):
    i = pl.program_id(1)
    row = lax.broadcasted_iota(jnp.int32, (tq, tq), 0)
    col = lax.broadcasted_iota(jnp.int32, (tq, tq), 1)
    for h in range(MLA_HEADS):
        sl = slice(h * HEAD_D, (h + 1) * HEAD_D)
        q = q_ref[:, sl]

        def body(j, carry, sl=sl, q=q):
            m, l, acc = carry
            r0 = pl.multiple_of(j * tq, tq)
            s = _dot_nt(q, k_ref[pl.ds(r0, tq), sl])
            s = jnp.where(col + j * tq <= row + i * tq, s, NEG_INF)
            m_new = jnp.maximum(m, jnp.max(s, axis=-1, keepdims=True))
            p = jnp.exp(s - m_new)
            alpha = jnp.exp(m - m_new)
            l = alpha * l + jnp.sum(p, axis=-1, keepdims=True)
            acc = alpha * acc + _dot(p.astype(BF16), v_ref[pl.ds(r0, tq), sl])
            return m_new, l, acc

        init = (jnp.full((tq, 1), NEG_INF, F32), jnp.zeros((tq, 1), F32),
                jnp.zeros((tq, HEAD_D), F32))
        _, l, acc = lax.fori_loop(0, i + 1, body, init)
        o_ref[:, sl] = acc / l


def _mla_attn(q, k, v, batch):
    tq = 256
    nq = SEQ // tq
    return pl.pallas_call(
        functools.partial(_mla_attn_kernel, tq=tq),
        grid=(batch, nq),
        in_specs=[pl.BlockSpec((tq, W_GROUP), lambda b, i: (b * nq + i, 0)),
                  pl.BlockSpec((SEQ, W_GROUP), lambda b, i: (b, 0)),
                  pl.BlockSpec((SEQ, W_GROUP), lambda b, i: (b, 0))],
        out_specs=pl.BlockSpec((tq, W_GROUP), lambda b, i: (b * nq + i, 0)),
        out_shape=jax.ShapeDtypeStruct((batch * SEQ, W_GROUP), F32),
        compiler_params=_cparams(("parallel", "arbitrary")),
        name="mla_attn",
    )(q, k, v)


def _nsa_prep_kernel(q_ref, kv_ref, gl_ref, c_ref, sa_ref, sb_ref,
                     q_out, kc_out, vc_out, kv_out, g_out):
    scale = 1.0 / math.sqrt(HEAD_D)
    c, sa, sb = c_ref[...], sa_ref[...], sb_ref[...]
    for h in range(4):
        sl = slice(h * HEAD_D, (h + 1) * HEAD_D)
        q_out[:, sl] = (_rope(q_ref[:, sl], c, sa, sb) * scale).astype(BF16)
    kv = lambda n: kv_ref[:, n * HEAD_D:(n + 1) * HEAD_D]
    kc_out[...] = _rope(kv(0), c, sa, sb)
    vc_out[...] = kv(1)
    kv_out[:, 0:128] = _rope(kv(2), c, sa, sb).astype(BF16)
    kv_out[:, 128:256] = kv(3).astype(BF16)
    kv_out[:, 256:384] = _rope(kv(4), c, sa, sb).astype(BF16)
    kv_out[:, 384:512] = kv(5).astype(BF16)
    g_out[...] = jax.nn.sigmoid(gl_ref[...])


def _nsa_prep(cols, tabs):
    t = cols.shape[0]
    tm = 512
    nsb = SEQ // tm
    tab = pl.BlockSpec((tm, LANES), lambda i: (i % nsb, 0))
    return pl.pallas_call(
        _nsa_prep_kernel,
        grid=(t // tm,),
        in_specs=[pl.BlockSpec((tm, 512), lambda i: (i, COL_NQ // 512)),
                  pl.BlockSpec((tm, 768), lambda i: (i, COL_KV6 // 768)),
                  pl.BlockSpec((tm, LANES), lambda i: (i, COL_GL // LANES)),
                  tab, tab, tab],
        out_specs=[pl.BlockSpec((tm, 512), lambda i: (i, 0)),
                   pl.BlockSpec((tm, LANES), lambda i: (i, 0)),
                   pl.BlockSpec((tm, LANES), lambda i: (i, 0)),
                   pl.BlockSpec((tm, 512), lambda i: (i, 0)),
                   pl.BlockSpec((tm, LANES), lambda i: (i, 0))],
        out_shape=[jax.ShapeDtypeStruct((t, 512), BF16),
                   jax.ShapeDtypeStruct((t, LANES), F32),
                   jax.ShapeDtypeStruct((t, LANES), F32),
                   jax.ShapeDtypeStruct((t, 512), BF16),
                   jax.ShapeDtypeStruct((t, LANES), F32)],
        compiler_params=_cparams(("parallel",)),
        name="nsa_prep",
    )(cols, cols, cols, *tabs)


def _nsa_compress_kernel(xk_ref, xv_ref, w1_ref, w2_ref, pe_ref, k_out, v_out):
    for c, (x_ref, o_ref) in enumerate(((xk_ref, k_out), (xv_ref, v_out))):
        a = jnp.zeros((N_CMP_PAD, HEAD_D), F32)
        b = jnp.zeros((N_CMP_PAD, HEAD_D), F32)
        for r in range(CMP_STRIDE):
            x = x_ref[pl.ds(r, N_CMP_PAD, stride=CMP_STRIDE), :].astype(BF16)
            a = a + _dot(x, w1_ref[c, r * HEAD_D:(r + 1) * HEAD_D, :])
            b = b + _dot(x, w1_ref[c, (CMP_STRIDE + r) * HEAD_D:(CMP_STRIDE + r + 1) * HEAD_D, :])
        b = pltpu.roll(b, N_CMP_PAD - 1, 0)
        pe = _dot(pe_ref[c], w1_ref[c])[0:1, :]
        hid = jax.nn.gelu(a + b + pe)
        o_ref[0] = _dot(hid.astype(BF16), w2_ref[c]).astype(BF16)


def _nsa_compress(xk, xv, w1, w2, pe, batch):
    xspec = pl.BlockSpec((SEQ, HEAD_D), lambda b: (b, 0))
    ospec = pl.BlockSpec((1, N_CMP_PAD, HEAD_D), lambda b: (b, 0, 0))
    return pl.pallas_call(
        _nsa_compress_kernel,
        grid=(batch,),
        in_specs=[xspec, xspec,
                  pl.BlockSpec((2, CMP_BLOCK * HEAD_D, HEAD_D), lambda b: (0, 0, 0)),
                  pl.BlockSpec((2, HEAD_D, HEAD_D), lambda b: (0, 0, 0)),
                  pl.BlockSpec((2, 8, CMP_BLOCK * HEAD_D), lambda b: (0, 0, 0))],
        out_specs=[ospec, ospec],
        out_shape=[jax.ShapeDtypeStruct((batch, N_CMP_PAD, HEAD_D), BF16)] * 2,
        compiler_params=_cparams(("parallel",)),
        name="nsa_compress",
    )(xk, xv, w1, w2, pe)


def _softmax_rows(s):
    m = jnp.max(s, axis=-1, keepdims=True)
    p = jnp.exp(s - m)
    return p / jnp.sum(p, axis=-1, keepdims=True)


def _nsa_attn_kernel(q_ref, kc_ref, vc_ref, kv_ref, g_ref, ovt_ref, e_ref, o_ref,
                     m_ref, l_ref, acc_ref, *, tq, ck):
    i = pl.program_id(1)
    q0 = i * tq
    nh = 4
    qs = jnp.concatenate([q_ref[:, h * HEAD_D:(h + 1) * HEAD_D] for h in range(nh)], axis=0)
    qpos = q0 + lax.broadcasted_iota(jnp.int32, (tq, 1), 0)
    stack = lambda a: jnp.concatenate([a] * nh, axis=0)

    sc = _dot_nt(qs, kc_ref[0])
    n_idx = lax.broadcasted_iota(jnp.int32, (tq, N_CMP_PAD), 1)
    valid_c = n_idx * CMP_STRIDE + (CMP_BLOCK - 1) <= qpos
    valid_c4 = stack(valid_c)
    pc = _softmax_rows(jnp.where(valid_c4, sc, NEG_INF))
    pc = jnp.where(valid_c4, pc, 0.0)
    o_c = _dot(pc.astype(BF16), vc_ref[0])
    psum = pc[0:tq] + pc[tq:2 * tq] + pc[2 * tq:3 * tq] + pc[3 * tq:4 * tq]

    imp_t = _dot_nt(ovt_ref[...], psum, precision=lax.Precision.HIGHEST)
    kblk = lax.broadcasted_iota(jnp.int32, (N_BLK, tq), 0)
    cur = (q0 + lax.broadcasted_iota(jnp.int32, (N_BLK, tq), 1)) // SEL_BLOCK
    forced = (kblk == 0) | ((kblk <= cur) & (kblk > cur - N_LOCAL))
    score = jnp.where(forced, FORCE_SCORE, jnp.where(kblk <= cur, imp_t, -1.0))
    cnt = jnp.zeros((N_BLK, tq), F32)
    for j in range(N_BLK):
        sj = score[j:j + 1, :]
        beats = (sj > score) | ((sj == score) & (kblk > j))
        cnt = cnt + beats.astype(F32)
    sel = jnp.transpose((cnt < N_SEL).astype(F32)).astype(BF16)

    m_ref[...] = jnp.full(m_ref.shape, NEG_INF, F32)
    l_ref[...] = jnp.zeros(l_ref.shape, F32)
    acc_ref[...] = jnp.zeros(acc_ref.shape, F32)

    def body(c, carry):
        r0 = pl.multiple_of(c * ck, ck)
        s = _dot_nt(qs, kv_ref[pl.ds(r0, ck), 0:128])
        chosen = _dot(sel, e_ref[c])
        kpos = r0 + lax.broadcasted_iota(jnp.int32, (tq, ck), 1)
        ok = (chosen > 0.5) & (kpos <= qpos)
        s = s + stack(jnp.where(ok, 0.0, NEG_INF))
        m_old = m_ref[...]
        m_new = jnp.maximum(m_old, jnp.max(s, axis=-1, keepdims=True))
        p = jnp.exp(s - m_new)
        alpha = jnp.exp(m_old - m_new)
        l_ref[...] = alpha * l_ref[...] + jnp.sum(p, axis=-1, keepdims=True)
        acc_ref[...] = alpha * acc_ref[...] + _dot(p.astype(BF16), kv_ref[pl.ds(r0, ck), 128:256])
        m_ref[...] = m_new
        return carry

    lax.fori_loop(0, q0 // ck + 1, body, 0)
    o_s = acc_ref[...] / l_ref[...]

    wlen = WINDOW + tq
    w0 = pl.multiple_of(jnp.maximum(q0 - WINDOW, 0), tq)
    sw = _dot_nt(qs, kv_ref[pl.ds(w0, wlen), 256:384])
    kpos = w0 + lax.broadcasted_iota(jnp.int32, (tq, wlen), 1)
    ok = (kpos <= qpos) & (kpos > qpos - WINDOW)
    pw = _softmax_rows(sw + stack(jnp.where(ok, 0.0, NEG_INF)))
    o_w = _dot(pw.astype(BF16), kv_ref[pl.ds(w0, wlen), 384:512])

    g = g_ref[...]
    for h in range(nh):
        rs = slice(h * tq, (h + 1) * tq)
        o_ref[:, h * HEAD_D:(h + 1) * HEAD_D] = (
            g[:, 3 * h:3 * h + 1] * o_c[rs] + g[:, 3 * h + 1:3 * h + 2] * o_s[rs]
            + g[:, 3 * h + 2:3 * h + 3] * o_w[rs])


def _nsa_attn(q, kcmp, vcmp, kv, gates, ovt, emat, batch):
    tq, ck = 128, 512
    nq = SEQ // tq
    return pl.pallas_call(
        functools.partial(_nsa_attn_kernel, tq=tq, ck=ck),
        grid=(batch, nq),
        in_specs=[pl.BlockSpec((tq, 512), lambda b, i: (b * nq + i, 0)),
                  pl.BlockSpec((1, N_CMP_PAD, HEAD_D), lambda b, i: (b, 0, 0)),
                  pl.BlockSpec((1, N_CMP_PAD, HEAD_D), lambda b, i: (b, 0, 0)),
                  pl.BlockSpec((SEQ, 512), lambda b, i: (b, 0)),
                  pl.BlockSpec((tq, LANES), lambda b, i: (b * nq + i, 0)),
                  pl.BlockSpec((N_BLK, N_CMP_PAD), lambda b, i: (0, 0)),
                  pl.BlockSpec((SEQ // ck, N_BLK, ck), lambda b, i: (0, 0, 0))],
        out_specs=pl.BlockSpec((tq, 512), lambda b, i: (b * nq + i, 0)),
        out_shape=jax.ShapeDtypeStruct((batch * SEQ, 512), F32),
        scratch_shapes=[pltpu.VMEM((4 * tq, 1), F32), pltpu.VMEM((4 * tq, 1), F32),
                        pltpu.VMEM((4 * tq, HEAD_D), F32)],
        compiler_params=_cparams(("parallel", "arbitrary")),
        name="nsa_attn",
    )(q, kcmp, vcmp, kv, gates, ovt, emat)


def _pool_kernel(u_ref, w_ref, b_ref, s_ref, o_ref, pad_ref):
    maxw = POOL_SIZES[-1]
    pad_ref[0:maxw, :] = jnp.zeros((maxw, W_GROUP), F32)
    pad_ref[maxw:maxw + SEQ, :] = u_ref[...]
    rc = 512
    for g, w in enumerate(POOL_SIZES):
        sl = slice(g * LANES, (g + 1) * LANES)
        for r in range(SEQ // rc):
            acc = pad_ref[maxw + r * rc:maxw + (r + 1) * rc, sl]
            tok = acc
            for j in range(1, w):
                acc = acc + pad_ref[maxw - j + r * rc:maxw - j + (r + 1) * rc, sl]
            t = r * rc + lax.broadcasted_iota(jnp.int32, (rc, 1), 0)
            cnt = jnp.minimum(t + 1, w).astype(F32)
            d = acc / cnt - tok
            y = _dot(d.astype(BF16), w_ref[g])
            o_ref[r * rc:(r + 1) * rc, sl] = (y + b_ref[:, sl]) * s_ref[:, sl]


def _pool(cols, w, b, s, batch):
    return pl.pallas_call(
        _pool_kernel,
        grid=(batch,),
        in_specs=[pl.BlockSpec((SEQ, W_GROUP), lambda i: (i, COL_POOL // W_GROUP)),
                  pl.BlockSpec((4, LANES, LANES), lambda i: (0, 0, 0)),
                  pl.BlockSpec((1, W_GROUP), lambda i: (0, 0)),
                  pl.BlockSpec((1, W_GROUP), lambda i: (0, 0))],
        out_specs=pl.BlockSpec((SEQ, W_GROUP), lambda i: (i, 0)),
        out_shape=jax.ShapeDtypeStruct((batch * SEQ, W_GROUP), F32),
        scratch_shapes=[pltpu.VMEM((SEQ + POOL_SIZES[-1], W_GROUP), F32)],
        compiler_params=_cparams(("parallel",)),
        name="pool",
    )(cols, w, b, s)


def _s5_kernel(u_ref, wb_ref, ar_ref, ai_ref, wc_ref, d_ref, gw_ref, gb_ref, o_ref,
               bu_ref, st_ref, tm_ref, *, batch, tc):
    @pl.when(pl.program_id(0) == 0)
    def _():
        st_ref[...] = jnp.zeros(st_ref.shape, F32)

    nslab = W_GROUP // LANES
    for b in range(batch):
        for j in range(nslab):
            tm_ref[j, pl.ds(b, tc, stride=batch), :] = u_ref[b, :, j * LANES:(j + 1) * LANES]
    u = jnp.concatenate([tm_ref[j] for j in range(nslab)], axis=1)
    bu_ref[...] = _dot(u.astype(BF16), wb_ref[...])
    lc = 512
    unroll = 8
    for c in range(S5_WIDTH // lc):
        re = slice(c * lc, (c + 1) * lc)
        im = slice(S5_WIDTH + c * lc, S5_WIDTH + (c + 1) * lc)
        ar, ai = ar_ref[:, re], ai_ref[:, re]

        def body(tb, carry, re=re, im=im, ar=ar, ai=ai):
            xr, xi = carry
            for k in range(unroll):
                r0 = pl.multiple_of((tb * unroll + k) * batch, batch)
                nxr = ar * xr - ai * xi + bu_ref[pl.ds(r0, batch), re]
                nxi = ar * xi + ai * xr + bu_ref[pl.ds(r0, batch), im]
                bu_ref[pl.ds(r0, batch), re] = nxr
                bu_ref[pl.ds(r0, batch), im] = nxi
                xr, xi = nxr, nxi
            return xr, xi

        xr, xi = lax.fori_loop(0, tc // unroll, body, (st_ref[:, re], st_ref[:, im]))
        st_ref[:, re] = xr
        st_ref[:, im] = xi

    y = _dot(bu_ref[...].astype(BF16), wc_ref[...]) + d_ref[...] * u
    y = jax.nn.gelu(y)
    z = _dot(y.astype(BF16), gw_ref[...]) + gb_ref[...]
    o = y * jax.nn.sigmoid(z)
    for j in range(nslab):
        tm_ref[j] = o[:, j * LANES:(j + 1) * LANES]
    for b in range(batch):
        for j in range(nslab):
            o_ref[b, :, j * LANES:(j + 1) * LANES] = tm_ref[j, pl.ds(b, tc, stride=batch), :]


def _s5(cols3, wb, ar, ai, wc, d, gw, gb):
    batch = cols3.shape[0]
    tc = 64
    rows = tc * batch
    full = lambda shape: pl.BlockSpec(shape, lambda i: (0, 0))
    return pl.pallas_call(
        functools.partial(_s5_kernel, batch=batch, tc=tc),
        grid=(SEQ // tc,),
        in_specs=[pl.BlockSpec((batch, tc, W_GROUP), lambda i: (0, i, COL_S5 // W_GROUP)),
                  full((W_GROUP, 2 * S5_WIDTH)), full((batch, S5_WIDTH)), full((batch, S5_WIDTH)),
                  full((2 * S5_WIDTH, W_GROUP)), full((1, W_GROUP)),
                  full((W_GROUP, W_GROUP)), full((1, W_GROUP))],
        out_specs=pl.BlockSpec((batch, tc, W_GROUP), lambda i: (0, i, 0)),
        out_shape=jax.ShapeDtypeStruct((batch, SEQ, W_GROUP), F32),
        scratch_shapes=[pltpu.VMEM((rows, 2 * S5_WIDTH), F32),
                        pltpu.VMEM((batch, 2 * S5_WIDTH), F32),
                        pltpu.VMEM((W_GROUP // LANES, rows, LANES), F32)],
        compiler_params=_cparams(("arbitrary",)),
        name="s5",
    )(cols3, wb, ar, ai, wc, d, gw, gb)


def _out_proj_kernel(ya_ref, yb_ref, yc_ref, yd_ref, mg_ref, wo_ref, x_ref, fg_ref,
                     wr_ref, br_ref, x1_ref, h2_ref, lg_ref, *, tm):
    acc = x_ref[...]
    for gi, y_ref in enumerate((ya_ref, yb_ref, yc_ref, yd_ref)):
        sl = slice(gi * W_GROUP, (gi + 1) * W_GROUP)
        n = (_rms(y_ref[...]) * mg_ref[:, sl]).astype(BF16)
        acc = acc + _dot(n, wo_ref[sl, :])
    x1_ref[...] = acc
    h2 = _rms(acc) * fg_ref[...]
    lg_ref[...] = jnp.dot(h2, wr_ref[...], preferred_element_type=F32,
                          precision=lax.Precision.HIGHEST) + br_ref[...]
    for s in range(ROW_CHUNKS):
        h2_ref[pl.ds(s, tm, stride=ROW_CHUNKS), :] = h2[:, s * LANES:(s + 1) * LANES]


def _out_proj(ya, yb, yc, yd, mg, wo, x, fg, wr, br):
    t = x.shape[0]
    tm = 256
    yspec = pl.BlockSpec((tm, W_GROUP), lambda i: (i, 0))
    full = lambda shape: pl.BlockSpec(shape, lambda i: (0, 0))
    return pl.pallas_call(
        functools.partial(_out_proj_kernel, tm=tm),
        grid=(t // tm,),
        in_specs=[yspec, yspec, yspec, yspec, full((1, D_MODEL)), full((D_MODEL, D_MODEL)),
                  pl.BlockSpec((tm, D_MODEL), lambda i: (i, 0)), full((1, D_MODEL)),
                  full((D_MODEL, LANES)), full((1, LANES))],
        out_specs=[pl.BlockSpec((tm, D_MODEL), lambda i: (i, 0)),
                   pl.BlockSpec((tm * ROW_CHUNKS, LANES), lambda i: (i, 0)),
                   pl.BlockSpec((tm, LANES), lambda i: (i, 0))],
        out_shape=[jax.ShapeDtypeStruct((t, D_MODEL), F32),
                   jax.ShapeDtypeStruct((t * ROW_CHUNKS, LANES), F32),
                   jax.ShapeDtypeStruct((t, LANES), F32)],
        compiler_params=_cparams(("parallel",)),
        name="out_proj",
    )(ya, yb, yc, yd, mg, wo, x, fg, wr, br)


def _router_kernel(lg_ref, meta_ref, cnt_ref, carry_ref, *, tm):
    @pl.when(pl.program_id(0) == 0)
    def _():
        carry_ref[...] = jnp.zeros(carry_ref.shape, F32)

    lg = lg_ref[...]
    lane = lax.broadcasted_iota(jnp.int32, (tm, LANES), 1)
    big = jnp.int32(1 << 20)
    rmax = lambda a: jnp.max(a, axis=-1, keepdims=True)
    rmin = lambda a: jnp.min(a, axis=-1, keepdims=True)
    rsum = lambda a: jnp.sum(a, axis=-1, keepdims=True)

    is_g = lane < MOE_GROUPS
    gl = jnp.where(is_g, lg, NEG_INF)
    gm = rmax(gl)
    p_top = 1.0 / rsum(jnp.where(is_g, jnp.exp(gl - gm), 0.0))
    g_top = rmin(jnp.where(is_g & (gl == gm), lane, big))

    is_e = (lane >= MOE_GROUPS) & (lane < MOE_GROUPS + MOE_EXPERTS) \
        & (((lane - MOE_GROUPS) // MOE_EPG) == g_top)
    el = jnp.where(is_e, lg, NEG_INF)
    ee = jnp.where(is_e, jnp.exp(el - rmax(el)), 0.0)
    p = jnp.where(is_e, ee / rsum(ee), -1.0)
    p1 = rmax(p)
    i1 = rmin(jnp.where(p == p1, lane, big))
    p_rest = jnp.where(lane == i1, -1.0, p)
    p2 = rmax(p_rest)
    i2 = rmin(jnp.where((p_rest == p2) & is_e & (lane != i1), lane, big))
    den = p1 + p2
    w1 = p_top * (p1 / den)
    w2 = p_top * (p2 / den)

    hit1, hit2 = lane == i1, lane == i2
    oh = (hit1 | hit2).astype(F32)
    r = lax.broadcasted_iota(jnp.int32, (tm, tm), 0)
    c = lax.broadcasted_iota(jnp.int32, (tm, tm), 1)
    before = _dot((c < r).astype(BF16), oh.astype(BF16)) + carry_ref[0:1, :]
    r1 = rsum(jnp.where(hit1, before, 0.0))
    r2 = rsum(jnp.where(hit2, before, 0.0))
    carry_ref[...] = carry_ref[...] + jnp.sum(oh, axis=0, keepdims=True)
    cnt_ref[...] = carry_ref[...]

    e1 = (i1 - MOE_GROUPS).astype(F32)
    e2 = (i2 - MOE_GROUPS).astype(F32)
    vals = (e1, e2, w1, w2, r1, r2)
    meta = jnp.zeros((tm, LANES), F32)
    for k, v in enumerate(vals):
        meta = jnp.where(lane == k, v, meta)
    meta_ref[...] = meta


def _router(logits):
    t = logits.shape[0]
    tm = 512
    return pl.pallas_call(
        functools.partial(_router_kernel, tm=tm),
        grid=(t // tm,),
        in_specs=[pl.BlockSpec((tm, LANES), lambda i: (i, 0))],
        out_specs=[pl.BlockSpec((tm, LANES), lambda i: (i, 0)),
                   pl.BlockSpec((8, LANES), lambda i: (i, 0))],
        out_shape=[jax.ShapeDtypeStruct((t, LANES), F32),
                   jax.ShapeDtypeStruct((t // tm * 8, LANES), F32)],
        scratch_shapes=[pltpu.VMEM((8, LANES), F32)],
        compiler_params=_cparams(("arbitrary",)),
        name="router",
    )(logits)


def _dispatch_kernel(d1_ref, d2_ref, h_ref, xs_ref, sem, *, td):
    base = pl.program_id(0) * td

    def start(r, carry):
        src = h_ref.at[pl.ds(r * ROW_CHUNKS, ROW_CHUNKS)]
        for d_ref in (d1_ref, d2_ref):
            dst = xs_ref.at[pl.ds(d_ref[base + r] * ROW_CHUNKS, ROW_CHUNKS)]
            pltpu.make_async_copy(src, dst, sem).start()
        return carry

    lax.fori_loop(0, td, start, 0)
    for _ in range(2):
        pltpu.make_async_copy(h_ref, xs_ref.at[pl.ds(0, td * ROW_CHUNKS)], sem).wait()


def _dispatch(dest1, dest2, h2):
    t = dest1.shape[0]
    td = 512
    return pl.pallas_call(
        functools.partial(_dispatch_kernel, td=td),
        grid_spec=pltpu.PrefetchScalarGridSpec(
            num_scalar_prefetch=2,
            grid=(t // td,),
            in_specs=[pl.BlockSpec((td * ROW_CHUNKS, LANES), lambda i, *_: (i, 0))],
            out_specs=pl.BlockSpec(memory_space=pl.ANY),
            scratch_shapes=[pltpu.SemaphoreType.DMA(())]),
        out_shape=jax.ShapeDtypeStruct((2 * t * ROW_CHUNKS, LANES), F32),
        compiler_params=pltpu.CompilerParams(dimension_semantics=("arbitrary",),
                                             has_side_effects=True),
        name="dispatch",
    )(dest1, dest2, h2)


def _expert_kernel(vt_ref, ve_ref, vlo_ref, vhi_ref, vfirst_ref, vvalid_ref,
                   xs_ref, wg_ref, wu_ref, wd_ref, ys_ref, wgb_ref, wub_ref, wdb_ref, *, tmx):
    v = pl.program_id(0)
    prev = jnp.maximum(v - 1, 0)

    @pl.when((v == 0) | (ve_ref[v] != ve_ref[prev]))
    def _():
        wgb_ref[...] = wg_ref[0].astype(BF16)
        wub_ref[...] = wu_ref[0].astype(BF16)
        wdb_ref[...] = wd_ref[0].astype(BF16)

    @pl.when(vvalid_ref[v] == 1)
    def _():
        x = jnp.concatenate([xs_ref[pl.ds(s, tmx, stride=ROW_CHUNKS), :] for s in range(ROW_CHUNKS)],
                            axis=1).astype(BF16)
        a = _dot(x, wgb_ref[...])
        u = _dot(x, wub_ref[...])
        hid = (jax.nn.silu(a) * u).astype(BF16)
        y = _dot(hid, wdb_ref[...])
        rows = lax.broadcasted_iota(jnp.int32, (tmx, 1), 0)
        mine = (rows >= vlo_ref[v]) & (rows < vhi_ref[v])

        @pl.when(vfirst_ref[v] == 1)
        def _():
            for s in range(ROW_CHUNKS):
                ys_ref[pl.ds(s, tmx, stride=ROW_CHUNKS), :] = jnp.where(
                    mine, y[:, s * LANES:(s + 1) * LANES], 0.0)

        @pl.when(vfirst_ref[v] == 0)
        def _():
            for s in range(ROW_CHUNKS):
                old = ys_ref[pl.ds(s, tmx, stride=ROW_CHUNKS), :]
                ys_ref[pl.ds(s, tmx, stride=ROW_CHUNKS), :] = jnp.where(
                    mine, y[:, s * LANES:(s + 1) * LANES], old)


def _experts(sched, xs, wg, wu, wd, tmx):
    nvis = sched[0].shape[0]
    rows = xs.shape[0]
    xspec = pl.BlockSpec((tmx * ROW_CHUNKS, LANES), lambda v, vt, *_: (vt[v], 0))
    return pl.pallas_call(
        functools.partial(_expert_kernel, tmx=tmx),
        grid_spec=pltpu.PrefetchScalarGridSpec(
            num_scalar_prefetch=6,
            grid=(nvis,),
            in_specs=[xspec,
                      pl.BlockSpec((1, D_MODEL, MOE_HIDDEN), lambda v, vt, ve, *_: (ve[v], 0, 0)),
                      pl.BlockSpec((1, D_MODEL, MOE_HIDDEN), lambda v, vt, ve, *_: (ve[v], 0, 0)),
                      pl.BlockSpec((1, MOE_HIDDEN, D_MODEL), lambda v, vt, ve, *_: (ve[v], 0, 0))],
            out_specs=xspec,
            scratch_shapes=[pltpu.VMEM((D_MODEL, MOE_HIDDEN), BF16),
                            pltpu.VMEM((D_MODEL, MOE_HIDDEN), BF16),
                            pltpu.VMEM((MOE_HIDDEN, D_MODEL), BF16)]),
        out_shape=jax.ShapeDtypeStruct((rows, LANES), F32),
        compiler_params=_cparams(("arbitrary",)),
        name="experts",
    )(*sched, xs, wg, wu, wd)


def _combine_kernel(d1_ref, d2_ref, ys_ref, meta_ref, x1_ref, fg_ref, o_ref,
                    b1_ref, b2_ref, sem, *, tc, final):
    base = pl.program_id(0) * tc

    def start(r, carry):
        dst = pl.ds(r * ROW_CHUNKS, ROW_CHUNKS)
        for d_ref, b_ref in ((d1_ref, b1_ref), (d2_ref, b2_ref)):
            src = ys_ref.at[pl.ds(d_ref[base + r] * ROW_CHUNKS, ROW_CHUNKS)]
            pltpu.make_async_copy(src, b_ref.at[dst], sem).start()
        return carry

    lax.fori_loop(0, tc, start, 0)
    for b_ref in (b1_ref, b2_ref):
        pltpu.make_async_copy(ys_ref.at[pl.ds(0, tc * ROW_CHUNKS)], b_ref, sem).wait()
    w1 = meta_ref[:, 2:3]
    w2 = meta_ref[:, 3:4]
    parts = []
    for s in range(ROW_CHUNKS):
        y = (w1 * b1_ref[pl.ds(s, tc, stride=ROW_CHUNKS), :]
             + w2 * b2_ref[pl.ds(s, tc, stride=ROW_CHUNKS), :])
        parts.append(x1_ref[:, s * LANES:(s + 1) * LANES] + y)
    x2 = jnp.concatenate(parts, axis=1)
    if final:
        x2 = _rms(x2) * fg_ref[...]
    o_ref[...] = x2


def _combine(dest1, dest2, ys, meta, x1, fg, final):
    t = x1.shape[0]
    tc = 256
    return pl.pallas_call(
        functools.partial(_combine_kernel, tc=tc, final=final),
        grid_spec=pltpu.PrefetchScalarGridSpec(
            num_scalar_prefetch=2,
            grid=(t // tc,),
            in_specs=[pl.BlockSpec(memory_space=pl.ANY),
                      pl.BlockSpec((tc, LANES), lambda i, *_: (i, 0)),
                      pl.BlockSpec((tc, D_MODEL), lambda i, *_: (i, 0)),
                      pl.BlockSpec((1, D_MODEL), lambda i, *_: (0, 0))],
            out_specs=pl.BlockSpec((tc, D_MODEL), lambda i, *_: (i, 0)),
            scratch_shapes=[pltpu.VMEM((tc * ROW_CHUNKS, LANES), F32),
                            pltpu.VMEM((tc * ROW_CHUNKS, LANES), F32),
                            pltpu.SemaphoreType.DMA(())]),
        out_shape=jax.ShapeDtypeStruct((t, D_MODEL), F32),
        compiler_params=_cparams(("arbitrary",)),
        name="combine",
    )(dest1, dest2, ys, meta, x1, fg)


def _rope_tables(r0):
    inv = ROPE_THETA ** (-jnp.arange(0, 2 * ROPE_HALF, 2, dtype=F32) / (2 * ROPE_HALF))
    ang = jnp.arange(SEQ, dtype=F32)[:, None] * inv[None, :]
    cos, sin = jnp.cos(ang), jnp.sin(ang)
    c = jnp.ones((SEQ, LANES), F32).at[:, r0:r0 + ROPE_HALF].set(cos)
    c = c.at[:, r0 + ROPE_HALF:r0 + 2 * ROPE_HALF].set(cos)
    sa = jnp.zeros((SEQ, LANES), F32).at[:, r0:r0 + ROPE_HALF].set(-sin)
    sb = jnp.zeros((SEQ, LANES), F32).at[:, r0 + ROPE_HALF:r0 + 2 * ROPE_HALF].set(sin)
    return c, sa, sb


def _pad_cols(w, width):
    return jnp.pad(w, ((0, 0), (0, width - w.shape[1])))


def _w_in_layout(w):
    a0 = 0
    b0 = 608
    c0 = b0 + 1292
    d0 = c0 + 512
    segs = [
        _pad_cols(w[:, a0:a0 + 448], 512),
        w[:, b0:b0 + 512],
        w[:, c0:c0 + 512],
        w[:, d0:d0 + 512],
        w[:, a0 + 448:a0 + 576],
        _pad_cols(w[:, a0 + 576:a0 + 608], 128),
        w[:, b0 + 512:b0 + 1280],
        _pad_cols(w[:, b0 + 1280:b0 + 1292], 128),
    ]
    return jnp.concatenate(segs, axis=1).astype(BF16)


def _s5_params(a_re, a_im, log_dt, b_re, b_im, c_re, c_im, batch):
    dt = jnp.exp(log_dt)[:, None]
    mag = jnp.exp(a_re * dt)
    abar_r, abar_i = mag * jnp.cos(a_im * dt), mag * jnp.sin(a_im * dt)
    den = a_re * a_re + a_im * a_im
    nr, ni = abar_r - 1.0, abar_i
    coef_r = (nr * a_re + ni * a_im) / den
    coef_i = (ni * a_re - nr * a_im) / den
    bbar_r = coef_r[..., None] * b_re - coef_i[..., None] * b_im
    bbar_i = coef_r[..., None] * b_im + coef_i[..., None] * b_re
    eye = jnp.eye(S5_GROUPS, dtype=F32)
    blk_b = lambda m: jnp.einsum('gpc,gh->gchp', m, eye).reshape(W_GROUP, S5_WIDTH)
    wb = jnp.concatenate([blk_b(bbar_r), blk_b(bbar_i)], axis=1).astype(BF16)
    blk_c = lambda m: jnp.einsum('gcp,gh->gphc', m, eye).reshape(S5_WIDTH, W_GROUP)
    wc = jnp.concatenate([blk_c(c_re), blk_c(-c_im)], axis=0).astype(BF16)
    ar = jnp.broadcast_to(abar_r.reshape(1, S5_WIDTH), (batch, S5_WIDTH))
    ai = jnp.broadcast_to(abar_i.reshape(1, S5_WIDTH), (batch, S5_WIDTH))
    return wb, ar, ai, wc


def _nsa_consts(ck):
    c_start = np.arange(N_CMP_PAD) * CMP_STRIDE
    b_start = np.arange(N_BLK) * SEL_BLOCK
    ov = ((c_start[None, :] < b_start[:, None] + SEL_BLOCK)
          & (c_start[None, :] + CMP_BLOCK > b_start[:, None])).astype(np.float32)
    key_blk = np.arange(SEQ) // SEL_BLOCK
    e = (key_blk[None, :] == np.arange(N_BLK)[:, None]).astype(np.float32)
    e = e.reshape(N_BLK, SEQ // ck, ck).transpose(1, 0, 2)
    return jnp.asarray(ov), jnp.asarray(e, dtype=BF16)


def _moe_schedule(meta, counts, tmx):
    e1 = meta[:, 0].astype(jnp.int32)
    e2 = meta[:, 1].astype(jnp.int32)
    r1 = meta[:, 4].astype(jnp.int32)
    r2 = meta[:, 5].astype(jnp.int32)
    cnt = counts.astype(jnp.int32)
    ends = jnp.cumsum(cnt)
    offs = ends - cnt
    dest1 = offs[e1] + r1
    dest2 = offs[e2] + r2
    n_tiles = 2 * meta.shape[0] // tmx
    nvis_max = n_tiles + MOE_EXPERTS
    first_tile = offs // tmx
    last_tile = jnp.maximum(ends - 1, 0) // tmx
    nvis = jnp.where(cnt > 0, last_tile - first_tile + 1, 0)
    cumv = jnp.cumsum(nvis)
    total = cumv[-1]
    v = jnp.arange(nvis_max, dtype=jnp.int32)
    vc = jnp.minimum(v, total - 1)
    ve = jnp.sum((cumv[None, :] <= vc[:, None]).astype(jnp.int32), axis=1)
    vt = first_tile[ve] + vc - (cumv[ve] - nvis[ve])
    vlo = jnp.clip(offs[ve] - vt * tmx, 0, tmx)
    vhi = jnp.clip(ends[ve] - vt * tmx, 0, tmx)
    vvalid = (v < total).astype(jnp.int32)
    vfirst = jnp.concatenate([jnp.ones((1,), jnp.int32), (vt[1:] != vt[:-1]).astype(jnp.int32)])
    sched = tuple(a.astype(jnp.int32) for a in (vt, ve, vlo, vhi, vfirst, vvalid))
    return dest1, dest2, sched


def kernel(x, attn_norm_g, w_in, mla_q_norm_g, mla_kv_norm_g, mla_w_uq, mla_w_ukv, nsa_cmp_pe, nsa_cmp_w1, nsa_cmp_w2, pool_w, pool_b, pool_scale, s5_a_re, s5_a_im, s5_log_dt, s5_b_re, s5_b_im, s5_c_re, s5_c_im, s5_d, s5_glu_w, s5_glu_b, mix_norm_g, w_out, ffn_norm_g, moe_w_group, moe_b_group, moe_w_expert, moe_b_expert, moe_w_gate, moe_w_up, moe_w_down, final_norm_g):
    batch, seq, d = x.shape
    depth = w_in.shape[0]
    t = batch * seq
    xf = x.reshape(t, d)
    qtabs = _rope_tables(MLA_NOPE)
    ktabs = _rope_tables(0)
    ovt, emat = _nsa_consts(512)
    row = lambda v: v.reshape(1, -1)
    tmx = 256

    for l in range(depth):
        cols = _in_proj(xf, row(attn_norm_g[l]), _w_in_layout(w_in[l]))

        wuq = jnp.pad(mla_w_uq[l], ((0, 512 - MLA_Q_LORA), (0, 0))).astype(BF16)
        ukv = mla_w_ukv[l].reshape(HEAD_D, MLA_HEADS, MLA_NOPE + MLA_V)
        wk = jnp.pad(ukv[:, :, :MLA_NOPE], ((0, 0), (0, 0), (0, HEAD_D - MLA_NOPE)))
        wk = wk.reshape(HEAD_D, 512).astype(BF16)
        wv = ukv[:, :, MLA_NOPE:].reshape(HEAD_D, 512).astype(BF16)
        qg = jnp.pad(mla_q_norm_g[l], (0, 512 - MLA_Q_LORA)).reshape(1, 512)
        q_a, k_a, v_a = _mla_prep(cols, qg, row(mla_kv_norm_g[l]), wuq, wk, wv, qtabs, ktabs)
        y_a = _mla_attn(q_a, k_a, v_a, batch)

        q_b, kc, vc, kv_b, gates = _nsa_prep(cols, ktabs)
        pe = jnp.broadcast_to(nsa_cmp_pe[l].reshape(2, 1, CMP_BLOCK * HEAD_D),
                              (2, 8, CMP_BLOCK * HEAD_D)).astype(BF16)
        kcmp, vcmp = _nsa_compress(kc, vc, nsa_cmp_w1[l].astype(BF16), nsa_cmp_w2[l].astype(BF16),
                                   pe, batch)
        y_b = _nsa_attn(q_b, kcmp, vcmp, kv_b, gates, ovt, emat, batch)

        y_c = _pool(cols, pool_w[l].astype(BF16), row(pool_b[l]), row(pool_scale[l]), batch)

        wb, ar, ai, wc = _s5_params(s5_a_re[l], s5_a_im[l], s5_log_dt[l], s5_b_re[l], s5_b_im[l],
                                    s5_c_re[l], s5_c_im[l], batch)
        y_d = _s5(cols.reshape(batch, seq, N_IN_PAD), wb, ar, ai, wc, row(s5_d[l]),
                  s5_glu_w[l].astype(BF16), row(s5_glu_b[l])).reshape(t, W_GROUP)

        wr = jnp.concatenate([moe_w_group[l], moe_w_expert[l]], axis=1)
        wr = _pad_cols(wr, LANES)
        br = jnp.pad(jnp.concatenate([moe_b_group[l], moe_b_expert[l]]), (0, LANES - 36)).reshape(1, LANES)
        x1, h2, logits = _out_proj(y_a, y_b, y_c, y_d, row(mix_norm_g[l]), w_out[l].astype(BF16),
                                   xf, row(ffn_norm_g[l]), wr, br)

        meta, counts = _router(logits)
        dest1, dest2, sched = _moe_schedule(meta, counts[-1, MOE_GROUPS:MOE_GROUPS + MOE_EXPERTS], tmx)
        xs = _dispatch(dest1, dest2, h2)
        ys = _experts(sched, xs, moe_w_gate[l], moe_w_up[l], moe_w_down[l], tmx)
        xf = _combine(dest1, dest2, ys, meta, x1, row(final_norm_g), final=(l == depth - 1))

    return xf.reshape(batch, seq, d)
```

```python
import functools
import math

import numpy as np
import jax
import jax.numpy as jnp
from jax import lax
from jax.experimental import pallas as pl
from jax.experimental.pallas import tpu as pltpu

F32 = jnp.float32
BF16 = jnp.bfloat16

D_MODEL = 2048
SEQ = 2048
W_GROUP = 512
LANES = 128
ROW_CHUNKS = D_MODEL // LANES
ROW_PITCH = ROW_CHUNKS + 1

ROPE_THETA = 500000.0
ROPE_HALF = 16
NEG_INF = -1.0e30
FORCE_SCORE = 1.0e4
EPS = 1e-6

MLA_HEADS = 4
MLA_Q_LORA = 448
MLA_NOPE = 96
MLA_V = 128
HEAD_D = 128

CMP_BLOCK = 32
CMP_STRIDE = 16
SEL_BLOCK = 64
N_SEL = 8
N_LOCAL = 2
WINDOW = 512
N_CMP_PAD = SEQ // CMP_STRIDE
N_BLK = SEQ // SEL_BLOCK

POOL_SIZES = (2, 4, 8, 16)
S5_GROUPS = 32
S5_CH = 16
S5_STATE = 64
S5_WIDTH = S5_GROUPS * S5_STATE

MOE_GROUPS = 4
MOE_EPG = 8
MOE_EXPERTS = 32
MOE_HIDDEN = 512

COL_CQ, COL_NQ, COL_POOL, COL_S5 = 0, 512, 1024, 1536
COL_CKV, COL_KR, COL_KV6, COL_GL = 2048, 2176, 2304, 3072
N_IN_PAD = 3200

VMEM_LIMIT = 56 * 1024 * 1024


def _cparams(sem, vmem=VMEM_LIMIT):
    return pltpu.CompilerParams(dimension_semantics=sem, vmem_limit_bytes=vmem)


def _rms(x, n=None):
    n = x.shape[-1] if n is None else n
    return x * lax.rsqrt(jnp.sum(x * x, axis=-1, keepdims=True) / n + EPS)


def _dot(a, b):
    return jnp.dot(a, b, preferred_element_type=F32)


def _dot_nt(a, b, precision=None):
    return lax.dot_general(a, b, (((1,), (1,)), ((), ())), preferred_element_type=F32,
                           precision=precision)


def _rope(x, c, sa, sb):
    return x * c + pltpu.roll(x, LANES - ROPE_HALF, 1) * sa + pltpu.roll(x, ROPE_HALF, 1) * sb


def _in_proj_kernel(x_ref, g_ref, w_ref, o_ref):
    h = (_rms(x_ref[...]) * g_ref[...]).astype(BF16)
    o_ref[...] = _dot(h, w_ref[...]).astype(BF16)


_W_IN_SEGMENTS = ((COL_CQ, 0, 448), (COL_NQ, 608, 512), (COL_POOL, 1900, 512), (COL_S5, 2412, 512),
                  (COL_CKV, 448, 128), (COL_KR, 576, 32), (COL_KV6, 1120, 768), (COL_GL, 1888, 12))


def _w_in_prep_kernel(w_ref, o_ref):
    o_ref[0] = jnp.zeros(o_ref.shape[1:], BF16)
    for dst, src, width in _W_IN_SEGMENTS:
        o_ref[0, :, dst:dst + width] = w_ref[0, :, src:src + width].astype(BF16)


def _w_in_prep(w_in):
    depth, d, n = w_in.shape
    tk = 256
    return pl.pallas_call(
        _w_in_prep_kernel,
        grid=(depth, d // tk),
        in_specs=[pl.BlockSpec((1, tk, n), lambda l, k: (l, k, 0))],
        out_specs=pl.BlockSpec((1, tk, N_IN_PAD), lambda l, k: (l, k, 0)),
        out_shape=jax.ShapeDtypeStruct((depth, d, N_IN_PAD), BF16),
        compiler_params=_cparams(("parallel", "parallel")),
        name="w_in_prep",
    )(w_in)


def _in_proj(x, g, w_all, layer):
    t = x.shape[0]
    tm = 512
    return pl.pallas_call(
        _in_proj_kernel,
        grid=(t // tm,),
        in_specs=[pl.BlockSpec((tm, D_MODEL), lambda i: (i, 0)),
                  pl.BlockSpec((1, D_MODEL), lambda i: (0, 0)),
                  pl.BlockSpec((None, D_MODEL, N_IN_PAD), lambda i: (layer, 0, 0))],
        out_specs=pl.BlockSpec((tm, N_IN_PAD), lambda i: (i, 0)),
        out_shape=jax.ShapeDtypeStruct((t, N_IN_PAD), BF16),
        compiler_params=_cparams(("parallel",)),
        name="in_proj",
    )(x, g, w_all)


def _mla_prep_kernel(cq_ref, ckv_ref, kr_ref, qg_ref, kvg_ref, wuq_ref, wk_ref, wv_ref,
                     qc_ref, qsa_ref, qsb_ref, kc_ref, ksa_ref, ksb_ref,
                     q_out, k_out, v_out):
    scale = 1.0 / math.sqrt(HEAD_D)
    qn = (_rms(cq_ref[...].astype(F32), MLA_Q_LORA) * qg_ref[...]).astype(BF16)
    q = _dot(qn, wuq_ref[...])
    kvn = (_rms(ckv_ref[...].astype(F32)) * kvg_ref[...]).astype(BF16)
    kn = _dot(kvn, wk_ref[...])
    v_out[...] = _dot(kvn, wv_ref[...]).astype(BF16)
    kr = _rope(kr_ref[...].astype(F32), kc_ref[...], ksa_ref[...], ksb_ref[...])
    kr = pltpu.roll(kr, MLA_NOPE, 1)
    for h in range(MLA_HEADS):
        sl = slice(h * HEAD_D, (h + 1) * HEAD_D)
        qh = _rope(q[:, sl], qc_ref[...], qsa_ref[...], qsb_ref[...])
        q_out[:, sl] = (qh * scale).astype(BF16)
        k_out[:, sl] = (kn[:, sl] + kr).astype(BF16)


def _mla_prep(cols, qg, kvg, wuq, wk, wv, qtabs, ktabs):
    t = cols.shape[0]
    tm = 512
    nsb = SEQ // tm
    tab = pl.BlockSpec((tm, LANES), lambda i: (i % nsb, 0))
    full = lambda shape: pl.BlockSpec(shape, lambda i: (0, 0))
    out = pl.BlockSpec((tm, W_GROUP), lambda i: (i, 0))
    return pl.pallas_call(
        _mla_prep_kernel,
        grid=(t // tm,),
        in_specs=[pl.BlockSpec((tm, 512), lambda i: (i, COL_CQ // 512)),
                  pl.BlockSpec((tm, LANES), lambda i: (i, COL_CKV // LANES)),
                  pl.BlockSpec((tm, LANES), lambda i: (i, COL_KR // LANES)),
                  full((1, 512)), full((1, LANES)),
                  full((512, 512)), full((LANES, 512)), full((LANES, 512)),
                  tab, tab, tab, tab, tab, tab],
        out_specs=[out, out, out],
        out_shape=[jax.ShapeDtypeStruct((t, W_GROUP), BF16)] * 3,
        compiler_params=_cparams(("parallel",)),
        name="mla_prep",
    )(cols, cols, cols, qg, kvg, wuq, wk, wv, *qtabs, *ktabs)


def _fold_lanes(a, op):
    out = a[:, 0:LANES]
    for c in range(1, a.shape[1] // LANES):
        out = op(out, a[:, c * LANES:(c + 1) * LANES])
    return out


def _mla_attn_kernel(q_ref, k_ref, v_ref, o_ref, s_ref, mx_ref, ls_ref, acc_ref, *, tq):
    i = pl.program_id(1)
    d0 = pl.multiple_of(i * tq, tq)
    row = lax.broadcasted_iota(jnp.int32, (tq, tq), 0)
    col = lax.broadcasted_iota(jnp.int32, (tq, tq), 1)
    heads = [slice(h * HEAD_D, (h + 1) * HEAD_D) for h in range(MLA_HEADS)]
    tile = lambda j: pl.ds(pl.multiple_of(j * tq, tq), tq)

    for h, sl in enumerate(heads):
        s = jnp.where(col <= row, _dot_nt(q_ref[:, sl], k_ref[pl.ds(d0, tq), sl]), NEG_INF)
        s_ref[h, i] = s
        mx_ref[h] = _fold_lanes(s, jnp.maximum)

    def scores(j, carry):
        for h, sl in enumerate(heads):
            s = _dot_nt(q_ref[:, sl], k_ref[tile(j), sl])
            s_ref[h, j] = s
            mx_ref[h] = jnp.maximum(mx_ref[h], _fold_lanes(s, jnp.maximum))
        return carry

    lax.fori_loop(0, i, scores, 0)
    for h in range(MLA_HEADS):
        m = jnp.max(mx_ref[h], axis=-1, keepdims=True)
        mx_ref[h] = jnp.broadcast_to(m, (tq, LANES))
        ls_ref[h] = jnp.zeros((tq, LANES), F32)
        acc_ref[h] = jnp.zeros((tq, HEAD_D), F32)

    def values(j, carry):
        for h, sl in enumerate(heads):
            m = mx_ref[h]
            s = s_ref[h, j]
            p = jnp.concatenate([jnp.exp(s[:, c * LANES:(c + 1) * LANES] - m)
                                 for c in range(tq // LANES)], axis=1)
            ls_ref[h] = ls_ref[h] + _fold_lanes(p, jnp.add)
            acc_ref[h] = acc_ref[h] + _dot(p.astype(BF16), v_ref[tile(j), sl])
        return carry

    lax.fori_loop(0, i + 1, values, 0)
    for h, sl in enumerate(heads):
        o_ref[:, sl] = acc_ref[h] / jnp.sum(ls_ref[h], axis=-1, keepdims=True)


def _mla_attn(q, k, v, batch):
    tq = 256
    nq = SEQ // tq
    return pl.pallas_call(
        functools.partial(_mla_attn_kernel, tq=tq),
        grid=(batch, nq),
        in_specs=[pl.BlockSpec((tq, W_GROUP), lambda b, i: (b * nq + i, 0)),
                  pl.BlockSpec((SEQ, W_GROUP), lambda b, i: (b, 0)),
                  pl.BlockSpec((SEQ, W_GROUP), lambda b, i: (b, 0))],
        out_specs=pl.BlockSpec((tq, W_GROUP), lambda b, i: (b * nq + i, 0)),
        out_shape=jax.ShapeDtypeStruct((batch * SEQ, W_GROUP), F32),
        scratch_shapes=[pltpu.VMEM((MLA_HEADS, nq, tq, tq), F32),
                        pltpu.VMEM((MLA_HEADS, tq, LANES), F32),
                        pltpu.VMEM((MLA_HEADS, tq, LANES), F32),
                        pltpu.VMEM((MLA_HEADS, tq, HEAD_D), F32)],
        compiler_params=_cparams(("parallel", "arbitrary")),
        name="mla_attn",
    )(q, k, v)


def _nsa_prep_kernel(q_ref, kv_ref, gl_ref, c_ref, sa_ref, sb_ref,
                     q_out, kc_out, vc_out, kv_out, g_out):
    scale = 1.0 / math.sqrt(HEAD_D)
    c, sa, sb = c_ref[...], sa_ref[...], sb_ref[...]
    for h in range(4):
        sl = slice(h * HEAD_D, (h + 1) * HEAD_D)
        q_out[:, sl] = (_rope(q_ref[:, sl].astype(F32), c, sa, sb) * scale).astype(BF16)
    kv = lambda n: kv_ref[:, n * HEAD_D:(n + 1) * HEAD_D].astype(F32)
    kc_out[...] = _rope(kv(0), c, sa, sb)
    vc_out[...] = kv(1)
    kv_out[:, 0:128] = _rope(kv(2), c, sa, sb).astype(BF16)
    kv_out[:, 128:256] = kv(3).astype(BF16)
    kv_out[:, 256:384] = _rope(kv(4), c, sa, sb).astype(BF16)
    kv_out[:, 384:512] = kv(5).astype(BF16)
    g_out[...] = jax.nn.sigmoid(gl_ref[...].astype(F32))


def _nsa_prep(cols, tabs):
    t = cols.shape[0]
    tm = 512
    nsb = SEQ // tm
    tab = pl.BlockSpec((tm, LANES), lambda i: (i % nsb, 0))
    return pl.pallas_call(
        _nsa_prep_kernel,
        grid=(t // tm,),
        in_specs=[pl.BlockSpec((tm, 512), lambda i: (i, COL_NQ // 512)),
                  pl.BlockSpec((tm, 768), lambda i: (i, COL_KV6 // 768)),
                  pl.BlockSpec((tm, LANES), lambda i: (i, COL_GL // LANES)),
                  tab, tab, tab],
        out_specs=[pl.BlockSpec((tm, 512), lambda i: (i, 0)),
                   pl.BlockSpec((tm, LANES), lambda i: (i, 0)),
                   pl.BlockSpec((tm, LANES), lambda i: (i, 0)),
                   pl.BlockSpec((tm, 512), lambda i: (i, 0)),
                   pl.BlockSpec((tm, LANES), lambda i: (i, 0))],
        out_shape=[jax.ShapeDtypeStruct((t, 512), BF16),
                   jax.ShapeDtypeStruct((t, LANES), F32),
                   jax.ShapeDtypeStruct((t, LANES), F32),
                   jax.ShapeDtypeStruct((t, 512), BF16),
                   jax.ShapeDtypeStruct((t, LANES), F32)],
        compiler_params=_cparams(("parallel",)),
        name="nsa_prep",
    )(cols, cols, cols, *tabs)


def _nsa_compress_kernel(xk_ref, xv_ref, w1_ref, w2_ref, pe_ref, k_out, v_out):
    for c, (x_ref, o_ref) in enumerate(((xk_ref, k_out), (xv_ref, v_out))):
        a = jnp.zeros((N_CMP_PAD, HEAD_D), F32)
        b = jnp.zeros((N_CMP_PAD, HEAD_D), F32)
        for r in range(CMP_STRIDE):
            x = x_ref[pl.ds(r, N_CMP_PAD, stride=CMP_STRIDE), :].astype(BF16)
            a = a + _dot(x, w1_ref[c, r * HEAD_D:(r + 1) * HEAD_D, :])
            b = b + _dot(x, w1_ref[c, (CMP_STRIDE + r) * HEAD_D:(CMP_STRIDE + r + 1) * HEAD_D, :])
        b = pltpu.roll(b, N_CMP_PAD - 1, 0)
        pe = _dot(pe_ref[c], w1_ref[c])[0:1, :]
        hid = jax.nn.gelu(a + b + pe)
        o_ref[0] = _dot(hid.astype(BF16), w2_ref[c]).astype(BF16)


def _nsa_compress(xk, xv, w1, w2, pe, batch):
    xspec = pl.BlockSpec((SEQ, HEAD_D), lambda b: (b, 0))
    ospec = pl.BlockSpec((1, N_CMP_PAD, HEAD_D), lambda b: (b, 0, 0))
    return pl.pallas_call(
        _nsa_compress_kernel,
        grid=(batch,),
        in_specs=[xspec, xspec,
                  pl.BlockSpec((2, CMP_BLOCK * HEAD_D, HEAD_D), lambda b: (0, 0, 0)),
                  pl.BlockSpec((2, HEAD_D, HEAD_D), lambda b: (0, 0, 0)),
                  pl.BlockSpec((2, 8, CMP_BLOCK * HEAD_D), lambda b: (0, 0, 0))],
        out_specs=[ospec, ospec],
        out_shape=[jax.ShapeDtypeStruct((batch, N_CMP_PAD, HEAD_D), BF16)] * 2,
        compiler_params=_cparams(("parallel",)),
        name="nsa_compress",
    )(xk, xv, w1, w2, pe)


def _softmax_rows(s):
    m = jnp.max(s, axis=-1, keepdims=True)
    p = jnp.exp(s - m)
    return p / jnp.sum(p, axis=-1, keepdims=True)


def _nsa_attn_kernel(q_ref, kc_ref, vc_ref, kv_ref, g_ref, ovt_ref, e_ref, o_ref,
                     m_ref, l_ref, acc_ref, s_ref, *, tq, ck):
    i = pl.program_id(1)
    q0 = i * tq
    nh = 4
    qs = jnp.concatenate([q_ref[:, h * HEAD_D:(h + 1) * HEAD_D] for h in range(nh)], axis=0)
    qpos = q0 + lax.broadcasted_iota(jnp.int32, (tq, 1), 0)
    stack = lambda a: jnp.concatenate([a] * nh, axis=0)

    sc = _dot_nt(qs, kc_ref[0])
    n_idx = lax.broadcasted_iota(jnp.int32, (tq, N_CMP_PAD), 1)
    valid_c = n_idx * CMP_STRIDE + (CMP_BLOCK - 1) <= qpos
    valid_c4 = stack(valid_c)
    pc = _softmax_rows(jnp.where(valid_c4, sc, NEG_INF))
    pc = jnp.where(valid_c4, pc, 0.0)
    o_c = _dot(pc.astype(BF16), vc_ref[0])
    psum = pc[0:tq] + pc[tq:2 * tq] + pc[2 * tq:3 * tq] + pc[3 * tq:4 * tq]

    imp_t = _dot_nt(ovt_ref[...], psum, precision=lax.Precision.HIGHEST)
    kblk = lax.broadcasted_iota(jnp.int32, (N_BLK, tq), 0)
    cur = (q0 + lax.broadcasted_iota(jnp.int32, (N_BLK, tq), 1)) // SEL_BLOCK
    forced = (kblk == 0) | ((kblk <= cur) & (kblk > cur - N_LOCAL))
    score = jnp.where(forced, FORCE_SCORE, jnp.where(kblk <= cur, imp_t, -1.0))
    cnt = jnp.zeros((N_BLK, tq), F32)
    for j in range(N_BLK):
        sj = score[j:j + 1, :]
        beats = (sj > score) | ((sj == score) & (kblk > j))
        cnt = cnt + beats.astype(F32)
    sel = jnp.transpose((cnt < N_SEL).astype(F32)).astype(BF16)

    wlen = WINDOW + tq
    w0 = pl.multiple_of(jnp.maximum(q0 - WINDOW, 0), tq)
    sw = _dot_nt(qs, kv_ref[pl.ds(w0, wlen), 256:384])
    kpos = w0 + lax.broadcasted_iota(jnp.int32, (tq, wlen), 1)
    ok = (kpos <= qpos) & (kpos > qpos - WINDOW)
    pw = _softmax_rows(sw + stack(jnp.where(ok, 0.0, NEG_INF)))
    o_w = _dot(pw.astype(BF16), kv_ref[pl.ds(w0, wlen), 384:512])

    nck = q0 // ck + 1
    m_ref[...] = jnp.full(m_ref.shape, NEG_INF, F32)

    def scores(c, carry):
        r0 = pl.multiple_of(c * ck, ck)
        chosen = _dot(sel, e_ref[c])
        kpos = r0 + lax.broadcasted_iota(jnp.int32, (tq, ck), 1)
        ok = (chosen > 0.5) & (kpos <= qpos)
        s = _dot_nt(qs, kv_ref[pl.ds(r0, ck), 0:128]) + stack(jnp.where(ok, 0.0, NEG_INF))
        s_ref[c] = s
        m_ref[...] = jnp.maximum(m_ref[...], _fold_lanes(s, jnp.maximum))
        return carry

    lax.fori_loop(0, nck, scores, 0)
    m_ref[...] = jnp.broadcast_to(jnp.max(m_ref[...], axis=-1, keepdims=True), m_ref.shape)
    l_ref[...] = jnp.zeros(l_ref.shape, F32)
    acc_ref[...] = jnp.zeros(acc_ref.shape, F32)

    def values(c, carry):
        r0 = pl.multiple_of(c * ck, ck)
        m = m_ref[...]
        s = s_ref[c]
        p = jnp.concatenate([jnp.exp(s[:, k * LANES:(k + 1) * LANES] - m)
                             for k in range(ck // LANES)], axis=1)
        l_ref[...] = l_ref[...] + _fold_lanes(p, jnp.add)
        acc_ref[...] = acc_ref[...] + _dot(p.astype(BF16), kv_ref[pl.ds(r0, ck), 128:256])
        return carry

    lax.fori_loop(0, nck, values, 0)
    o_s = acc_ref[...] / jnp.sum(l_ref[...], axis=-1, keepdims=True)

    g = g_ref[...]
    for h in range(nh):
        rs = slice(h * tq, (h + 1) * tq)
        o_ref[:, h * HEAD_D:(h + 1) * HEAD_D] = (
            g[:, 3 * h:3 * h + 1] * o_c[rs] + g[:, 3 * h + 1:3 * h + 2] * o_s[rs]
            + g[:, 3 * h + 2:3 * h + 3] * o_w[rs])


def _nsa_attn(q, kcmp, vcmp, kv, gates, ovt, emat, batch):
    tq, ck = 128, 512
    nq = SEQ // tq
    return pl.pallas_call(
        functools.partial(_nsa_attn_kernel, tq=tq, ck=ck),
        grid=(batch, nq),
        in_specs=[pl.BlockSpec((tq, 512), lambda b, i: (b * nq + i, 0)),
                  pl.BlockSpec((1, N_CMP_PAD, HEAD_D), lambda b, i: (b, 0, 0)),
                  pl.BlockSpec((1, N_CMP_PAD, HEAD_D), lambda b, i: (b, 0, 0)),
                  pl.BlockSpec((SEQ, 512), lambda b, i: (b, 0)),
                  pl.BlockSpec((tq, LANES), lambda b, i: (b * nq + i, 0)),
                  pl.BlockSpec((N_BLK, N_CMP_PAD), lambda b, i: (0, 0)),
                  pl.BlockSpec((SEQ // ck, N_BLK, ck), lambda b, i: (0, 0, 0))],
        out_specs=pl.BlockSpec((tq, 512), lambda b, i: (b * nq + i, 0)),
        out_shape=jax.ShapeDtypeStruct((batch * SEQ, 512), F32),
        scratch_shapes=[pltpu.VMEM((4 * tq, LANES), F32), pltpu.VMEM((4 * tq, LANES), F32),
                        pltpu.VMEM((4 * tq, HEAD_D), F32),
                        pltpu.VMEM((SEQ // ck, 4 * tq, ck), F32)],
        compiler_params=_cparams(("parallel", "arbitrary")),
        name="nsa_attn",
    )(q, kcmp, vcmp, kv, gates, ovt, emat)


def _pool_kernel(u_ref, w_ref, b_ref, s_ref, o_ref, pad_ref):
    maxw = POOL_SIZES[-1]
    pad_ref[0:maxw, :] = jnp.zeros((maxw, W_GROUP), F32)
    pad_ref[maxw:maxw + SEQ, :] = u_ref[...].astype(F32)
    rc = 512
    for g, w in enumerate(POOL_SIZES):
        sl = slice(g * LANES, (g + 1) * LANES)
        for r in range(SEQ // rc):
            acc = pad_ref[maxw + r * rc:maxw + (r + 1) * rc, sl]
            tok = acc
            for j in range(1, w):
                acc = acc + pad_ref[maxw - j + r * rc:maxw - j + (r + 1) * rc, sl]
            t = r * rc + lax.broadcasted_iota(jnp.int32, (rc, 1), 0)
            cnt = jnp.minimum(t + 1, w).astype(F32)
            d = acc / cnt - tok
            y = _dot(d.astype(BF16), w_ref[g])
            o_ref[r * rc:(r + 1) * rc, sl] = (y + b_ref[:, sl]) * s_ref[:, sl]


def _pool(cols, w, b, s, batch):
    return pl.pallas_call(
        _pool_kernel,
        grid=(batch,),
        in_specs=[pl.BlockSpec((SEQ, W_GROUP), lambda i: (i, COL_POOL // W_GROUP)),
                  pl.BlockSpec((4, LANES, LANES), lambda i: (0, 0, 0)),
                  pl.BlockSpec((1, W_GROUP), lambda i: (0, 0)),
                  pl.BlockSpec((1, W_GROUP), lambda i: (0, 0))],
        out_specs=pl.BlockSpec((SEQ, W_GROUP), lambda i: (i, 0)),
        out_shape=jax.ShapeDtypeStruct((batch * SEQ, W_GROUP), F32),
        scratch_shapes=[pltpu.VMEM((SEQ + POOL_SIZES[-1], W_GROUP), F32)],
        compiler_params=_cparams(("parallel",)),
        name="pool",
    )(cols, w, b, s)


def _s5_kernel(u_ref, wb_ref, ar_ref, ai_ref, wc_ref, d_ref, gw_ref, gb_ref, o_ref,
               bu_ref, st_ref, tm_ref, *, batch, tc):
    @pl.when(pl.program_id(0) == 0)
    def _():
        st_ref[...] = jnp.zeros(st_ref.shape, F32)

    nslab = W_GROUP // LANES
    for b in range(batch):
        for j in range(nslab):
            tm_ref[j, pl.ds(b, tc, stride=batch), :] = u_ref[b, :, j * LANES:(j + 1) * LANES].astype(F32)
    u = jnp.concatenate([tm_ref[j] for j in range(nslab)], axis=1)
    gl = 8 * S5_STATE
    ub = u.astype(BF16)
    for j in range(nslab):
        uj = ub[:, j * LANES:(j + 1) * LANES]
        for part in (0, S5_WIDTH):
            cs = slice(part + j * gl, part + (j + 1) * gl)
            bu_ref[:, cs] = _dot(uj, wb_ref[j * LANES:(j + 1) * LANES, cs])
    lc = 512
    unroll = 8
    for c in range(S5_WIDTH // lc):
        re = slice(c * lc, (c + 1) * lc)
        im = slice(S5_WIDTH + c * lc, S5_WIDTH + (c + 1) * lc)
        ar, ai = ar_ref[:, re], ai_ref[:, re]

        def body(tb, carry, re=re, im=im, ar=ar, ai=ai):
            xr, xi = carry
            for k in range(unroll):
                r0 = pl.multiple_of((tb * unroll + k) * batch, batch)
                nxr = ar * xr - ai * xi + bu_ref[pl.ds(r0, batch), re]
                nxi = ar * xi + ai * xr + bu_ref[pl.ds(r0, batch), im]
                bu_ref[pl.ds(r0, batch), re] = nxr
                bu_ref[pl.ds(r0, batch), im] = nxi
                xr, xi = nxr, nxi
            return xr, xi

        xr, xi = lax.fori_loop(0, tc // unroll, body, (st_ref[:, re], st_ref[:, im]))
        st_ref[:, re] = xr
        st_ref[:, im] = xi

    ys = []
    for j in range(nslab):
        osl = slice(j * LANES, (j + 1) * LANES)
        yj = 0.0
        for part in (0, S5_WIDTH):
            cs = slice(part + j * gl, part + (j + 1) * gl)
            yj = yj + _dot(bu_ref[:, cs].astype(BF16), wc_ref[cs, osl])
        ys.append(yj)
    y = jnp.concatenate(ys, axis=1) + d_ref[...] * u
    y = jax.nn.gelu(y)
    z = _dot(y.astype(BF16), gw_ref[...]) + gb_ref[...]
    o = y * jax.nn.sigmoid(z)
    for j in range(nslab):
        tm_ref[j] = o[:, j * LANES:(j + 1) * LANES]
    for b in range(batch):
        for j in range(nslab):
            o_ref[b, :, j * LANES:(j + 1) * LANES] = tm_ref[j, pl.ds(b, tc, stride=batch), :]


def _s5(cols3, wb, ar, ai, wc, d, gw, gb):
    batch = cols3.shape[0]
    tc = 64
    rows = tc * batch
    full = lambda shape: pl.BlockSpec(shape, lambda i: (0, 0))
    return pl.pallas_call(
        functools.partial(_s5_kernel, batch=batch, tc=tc),
        grid=(SEQ // tc,),
        in_specs=[pl.BlockSpec((batch, tc, W_GROUP), lambda i: (0, i, COL_S5 // W_GROUP)),
                  full((W_GROUP, 2 * S5_WIDTH)), full((batch, S5_WIDTH)), full((batch, S5_WIDTH)),
                  full((2 * S5_WIDTH, W_GROUP)), full((1, W_GROUP)),
                  full((W_GROUP, W_GROUP)), full((1, W_GROUP))],
        out_specs=pl.BlockSpec((batch, tc, W_GROUP), lambda i: (0, i, 0)),
        out_shape=jax.ShapeDtypeStruct((batch, SEQ, W_GROUP), F32),
        scratch_shapes=[pltpu.VMEM((rows, 2 * S5_WIDTH), F32),
                        pltpu.VMEM((batch, 2 * S5_WIDTH), F32),
                        pltpu.VMEM((W_GROUP // LANES, rows, LANES), F32)],
        compiler_params=_cparams(("arbitrary",)),
        name="s5",
    )(cols3, wb, ar, ai, wc, d, gw, gb)


def _out_proj_kernel(ya_ref, yb_ref, yc_ref, yd_ref, mg_ref, wo_ref, x_ref, fg_ref,
                     wrh_ref, wrl_ref, br_ref, x1_ref, h2_ref, lg_ref, *, tm):
    acc = x_ref[...]
    for gi, y_ref in enumerate((ya_ref, yb_ref, yc_ref, yd_ref)):
        sl = slice(gi * W_GROUP, (gi + 1) * W_GROUP)
        n = (_rms(y_ref[...]) * mg_ref[:, sl]).astype(BF16)
        acc = acc + _dot(n, wo_ref[sl, :])
    x1_ref[...] = acc
    h2 = _rms(acc) * fg_ref[...]
    hi = h2.astype(BF16)
    lo = (h2 - hi.astype(F32)).astype(BF16)
    lg_ref[...] = (_dot(hi, wrh_ref[...]) + _dot(hi, wrl_ref[...]) + _dot(lo, wrh_ref[...])
                   + br_ref[...])
    for s in range(ROW_CHUNKS):
        h2_ref[pl.ds(s, tm, stride=ROW_PITCH), :] = h2[:, s * LANES:(s + 1) * LANES]
    h2_ref[pl.ds(ROW_CHUNKS, tm, stride=ROW_PITCH), :] = jnp.zeros((tm, LANES), F32)


def _out_proj(ya, yb, yc, yd, mg, wo_all, layer, x, fg, wr_hi, wr_lo, br):
    t = x.shape[0]
    tm = 256
    yspec = pl.BlockSpec((tm, W_GROUP), lambda i: (i, 0))
    full = lambda shape: pl.BlockSpec(shape, lambda i: (0, 0))
    return pl.pallas_call(
        functools.partial(_out_proj_kernel, tm=tm),
        grid=(t // tm,),
        in_specs=[yspec, yspec, yspec, yspec, full((1, D_MODEL)),
                  pl.BlockSpec((None, D_MODEL, D_MODEL), lambda i: (layer, 0, 0)),
                  pl.BlockSpec((tm, D_MODEL), lambda i: (i, 0)), full((1, D_MODEL)),
                  full((D_MODEL, LANES)), full((D_MODEL, LANES)), full((1, LANES))],
        out_specs=[pl.BlockSpec((tm, D_MODEL), lambda i: (i, 0)),
                   pl.BlockSpec((tm * ROW_PITCH, LANES), lambda i: (i, 0)),
                   pl.BlockSpec((tm, LANES), lambda i: (i, 0))],
        out_shape=[jax.ShapeDtypeStruct((t, D_MODEL), F32),
                   jax.ShapeDtypeStruct((t * ROW_PITCH, LANES), F32),
                   jax.ShapeDtypeStruct((t, LANES), F32)],
        compiler_params=_cparams(("parallel",)),
        name="out_proj",
    )(ya, yb, yc, yd, mg, wo_all, x, fg, wr_hi, wr_lo, br)


def _router_kernel(lg_ref, meta_ref, cnt_ref, carry_ref, *, tm):
    @pl.when(pl.program_id(0) == 0)
    def _():
        carry_ref[...] = jnp.zeros(carry_ref.shape, F32)

    lg = lg_ref[...]
    lane = lax.broadcasted_iota(jnp.int32, (tm, LANES), 1)
    big = jnp.int32(1 << 20)
    rmax = lambda a: jnp.max(a, axis=-1, keepdims=True)
    rmin = lambda a: jnp.min(a, axis=-1, keepdims=True)
    rsum = lambda a: jnp.sum(a, axis=-1, keepdims=True)

    is_g = lane < MOE_GROUPS
    gl = jnp.where(is_g, lg, NEG_INF)
    gm = rmax(gl)
    p_top = 1.0 / rsum(jnp.where(is_g, jnp.exp(gl - gm), 0.0))
    g_top = rmin(jnp.where(is_g & (gl == gm), lane, big))

    is_e = (lane >= MOE_GROUPS) & (lane < MOE_GROUPS + MOE_EXPERTS) \
        & (((lane - MOE_GROUPS) // MOE_EPG) == g_top)
    el = jnp.where(is_e, lg, NEG_INF)
    ee = jnp.where(is_e, jnp.exp(el - rmax(el)), 0.0)
    p = jnp.where(is_e, ee / rsum(ee), -1.0)
    p1 = rmax(p)
    i1 = rmin(jnp.where(p == p1, lane, big))
    p_rest = jnp.where(lane == i1, -1.0, p)
    p2 = rmax(p_rest)
    i2 = rmin(jnp.where((p_rest == p2) & is_e & (lane != i1), lane, big))
    den = p1 + p2
    w1 = p_top * (p1 / den)
    w2 = p_top * (p2 / den)

    hit1, hit2 = lane == i1, lane == i2
    oh = (hit1 | hit2).astype(F32)
    r = lax.broadcasted_iota(jnp.int32, (tm, tm), 0)
    c = lax.broadcasted_iota(jnp.int32, (tm, tm), 1)
    before = _dot((c < r).astype(BF16), oh.astype(BF16)) + carry_ref[0:1, :]
    r1 = rsum(jnp.where(hit1, before, 0.0))
    r2 = rsum(jnp.where(hit2, before, 0.0))
    carry_ref[...] = carry_ref[...] + jnp.sum(oh, axis=0, keepdims=True)
    cnt_ref[...] = carry_ref[...]

    e1 = (i1 - MOE_GROUPS).astype(F32)
    e2 = (i2 - MOE_GROUPS).astype(F32)
    vals = (e1, e2, w1, w2, r1, r2)
    meta = jnp.zeros((tm, LANES), F32)
    for k, v in enumerate(vals):
        meta = jnp.where(lane == k, v, meta)
    meta_ref[...] = meta


def _router(logits):
    t = logits.shape[0]
    tm = 512
    return pl.pallas_call(
        functools.partial(_router_kernel, tm=tm),
        grid=(t // tm,),
        in_specs=[pl.BlockSpec((tm, LANES), lambda i: (i, 0))],
        out_specs=[pl.BlockSpec((tm, LANES), lambda i: (i, 0)),
                   pl.BlockSpec((8, LANES), lambda i: (i, 0))],
        out_shape=[jax.ShapeDtypeStruct((t, LANES), F32),
                   jax.ShapeDtypeStruct((t // tm * 8, LANES), F32)],
        scratch_shapes=[pltpu.VMEM((8, LANES), F32)],
        compiler_params=_cparams(("arbitrary",)),
        name="router",
    )(logits)


def _dest_kernel(meta_ref, cnt_ref, o_ref, *, tm):
    cnt = cnt_ref[...]
    hi = jnp.floor(cnt * (1.0 / 256.0))
    lo = cnt - 256.0 * hi
    r = lax.broadcasted_iota(jnp.int32, (LANES, LANES), 0)
    c = lax.broadcasted_iota(jnp.int32, (LANES, LANES), 1)
    before = (r < c).astype(BF16)
    start = (256.0 * _dot(hi.astype(BF16), before) + _dot(lo.astype(BF16), before))[0:1, :]
    meta = meta_ref[...]
    lane = lax.broadcasted_iota(jnp.int32, (tm, LANES), 1)
    rsum = lambda a: jnp.sum(a, axis=-1, keepdims=True)
    out = jnp.zeros((tm, LANES), F32)
    for k in range(2):
        e_lane = meta[:, k:k + 1].astype(jnp.int32) + MOE_GROUPS
        d = rsum(jnp.where(lane == e_lane, start, 0.0)) + meta[:, 4 + k:5 + k]
        out = jnp.where(lane == k, d, out)
    o_ref[...] = out.astype(jnp.int32)


def _dest(meta, counts):
    t = meta.shape[0]
    tm = 512
    last = counts.shape[0] // 8 - 1
    return pl.pallas_call(
        functools.partial(_dest_kernel, tm=tm),
        grid=(t // tm,),
        in_specs=[pl.BlockSpec((tm, LANES), lambda i: (i, 0)),
                  pl.BlockSpec((8, LANES), lambda i: (last, 0))],
        out_specs=pl.BlockSpec((tm, LANES), lambda i: (i, 0)),
        out_shape=jax.ShapeDtypeStruct((t, LANES), jnp.int32),
        compiler_params=_cparams(("parallel",)),
        name="dest",
    )(meta, counts)


def _dispatch_kernel(d1_ref, d2_ref, h_ref, xs_ref, sem, *, td):
    base = pl.program_id(0) * td

    def start(r, carry):
        src = h_ref.at[pl.ds(r * ROW_PITCH, ROW_PITCH)]
        for d_ref in (d1_ref, d2_ref):
            dst = xs_ref.at[pl.ds(d_ref[base + r] * ROW_PITCH, ROW_PITCH)]
            pltpu.make_async_copy(src, dst, sem).start()
        return carry

    lax.fori_loop(0, td, start, 0)
    for _ in range(2):
        pltpu.make_async_copy(h_ref, xs_ref.at[pl.ds(0, td * ROW_PITCH)], sem).wait()


def _dispatch(dest1, dest2, h2):
    t = dest1.shape[0]
    td = 512
    return pl.pallas_call(
        functools.partial(_dispatch_kernel, td=td),
        grid_spec=pltpu.PrefetchScalarGridSpec(
            num_scalar_prefetch=2,
            grid=(t // td,),
            in_specs=[pl.BlockSpec((td * ROW_PITCH, LANES), lambda i, *_: (i, 0))],
            out_specs=pl.BlockSpec(memory_space=pl.ANY),
            scratch_shapes=[pltpu.SemaphoreType.DMA(())]),
        out_shape=jax.ShapeDtypeStruct((2 * t * ROW_PITCH, LANES), F32),
        compiler_params=pltpu.CompilerParams(dimension_semantics=("arbitrary",),
                                             has_side_effects=True),
        name="dispatch",
    )(dest1, dest2, h2)


def _expert_kernel(vt_ref, ve_ref, vlo_ref, vhi_ref, vfirst_ref, vvalid_ref,
                   xs_ref, wg_ref, wu_ref, wd_ref, ys_ref, wgb_ref, wub_ref, wdb_ref, *, tmx):
    v = pl.program_id(0)
    prev = jnp.maximum(v - 1, 0)

    @pl.when((v == 0) | (ve_ref[v] != ve_ref[prev]))
    def _():
        wgb_ref[...] = wg_ref[0].astype(BF16)
        wub_ref[...] = wu_ref[0].astype(BF16)
        wdb_ref[...] = wd_ref[0].astype(BF16)

    @pl.when(vvalid_ref[v] == 1)
    def _():
        x = jnp.concatenate([xs_ref[pl.ds(s, tmx, stride=ROW_PITCH), :] for s in range(ROW_CHUNKS)],
                            axis=1).astype(BF16)
        a = _dot(x, wgb_ref[...])
        u = _dot(x, wub_ref[...])
        hid = (jax.nn.silu(a) * u).astype(BF16)
        y = _dot(hid, wdb_ref[...])
        rows = lax.broadcasted_iota(jnp.int32, (tmx, 1), 0)
        mine = (rows >= vlo_ref[v]) & (rows < vhi_ref[v])

        @pl.when(vfirst_ref[v] == 1)
        def _():
            for s in range(ROW_CHUNKS):
                ys_ref[pl.ds(s, tmx, stride=ROW_PITCH), :] = jnp.where(
                    mine, y[:, s * LANES:(s + 1) * LANES], 0.0)
            ys_ref[pl.ds(ROW_CHUNKS, tmx, stride=ROW_PITCH), :] = jnp.zeros((tmx, LANES), F32)

        @pl.when(vfirst_ref[v] == 0)
        def _():
            for s in range(ROW_CHUNKS):
                old = ys_ref[pl.ds(s, tmx, stride=ROW_PITCH), :]
                ys_ref[pl.ds(s, tmx, stride=ROW_PITCH), :] = jnp.where(
                    mine, y[:, s * LANES:(s + 1) * LANES], old)


def _experts(sched, xs, wg, wu, wd, layer, tmx):
    nvis = sched[0].shape[0]
    rows = xs.shape[0]
    xspec = pl.BlockSpec((tmx * ROW_PITCH, LANES), lambda v, vt, *_: (vt[v], 0))
    wspec = lambda shape: pl.BlockSpec((None, 1) + shape, lambda v, vt, ve, *_: (layer, ve[v], 0, 0))
    return pl.pallas_call(
        functools.partial(_expert_kernel, tmx=tmx),
        grid_spec=pltpu.PrefetchScalarGridSpec(
            num_scalar_prefetch=6,
            grid=(nvis,),
            in_specs=[xspec, wspec((D_MODEL, MOE_HIDDEN)), wspec((D_MODEL, MOE_HIDDEN)),
                      wspec((MOE_HIDDEN, D_MODEL))],
            out_specs=xspec,
            scratch_shapes=[pltpu.VMEM((D_MODEL, MOE_HIDDEN), BF16),
                            pltpu.VMEM((D_MODEL, MOE_HIDDEN), BF16),
                            pltpu.VMEM((MOE_HIDDEN, D_MODEL), BF16)]),
        out_shape=jax.ShapeDtypeStruct((rows, LANES), F32),
        compiler_params=_cparams(("arbitrary",)),
        name="experts",
    )(*sched, xs, wg, wu, wd)


def _combine_kernel(d1_ref, d2_ref, ys_ref, meta_ref, x1_ref, fg_ref, o_ref,
                    b1_ref, b2_ref, sem, *, tc, final):
    i = pl.program_id(0)
    n = pl.num_programs(0)
    slot = i % 2

    def gather(tile, slot):
        def start(r, carry):
            dst = pl.ds(r * ROW_PITCH, ROW_CHUNKS)
            for d_ref, b_ref in ((d1_ref, b1_ref), (d2_ref, b2_ref)):
                src = ys_ref.at[pl.ds(d_ref[tile * tc + r] * ROW_PITCH, ROW_CHUNKS)]
                pltpu.make_async_copy(src, b_ref.at[slot, dst], sem.at[slot]).start()
            return carry

        lax.fori_loop(0, tc, start, 0)

    @pl.when(i == 0)
    def _():
        gather(0, 0)

    @pl.when(i + 1 < n)
    def _():
        gather(i + 1, 1 - slot)

    for b_ref in (b1_ref, b2_ref):
        pltpu.make_async_copy(ys_ref.at[pl.ds(0, tc * ROW_CHUNKS)],
                              b_ref.at[slot, pl.ds(0, tc * ROW_CHUNKS)], sem.at[slot]).wait()
    w1 = meta_ref[:, 2:3]
    w2 = meta_ref[:, 3:4]
    parts = []
    for s in range(ROW_CHUNKS):
        y = (w1 * b1_ref[slot, pl.ds(s, tc, stride=ROW_PITCH), :]
             + w2 * b2_ref[slot, pl.ds(s, tc, stride=ROW_PITCH), :])
        parts.append(x1_ref[:, s * LANES:(s + 1) * LANES] + y)
    x2 = jnp.concatenate(parts, axis=1)
    if final:
        x2 = _rms(x2) * fg_ref[...]
    o_ref[...] = x2


def _combine(dest1, dest2, ys, meta, x1, fg, final):
    t = x1.shape[0]
    tc = 256
    return pl.pallas_call(
        functools.partial(_combine_kernel, tc=tc, final=final),
        grid_spec=pltpu.PrefetchScalarGridSpec(
            num_scalar_prefetch=2,
            grid=(t // tc,),
            in_specs=[pl.BlockSpec(memory_space=pl.ANY),
                      pl.BlockSpec((tc, LANES), lambda i, *_: (i, 0)),
                      pl.BlockSpec((tc, D_MODEL), lambda i, *_: (i, 0)),
                      pl.BlockSpec((1, D_MODEL), lambda i, *_: (0, 0))],
            out_specs=pl.BlockSpec((tc, D_MODEL), lambda i, *_: (i, 0)),
            scratch_shapes=[pltpu.VMEM((2, tc * ROW_PITCH, LANES), F32),
                            pltpu.VMEM((2, tc * ROW_PITCH, LANES), F32),
                            pltpu.SemaphoreType.DMA((2,))]),
        out_shape=jax.ShapeDtypeStruct((t, D_MODEL), F32),
        compiler_params=_cparams(("arbitrary",)),
        name="combine",
    )(dest1, dest2, ys, meta, x1, fg)


def _rope_tables(r0):
    inv = ROPE_THETA ** (-jnp.arange(0, 2 * ROPE_HALF, 2, dtype=F32) / (2 * ROPE_HALF))
    ang = jnp.arange(SEQ, dtype=F32)[:, None] * inv[None, :]
    cos, sin = jnp.cos(ang), jnp.sin(ang)
    c = jnp.ones((SEQ, LANES), F32).at[:, r0:r0 + ROPE_HALF].set(cos)
    c = c.at[:, r0 + ROPE_HALF:r0 + 2 * ROPE_HALF].set(cos)
    sa = jnp.zeros((SEQ, LANES), F32).at[:, r0:r0 + ROPE_HALF].set(-sin)
    sb = jnp.zeros((SEQ, LANES), F32).at[:, r0 + ROPE_HALF:r0 + 2 * ROPE_HALF].set(sin)
    return c, sa, sb


def _pad_cols(w, width):
    return jnp.pad(w, ((0, 0), (0, width - w.shape[1])))


def _s5_params(a_re, a_im, log_dt, b_re, b_im, c_re, c_im, batch):
    dt = jnp.exp(log_dt)[:, None]
    mag = jnp.exp(a_re * dt)
    abar_r, abar_i = mag * jnp.cos(a_im * dt), mag * jnp.sin(a_im * dt)
    den = a_re * a_re + a_im * a_im
    nr, ni = abar_r - 1.0, abar_i
    coef_r = (nr * a_re + ni * a_im) / den
    coef_i = (ni * a_re - nr * a_im) / den
    bbar_r = coef_r[..., None] * b_re - coef_i[..., None] * b_im
    bbar_i = coef_r[..., None] * b_im + coef_i[..., None] * b_re
    eye = jnp.eye(S5_GROUPS, dtype=F32)
    blk_b = lambda m: jnp.einsum('gpc,gh->gchp', m, eye).reshape(W_GROUP, S5_WIDTH)
    wb = jnp.concatenate([blk_b(bbar_r), blk_b(bbar_i)], axis=1).astype(BF16)
    blk_c = lambda m: jnp.einsum('gcp,gh->gphc', m, eye).reshape(S5_WIDTH, W_GROUP)
    wc = jnp.concatenate([blk_c(c_re), blk_c(-c_im)], axis=0).astype(BF16)
    ar = jnp.broadcast_to(abar_r.reshape(1, S5_WIDTH), (batch, S5_WIDTH))
    ai = jnp.broadcast_to(abar_i.reshape(1, S5_WIDTH), (batch, S5_WIDTH))
    return wb, ar, ai, wc


def _nsa_consts(ck):
    c_start = np.arange(N_CMP_PAD) * CMP_STRIDE
    b_start = np.arange(N_BLK) * SEL_BLOCK
    ov = ((c_start[None, :] < b_start[:, None] + SEL_BLOCK)
          & (c_start[None, :] + CMP_BLOCK > b_start[:, None])).astype(np.float32)
    key_blk = np.arange(SEQ) // SEL_BLOCK
    e = (key_blk[None, :] == np.arange(N_BLK)[:, None]).astype(np.float32)
    e = e.reshape(N_BLK, SEQ // ck, ck).transpose(1, 0, 2)
    return jnp.asarray(ov), jnp.asarray(e, dtype=BF16)


def _moe_schedule(counts, n_rows, tmx):
    cnt = counts.astype(jnp.int32)
    ends = jnp.cumsum(cnt)
    offs = ends - cnt
    n_tiles = n_rows // tmx
    nvis_max = n_tiles + MOE_EXPERTS
    first_tile = offs // tmx
    last_tile = jnp.maximum(ends - 1, 0) // tmx
    nvis = jnp.where(cnt > 0, last_tile - first_tile + 1, 0)
    cumv = jnp.cumsum(nvis)
    total = cumv[-1]
    v = jnp.arange(nvis_max, dtype=jnp.int32)
    vc = jnp.minimum(v, total - 1)
    ve = jnp.sum((cumv[None, :] <= vc[:, None]).astype(jnp.int32), axis=1)
    vt = first_tile[ve] + vc - (cumv[ve] - nvis[ve])
    vlo = jnp.clip(offs[ve] - vt * tmx, 0, tmx)
    vhi = jnp.clip(ends[ve] - vt * tmx, 0, tmx)
    vvalid = (v < total).astype(jnp.int32)
    vfirst = jnp.concatenate([jnp.ones((1,), jnp.int32), (vt[1:] != vt[:-1]).astype(jnp.int32)])
    return tuple(a.astype(jnp.int32) for a in (vt, ve, vlo, vhi, vfirst, vvalid))


def kernel(x, attn_norm_g, w_in, mla_q_norm_g, mla_kv_norm_g, mla_w_uq, mla_w_ukv, nsa_cmp_pe, nsa_cmp_w1, nsa_cmp_w2, pool_w, pool_b, pool_scale, s5_a_re, s5_a_im, s5_log_dt, s5_b_re, s5_b_im, s5_c_re, s5_c_im, s5_d, s5_glu_w, s5_glu_b, mix_norm_g, w_out, ffn_norm_g, moe_w_group, moe_b_group, moe_w_expert, moe_b_expert, moe_w_gate, moe_w_up, moe_w_down, final_norm_g):
    batch, seq, d = x.shape
    depth = w_in.shape[0]
    t = batch * seq
    xf = x.reshape(t, d)
    qtabs = _rope_tables(MLA_NOPE)
    ktabs = _rope_tables(0)
    ovt, emat = _nsa_consts(512)
    row = lambda v: v.reshape(1, -1)
    tmx = 256
    w_in_all = _w_in_prep(w_in)
    w_out_all = w_out.astype(BF16)

    for l in range(depth):
        cols = _in_proj(xf, row(attn_norm_g[l]), w_in_all, l)

        wuq = jnp.pad(mla_w_uq[l], ((0, 512 - MLA_Q_LORA), (0, 0))).astype(BF16)
        ukv = mla_w_ukv[l].reshape(HEAD_D, MLA_HEADS, MLA_NOPE + MLA_V)
        wk = jnp.pad(ukv[:, :, :MLA_NOPE], ((0, 0), (0, 0), (0, HEAD_D - MLA_NOPE)))
        wk = wk.reshape(HEAD_D, 512).astype(BF16)
        wv = ukv[:, :, MLA_NOPE:].reshape(HEAD_D, 512).astype(BF16)
        qg = jnp.pad(mla_q_norm_g[l], (0, 512 - MLA_Q_LORA)).reshape(1, 512)
        q_a, k_a, v_a = _mla_prep(cols, qg, row(mla_kv_norm_g[l]), wuq, wk, wv, qtabs, ktabs)
        y_a = _mla_attn(q_a, k_a, v_a, batch)

        q_b, kc, vc, kv_b, gates = _nsa_prep(cols, ktabs)
        pe = jnp.broadcast_to(nsa_cmp_pe[l].reshape(2, 1, CMP_BLOCK * HEAD_D),
                              (2, 8, CMP_BLOCK * HEAD_D)).astype(BF16)
        kcmp, vcmp = _nsa_compress(kc, vc, nsa_cmp_w1[l].astype(BF16), nsa_cmp_w2[l].astype(BF16),
                                   pe, batch)
        y_b = _nsa_attn(q_b, kcmp, vcmp, kv_b, gates, ovt, emat, batch)

        y_c = _pool(cols, pool_w[l].astype(BF16), row(pool_b[l]), row(pool_scale[l]), batch)

        wb, ar, ai, wc = _s5_params(s5_a_re[l], s5_a_im[l], s5_log_dt[l], s5_b_re[l], s5_b_im[l],
                                    s5_c_re[l], s5_c_im[l], batch)
        y_d = _s5(cols.reshape(batch, seq, N_IN_PAD), wb, ar, ai, wc, row(s5_d[l]),
                  s5_glu_w[l].astype(BF16), row(s5_glu_b[l])).reshape(t, W_GROUP)

        wr = jnp.concatenate([moe_w_group[l], moe_w_expert[l]], axis=1)
        wr = _pad_cols(wr, LANES)
        wr_hi = wr.astype(BF16)
        wr_lo = (wr - wr_hi.astype(F32)).astype(BF16)
        br = jnp.pad(jnp.concatenate([moe_b_group[l], moe_b_expert[l]]), (0, LANES - 36)).reshape(1, LANES)
        x1, h2, logits = _out_proj(y_a, y_b, y_c, y_d, row(mix_norm_g[l]), w_out_all, l,
                                   xf, row(ffn_norm_g[l]), wr_hi, wr_lo, br)

        meta, counts = _router(logits)
        dest = _dest(meta, counts)
        dest1, dest2 = dest[:, 0], dest[:, 1]
        sched = _moe_schedule(counts[-1, MOE_GROUPS:MOE_GROUPS + MOE_EXPERTS], 2 * t, tmx)
        xs = _dispatch(dest1, dest2, h2)
        ys = _experts(sched, xs, moe_w_gate, moe_w_up, moe_w_down, l, tmx)
        xf = _combine(dest1, dest2, ys, meta, x1, row(final_norm_g), final=(l == depth - 1))

    return xf.reshape(batch, seq, d)
```

```python
import functools
import math

import numpy as np
import jax
import jax.numpy as jnp
from jax import lax
from jax.experimental import pallas as pl
from jax.experimental.pallas import tpu as pltpu

F32 = jnp.float32
BF16 = jnp.bfloat16

D_MODEL = 2048
SEQ = 2048
W_GROUP = 512
LANES = 128
ROW_CHUNKS = D_MODEL // LANES
ROW_PITCH = ROW_CHUNKS + 1
DMA_UNROLL = 4

ROPE_THETA = 500000.0
ROPE_HALF = 16
NEG_INF = -1.0e30
FORCE_SCORE = 1.0e4
EPS = 1e-6

MLA_HEADS = 4
MLA_Q_LORA = 448
MLA_NOPE = 96
MLA_V = 128
HEAD_D = 128

CMP_BLOCK = 32
CMP_STRIDE = 16
SEL_BLOCK = 64
N_SEL = 8
N_LOCAL = 2
WINDOW = 512
N_CMP_PAD = SEQ // CMP_STRIDE
N_BLK = SEQ // SEL_BLOCK

POOL_SIZES = (2, 4, 8, 16)
S5_GROUPS = 32
S5_CH = 16
S5_STATE = 64
S5_WIDTH = S5_GROUPS * S5_STATE

MOE_GROUPS = 4
MOE_EPG = 8
MOE_EXPERTS = 32
MOE_HIDDEN = 512

COL_CQ, COL_NQ, COL_POOL, COL_S5 = 0, 512, 1024, 1536
COL_CKV, COL_KR, COL_KV6, COL_GL = 2048, 2176, 2304, 3072
N_IN_PAD = 3200

VMEM_LIMIT = 56 * 1024 * 1024


def _cparams(sem, vmem=VMEM_LIMIT):
    return pltpu.CompilerParams(dimension_semantics=sem, vmem_limit_bytes=vmem)


def _rms(x, n=None):
    n = x.shape[-1] if n is None else n
    return x * lax.rsqrt(jnp.sum(x * x, axis=-1, keepdims=True) / n + EPS)


def _dot(a, b):
    return jnp.dot(a, b, preferred_element_type=F32)


def _dot_nt(a, b, precision=None):
    return lax.dot_general(a, b, (((1,), (1,)), ((), ())), preferred_element_type=F32,
                           precision=precision)


def _rope(x, c, sa, sb):
    return x * c + pltpu.roll(x, LANES - ROPE_HALF, 1) * sa + pltpu.roll(x, ROPE_HALF, 1) * sb


def _in_proj_kernel(x_ref, g_ref, w_ref, o_ref):
    h = (_rms(x_ref[...]) * g_ref[...]).astype(BF16)
    o_ref[...] = _dot(h, w_ref[...]).astype(BF16)


_W_IN_SEGMENTS = ((COL_CQ, 0, 448), (COL_NQ, 608, 512), (COL_POOL, 1900, 512), (COL_S5, 2412, 512),
                  (COL_CKV, 448, 128), (COL_KR, 576, 32), (COL_KV6, 1120, 768), (COL_GL, 1888, 12))


def _w_in_prep_kernel(w_ref, o_ref):
    o_ref[0] = jnp.zeros(o_ref.shape[1:], BF16)
    for dst, src, width in _W_IN_SEGMENTS:
        o_ref[0, :, dst:dst + width] = w_ref[0, :, src:src + width].astype(BF16)


def _w_in_prep(w_in):
    depth, d, n = w_in.shape
    tk = 256
    return pl.pallas_call(
        _w_in_prep_kernel,
        grid=(depth, d // tk),
        in_specs=[pl.BlockSpec((1, tk, n), lambda l, k: (l, k, 0))],
        out_specs=pl.BlockSpec((1, tk, N_IN_PAD), lambda l, k: (l, k, 0)),
        out_shape=jax.ShapeDtypeStruct((depth, d, N_IN_PAD), BF16),
        compiler_params=_cparams(("parallel", "parallel")),
        name="w_in_prep",
    )(w_in)


def _in_proj(x, g, w_all, layer):
    t = x.shape[0]
    tm = 512
    return pl.pallas_call(
        _in_proj_kernel,
        grid=(t // tm,),
        in_specs=[pl.BlockSpec((tm, D_MODEL), lambda i: (i, 0)),
                  pl.BlockSpec((1, D_MODEL), lambda i: (0, 0)),
                  pl.BlockSpec((None, D_MODEL, N_IN_PAD), lambda i: (layer, 0, 0))],
        out_specs=pl.BlockSpec((tm, N_IN_PAD), lambda i: (i, 0)),
        out_shape=jax.ShapeDtypeStruct((t, N_IN_PAD), BF16),
        compiler_params=_cparams(("parallel",)),
        name="in_proj",
    )(x, g, w_all)


def _mla_prep_kernel(cq_ref, ckv_ref, kr_ref, qg_ref, kvg_ref, wuq_ref, wk_ref, wv_ref,
                     qc_ref, qsa_ref, qsb_ref, kc_ref, ksa_ref, ksb_ref,
                     q_out, k_out, v_out):
    scale = 1.0 / math.sqrt(HEAD_D)
    qn = (_rms(cq_ref[...].astype(F32), MLA_Q_LORA) * qg_ref[...]).astype(BF16)
    q = _dot(qn, wuq_ref[...])
    kvn = (_rms(ckv_ref[...].astype(F32)) * kvg_ref[...]).astype(BF16)
    kn = _dot(kvn, wk_ref[...])
    v_out[...] = _dot(kvn, wv_ref[...]).astype(BF16)
    kr = _rope(kr_ref[...].astype(F32), kc_ref[...], ksa_ref[...], ksb_ref[...])
    kr = pltpu.roll(kr, MLA_NOPE, 1)
    for h in range(MLA_HEADS):
        sl = slice(h * HEAD_D, (h + 1) * HEAD_D)
        qh = _rope(q[:, sl], qc_ref[...], qsa_ref[...], qsb_ref[...])
        q_out[:, sl] = (qh * scale).astype(BF16)
        k_out[:, sl] = (kn[:, sl] + kr).astype(BF16)


def _mla_prep(cols, qg, kvg, wuq, wk, wv, qtabs, ktabs):
    t = cols.shape[0]
    tm = 512
    nsb = SEQ // tm
    tab = pl.BlockSpec((tm, LANES), lambda i: (i % nsb, 0))
    full = lambda shape: pl.BlockSpec(shape, lambda i: (0, 0))
    out = pl.BlockSpec((tm, W_GROUP), lambda i: (i, 0))
    return pl.pallas_call(
        _mla_prep_kernel,
        grid=(t // tm,),
        in_specs=[pl.BlockSpec((tm, 512), lambda i: (i, COL_CQ // 512)),
                  pl.BlockSpec((tm, LANES), lambda i: (i, COL_CKV // LANES)),
                  pl.BlockSpec((tm, LANES), lambda i: (i, COL_KR // LANES)),
                  full((1, 512)), full((1, LANES)),
                  full((512, 512)), full((LANES, 512)), full((LANES, 512)),
                  tab, tab, tab, tab, tab, tab],
        out_specs=[out, out, out],
        out_shape=[jax.ShapeDtypeStruct((t, W_GROUP), BF16)] * 3,
        compiler_params=_cparams(("parallel",)),
        name="mla_prep",
    )(cols, cols, cols, qg, kvg, wuq, wk, wv, *qtabs, *ktabs)


def _fold_lanes(a, op):
    out = a[:, 0:LANES]
    for c in range(1, a.shape[1] // LANES):
        out = op(out, a[:, c * LANES:(c + 1) * LANES])
    return out


def _mla_attn_kernel(q_ref, k_ref, v_ref, o_ref, s_ref, mx_ref, ls_ref, acc_ref, *, tq):
    i = pl.program_id(1)
    d0 = pl.multiple_of(i * tq, tq)
    row = lax.broadcasted_iota(jnp.int32, (tq, tq), 0)
    col = lax.broadcasted_iota(jnp.int32, (tq, tq), 1)
    heads = [slice(h * HEAD_D, (h + 1) * HEAD_D) for h in range(MLA_HEADS)]
    tile = lambda j: pl.ds(pl.multiple_of(j * tq, tq), tq)

    for h, sl in enumerate(heads):
        s = jnp.where(col <= row, _dot_nt(q_ref[:, sl], k_ref[pl.ds(d0, tq), sl]), NEG_INF)
        s_ref[h, i] = s
        mx_ref[h] = _fold_lanes(s, jnp.maximum)

    def scores(j, carry):
        for h, sl in enumerate(heads):
            s = _dot_nt(q_ref[:, sl], k_ref[tile(j), sl])
            s_ref[h, j] = s
            mx_ref[h] = jnp.maximum(mx_ref[h], _fold_lanes(s, jnp.maximum))
        return carry

    lax.fori_loop(0, i, scores, 0)
    for h in range(MLA_HEADS):
        m = jnp.max(mx_ref[h], axis=-1, keepdims=True)
        mx_ref[h] = jnp.broadcast_to(m, (tq, LANES))
        ls_ref[h] = jnp.zeros((tq, LANES), F32)
        acc_ref[h] = jnp.zeros((tq, HEAD_D), F32)

    def values(j, carry):
        for h, sl in enumerate(heads):
            m = mx_ref[h]
            s = s_ref[h, j]
            p = jnp.concatenate([jnp.exp(s[:, c * LANES:(c + 1) * LANES] - m)
                                 for c in range(tq // LANES)], axis=1)
            ls_ref[h] = ls_ref[h] + _fold_lanes(p, jnp.add)
            acc_ref[h] = acc_ref[h] + _dot(p.astype(BF16), v_ref[tile(j), sl])
        return carry

    lax.fori_loop(0, i + 1, values, 0)
    for h, sl in enumerate(heads):
        o_ref[:, sl] = acc_ref[h] / jnp.sum(ls_ref[h], axis=-1, keepdims=True)


def _mla_attn(q, k, v, batch):
    tq = 256
    nq = SEQ // tq
    return pl.pallas_call(
        functools.partial(_mla_attn_kernel, tq=tq),
        grid=(batch, nq),
        in_specs=[pl.BlockSpec((tq, W_GROUP), lambda b, i: (b * nq + i, 0)),
                  pl.BlockSpec((SEQ, W_GROUP), lambda b, i: (b, 0)),
                  pl.BlockSpec((SEQ, W_GROUP), lambda b, i: (b, 0))],
        out_specs=pl.BlockSpec((tq, W_GROUP), lambda b, i: (b * nq + i, 0)),
        out_shape=jax.ShapeDtypeStruct((batch * SEQ, W_GROUP), F32),
        scratch_shapes=[pltpu.VMEM((MLA_HEADS, nq, tq, tq), F32),
                        pltpu.VMEM((MLA_HEADS, tq, LANES), F32),
                        pltpu.VMEM((MLA_HEADS, tq, LANES), F32),
                        pltpu.VMEM((MLA_HEADS, tq, HEAD_D), F32)],
        compiler_params=_cparams(("parallel", "arbitrary")),
        name="mla_attn",
    )(q, k, v)


def _nsa_prep_kernel(q_ref, kv_ref, gl_ref, c_ref, sa_ref, sb_ref,
                     q_out, kc_out, vc_out, kv_out, g_out):
    scale = 1.0 / math.sqrt(HEAD_D)
    c, sa, sb = c_ref[...], sa_ref[...], sb_ref[...]
    for h in range(4):
        sl = slice(h * HEAD_D, (h + 1) * HEAD_D)
        q_out[:, sl] = (_rope(q_ref[:, sl].astype(F32), c, sa, sb) * scale).astype(BF16)
    kv = lambda n: kv_ref[:, n * HEAD_D:(n + 1) * HEAD_D].astype(F32)
    kc_out[...] = _rope(kv(0), c, sa, sb)
    vc_out[...] = kv(1)
    kv_out[:, 0:128] = _rope(kv(2), c, sa, sb).astype(BF16)
    kv_out[:, 128:256] = kv(3).astype(BF16)
    kv_out[:, 256:384] = _rope(kv(4), c, sa, sb).astype(BF16)
    kv_out[:, 384:512] = kv(5).astype(BF16)
    g_out[...] = jax.nn.sigmoid(gl_ref[...].astype(F32))


def _nsa_prep(cols, tabs):
    t = cols.shape[0]
    tm = 512
    nsb = SEQ // tm
    tab = pl.BlockSpec((tm, LANES), lambda i: (i % nsb, 0))
    return pl.pallas_call(
        _nsa_prep_kernel,
        grid=(t // tm,),
        in_specs=[pl.BlockSpec((tm, 512), lambda i: (i, COL_NQ // 512)),
                  pl.BlockSpec((tm, 768), lambda i: (i, COL_KV6 // 768)),
                  pl.BlockSpec((tm, LANES), lambda i: (i, COL_GL // LANES)),
                  tab, tab, tab],
        out_specs=[pl.BlockSpec((tm, 512), lambda i: (i, 0)),
                   pl.BlockSpec((tm, LANES), lambda i: (i, 0)),
                   pl.BlockSpec((tm, LANES), lambda i: (i, 0)),
                   pl.BlockSpec((tm, 512), lambda i: (i, 0)),
                   pl.BlockSpec((tm, LANES), lambda i: (i, 0))],
        out_shape=[jax.ShapeDtypeStruct((t, 512), BF16),
                   jax.ShapeDtypeStruct((t, LANES), F32),
                   jax.ShapeDtypeStruct((t, LANES), F32),
                   jax.ShapeDtypeStruct((t, 512), BF16),
                   jax.ShapeDtypeStruct((t, LANES), F32)],
        compiler_params=_cparams(("parallel",)),
        name="nsa_prep",
    )(cols, cols, cols, *tabs)


def _nsa_compress_kernel(xk_ref, xv_ref, w1_ref, w2_ref, pe_ref, k_out, v_out):
    for c, (x_ref, o_ref) in enumerate(((xk_ref, k_out), (xv_ref, v_out))):
        a = jnp.zeros((N_CMP_PAD, HEAD_D), F32)
        b = jnp.zeros((N_CMP_PAD, HEAD_D), F32)
        for r in range(CMP_STRIDE):
            x = x_ref[pl.ds(r, N_CMP_PAD, stride=CMP_STRIDE), :].astype(BF16)
            a = a + _dot(x, w1_ref[c, r * HEAD_D:(r + 1) * HEAD_D, :])
            b = b + _dot(x, w1_ref[c, (CMP_STRIDE + r) * HEAD_D:(CMP_STRIDE + r + 1) * HEAD_D, :])
        b = pltpu.roll(b, N_CMP_PAD - 1, 0)
        pe = _dot(pe_ref[c], w1_ref[c])[0:1, :]
        hid = jax.nn.gelu(a + b + pe)
        o_ref[0] = _dot(hid.astype(BF16), w2_ref[c]).astype(BF16)


def _nsa_compress(xk, xv, w1, w2, pe, batch):
    xspec = pl.BlockSpec((SEQ, HEAD_D), lambda b: (b, 0))
    ospec = pl.BlockSpec((1, N_CMP_PAD, HEAD_D), lambda b: (b, 0, 0))
    return pl.pallas_call(
        _nsa_compress_kernel,
        grid=(batch,),
        in_specs=[xspec, xspec,
                  pl.BlockSpec((2, CMP_BLOCK * HEAD_D, HEAD_D), lambda b: (0, 0, 0)),
                  pl.BlockSpec((2, HEAD_D, HEAD_D), lambda b: (0, 0, 0)),
                  pl.BlockSpec((2, 8, CMP_BLOCK * HEAD_D), lambda b: (0, 0, 0))],
        out_specs=[ospec, ospec],
        out_shape=[jax.ShapeDtypeStruct((batch, N_CMP_PAD, HEAD_D), BF16)] * 2,
        compiler_params=_cparams(("parallel",)),
        name="nsa_compress",
    )(xk, xv, w1, w2, pe)


def _softmax_rows(s):
    m = jnp.max(s, axis=-1, keepdims=True)
    p = jnp.exp(s - m)
    return p / jnp.sum(p, axis=-1, keepdims=True)


def _nsa_attn_kernel(q_ref, kc_ref, vc_ref, kv_ref, g_ref, ovt_ref, e_ref, o_ref,
                     m_ref, l_ref, acc_ref, s_ref, *, tq, ck):
    i = pl.program_id(1)
    q0 = i * tq
    nh = 4
    qs = jnp.concatenate([q_ref[:, h * HEAD_D:(h + 1) * HEAD_D] for h in range(nh)], axis=0)
    qpos = q0 + lax.broadcasted_iota(jnp.int32, (tq, 1), 0)
    stack = lambda a: jnp.concatenate([a] * nh, axis=0)

    sc = _dot_nt(qs, kc_ref[0])
    n_idx = lax.broadcasted_iota(jnp.int32, (tq, N_CMP_PAD), 1)
    valid_c = n_idx * CMP_STRIDE + (CMP_BLOCK - 1) <= qpos
    valid_c4 = stack(valid_c)
    pc = _softmax_rows(jnp.where(valid_c4, sc, NEG_INF))
    pc = jnp.where(valid_c4, pc, 0.0)
    o_c = _dot(pc.astype(BF16), vc_ref[0])
    psum = pc[0:tq] + pc[tq:2 * tq] + pc[2 * tq:3 * tq] + pc[3 * tq:4 * tq]

    imp_t = _dot_nt(ovt_ref[...], psum, precision=lax.Precision.HIGHEST)
    kblk = lax.broadcasted_iota(jnp.int32, (N_BLK, tq), 0)
    cur = (q0 + lax.broadcasted_iota(jnp.int32, (N_BLK, tq), 1)) // SEL_BLOCK
    forced = (kblk == 0) | ((kblk <= cur) & (kblk > cur - N_LOCAL))
    score = jnp.where(forced, FORCE_SCORE, jnp.where(kblk <= cur, imp_t, -1.0))
    cnt = jnp.zeros((N_BLK, tq), F32)
    for j in range(N_BLK):
        sj = score[j:j + 1, :]
        beats = (sj > score) | ((sj == score) & (kblk > j))
        cnt = cnt + beats.astype(F32)
    sel = jnp.transpose((cnt < N_SEL).astype(F32)).astype(BF16)

    wlen = WINDOW + tq
    w0 = pl.multiple_of(jnp.maximum(q0 - WINDOW, 0), tq)
    sw = _dot_nt(qs, kv_ref[pl.ds(w0, wlen), 256:384])
    kpos = w0 + lax.broadcasted_iota(jnp.int32, (tq, wlen), 1)
    ok = (kpos <= qpos) & (kpos > qpos - WINDOW)
    sw = sw + stack(jnp.where(ok, 0.0, NEG_INF))
    pw = jnp.exp(sw - jnp.max(_fold_lanes(sw, jnp.maximum), axis=-1, keepdims=True))
    lw = jnp.sum(_fold_lanes(pw, jnp.add), axis=-1, keepdims=True)
    o_w = _dot(pw.astype(BF16), kv_ref[pl.ds(w0, wlen), 384:512]) / lw

    nck = q0 // ck + 1
    m_ref[...] = jnp.full(m_ref.shape, NEG_INF, F32)

    def scores(c, carry):
        r0 = pl.multiple_of(c * ck, ck)
        chosen = _dot(sel, e_ref[c])
        kpos = r0 + lax.broadcasted_iota(jnp.int32, (tq, ck), 1)
        ok = (chosen > 0.5) & (kpos <= qpos)
        s = _dot_nt(qs, kv_ref[pl.ds(r0, ck), 0:128]) + stack(jnp.where(ok, 0.0, NEG_INF))
        s_ref[c] = s
        m_ref[...] = jnp.maximum(m_ref[...], _fold_lanes(s, jnp.maximum))
        return carry

    lax.fori_loop(0, nck, scores, 0)
    m_ref[...] = jnp.broadcast_to(jnp.max(m_ref[...], axis=-1, keepdims=True), m_ref.shape)
    l_ref[...] = jnp.zeros(l_ref.shape, F32)
    acc_ref[...] = jnp.zeros(acc_ref.shape, F32)

    def values(c, carry):
        r0 = pl.multiple_of(c * ck, ck)
        m = m_ref[...]
        s = s_ref[c]
        p = jnp.concatenate([jnp.exp(s[:, k * LANES:(k + 1) * LANES] - m)
                             for k in range(ck // LANES)], axis=1)
        l_ref[...] = l_ref[...] + _fold_lanes(p, jnp.add)
        acc_ref[...] = acc_ref[...] + _dot(p.astype(BF16), kv_ref[pl.ds(r0, ck), 128:256])
        return carry

    lax.fori_loop(0, nck, values, 0)
    o_s = acc_ref[...] / jnp.sum(l_ref[...], axis=-1, keepdims=True)

    g = g_ref[...]
    for h in range(nh):
        rs = slice(h * tq, (h + 1) * tq)
        o_ref[:, h * HEAD_D:(h + 1) * HEAD_D] = (
            g[:, 3 * h:3 * h + 1] * o_c[rs] + g[:, 3 * h + 1:3 * h + 2] * o_s[rs]
            + g[:, 3 * h + 2:3 * h + 3] * o_w[rs])


def _nsa_attn(q, kcmp, vcmp, kv, gates, ovt, emat, batch):
    tq, ck = 256, 512
    nq = SEQ // tq
    return pl.pallas_call(
        functools.partial(_nsa_attn_kernel, tq=tq, ck=ck),
        grid=(batch, nq),
        in_specs=[pl.BlockSpec((tq, 512), lambda b, i: (b * nq + i, 0)),
                  pl.BlockSpec((1, N_CMP_PAD, HEAD_D), lambda b, i: (b, 0, 0)),
                  pl.BlockSpec((1, N_CMP_PAD, HEAD_D), lambda b, i: (b, 0, 0)),
                  pl.BlockSpec((SEQ, 512), lambda b, i: (b, 0)),
                  pl.BlockSpec((tq, LANES), lambda b, i: (b * nq + i, 0)),
                  pl.BlockSpec((N_BLK, N_CMP_PAD), lambda b, i: (0, 0)),
                  pl.BlockSpec((SEQ // ck, N_BLK, ck), lambda b, i: (0, 0, 0))],
        out_specs=pl.BlockSpec((tq, 512), lambda b, i: (b * nq + i, 0)),
        out_shape=jax.ShapeDtypeStruct((batch * SEQ, 512), F32),
        scratch_shapes=[pltpu.VMEM((4 * tq, LANES), F32), pltpu.VMEM((4 * tq, LANES), F32),
                        pltpu.VMEM((4 * tq, HEAD_D), F32),
                        pltpu.VMEM((SEQ // ck, 4 * tq, ck), F32)],
        compiler_params=_cparams(("parallel", "arbitrary")),
        name="nsa_attn",
    )(q, kcmp, vcmp, kv, gates, ovt, emat)


def _pool_kernel(u_ref, w_ref, b_ref, s_ref, o_ref, pad_ref):
    maxw = POOL_SIZES[-1]
    pad_ref[0:maxw, :] = jnp.zeros((maxw, W_GROUP), F32)
    pad_ref[maxw:maxw + SEQ, :] = u_ref[...].astype(F32)
    rc = 512
    for g, w in enumerate(POOL_SIZES):
        sl = slice(g * LANES, (g + 1) * LANES)
        for r in range(SEQ // rc):
            acc = pad_ref[maxw + r * rc:maxw + (r + 1) * rc, sl]
            tok = acc
            for j in range(1, w):
                acc = acc + pad_ref[maxw - j + r * rc:maxw - j + (r + 1) * rc, sl]
            t = r * rc + lax.broadcasted_iota(jnp.int32, (rc, 1), 0)
            cnt = jnp.minimum(t + 1, w).astype(F32)
            d = acc / cnt - tok
            y = _dot(d.astype(BF16), w_ref[g])
            o_ref[r * rc:(r + 1) * rc, sl] = (y + b_ref[:, sl]) * s_ref[:, sl]


def _pool(cols, w, b, s, batch):
    return pl.pallas_call(
        _pool_kernel,
        grid=(batch,),
        in_specs=[pl.BlockSpec((SEQ, W_GROUP), lambda i: (i, COL_POOL // W_GROUP)),
                  pl.BlockSpec((4, LANES, LANES), lambda i: (0, 0, 0)),
                  pl.BlockSpec((1, W_GROUP), lambda i: (0, 0)),
                  pl.BlockSpec((1, W_GROUP), lambda i: (0, 0))],
        out_specs=pl.BlockSpec((SEQ, W_GROUP), lambda i: (i, 0)),
        out_shape=jax.ShapeDtypeStruct((batch * SEQ, W_GROUP), F32),
        scratch_shapes=[pltpu.VMEM((SEQ + POOL_SIZES[-1], W_GROUP), F32)],
        compiler_params=_cparams(("parallel",)),
        name="pool",
    )(cols, w, b, s)


def _s5_kernel(u_ref, wb_ref, ar_ref, ai_ref, wc_ref, d_ref, gw_ref, gb_ref, o_ref,
               bu_ref, st_ref, tm_ref, *, batch, tc):
    @pl.when(pl.program_id(0) == 0)
    def _():
        st_ref[...] = jnp.zeros(st_ref.shape, F32)

    nslab = W_GROUP // LANES
    for b in range(batch):
        for j in range(nslab):
            tm_ref[j, pl.ds(b, tc, stride=batch), :] = u_ref[b, :, j * LANES:(j + 1) * LANES].astype(F32)
    u = jnp.concatenate([tm_ref[j] for j in range(nslab)], axis=1)
    gl = 8 * S5_STATE
    ub = u.astype(BF16)
    for j in range(nslab):
        uj = ub[:, j * LANES:(j + 1) * LANES]
        for part in (0, S5_WIDTH):
            cs = slice(part + j * gl, part + (j + 1) * gl)
            bu_ref[:, cs] = _dot(uj, wb_ref[j * LANES:(j + 1) * LANES, cs])
    lc = 512
    unroll = 8
    for c in range(S5_WIDTH // lc):
        re = slice(c * lc, (c + 1) * lc)
        im = slice(S5_WIDTH + c * lc, S5_WIDTH + (c + 1) * lc)
        ar, ai = ar_ref[:, re], ai_ref[:, re]

        def body(tb, carry, re=re, im=im, ar=ar, ai=ai):
            xr, xi = carry
            for k in range(unroll):
                r0 = pl.multiple_of((tb * unroll + k) * batch, batch)
                nxr = ar * xr - ai * xi + bu_ref[pl.ds(r0, batch), re]
                nxi = ar * xi + ai * xr + bu_ref[pl.ds(r0, batch), im]
                bu_ref[pl.ds(r0, batch), re] = nxr
                bu_ref[pl.ds(r0, batch), im] = nxi
                xr, xi = nxr, nxi
            return xr, xi

        xr, xi = lax.fori_loop(0, tc // unroll, body, (st_ref[:, re], st_ref[:, im]))
        st_ref[:, re] = xr
        st_ref[:, im] = xi

    ys = []
    for j in range(nslab):
        osl = slice(j * LANES, (j + 1) * LANES)
        yj = 0.0
        for part in (0, S5_WIDTH):
            cs = slice(part + j * gl, part + (j + 1) * gl)
            yj = yj + _dot(bu_ref[:, cs].astype(BF16), wc_ref[cs, osl])
        ys.append(yj)
    y = jnp.concatenate(ys, axis=1) + d_ref[...] * u
    y = jax.nn.gelu(y)
    z = _dot(y.astype(BF16), gw_ref[...]) + gb_ref[...]
    o = y * jax.nn.sigmoid(z)
    for j in range(nslab):
        tm_ref[j] = o[:, j * LANES:(j + 1) * LANES]
    for b in range(batch):
        for j in range(nslab):
            o_ref[b, :, j * LANES:(j + 1) * LANES] = tm_ref[j, pl.ds(b, tc, stride=batch), :]


def _s5(cols3, wb, ar, ai, wc, d, gw, gb):
    batch = cols3.shape[0]
    tc = 64
    rows = tc * batch
    full = lambda shape: pl.BlockSpec(shape, lambda i: (0, 0))
    return pl.pallas_call(
        functools.partial(_s5_kernel, batch=batch, tc=tc),
        grid=(SEQ // tc,),
        in_specs=[pl.BlockSpec((batch, tc, W_GROUP), lambda i: (0, i, COL_S5 // W_GROUP)),
                  full((W_GROUP, 2 * S5_WIDTH)), full((batch, S5_WIDTH)), full((batch, S5_WIDTH)),
                  full((2 * S5_WIDTH, W_GROUP)), full((1, W_GROUP)),
                  full((W_GROUP, W_GROUP)), full((1, W_GROUP))],
        out_specs=pl.BlockSpec((batch, tc, W_GROUP), lambda i: (0, i, 0)),
        out_shape=jax.ShapeDtypeStruct((batch, SEQ, W_GROUP), F32),
        scratch_shapes=[pltpu.VMEM((rows, 2 * S5_WIDTH), F32),
                        pltpu.VMEM((batch, 2 * S5_WIDTH), F32),
                        pltpu.VMEM((W_GROUP // LANES, rows, LANES), F32)],
        compiler_params=_cparams(("arbitrary",)),
        name="s5",
    )(cols3, wb, ar, ai, wc, d, gw, gb)


def _out_proj_kernel(ya_ref, yb_ref, yc_ref, yd_ref, mg_ref, wo_ref, x_ref, fg_ref,
                     wrh_ref, wrl_ref, br_ref, x1_ref, h2_ref, lg_ref, *, tm):
    acc = x_ref[...]
    for gi, y_ref in enumerate((ya_ref, yb_ref, yc_ref, yd_ref)):
        sl = slice(gi * W_GROUP, (gi + 1) * W_GROUP)
        n = (_rms(y_ref[...]) * mg_ref[:, sl]).astype(BF16)
        acc = acc + _dot(n, wo_ref[sl, :])
    x1_ref[...] = acc
    h2 = _rms(acc) * fg_ref[...]
    hi = h2.astype(BF16)
    lo = (h2 - hi.astype(F32)).astype(BF16)
    lg_ref[...] = (_dot(hi, wrh_ref[...]) + _dot(hi, wrl_ref[...]) + _dot(lo, wrh_ref[...])
                   + br_ref[...])
    for s in range(ROW_CHUNKS):
        h2_ref[pl.ds(s, tm, stride=ROW_PITCH), :] = h2[:, s * LANES:(s + 1) * LANES]
    h2_ref[pl.ds(ROW_CHUNKS, tm, stride=ROW_PITCH), :] = jnp.zeros((tm, LANES), F32)


def _out_proj(ya, yb, yc, yd, mg, wo_all, layer, x, fg, wr_hi, wr_lo, br):
    t = x.shape[0]
    tm = 256
    yspec = pl.BlockSpec((tm, W_GROUP), lambda i: (i, 0))
    full = lambda shape: pl.BlockSpec(shape, lambda i: (0, 0))
    return pl.pallas_call(
        functools.partial(_out_proj_kernel, tm=tm),
        grid=(t // tm,),
        in_specs=[yspec, yspec, yspec, yspec, full((1, D_MODEL)),
                  pl.BlockSpec((None, D_MODEL, D_MODEL), lambda i: (layer, 0, 0)),
                  pl.BlockSpec((tm, D_MODEL), lambda i: (i, 0)), full((1, D_MODEL)),
                  full((D_MODEL, LANES)), full((D_MODEL, LANES)), full((1, LANES))],
        out_specs=[pl.BlockSpec((tm, D_MODEL), lambda i: (i, 0)),
                   pl.BlockSpec((tm * ROW_PITCH, LANES), lambda i: (i, 0)),
                   pl.BlockSpec((tm, LANES), lambda i: (i, 0))],
        out_shape=[jax.ShapeDtypeStruct((t, D_MODEL), F32),
                   jax.ShapeDtypeStruct((t * ROW_PITCH, LANES), F32),
                   jax.ShapeDtypeStruct((t, LANES), F32)],
        compiler_params=_cparams(("parallel",)),
        name="out_proj",
    )(ya, yb, yc, yd, mg, wo_all, x, fg, wr_hi, wr_lo, br)


def _router_kernel(lg_ref, meta_ref, cnt_ref, carry_ref, *, tm):
    @pl.when(pl.program_id(0) == 0)
    def _():
        carry_ref[...] = jnp.zeros(carry_ref.shape, F32)

    lg = lg_ref[...]
    lane = lax.broadcasted_iota(jnp.int32, (tm, LANES), 1)
    big = jnp.int32(1 << 20)
    rmax = lambda a: jnp.max(a, axis=-1, keepdims=True)
    rmin = lambda a: jnp.min(a, axis=-1, keepdims=True)
    rsum = lambda a: jnp.sum(a, axis=-1, keepdims=True)

    is_g = lane < MOE_GROUPS
    gl = jnp.where(is_g, lg, NEG_INF)
    gm = rmax(gl)
    p_top = 1.0 / rsum(jnp.where(is_g, jnp.exp(gl - gm), 0.0))
    g_top = rmin(jnp.where(is_g & (gl == gm), lane, big))

    is_e = (lane >= MOE_GROUPS) & (lane < MOE_GROUPS + MOE_EXPERTS) \
        & (((lane - MOE_GROUPS) // MOE_EPG) == g_top)
    el = jnp.where(is_e, lg, NEG_INF)
    ee = jnp.where(is_e, jnp.exp(el - rmax(el)), 0.0)
    p = jnp.where(is_e, ee / rsum(ee), -1.0)
    p1 = rmax(p)
    i1 = rmin(jnp.where(p == p1, lane, big))
    p_rest = jnp.where(lane == i1, -1.0, p)
    p2 = rmax(p_rest)
    i2 = rmin(jnp.where((p_rest == p2) & is_e & (lane != i1), lane, big))
    den = p1 + p2
    w1 = p_top * (p1 / den)
    w2 = p_top * (p2 / den)

    hit1, hit2 = lane == i1, lane == i2
    oh = (hit1 | hit2).astype(F32)
    r = lax.broadcasted_iota(jnp.int32, (tm, tm), 0)
    c = lax.broadcasted_iota(jnp.int32, (tm, tm), 1)
    before = _dot((c < r).astype(BF16), oh.astype(BF16)) + carry_ref[0:1, :]
    r1 = rsum(jnp.where(hit1, before, 0.0))
    r2 = rsum(jnp.where(hit2, before, 0.0))
    carry_ref[...] = carry_ref[...] + jnp.sum(oh, axis=0, keepdims=True)
    cnt_ref[...] = carry_ref[...]

    e1 = (i1 - MOE_GROUPS).astype(F32)
    e2 = (i2 - MOE_GROUPS).astype(F32)
    vals = (e1, e2, w1, w2, r1, r2)
    meta = jnp.zeros((tm, LANES), F32)
    for k, v in enumerate(vals):
        meta = jnp.where(lane == k, v, meta)
    meta_ref[...] = meta


def _router(logits):
    t = logits.shape[0]
    tm = 512
    return pl.pallas_call(
        functools.partial(_router_kernel, tm=tm),
        grid=(t // tm,),
        in_specs=[pl.BlockSpec((tm, LANES), lambda i: (i, 0))],
        out_specs=[pl.BlockSpec((tm, LANES), lambda i: (i, 0)),
                   pl.BlockSpec((8, LANES), lambda i: (i, 0))],
        out_shape=[jax.ShapeDtypeStruct((t, LANES), F32),
                   jax.ShapeDtypeStruct((t // tm * 8, LANES), F32)],
        scratch_shapes=[pltpu.VMEM((8, LANES), F32)],
        compiler_params=_cparams(("arbitrary",)),
        name="router",
    )(logits)


def _dest_kernel(meta_ref, cnt_ref, o_ref, *, tm):
    cnt = cnt_ref[...]
    hi = jnp.floor(cnt * (1.0 / 256.0))
    lo = cnt - 256.0 * hi
    r = lax.broadcasted_iota(jnp.int32, (LANES, LANES), 0)
    c = lax.broadcasted_iota(jnp.int32, (LANES, LANES), 1)
    before = (r < c).astype(BF16)
    start = (256.0 * _dot(hi.astype(BF16), before) + _dot(lo.astype(BF16), before))[0:1, :]
    meta = meta_ref[...]
    lane = lax.broadcasted_iota(jnp.int32, (tm, LANES), 1)
    rsum = lambda a: jnp.sum(a, axis=-1, keepdims=True)
    out = jnp.zeros((tm, LANES), F32)
    for k in range(2):
        e_lane = meta[:, k:k + 1].astype(jnp.int32) + MOE_GROUPS
        d = rsum(jnp.where(lane == e_lane, start, 0.0)) + meta[:, 4 + k:5 + k]
        out = jnp.where(lane == k, d, out)
    o_ref[...] = out.astype(jnp.int32)


def _dest(meta, counts):
    t = meta.shape[0]
    tm = 512
    last = counts.shape[0] // 8 - 1
    return pl.pallas_call(
        functools.partial(_dest_kernel, tm=tm),
        grid=(t // tm,),
        in_specs=[pl.BlockSpec((tm, LANES), lambda i: (i, 0)),
                  pl.BlockSpec((8, LANES), lambda i: (last, 0))],
        out_specs=pl.BlockSpec((tm, LANES), lambda i: (i, 0)),
        out_shape=jax.ShapeDtypeStruct((t, LANES), jnp.int32),
        compiler_params=_cparams(("parallel",)),
        name="dest",
    )(meta, counts)


def _dispatch_kernel(d1_ref, d2_ref, h_ref, xs_ref, sem, *, td):
    base = pl.program_id(0) * td

    def start(rb, carry):
        for k in range(DMA_UNROLL):
            r = rb * DMA_UNROLL + k
            src = h_ref.at[pl.ds(r * ROW_PITCH, ROW_PITCH)]
            for d_ref in (d1_ref, d2_ref):
                dst = xs_ref.at[pl.ds(d_ref[base + r] * ROW_PITCH, ROW_PITCH)]
                pltpu.make_async_copy(src, dst, sem).start()
        return carry

    lax.fori_loop(0, td // DMA_UNROLL, start, 0)
    for _ in range(2):
        pltpu.make_async_copy(h_ref, xs_ref.at[pl.ds(0, td * ROW_PITCH)], sem).wait()


def _dispatch(dest1, dest2, h2):
    t = dest1.shape[0]
    td = 512
    return pl.pallas_call(
        functools.partial(_dispatch_kernel, td=td),
        grid_spec=pltpu.PrefetchScalarGridSpec(
            num_scalar_prefetch=2,
            grid=(t // td,),
            in_specs=[pl.BlockSpec((td * ROW_PITCH, LANES), lambda i, *_: (i, 0))],
            out_specs=pl.BlockSpec(memory_space=pl.ANY),
            scratch_shapes=[pltpu.SemaphoreType.DMA(())]),
        out_shape=jax.ShapeDtypeStruct((2 * t * ROW_PITCH, LANES), F32),
        compiler_params=pltpu.CompilerParams(dimension_semantics=("arbitrary",),
                                             has_side_effects=True),
        name="dispatch",
    )(dest1, dest2, h2)


def _expert_kernel(vt_ref, ve_ref, vlo_ref, vhi_ref, vfirst_ref, vvalid_ref, vnew_ref, vnext_ref,
                   vslot_ref, xs_ref, wg_ref, wu_ref, wd_ref, ys_ref,
                   wgf_ref, wuf_ref, wdf_ref, wgb_ref, wub_ref, wdb_ref, sem, *, tmx, layer):
    v = pl.program_id(0)

    def weight_copies(e, slot):
        return [pltpu.make_async_copy(w_ref.at[layer, e], f_ref.at[slot], sem.at[slot])
                for w_ref, f_ref in ((wg_ref, wgf_ref), (wu_ref, wuf_ref), (wd_ref, wdf_ref))]

    @pl.when(vnew_ref[v] == 1)
    def _():
        slot = vslot_ref[v]

        @pl.when(v == 0)
        def _():
            for cp in weight_copies(ve_ref[v], slot):
                cp.start()

        for cp in weight_copies(ve_ref[v], slot):
            cp.wait()

        @pl.when(vnext_ref[v] >= 0)
        def _():
            for cp in weight_copies(vnext_ref[v], 1 - slot):
                cp.start()

        wgb_ref[...] = wgf_ref[slot].astype(BF16)
        wub_ref[...] = wuf_ref[slot].astype(BF16)
        wdb_ref[...] = wdf_ref[slot].astype(BF16)

    @pl.when(vvalid_ref[v] == 1)
    def _():
        x = jnp.concatenate([xs_ref[pl.ds(s, tmx, stride=ROW_PITCH), :] for s in range(ROW_CHUNKS)],
                            axis=1).astype(BF16)
        a = _dot(x, wgb_ref[...])
        u = _dot(x, wub_ref[...])
        hid = (jax.nn.silu(a) * u).astype(BF16)
        y = _dot(hid, wdb_ref[...])
        rows = lax.broadcasted_iota(jnp.int32, (tmx, 1), 0)
        mine = (rows >= vlo_ref[v]) & (rows < vhi_ref[v])

        @pl.when(vfirst_ref[v] == 1)
        def _():
            for s in range(ROW_CHUNKS):
                ys_ref[pl.ds(s, tmx, stride=ROW_PITCH), :] = jnp.where(
                    mine, y[:, s * LANES:(s + 1) * LANES], 0.0)
            ys_ref[pl.ds(ROW_CHUNKS, tmx, stride=ROW_PITCH), :] = jnp.zeros((tmx, LANES), F32)

        @pl.when(vfirst_ref[v] == 0)
        def _():
            for s in range(ROW_CHUNKS):
                old = ys_ref[pl.ds(s, tmx, stride=ROW_PITCH), :]
                ys_ref[pl.ds(s, tmx, stride=ROW_PITCH), :] = jnp.where(
                    mine, y[:, s * LANES:(s + 1) * LANES], old)


def _experts(sched, xs, wg, wu, wd, layer, tmx):
    nvis = sched[0].shape[0]
    rows = xs.shape[0]
    xspec = pl.BlockSpec((tmx * ROW_PITCH, LANES), lambda v, vt, *_: (vt[v], 0))
    hbm = pl.BlockSpec(memory_space=pl.ANY)
    return pl.pallas_call(
        functools.partial(_expert_kernel, tmx=tmx, layer=layer),
        grid_spec=pltpu.PrefetchScalarGridSpec(
            num_scalar_prefetch=9,
            grid=(nvis,),
            in_specs=[xspec, hbm, hbm, hbm],
            out_specs=xspec,
            scratch_shapes=[pltpu.VMEM((2, D_MODEL, MOE_HIDDEN), F32),
                            pltpu.VMEM((2, D_MODEL, MOE_HIDDEN), F32),
                            pltpu.VMEM((2, MOE_HIDDEN, D_MODEL), F32),
                            pltpu.VMEM((D_MODEL, MOE_HIDDEN), BF16),
                            pltpu.VMEM((D_MODEL, MOE_HIDDEN), BF16),
                            pltpu.VMEM((MOE_HIDDEN, D_MODEL), BF16),
                            pltpu.SemaphoreType.DMA((2,))]),
        out_shape=jax.ShapeDtypeStruct((rows, LANES), F32),
        compiler_params=_cparams(("arbitrary",)),
        name="experts",
    )(*sched, xs, wg, wu, wd)


def _combine_kernel(d1_ref, d2_ref, ys_ref, meta_ref, x1_ref, fg_ref, o_ref,
                    b1_ref, b2_ref, sem, *, tc, final):
    i = pl.program_id(0)
    n = pl.num_programs(0)
    slot = i % 2

    def gather(tile, slot):
        def start(rb, carry):
            for k in range(DMA_UNROLL):
                r = rb * DMA_UNROLL + k
                dst = pl.ds(r * ROW_PITCH, ROW_CHUNKS)
                for d_ref, b_ref in ((d1_ref, b1_ref), (d2_ref, b2_ref)):
                    src = ys_ref.at[pl.ds(d_ref[tile * tc + r] * ROW_PITCH, ROW_CHUNKS)]
                    pltpu.make_async_copy(src, b_ref.at[slot, dst], sem.at[slot]).start()
            return carry

        lax.fori_loop(0, tc // DMA_UNROLL, start, 0)

    @pl.when(i == 0)
    def _():
        gather(0, 0)

    @pl.when(i + 1 < n)
    def _():
        gather(i + 1, 1 - slot)

    for b_ref in (b1_ref, b2_ref):
        pltpu.make_async_copy(ys_ref.at[pl.ds(0, tc * ROW_CHUNKS)],
                              b_ref.at[slot, pl.ds(0, tc * ROW_CHUNKS)], sem.at[slot]).wait()
    w1 = meta_ref[:, 2:3]
    w2 = meta_ref[:, 3:4]
    parts = []
    for s in range(ROW_CHUNKS):
        y = (w1 * b1_ref[slot, pl.ds(s, tc, stride=ROW_PITCH), :]
             + w2 * b2_ref[slot, pl.ds(s, tc, stride=ROW_PITCH), :])
        parts.append(x1_ref[:, s * LANES:(s + 1) * LANES] + y)
    x2 = jnp.concatenate(parts, axis=1)
    if final:
        x2 = _rms(x2) * fg_ref[...]
    o_ref[...] = x2


def _combine(dest1, dest2, ys, meta, x1, fg, final):
    t = x1.shape[0]
    tc = 256
    return pl.pallas_call(
        functools.partial(_combine_kernel, tc=tc, final=final),
        grid_spec=pltpu.PrefetchScalarGridSpec(
            num_scalar_prefetch=2,
            grid=(t // tc,),
            in_specs=[pl.BlockSpec(memory_space=pl.ANY),
                      pl.BlockSpec((tc, LANES), lambda i, *_: (i, 0)),
                      pl.BlockSpec((tc, D_MODEL), lambda i, *_: (i, 0)),
                      pl.BlockSpec((1, D_MODEL), lambda i, *_: (0, 0))],
            out_specs=pl.BlockSpec((tc, D_MODEL), lambda i, *_: (i, 0)),
            scratch_shapes=[pltpu.VMEM((2, tc * ROW_PITCH, LANES), F32),
                            pltpu.VMEM((2, tc * ROW_PITCH, LANES), F32),
                            pltpu.SemaphoreType.DMA((2,))]),
        out_shape=jax.ShapeDtypeStruct((t, D_MODEL), F32),
        compiler_params=_cparams(("arbitrary",)),
        name="combine",
    )(dest1, dest2, ys, meta, x1, fg)


def _rope_tables(r0):
    inv = ROPE_THETA ** (-jnp.arange(0, 2 * ROPE_HALF, 2, dtype=F32) / (2 * ROPE_HALF))
    ang = jnp.arange(SEQ, dtype=F32)[:, None] * inv[None, :]
    cos, sin = jnp.cos(ang), jnp.sin(ang)
    c = jnp.ones((SEQ, LANES), F32).at[:, r0:r0 + ROPE_HALF].set(cos)
    c = c.at[:, r0 + ROPE_HALF:r0 + 2 * ROPE_HALF].set(cos)
    sa = jnp.zeros((SEQ, LANES), F32).at[:, r0:r0 + ROPE_HALF].set(-sin)
    sb = jnp.zeros((SEQ, LANES), F32).at[:, r0 + ROPE_HALF:r0 + 2 * ROPE_HALF].set(sin)
    return c, sa, sb


def _pad_cols(w, width):
    return jnp.pad(w, ((0, 0), (0, width - w.shape[1])))


def _s5_params(a_re, a_im, log_dt, b_re, b_im, c_re, c_im, batch):
    dt = jnp.exp(log_dt)[:, None]
    mag = jnp.exp(a_re * dt)
    abar_r, abar_i = mag * jnp.cos(a_im * dt), mag * jnp.sin(a_im * dt)
    den = a_re * a_re + a_im * a_im
    nr, ni = abar_r - 1.0, abar_i
    coef_r = (nr * a_re + ni * a_im) / den
    coef_i = (ni * a_re - nr * a_im) / den
    bbar_r = coef_r[..., None] * b_re - coef_i[..., None] * b_im
    bbar_i = coef_r[..., None] * b_im + coef_i[..., None] * b_re
    eye = jnp.eye(S5_GROUPS, dtype=F32)
    blk_b = lambda m: jnp.einsum('gpc,gh->gchp', m, eye).reshape(W_GROUP, S5_WIDTH)
    wb = jnp.concatenate([blk_b(bbar_r), blk_b(bbar_i)], axis=1).astype(BF16)
    blk_c = lambda m: jnp.einsum('gcp,gh->gphc', m, eye).reshape(S5_WIDTH, W_GROUP)
    wc = jnp.concatenate([blk_c(c_re), blk_c(-c_im)], axis=0).astype(BF16)
    ar = jnp.broadcast_to(abar_r.reshape(1, S5_WIDTH), (batch, S5_WIDTH))
    ai = jnp.broadcast_to(abar_i.reshape(1, S5_WIDTH), (batch, S5_WIDTH))
    return wb, ar, ai, wc


def _nsa_consts(ck):
    c_start = np.arange(N_CMP_PAD) * CMP_STRIDE
    b_start = np.arange(N_BLK) * SEL_BLOCK
    ov = ((c_start[None, :] < b_start[:, None] + SEL_BLOCK)
          & (c_start[None, :] + CMP_BLOCK > b_start[:, None])).astype(np.float32)
    key_blk = np.arange(SEQ) // SEL_BLOCK
    e = (key_blk[None, :] == np.arange(N_BLK)[:, None]).astype(np.float32)
    e = e.reshape(N_BLK, SEQ // ck, ck).transpose(1, 0, 2)
    return jnp.asarray(ov), jnp.asarray(e, dtype=BF16)


def _lane_cumsum(v):
    r = lax.broadcasted_iota(jnp.int32, (LANES, LANES), 0)
    c = lax.broadcasted_iota(jnp.int32, (LANES, LANES), 1)
    incl = (r <= c).astype(BF16)
    hi = jnp.floor(v * (1.0 / 256.0))
    lo = v - 256.0 * hi
    return 256.0 * _dot(hi.astype(BF16), incl) + _dot(lo.astype(BF16), incl)


def _sched_kernel(cnt_ref, o_ref, *, tmx, nv):
    cnt = cnt_ref[...]
    ends = _lane_cumsum(cnt)
    offs = ends - cnt
    first = jnp.floor(offs * (1.0 / tmx))
    last = jnp.floor(jnp.maximum(ends - 1.0, 0.0) * (1.0 / tmx))
    nvis = jnp.where(cnt > 0.0, last - first + 1.0, 0.0)
    cumv = _lane_cumsum(nvis)
    row1 = lambda a: a[0:1, :]
    total = jnp.max(row1(cumv), axis=-1, keepdims=True)
    rsum = lambda a: jnp.sum(a, axis=-1, keepdims=True)
    v = lax.broadcasted_iota(jnp.int32, (nv, LANES), 0).astype(F32)
    lane = lax.broadcasted_iota(jnp.int32, (nv, LANES), 1).astype(F32)
    vc = jnp.minimum(v, total - 1.0)
    e_lane = rsum((row1(cumv) <= vc).astype(F32))
    hit = lane == e_lane
    pick = lambda a: rsum(jnp.where(hit, row1(a), 0.0))
    vt = pick(first) + vc[:, 0:1] - (pick(cumv) - pick(nvis))
    vlo = jnp.clip(pick(offs) - vt * tmx, 0.0, float(tmx))
    vhi = jnp.clip(pick(ends) - vt * tmx, 0.0, float(tmx))
    changed = lambda a: (v == 0.0) | (a != pltpu.roll(a, 1, 0))
    vfirst = changed(jnp.broadcast_to(vt, (nv, LANES))).astype(F32)
    vnew = changed(jnp.broadcast_to(e_lane, (nv, LANES))).astype(F32)
    vvalid = (v < total).astype(F32)
    nonempty = (cnt > 0.0).astype(F32)
    order = pick(_lane_cumsum(nonempty) - nonempty)
    vslot = order - 2.0 * jnp.floor(order * 0.5)
    far = float(1 << 20)
    nxt = jnp.min(jnp.where((row1(nonempty) > 0.0) & (lane > e_lane), lane, far), axis=-1, keepdims=True)
    vnext = jnp.where(nxt >= far, -1.0, nxt - MOE_GROUPS)
    out = jnp.zeros((nv, LANES), F32)
    cols = (vt, e_lane - MOE_GROUPS, vlo, vhi, vfirst, vvalid, vnew, vnext, vslot)
    for k, col in enumerate(cols):
        out = jnp.where(lane == k, col, out)
    o_ref[...] = out.astype(jnp.int32)


def _moe_schedule(counts, n_rows, tmx):
    nvis_max = n_rows // tmx + MOE_EXPERTS
    nv = 256
    last = counts.shape[0] // 8 - 1
    sched = pl.pallas_call(
        functools.partial(_sched_kernel, tmx=tmx, nv=nv),
        grid=(1,),
        in_specs=[pl.BlockSpec((8, LANES), lambda i: (last, 0))],
        out_specs=pl.BlockSpec((nv, LANES), lambda i: (0, 0)),
        out_shape=jax.ShapeDtypeStruct((nv, LANES), jnp.int32),
        name="sched",
    )(counts)
    return tuple(sched[:nvis_max, k] for k in range(9))


def kernel(x, attn_norm_g, w_in, mla_q_norm_g, mla_kv_norm_g, mla_w_uq, mla_w_ukv, nsa_cmp_pe, nsa_cmp_w1, nsa_cmp_w2, pool_w, pool_b, pool_scale, s5_a_re, s5_a_im, s5_log_dt, s5_b_re, s5_b_im, s5_c_re, s5_c_im, s5_d, s5_glu_w, s5_glu_b, mix_norm_g, w_out, ffn_norm_g, moe_w_group, moe_b_group, moe_w_expert, moe_b_expert, moe_w_gate, moe_w_up, moe_w_down, final_norm_g):
    batch, seq, d = x.shape
    depth = w_in.shape[0]
    t = batch * seq
    xf = x.reshape(t, d)
    qtabs = _rope_tables(MLA_NOPE)
    ktabs = _rope_tables(0)
    ovt, emat = _nsa_consts(512)
    row = lambda v: v.reshape(1, -1)
    tmx = 256
    w_in_all = _w_in_prep(w_in)
    w_out_all = w_out.astype(BF16)

    for l in range(depth):
        cols = _in_proj(xf, row(attn_norm_g[l]), w_in_all, l)

        wuq = jnp.pad(mla_w_uq[l], ((0, 512 - MLA_Q_LORA), (0, 0))).astype(BF16)
        ukv = mla_w_ukv[l].reshape(HEAD_D, MLA_HEADS, MLA_NOPE + MLA_V)
        wk = jnp.pad(ukv[:, :, :MLA_NOPE], ((0, 0), (0, 0), (0, HEAD_D - MLA_NOPE)))
        wk = wk.reshape(HEAD_D, 512).astype(BF16)
        wv = ukv[:, :, MLA_NOPE:].reshape(HEAD_D, 512).astype(BF16)
        qg = jnp.pad(mla_q_norm_g[l], (0, 512 - MLA_Q_LORA)).reshape(1, 512)
        q_a, k_a, v_a = _mla_prep(cols, qg, row(mla_kv_norm_g[l]), wuq, wk, wv, qtabs, ktabs)
        y_a = _mla_attn(q_a, k_a, v_a, batch)

        q_b, kc, vc, kv_b, gates = _nsa_prep(cols, ktabs)
        pe = jnp.broadcast_to(nsa_cmp_pe[l].reshape(2, 1, CMP_BLOCK * HEAD_D),
                              (2, 8, CMP_BLOCK * HEAD_D)).astype(BF16)
        kcmp, vcmp = _nsa_compress(kc, vc, nsa_cmp_w1[l].astype(BF16), nsa_cmp_w2[l].astype(BF16),
                                   pe, batch)
        y_b = _nsa_attn(q_b, kcmp, vcmp, kv_b, gates, ovt, emat, batch)

        y_c = _pool(cols, pool_w[l].astype(BF16), row(pool_b[l]), row(pool_scale[l]), batch)

        wb, ar, ai, wc = _s5_params(s5_a_re[l], s5_a_im[l], s5_log_dt[l], s5_b_re[l], s5_b_im[l],
                                    s5_c_re[l], s5_c_im[l], batch)
        y_d = _s5(cols.reshape(batch, seq, N_IN_PAD), wb, ar, ai, wc, row(s5_d[l]),
                  s5_glu_w[l].astype(BF16), row(s5_glu_b[l])).reshape(t, W_GROUP)

        wr = jnp.concatenate([moe_w_group[l], moe_w_expert[l]], axis=1)
        wr = _pad_cols(wr, LANES)
        wr_hi = wr.astype(BF16)
        wr_lo = (wr - wr_hi.astype(F32)).astype(BF16)
        br = jnp.pad(jnp.concatenate([moe_b_group[l], moe_b_expert[l]]), (0, LANES - 36)).reshape(1, LANES)
        x1, h2, logits = _out_proj(y_a, y_b, y_c, y_d, row(mix_norm_g[l]), w_out_all, l,
                                   xf, row(ffn_norm_g[l]), wr_hi, wr_lo, br)

        meta, counts = _router(logits)
        dest = _dest(meta, counts)
        dest1, dest2 = dest[:, 0], dest[:, 1]
        sched = _moe_schedule(counts, 2 * t, tmx)
        xs = _dispatch(dest1, dest2, h2)
        ys = _experts(sched, xs, moe_w_gate, moe_w_up, moe_w_down, l, tmx)
        xf = _combine(dest1, dest2, ys, meta, x1, row(final_norm_g), final=(l == depth - 1))

    return xf.reshape(batch, seq, d)
```

```python
import functools
import math

import numpy as np
import jax
import jax.numpy as jnp
from jax import lax
from jax.experimental import pallas as pl
from jax.experimental.pallas import tpu as pltpu

F32 = jnp.float32
BF16 = jnp.bfloat16

D_MODEL = 2048
SEQ = 2048
W_GROUP = 512
LANES = 128
ROW_CHUNKS = D_MODEL // LANES // 2
ROW_PITCH = ROW_CHUNKS + 1
U32 = jnp.uint32
DMA_UNROLL = 4

ROPE_THETA = 500000.0
ROPE_HALF = 16
NEG_INF = -1.0e30
FORCE_SCORE = 1.0e4
EPS = 1e-6

MLA_HEADS = 4
MLA_Q_LORA = 448
MLA_NOPE = 96
MLA_V = 128
HEAD_D = 128

CMP_BLOCK = 32
CMP_STRIDE = 16
SEL_BLOCK = 64
N_SEL = 8
N_LOCAL = 2
WINDOW = 512
N_CMP_PAD = SEQ // CMP_STRIDE
N_BLK = SEQ // SEL_BLOCK

POOL_SIZES = (2, 4, 8, 16)
S5_GROUPS = 32
S5_CH = 16
S5_STATE = 64
S5_WIDTH = S5_GROUPS * S5_STATE

MOE_GROUPS = 4
MOE_EPG = 8
MOE_EXPERTS = 32
MOE_HIDDEN = 512

COL_CQ, COL_NQ, COL_POOL, COL_S5 = 0, 512, 1024, 1536
COL_CKV, COL_KR, COL_KV6, COL_GL = 2048, 2176, 2304, 3072
N_IN_PAD = 3200

VMEM_LIMIT = 56 * 1024 * 1024


def _cparams(sem, vmem=VMEM_LIMIT):
    return pltpu.CompilerParams(dimension_semantics=sem, vmem_limit_bytes=vmem)


def _rms(x, n=None):
    n = x.shape[-1] if n is None else n
    return x * lax.rsqrt(jnp.sum(x * x, axis=-1, keepdims=True) / n + EPS)


def _dot(a, b):
    return jnp.dot(a, b, preferred_element_type=F32)


def _dot_nt(a, b, precision=None):
    return lax.dot_general(a, b, (((1,), (1,)), ((), ())), preferred_element_type=F32,
                           precision=precision)


def _rope(x, c, sa, sb):
    return x * c + pltpu.roll(x, LANES - ROPE_HALF, 1) * sa + pltpu.roll(x, ROPE_HALF, 1) * sb


def _pack_pair(lo, hi):
    bits = lambda a: lax.bitcast_convert_type(a.astype(BF16).astype(F32), U32)
    return (bits(lo) >> 16) | bits(hi)


def _unpack_pair(w):
    return (lax.bitcast_convert_type(w << 16, F32),
            lax.bitcast_convert_type(w & jnp.uint32(0xFFFF0000), F32))


def _pack_rows(ref, x, n, mask=None, old=False):
    for s in range(ROW_CHUNKS):
        rows = pl.ds(s, n, stride=ROW_PITCH)
        w = _pack_pair(x[:, s * LANES:(s + 1) * LANES],
                       x[:, (s + ROW_CHUNKS) * LANES:(s + ROW_CHUNKS + 1) * LANES])
        if mask is not None:
            w = jnp.where(mask, w, ref[rows, :] if old else jnp.zeros_like(w))
        ref[rows, :] = w
    if not old:
        ref[pl.ds(ROW_CHUNKS, n, stride=ROW_PITCH), :] = jnp.zeros((n, LANES), U32)


def _unpack_rows(load, n):
    pairs = [_unpack_pair(load(pl.ds(s, n, stride=ROW_PITCH))) for s in range(ROW_CHUNKS)]
    return [p[0] for p in pairs] + [p[1] for p in pairs]


_W_IN_SEGMENTS = ((COL_CQ, 0, 448), (COL_NQ, 608, 512), (COL_POOL, 1900, 512), (COL_S5, 2412, 512),
                  (COL_CKV, 448, 128), (COL_KR, 576, 32), (COL_KV6, 1120, 768), (COL_GL, 1888, 12))


def _w_in_prep_kernel(w_ref, o_ref):
    o_ref[0] = jnp.zeros(o_ref.shape[1:], BF16)
    for dst, src, width in _W_IN_SEGMENTS:
        o_ref[0, :, dst:dst + width] = w_ref[0, :, src:src + width].astype(BF16)


def _w_in_prep(w_in):
    depth, d, n = w_in.shape
    tk = 256
    return pl.pallas_call(
        _w_in_prep_kernel,
        grid=(depth, d // tk),
        in_specs=[pl.BlockSpec((1, tk, n), lambda l, k: (l, k, 0))],
        out_specs=pl.BlockSpec((1, tk, N_IN_PAD), lambda l, k: (l, k, 0)),
        out_shape=jax.ShapeDtypeStruct((depth, d, N_IN_PAD), BF16),
        compiler_params=_cparams(("parallel", "parallel")),
        name="w_in_prep",
    )(w_in)


def _in_proj_kernel(x_ref, g_ref, w_ref, qg_ref, kvg_ref, wuq_ref, wk_ref, wv_ref,
                    qc_ref, qsa_ref, qsb_ref, kc_ref, ksa_ref, ksb_ref,
                    qa_out, ka_out, va_out, qb_out, kcmp_out, vcmp_out, kvb_out, gate_out,
                    pool_out, s5_out):
    scale = 1.0 / math.sqrt(HEAD_D)
    h = (_rms(x_ref[...]) * g_ref[...]).astype(BF16)
    seg = lambda col, width: _dot(h, w_ref[:, col:col + width])
    heads = [slice(n * HEAD_D, (n + 1) * HEAD_D) for n in range(4)]

    qn = (_rms(seg(COL_CQ, 512), MLA_Q_LORA) * qg_ref[...]).astype(BF16)
    q = _dot(qn, wuq_ref[...])
    kvn = (_rms(seg(COL_CKV, LANES)) * kvg_ref[...]).astype(BF16)
    kn = _dot(kvn, wk_ref[...])
    va_out[...] = _dot(kvn, wv_ref[...]).astype(BF16)
    kc, ksa, ksb = kc_ref[...], ksa_ref[...], ksb_ref[...]
    kr = pltpu.roll(_rope(seg(COL_KR, LANES), kc, ksa, ksb), MLA_NOPE, 1)
    for sl in heads:
        qa_out[:, sl] = (_rope(q[:, sl], qc_ref[...], qsa_ref[...], qsb_ref[...]) * scale).astype(BF16)
        ka_out[:, sl] = (kn[:, sl] + kr).astype(BF16)

    qb = seg(COL_NQ, 512)
    for sl in heads:
        qb_out[:, sl] = (_rope(qb[:, sl], kc, ksa, ksb) * scale).astype(BF16)
    kv = seg(COL_KV6, 6 * HEAD_D)
    part = lambda n: kv[:, n * HEAD_D:(n + 1) * HEAD_D]
    kcmp_out[...] = _rope(part(0), kc, ksa, ksb)
    vcmp_out[...] = part(1)
    kvb_out[:, 0:128] = _rope(part(2), kc, ksa, ksb).astype(BF16)
    kvb_out[:, 128:256] = part(3).astype(BF16)
    kvb_out[:, 256:384] = _rope(part(4), kc, ksa, ksb).astype(BF16)
    kvb_out[:, 384:512] = part(5).astype(BF16)
    gate_out[...] = jax.nn.sigmoid(seg(COL_GL, LANES))

    pool_out[...] = seg(COL_POOL, W_GROUP).astype(BF16)
    s5_out[...] = seg(COL_S5, W_GROUP).astype(BF16)


def _in_proj(x, g, w_all, layer, qg, kvg, wuq, wk, wv, qtabs, ktabs):
    t = x.shape[0]
    tm = 512
    nsb = SEQ // tm
    tab = pl.BlockSpec((tm, LANES), lambda i: (i % nsb, 0))
    full = lambda shape: pl.BlockSpec(shape, lambda i: (0, 0))
    out = lambda width: pl.BlockSpec((tm, width), lambda i: (i, 0))
    widths = (512, 512, 512, 512, LANES, LANES, 512, LANES, W_GROUP, W_GROUP)
    dtypes = (BF16, BF16, BF16, BF16, F32, F32, BF16, F32, BF16, BF16)
    return pl.pallas_call(
        _in_proj_kernel,
        grid=(t // tm,),
        in_specs=[pl.BlockSpec((tm, D_MODEL), lambda i: (i, 0)),
                  full((1, D_MODEL)),
                  pl.BlockSpec((None, D_MODEL, N_IN_PAD), lambda i: (layer, 0, 0)),
                  full((1, 512)), full((1, LANES)),
                  full((512, 512)), full((LANES, 512)), full((LANES, 512)),
                  tab, tab, tab, tab, tab, tab],
        out_specs=[out(w) for w in widths],
        out_shape=[jax.ShapeDtypeStruct((t, w), d) for w, d in zip(widths, dtypes)],
        compiler_params=_cparams(("parallel",)),
        name="in_proj",
    )(x, g, w_all, qg, kvg, wuq, wk, wv, *qtabs, *ktabs)


def _fold_lanes(a, op):
    out = a[:, 0:LANES]
    for c in range(1, a.shape[1] // LANES):
        out = op(out, a[:, c * LANES:(c + 1) * LANES])
    return out


def _mla_attn_kernel(q_ref, k_ref, v_ref, o_ref, s_ref, mx_ref, acc_ref, *, tq):
    i = pl.program_id(1)
    d0 = pl.multiple_of(i * tq, tq)
    row = lax.broadcasted_iota(jnp.int32, (tq, tq), 0)
    col = lax.broadcasted_iota(jnp.int32, (tq, tq), 1)
    heads = [slice(h * HEAD_D, (h + 1) * HEAD_D) for h in range(MLA_HEADS)]
    tile = lambda j: pl.ds(pl.multiple_of(j * tq, tq), tq)

    for h, sl in enumerate(heads):
        s = jnp.where(col <= row, _dot_nt(q_ref[:, sl], k_ref[pl.ds(d0, tq), sl]), NEG_INF)
        s_ref[h, i] = s
        mx_ref[h] = _fold_lanes(s, jnp.maximum)

    def scores(j, carry):
        for h, sl in enumerate(heads):
            s = _dot_nt(q_ref[:, sl], k_ref[tile(j), sl])
            s_ref[h, j] = s
            mx_ref[h] = jnp.maximum(mx_ref[h], _fold_lanes(s, jnp.maximum))
        return carry

    lax.fori_loop(0, i, scores, 0)
    for h in range(MLA_HEADS):
        m = jnp.max(mx_ref[h], axis=-1, keepdims=True)
        mx_ref[h] = jnp.broadcast_to(m, (tq, LANES))
        acc_ref[h] = jnp.zeros((tq, 2 * HEAD_D), F32)

    ones = jnp.ones((tq, LANES), BF16)

    def values(j, carry):
        for h, sl in enumerate(heads):
            m = mx_ref[h]
            s = s_ref[h, j]
            p = jnp.concatenate([jnp.exp((s[:, c * LANES:(c + 1) * LANES] - m).astype(BF16))
                                 for c in range(tq // LANES)], axis=1)
            v_aug = jnp.concatenate([v_ref[tile(j), sl], ones], axis=1)
            acc_ref[h] = acc_ref[h] + _dot(p, v_aug)
        return carry

    lax.fori_loop(0, i + 1, values, 0)
    for h, sl in enumerate(heads):
        o_ref[:, sl] = acc_ref[h, :, 0:HEAD_D] / acc_ref[h, :, HEAD_D:2 * HEAD_D]


def _mla_attn(q, k, v, batch):
    tq = 256
    nq = SEQ // tq
    return pl.pallas_call(
        functools.partial(_mla_attn_kernel, tq=tq),
        grid=(batch, nq),
        in_specs=[pl.BlockSpec((tq, W_GROUP), lambda b, i: (b * nq + i, 0)),
                  pl.BlockSpec((SEQ, W_GROUP), lambda b, i: (b, 0)),
                  pl.BlockSpec((SEQ, W_GROUP), lambda b, i: (b, 0))],
        out_specs=pl.BlockSpec((tq, W_GROUP), lambda b, i: (b * nq + i, 0)),
        out_shape=jax.ShapeDtypeStruct((batch * SEQ, W_GROUP), F32),
        scratch_shapes=[pltpu.VMEM((MLA_HEADS, nq, tq, tq), F32),
                        pltpu.VMEM((MLA_HEADS, tq, LANES), F32),
                        pltpu.VMEM((MLA_HEADS, tq, 2 * HEAD_D), F32)],
        compiler_params=_cparams(("parallel", "arbitrary")),
        name="mla_attn",
    )(q, k, v)


def _nsa_compress_kernel(xk_ref, xv_ref, w1_ref, w2_ref, pe_ref, k_out, v_out):
    for c, (x_ref, o_ref) in enumerate(((xk_ref, k_out), (xv_ref, v_out))):
        a = jnp.zeros((N_CMP_PAD, HEAD_D), F32)
        b = jnp.zeros((N_CMP_PAD, HEAD_D), F32)
        for r in range(CMP_STRIDE):
            x = x_ref[pl.ds(r, N_CMP_PAD, stride=CMP_STRIDE), :].astype(BF16)
            a = a + _dot(x, w1_ref[c, r * HEAD_D:(r + 1) * HEAD_D, :])
            b = b + _dot(x, w1_ref[c, (CMP_STRIDE + r) * HEAD_D:(CMP_STRIDE + r + 1) * HEAD_D, :])
        b = pltpu.roll(b, N_CMP_PAD - 1, 0)
        pe = _dot(pe_ref[c], w1_ref[c])[0:1, :]
        hid = jax.nn.gelu(a + b + pe)
        o_ref[0] = _dot(hid.astype(BF16), w2_ref[c]).astype(BF16)


def _nsa_compress(xk, xv, w1, w2, pe, batch):
    xspec = pl.BlockSpec((SEQ, HEAD_D), lambda b: (b, 0))
    ospec = pl.BlockSpec((1, N_CMP_PAD, HEAD_D), lambda b: (b, 0, 0))
    return pl.pallas_call(
        _nsa_compress_kernel,
        grid=(batch,),
        in_specs=[xspec, xspec,
                  pl.BlockSpec((2, CMP_BLOCK * HEAD_D, HEAD_D), lambda b: (0, 0, 0)),
                  pl.BlockSpec((2, HEAD_D, HEAD_D), lambda b: (0, 0, 0)),
                  pl.BlockSpec((2, 8, CMP_BLOCK * HEAD_D), lambda b: (0, 0, 0))],
        out_specs=[ospec, ospec],
        out_shape=[jax.ShapeDtypeStruct((batch, N_CMP_PAD, HEAD_D), BF16)] * 2,
        compiler_params=_cparams(("parallel",)),
        name="nsa_compress",
    )(xk, xv, w1, w2, pe)


def _softmax_rows(s):
    m = jnp.max(s, axis=-1, keepdims=True)
    p = jnp.exp(s - m)
    return p / jnp.sum(p, axis=-1, keepdims=True)


def _nsa_attn_kernel(q_ref, kc_ref, vc_ref, kv_ref, g_ref, ovt_ref, e_ref, o_ref,
                     m_ref, acc_ref, s_ref, *, tq, ck):
    i = pl.program_id(1)
    q0 = i * tq
    nh = 4
    qs = jnp.concatenate([q_ref[:, h * HEAD_D:(h + 1) * HEAD_D] for h in range(nh)], axis=0)
    qpos = q0 + lax.broadcasted_iota(jnp.int32, (tq, 1), 0)
    stack = lambda a: jnp.concatenate([a] * nh, axis=0)

    sc = _dot_nt(qs, kc_ref[0])
    n_idx = lax.broadcasted_iota(jnp.int32, (tq, N_CMP_PAD), 1)
    valid_c = n_idx * CMP_STRIDE + (CMP_BLOCK - 1) <= qpos
    valid_c4 = stack(valid_c)
    pc = _softmax_rows(jnp.where(valid_c4, sc, NEG_INF))
    pc = jnp.where(valid_c4, pc, 0.0)
    o_c = _dot(pc.astype(BF16), vc_ref[0])
    psum = pc[0:tq] + pc[tq:2 * tq] + pc[2 * tq:3 * tq] + pc[3 * tq:4 * tq]

    imp_t = _dot_nt(ovt_ref[...], psum, precision=lax.Precision.HIGHEST)
    kblk = lax.broadcasted_iota(jnp.int32, (N_BLK, tq), 0)
    cur = (q0 + lax.broadcasted_iota(jnp.int32, (N_BLK, tq), 1)) // SEL_BLOCK
    forced = (kblk == 0) | ((kblk <= cur) & (kblk > cur - N_LOCAL))
    score = jnp.where(forced, FORCE_SCORE, jnp.where(kblk <= cur, imp_t, -1.0))
    cnt = jnp.zeros((N_BLK, tq), F32)
    for j in range(N_BLK):
        sj = score[j:j + 1, :]
        beats = (sj > score) | ((sj == score) & (kblk > j))
        cnt = cnt + beats.astype(F32)
    sel = jnp.transpose((cnt < N_SEL).astype(F32)).astype(BF16)

    wlen = WINDOW + tq
    w0 = pl.multiple_of(jnp.maximum(q0 - WINDOW, 0), tq)
    sw = _dot_nt(qs, kv_ref[pl.ds(w0, wlen), 256:384])
    kpos = w0 + lax.broadcasted_iota(jnp.int32, (tq, wlen), 1)
    ok = (kpos <= qpos) & (kpos > qpos - WINDOW)
    sw = sw + stack(jnp.where(ok, 0.0, NEG_INF))
    pw = jnp.exp((sw - jnp.max(_fold_lanes(sw, jnp.maximum), axis=-1, keepdims=True)).astype(BF16))
    aug = lambda v: jnp.concatenate([v, jnp.ones(v.shape, BF16)], axis=1)
    ow = _dot(pw, aug(kv_ref[pl.ds(w0, wlen), 384:512]))
    o_w = ow[:, 0:HEAD_D] / ow[:, HEAD_D:2 * HEAD_D]

    nck = q0 // ck + 1
    m_ref[...] = jnp.full(m_ref.shape, NEG_INF, F32)

    def scores(c, carry):
        r0 = pl.multiple_of(c * ck, ck)
        chosen = _dot(sel, e_ref[c])
        kpos = r0 + lax.broadcasted_iota(jnp.int32, (tq, ck), 1)
        ok = (chosen > 0.5) & (kpos <= qpos)
        s = _dot_nt(qs, kv_ref[pl.ds(r0, ck), 0:128]) + stack(jnp.where(ok, 0.0, NEG_INF))
        s_ref[c] = s
        m_ref[...] = jnp.maximum(m_ref[...], _fold_lanes(s, jnp.maximum))
        return carry

    lax.fori_loop(0, nck, scores, 0)
    m_ref[...] = jnp.broadcast_to(jnp.max(m_ref[...], axis=-1, keepdims=True), m_ref.shape)
    acc_ref[...] = jnp.zeros(acc_ref.shape, F32)

    def values(c, carry):
        r0 = pl.multiple_of(c * ck, ck)
        m = m_ref[...]
        s = s_ref[c]
        p = jnp.concatenate([jnp.exp((s[:, k * LANES:(k + 1) * LANES] - m).astype(BF16))
                             for k in range(ck // LANES)], axis=1)
        acc_ref[...] = acc_ref[...] + _dot(p, aug(kv_ref[pl.ds(r0, ck), 128:256]))
        return carry

    lax.fori_loop(0, nck, values, 0)
    o_s = acc_ref[:, 0:HEAD_D] / acc_ref[:, HEAD_D:2 * HEAD_D]

    g = g_ref[...]
    for h in range(nh):
        rs = slice(h * tq, (h + 1) * tq)
        o_ref[:, h * HEAD_D:(h + 1) * HEAD_D] = (
            g[:, 3 * h:3 * h + 1] * o_c[rs] + g[:, 3 * h + 1:3 * h + 2] * o_s[rs]
            + g[:, 3 * h + 2:3 * h + 3] * o_w[rs])


def _nsa_attn(q, kcmp, vcmp, kv, gates, ovt, emat, batch):
    tq, ck = 256, 512
    nq = SEQ // tq
    return pl.pallas_call(
        functools.partial(_nsa_attn_kernel, tq=tq, ck=ck),
        grid=(batch, nq),
        in_specs=[pl.BlockSpec((tq, 512), lambda b, i: (b * nq + i, 0)),
                  pl.BlockSpec((1, N_CMP_PAD, HEAD_D), lambda b, i: (b, 0, 0)),
                  pl.BlockSpec((1, N_CMP_PAD, HEAD_D), lambda b, i: (b, 0, 0)),
                  pl.BlockSpec((SEQ, 512), lambda b, i: (b, 0)),
                  pl.BlockSpec((tq, LANES), lambda b, i: (b * nq + i, 0)),
                  pl.BlockSpec((N_BLK, N_CMP_PAD), lambda b, i: (0, 0)),
                  pl.BlockSpec((SEQ // ck, N_BLK, ck), lambda b, i: (0, 0, 0))],
        out_specs=pl.BlockSpec((tq, 512), lambda b, i: (b * nq + i, 0)),
        out_shape=jax.ShapeDtypeStruct((batch * SEQ, 512), F32),
        scratch_shapes=[pltpu.VMEM((4 * tq, LANES), F32),
                        pltpu.VMEM((4 * tq, 2 * HEAD_D), F32),
                        pltpu.VMEM((SEQ // ck, 4 * tq, ck), F32)],
        compiler_params=_cparams(("parallel", "arbitrary")),
        name="nsa_attn",
    )(q, kcmp, vcmp, kv, gates, ovt, emat)


def _pool_kernel(u_ref, w_ref, b_ref, s_ref, o_ref, pad_ref):
    maxw = POOL_SIZES[-1]
    pad_ref[0:maxw, :] = jnp.zeros((maxw, W_GROUP), F32)
    pad_ref[maxw:maxw + SEQ, :] = u_ref[...].astype(F32)
    rc = 512
    for g, w in enumerate(POOL_SIZES):
        sl = slice(g * LANES, (g + 1) * LANES)
        for r in range(SEQ // rc):
            acc = pad_ref[maxw + r * rc:maxw + (r + 1) * rc, sl]
            tok = acc
            for j in range(1, w):
                acc = acc + pad_ref[maxw - j + r * rc:maxw - j + (r + 1) * rc, sl]
            t = r * rc + lax.broadcasted_iota(jnp.int32, (rc, 1), 0)
            cnt = jnp.minimum(t + 1, w).astype(F32)
            d = acc / cnt - tok
            y = _dot(d.astype(BF16), w_ref[g])
            o_ref[r * rc:(r + 1) * rc, sl] = (y + b_ref[:, sl]) * s_ref[:, sl]


def _pool(u, w, b, s, batch):
    return pl.pallas_call(
        _pool_kernel,
        grid=(batch,),
        in_specs=[pl.BlockSpec((SEQ, W_GROUP), lambda i: (i, 0)),
                  pl.BlockSpec((4, LANES, LANES), lambda i: (0, 0, 0)),
                  pl.BlockSpec((1, W_GROUP), lambda i: (0, 0)),
                  pl.BlockSpec((1, W_GROUP), lambda i: (0, 0))],
        out_specs=pl.BlockSpec((SEQ, W_GROUP), lambda i: (i, 0)),
        out_shape=jax.ShapeDtypeStruct((batch * SEQ, W_GROUP), F32),
        scratch_shapes=[pltpu.VMEM((SEQ + POOL_SIZES[-1], W_GROUP), F32)],
        compiler_params=_cparams(("parallel",)),
        name="pool",
    )(u, w, b, s)


def _s5_kernel(u_ref, wb_ref, ar_ref, ai_ref, wc_ref, d_ref, gw_ref, gb_ref, o_ref,
               bu_ref, st_ref, tm_ref, *, batch, tc):
    @pl.when(pl.program_id(0) == 0)
    def _():
        st_ref[...] = jnp.zeros(st_ref.shape, F32)

    nslab = W_GROUP // LANES
    for b in range(batch):
        for j in range(nslab):
            tm_ref[j, pl.ds(b, tc, stride=batch), :] = u_ref[b, :, j * LANES:(j + 1) * LANES].astype(F32)
    u = jnp.concatenate([tm_ref[j] for j in range(nslab)], axis=1)
    gl = 8 * S5_STATE
    ub = u.astype(BF16)
    for j in range(nslab):
        uj = ub[:, j * LANES:(j + 1) * LANES]
        for part in (0, S5_WIDTH):
            cs = slice(part + j * gl, part + (j + 1) * gl)
            bu_ref[:, cs] = _dot(uj, wb_ref[j * LANES:(j + 1) * LANES, cs])
    lc = 512
    unroll = 8
    for c in range(S5_WIDTH // lc):
        re = slice(c * lc, (c + 1) * lc)
        im = slice(S5_WIDTH + c * lc, S5_WIDTH + (c + 1) * lc)
        ar, ai = ar_ref[:, re], ai_ref[:, re]

        def body(tb, carry, re=re, im=im, ar=ar, ai=ai):
            xr, xi = carry
            for k in range(unroll):
                r0 = pl.multiple_of((tb * unroll + k) * batch, batch)
                nxr = ar * xr - ai * xi + bu_ref[pl.ds(r0, batch), re]
                nxi = ar * xi + ai * xr + bu_ref[pl.ds(r0, batch), im]
                bu_ref[pl.ds(r0, batch), re] = nxr
                bu_ref[pl.ds(r0, batch), im] = nxi
                xr, xi = nxr, nxi
            return xr, xi

        xr, xi = lax.fori_loop(0, tc // unroll, body, (st_ref[:, re], st_ref[:, im]))
        st_ref[:, re] = xr
        st_ref[:, im] = xi

    ys = []
    for j in range(nslab):
        osl = slice(j * LANES, (j + 1) * LANES)
        yj = 0.0
        for part in (0, S5_WIDTH):
            cs = slice(part + j * gl, part + (j + 1) * gl)
            yj = yj + _dot(bu_ref[:, cs].astype(BF16), wc_ref[cs, osl])
        ys.append(yj)
    y = jnp.concatenate(ys, axis=1) + d_ref[...] * u
    y = jax.nn.gelu(y)
    z = _dot(y.astype(BF16), gw_ref[...]) + gb_ref[...]
    o = y * jax.nn.sigmoid(z)
    for j in range(nslab):
        tm_ref[j] = o[:, j * LANES:(j + 1) * LANES]
    for b in range(batch):
        for j in range(nslab):
            o_ref[b, :, j * LANES:(j + 1) * LANES] = tm_ref[j, pl.ds(b, tc, stride=batch), :]


def _s5(u3, wb, ar, ai, wc, d, gw, gb):
    batch = u3.shape[0]
    tc = 64
    rows = tc * batch
    full = lambda shape: pl.BlockSpec(shape, lambda i: (0, 0))
    return pl.pallas_call(
        functools.partial(_s5_kernel, batch=batch, tc=tc),
        grid=(SEQ // tc,),
        in_specs=[pl.BlockSpec((batch, tc, W_GROUP), lambda i: (0, i, 0)),
                  full((W_GROUP, 2 * S5_WIDTH)), full((batch, S5_WIDTH)), full((batch, S5_WIDTH)),
                  full((2 * S5_WIDTH, W_GROUP)), full((1, W_GROUP)),
                  full((W_GROUP, W_GROUP)), full((1, W_GROUP))],
        out_specs=pl.BlockSpec((batch, tc, W_GROUP), lambda i: (0, i, 0)),
        out_shape=jax.ShapeDtypeStruct((batch, SEQ, W_GROUP), F32),
        scratch_shapes=[pltpu.VMEM((rows, 2 * S5_WIDTH), F32),
                        pltpu.VMEM((batch, 2 * S5_WIDTH), F32),
                        pltpu.VMEM((W_GROUP // LANES, rows, LANES), F32)],
        compiler_params=_cparams(("arbitrary",)),
        name="s5",
    )(u3, wb, ar, ai, wc, d, gw, gb)


def _out_proj_kernel(ya_ref, yb_ref, yc_ref, yd_ref, mg_ref, wo_ref, x_ref, fg_ref,
                     wrh_ref, wrl_ref, br_ref, x1_ref, h2_ref, meta_ref, cnt_ref, carry_ref, *, tm):
    acc = x_ref[...]
    for gi, y_ref in enumerate((ya_ref, yb_ref, yc_ref, yd_ref)):
        sl = slice(gi * W_GROUP, (gi + 1) * W_GROUP)
        n = (_rms(y_ref[...]) * mg_ref[:, sl]).astype(BF16)
        acc = acc + _dot(n, wo_ref[sl, :])
    x1_ref[...] = acc
    h2 = _rms(acc) * fg_ref[...]
    hi = h2.astype(BF16)
    lo = (h2 - hi.astype(F32)).astype(BF16)
    logits = _dot(hi, wrh_ref[...]) + _dot(hi, wrl_ref[...]) + _dot(lo, wrh_ref[...]) + br_ref[...]
    _route(logits, meta_ref, cnt_ref, carry_ref, tm)
    _pack_rows(h2_ref, h2, tm)


def _out_proj(ya, yb, yc, yd, mg, wo_all, layer, x, fg, wr_hi, wr_lo, br):
    t = x.shape[0]
    tm = 256
    yspec = pl.BlockSpec((tm, W_GROUP), lambda i: (i, 0))
    full = lambda shape: pl.BlockSpec(shape, lambda i: (0, 0))
    return pl.pallas_call(
        functools.partial(_out_proj_kernel, tm=tm),
        grid=(t // tm,),
        in_specs=[yspec, yspec, yspec, yspec, full((1, D_MODEL)),
                  pl.BlockSpec((None, D_MODEL, D_MODEL), lambda i: (layer, 0, 0)),
                  pl.BlockSpec((tm, D_MODEL), lambda i: (i, 0)), full((1, D_MODEL)),
                  full((D_MODEL, LANES)), full((D_MODEL, LANES)), full((1, LANES))],
        out_specs=[pl.BlockSpec((tm, D_MODEL), lambda i: (i, 0)),
                   pl.BlockSpec((tm * ROW_PITCH, LANES), lambda i: (i, 0)),
                   pl.BlockSpec((tm, LANES), lambda i: (i, 0)),
                   pl.BlockSpec((8, LANES), lambda i: (i, 0))],
        out_shape=[jax.ShapeDtypeStruct((t, D_MODEL), F32),
                   jax.ShapeDtypeStruct((t * ROW_PITCH, LANES), U32),
                   jax.ShapeDtypeStruct((t, LANES), F32),
                   jax.ShapeDtypeStruct((t // tm * 8, LANES), F32)],
        scratch_shapes=[pltpu.VMEM((8, LANES), F32)],
        compiler_params=_cparams(("arbitrary",)),
        name="out_proj",
    )(ya, yb, yc, yd, mg, wo_all, x, fg, wr_hi, wr_lo, br)


def _route(lg, meta_ref, cnt_ref, carry_ref, tm):
    @pl.when(pl.program_id(0) == 0)
    def _():
        carry_ref[...] = jnp.zeros(carry_ref.shape, F32)

    lane = lax.broadcasted_iota(jnp.int32, (tm, LANES), 1)
    big = jnp.int32(1 << 20)
    rmax = lambda a: jnp.max(a, axis=-1, keepdims=True)
    rmin = lambda a: jnp.min(a, axis=-1, keepdims=True)
    rsum = lambda a: jnp.sum(a, axis=-1, keepdims=True)

    is_g = lane < MOE_GROUPS
    gl = jnp.where(is_g, lg, NEG_INF)
    gm = rmax(gl)
    p_top = 1.0 / rsum(jnp.where(is_g, jnp.exp(gl - gm), 0.0))
    g_top = rmin(jnp.where(is_g & (gl == gm), lane, big))

    is_e = (lane >= MOE_GROUPS) & (lane < MOE_GROUPS + MOE_EXPERTS) \
        & (((lane - MOE_GROUPS) // MOE_EPG) == g_top)
    el = jnp.where(is_e, lg, NEG_INF)
    ee = jnp.where(is_e, jnp.exp(el - rmax(el)), 0.0)
    p = jnp.where(is_e, ee / rsum(ee), -1.0)
    p1 = rmax(p)
    i1 = rmin(jnp.where(p == p1, lane, big))
    p_rest = jnp.where(lane == i1, -1.0, p)
    p2 = rmax(p_rest)
    i2 = rmin(jnp.where((p_rest == p2) & is_e & (lane != i1), lane, big))
    den = p1 + p2
    w1 = p_top * (p1 / den)
    w2 = p_top * (p2 / den)

    hit1, hit2 = lane == i1, lane == i2
    oh = (hit1 | hit2).astype(F32)
    r = lax.broadcasted_iota(jnp.int32, (tm, tm), 0)
    c = lax.broadcasted_iota(jnp.int32, (tm, tm), 1)
    before = _dot((c < r).astype(BF16), oh.astype(BF16)) + carry_ref[0:1, :]
    r1 = rsum(jnp.where(hit1, before, 0.0))
    r2 = rsum(jnp.where(hit2, before, 0.0))
    carry_ref[...] = carry_ref[...] + jnp.sum(oh, axis=0, keepdims=True)
    cnt_ref[...] = carry_ref[...]

    e1 = (i1 - MOE_GROUPS).astype(F32)
    e2 = (i2 - MOE_GROUPS).astype(F32)
    vals = (e1, e2, w1, w2, r1, r2)
    meta = jnp.zeros((tm, LANES), F32)
    for k, v in enumerate(vals):
        meta = jnp.where(lane == k, v, meta)
    meta_ref[...] = meta


def _dest_kernel(meta_ref, cnt_ref, o_ref, *, tm):
    cnt = cnt_ref[...]
    hi = jnp.floor(cnt * (1.0 / 256.0))
    lo = cnt - 256.0 * hi
    r = lax.broadcasted_iota(jnp.int32, (LANES, LANES), 0)
    c = lax.broadcasted_iota(jnp.int32, (LANES, LANES), 1)
    before = (r < c).astype(BF16)
    start = (256.0 * _dot(hi.astype(BF16), before) + _dot(lo.astype(BF16), before))[0:1, :]
    meta = meta_ref[...]
    lane = lax.broadcasted_iota(jnp.int32, (tm, LANES), 1)
    rsum = lambda a: jnp.sum(a, axis=-1, keepdims=True)
    out = jnp.zeros((tm, LANES), F32)
    for k in range(2):
        e_lane = meta[:, k:k + 1].astype(jnp.int32) + MOE_GROUPS
        d = rsum(jnp.where(lane == e_lane, start, 0.0)) + meta[:, 4 + k:5 + k]
        out = jnp.where(lane == k, d, out)
    o_ref[...] = out.astype(jnp.int32)


def _dest(meta, counts):
    t = meta.shape[0]
    tm = 512
    last = counts.shape[0] // 8 - 1
    return pl.pallas_call(
        functools.partial(_dest_kernel, tm=tm),
        grid=(t // tm,),
        in_specs=[pl.BlockSpec((tm, LANES), lambda i: (i, 0)),
                  pl.BlockSpec((8, LANES), lambda i: (last, 0))],
        out_specs=pl.BlockSpec((tm, LANES), lambda i: (i, 0)),
        out_shape=jax.ShapeDtypeStruct((t, LANES), jnp.int32),
        compiler_params=_cparams(("parallel",)),
        name="dest",
    )(meta, counts)


def _dispatch_kernel(d1_ref, d2_ref, h_ref, xs_ref, sem, *, td):
    base = pl.program_id(0) * td

    def start(rb, carry):
        for k in range(DMA_UNROLL):
            r = rb * DMA_UNROLL + k
            src = h_ref.at[pl.ds(r * ROW_PITCH, ROW_PITCH)]
            for d_ref in (d1_ref, d2_ref):
                dst = xs_ref.at[pl.ds(d_ref[base + r] * ROW_PITCH, ROW_PITCH)]
                pltpu.make_async_copy(src, dst, sem).start()
        return carry

    lax.fori_loop(0, td // DMA_UNROLL, start, 0)
    for _ in range(2):
        pltpu.make_async_copy(h_ref, xs_ref.at[pl.ds(0, td * ROW_PITCH)], sem).wait()


def _dispatch(dest1, dest2, h2):
    t = dest1.shape[0]
    td = 512
    return pl.pallas_call(
        functools.partial(_dispatch_kernel, td=td),
        grid_spec=pltpu.PrefetchScalarGridSpec(
            num_scalar_prefetch=2,
            grid=(t // td,),
            in_specs=[pl.BlockSpec((td * ROW_PITCH, LANES), lambda i, *_: (i, 0))],
            out_specs=pl.BlockSpec(memory_space=pl.ANY),
            scratch_shapes=[pltpu.SemaphoreType.DMA(())]),
        out_shape=jax.ShapeDtypeStruct((2 * t * ROW_PITCH, LANES), U32),
        compiler_params=pltpu.CompilerParams(dimension_semantics=("arbitrary",),
                                             has_side_effects=True),
        name="dispatch",
    )(dest1, dest2, h2)


def _expert_kernel(vt_ref, ve_ref, vlo_ref, vhi_ref, vfirst_ref, vvalid_ref, vnew_ref, vnext_ref,
                   vslot_ref, xs_ref, wg_ref, wu_ref, wd_ref, ys_ref,
                   wgf_ref, wuf_ref, wdf_ref, wgb_ref, wub_ref, wdb_ref, sem, *, tmx, layer):
    v = pl.program_id(0)

    def weight_copies(e, slot):
        return [pltpu.make_async_copy(w_ref.at[layer, e], f_ref.at[slot], sem.at[slot])
                for w_ref, f_ref in ((wg_ref, wgf_ref), (wu_ref, wuf_ref), (wd_ref, wdf_ref))]

    @pl.when(vnew_ref[v] == 1)
    def _():
        slot = vslot_ref[v]

        @pl.when(v == 0)
        def _():
            for cp in weight_copies(ve_ref[v], slot):
                cp.start()

        for cp in weight_copies(ve_ref[v], slot):
            cp.wait()

        @pl.when(vnext_ref[v] >= 0)
        def _():
            for cp in weight_copies(vnext_ref[v], 1 - slot):
                cp.start()

        wgb_ref[...] = wgf_ref[slot].astype(BF16)
        wub_ref[...] = wuf_ref[slot].astype(BF16)
        wdb_ref[...] = wdf_ref[slot].astype(BF16)

    @pl.when(vvalid_ref[v] == 1)
    def _():
        x = jnp.concatenate(_unpack_rows(lambda rows: xs_ref[rows, :], tmx), axis=1).astype(BF16)
        a = _dot(x, wgb_ref[...])
        u = _dot(x, wub_ref[...])
        hid = (jax.nn.silu(a) * u).astype(BF16)
        y = _dot(hid, wdb_ref[...])
        rows = lax.broadcasted_iota(jnp.int32, (tmx, 1), 0)
        mine = (rows >= vlo_ref[v]) & (rows < vhi_ref[v])

        @pl.when(vfirst_ref[v] == 1)
        def _():
            _pack_rows(ys_ref, y, tmx, mask=mine)

        @pl.when(vfirst_ref[v] == 0)
        def _():
            _pack_rows(ys_ref, y, tmx, mask=mine, old=True)


def _experts(sched, xs, wg, wu, wd, layer, tmx):
    nvis = sched[0].shape[0]
    rows = xs.shape[0]
    xspec = pl.BlockSpec((tmx * ROW_PITCH, LANES), lambda v, vt, *_: (vt[v], 0))
    hbm = pl.BlockSpec(memory_space=pl.ANY)
    return pl.pallas_call(
        functools.partial(_expert_kernel, tmx=tmx, layer=layer),
        grid_spec=pltpu.PrefetchScalarGridSpec(
            num_scalar_prefetch=9,
            grid=(nvis,),
            in_specs=[xspec, hbm, hbm, hbm],
            out_specs=xspec,
            scratch_shapes=[pltpu.VMEM((2, D_MODEL, MOE_HIDDEN), F32),
                            pltpu.VMEM((2, D_MODEL, MOE_HIDDEN), F32),
                            pltpu.VMEM((2, MOE_HIDDEN, D_MODEL), F32),
                            pltpu.VMEM((D_MODEL, MOE_HIDDEN), BF16),
                            pltpu.VMEM((D_MODEL, MOE_HIDDEN), BF16),
                            pltpu.VMEM((MOE_HIDDEN, D_MODEL), BF16),
                            pltpu.SemaphoreType.DMA((2,))]),
        out_shape=jax.ShapeDtypeStruct((rows, LANES), U32),
        compiler_params=_cparams(("arbitrary",)),
        name="experts",
    )(*sched, xs, wg, wu, wd)


def _combine_kernel(d1_ref, d2_ref, ys_ref, meta_ref, x1_ref, fg_ref, o_ref,
                    b1_ref, b2_ref, sem, *, tc, final):
    i = pl.program_id(0)
    n = pl.num_programs(0)
    slot = i % 2

    def gather(tile, slot):
        def start(rb, carry):
            for k in range(DMA_UNROLL):
                r = rb * DMA_UNROLL + k
                dst = pl.ds(r * ROW_PITCH, ROW_CHUNKS)
                for d_ref, b_ref in ((d1_ref, b1_ref), (d2_ref, b2_ref)):
                    src = ys_ref.at[pl.ds(d_ref[tile * tc + r] * ROW_PITCH, ROW_CHUNKS)]
                    pltpu.make_async_copy(src, b_ref.at[slot, dst], sem.at[slot]).start()
            return carry

        lax.fori_loop(0, tc // DMA_UNROLL, start, 0)

    @pl.when(i == 0)
    def _():
        gather(0, 0)

    @pl.when(i + 1 < n)
    def _():
        gather(i + 1, 1 - slot)

    for b_ref in (b1_ref, b2_ref):
        pltpu.make_async_copy(ys_ref.at[pl.ds(0, tc * ROW_CHUNKS)],
                              b_ref.at[slot, pl.ds(0, tc * ROW_CHUNKS)], sem.at[slot]).wait()
    w1 = meta_ref[:, 2:3]
    w2 = meta_ref[:, 3:4]
    y1 = _unpack_rows(lambda rows: b1_ref[slot, rows, :], tc)
    y2 = _unpack_rows(lambda rows: b2_ref[slot, rows, :], tc)
    x2 = jnp.concatenate([x1_ref[:, c * LANES:(c + 1) * LANES] + (w1 * y1[c] + w2 * y2[c])
                          for c in range(2 * ROW_CHUNKS)], axis=1)
    if final:
        x2 = _rms(x2) * fg_ref[...]
    o_ref[...] = x2


def _combine(dest1, dest2, ys, meta, x1, fg, final):
    t = x1.shape[0]
    tc = 256
    return pl.pallas_call(
        functools.partial(_combine_kernel, tc=tc, final=final),
        grid_spec=pltpu.PrefetchScalarGridSpec(
            num_scalar_prefetch=2,
            grid=(t // tc,),
            in_specs=[pl.BlockSpec(memory_space=pl.ANY),
                      pl.BlockSpec((tc, LANES), lambda i, *_: (i, 0)),
                      pl.BlockSpec((tc, D_MODEL), lambda i, *_: (i, 0)),
                      pl.BlockSpec((1, D_MODEL), lambda i, *_: (0, 0))],
            out_specs=pl.BlockSpec((tc, D_MODEL), lambda i, *_: (i, 0)),
            scratch_shapes=[pltpu.VMEM((2, tc * ROW_PITCH, LANES), U32),
                            pltpu.VMEM((2, tc * ROW_PITCH, LANES), U32),
                            pltpu.SemaphoreType.DMA((2,))]),
        out_shape=jax.ShapeDtypeStruct((t, D_MODEL), F32),
        compiler_params=_cparams(("arbitrary",)),
        name="combine",
    )(dest1, dest2, ys, meta, x1, fg)


def _rope_tables(r0):
    inv = ROPE_THETA ** (-jnp.arange(0, 2 * ROPE_HALF, 2, dtype=F32) / (2 * ROPE_HALF))
    ang = jnp.arange(SEQ, dtype=F32)[:, None] * inv[None, :]
    cos, sin = jnp.cos(ang), jnp.sin(ang)
    c = jnp.ones((SEQ, LANES), F32).at[:, r0:r0 + ROPE_HALF].set(cos)
    c = c.at[:, r0 + ROPE_HALF:r0 + 2 * ROPE_HALF].set(cos)
    sa = jnp.zeros((SEQ, LANES), F32).at[:, r0:r0 + ROPE_HALF].set(-sin)
    sb = jnp.zeros((SEQ, LANES), F32).at[:, r0 + ROPE_HALF:r0 + 2 * ROPE_HALF].set(sin)
    return c, sa, sb


def _pad_cols(w, width):
    return jnp.pad(w, ((0, 0), (0, width - w.shape[1])))


def _s5_params(a_re, a_im, log_dt, b_re, b_im, c_re, c_im, batch):
    dt = jnp.exp(log_dt)[:, None]
    mag = jnp.exp(a_re * dt)
    abar_r, abar_i = mag * jnp.cos(a_im * dt), mag * jnp.sin(a_im * dt)
    den = a_re * a_re + a_im * a_im
    nr, ni = abar_r - 1.0, abar_i
    coef_r = (nr * a_re + ni * a_im) / den
    coef_i = (ni * a_re - nr * a_im) / den
    bbar_r = coef_r[..., None] * b_re - coef_i[..., None] * b_im
    bbar_i = coef_r[..., None] * b_im + coef_i[..., None] * b_re
    eye = jnp.eye(S5_GROUPS, dtype=F32)
    blk_b = lambda m: jnp.einsum('gpc,gh->gchp', m, eye).reshape(W_GROUP, S5_WIDTH)
    wb = jnp.concatenate([blk_b(bbar_r), blk_b(bbar_i)], axis=1).astype(BF16)
    blk_c = lambda m: jnp.einsum('gcp,gh->gphc', m, eye).reshape(S5_WIDTH, W_GROUP)
    wc = jnp.concatenate([blk_c(c_re), blk_c(-c_im)], axis=0).astype(BF16)
    ar = jnp.broadcast_to(abar_r.reshape(1, S5_WIDTH), (batch, S5_WIDTH))
    ai = jnp.broadcast_to(abar_i.reshape(1, S5_WIDTH), (batch, S5_WIDTH))
    return wb, ar, ai, wc


def _nsa_consts(ck):
    c_start = np.arange(N_CMP_PAD) * CMP_STRIDE
    b_start = np.arange(N_BLK) * SEL_BLOCK
    ov = ((c_start[None, :] < b_start[:, None] + SEL_BLOCK)
          & (c_start[None, :] + CMP_BLOCK > b_start[:, None])).astype(np.float32)
    key_blk = np.arange(SEQ) // SEL_BLOCK
    e = (key_blk[None, :] == np.arange(N_BLK)[:, None]).astype(np.float32)
    e = e.reshape(N_BLK, SEQ // ck, ck).transpose(1, 0, 2)
    return jnp.asarray(ov), jnp.asarray(e, dtype=BF16)


def _lane_cumsum(v):
    r = lax.broadcasted_iota(jnp.int32, (LANES, LANES), 0)
    c = lax.broadcasted_iota(jnp.int32, (LANES, LANES), 1)
    incl = (r <= c).astype(BF16)
    hi = jnp.floor(v * (1.0 / 256.0))
    lo = v - 256.0 * hi
    return 256.0 * _dot(hi.astype(BF16), incl) + _dot(lo.astype(BF16), incl)


def _sched_kernel(cnt_ref, o_ref, *, tmx, nv):
    cnt = cnt_ref[...]
    ends = _lane_cumsum(cnt)
    offs = ends - cnt
    first = jnp.floor(offs * (1.0 / tmx))
    last = jnp.floor(jnp.maximum(ends - 1.0, 0.0) * (1.0 / tmx))
    nvis = jnp.where(cnt > 0.0, last - first + 1.0, 0.0)
    cumv = _lane_cumsum(nvis)
    row1 = lambda a: a[0:1, :]
    total = jnp.max(row1(cumv), axis=-1, keepdims=True)
    rsum = lambda a: jnp.sum(a, axis=-1, keepdims=True)
    v = lax.broadcasted_iota(jnp.int32, (nv, LANES), 0).astype(F32)
    lane = lax.broadcasted_iota(jnp.int32, (nv, LANES), 1).astype(F32)
    vc = jnp.minimum(v, total - 1.0)
    e_lane = rsum((row1(cumv) <= vc).astype(F32))
    hit = lane == e_lane
    pick = lambda a: rsum(jnp.where(hit, row1(a), 0.0))
    vt = pick(first) + vc[:, 0:1] - (pick(cumv) - pick(nvis))
    vlo = jnp.clip(pick(offs) - vt * tmx, 0.0, float(tmx))
    vhi = jnp.clip(pick(ends) - vt * tmx, 0.0, float(tmx))
    changed = lambda a: (v == 0.0) | (a != pltpu.roll(a, 1, 0))
    vfirst = changed(jnp.broadcast_to(vt, (nv, LANES))).astype(F32)
    vnew = changed(jnp.broadcast_to(e_lane, (nv, LANES))).astype(F32)
    vvalid = (v < total).astype(F32)
    nonempty = (cnt > 0.0).astype(F32)
    order = pick(_lane_cumsum(nonempty) - nonempty)
    vslot = order - 2.0 * jnp.floor(order * 0.5)
    far = float(1 << 20)
    nxt = jnp.min(jnp.where((row1(nonempty) > 0.0) & (lane > e_lane), lane, far), axis=-1, keepdims=True)
    vnext = jnp.where(nxt >= far, -1.0, nxt - MOE_GROUPS)
    out = jnp.zeros((nv, LANES), F32)
    cols = (vt, e_lane - MOE_GROUPS, vlo, vhi, vfirst, vvalid, vnew, vnext, vslot)
    for k, col in enumerate(cols):
        out = jnp.where(lane == k, col, out)
    o_ref[...] = out.astype(jnp.int32)


def _moe_schedule(counts, n_rows, tmx):
    nvis_max = n_rows // tmx + MOE_EXPERTS
    nv = 256
    last = counts.shape[0] // 8 - 1
    sched = pl.pallas_call(
        functools.partial(_sched_kernel, tmx=tmx, nv=nv),
        grid=(1,),
        in_specs=[pl.BlockSpec((8, LANES), lambda i: (last, 0))],
        out_specs=pl.BlockSpec((nv, LANES), lambda i: (0, 0)),
        out_shape=jax.ShapeDtypeStruct((nv, LANES), jnp.int32),
        name="sched",
    )(counts)
    return tuple(sched[:nvis_max, k] for k in range(9))


def kernel(x, attn_norm_g, w_in, mla_q_norm_g, mla_kv_norm_g, mla_w_uq, mla_w_ukv, nsa_cmp_pe, nsa_cmp_w1, nsa_cmp_w2, pool_w, pool_b, pool_scale, s5_a_re, s5_a_im, s5_log_dt, s5_b_re, s5_b_im, s5_c_re, s5_c_im, s5_d, s5_glu_w, s5_glu_b, mix_norm_g, w_out, ffn_norm_g, moe_w_group, moe_b_group, moe_w_expert, moe_b_expert, moe_w_gate, moe_w_up, moe_w_down, final_norm_g):
    batch, seq, d = x.shape
    depth = w_in.shape[0]
    t = batch * seq
    xf = x.reshape(t, d)
    qtabs = _rope_tables(MLA_NOPE)
    ktabs = _rope_tables(0)
    ovt, emat = _nsa_consts(512)
    row = lambda v: v.reshape(1, -1)
    tmx = 256
    w_in_all = _w_in_prep(w_in)
    w_out_all = w_out.astype(BF16)

    for l in range(depth):
        wuq = jnp.pad(mla_w_uq[l], ((0, 512 - MLA_Q_LORA), (0, 0))).astype(BF16)
        ukv = mla_w_ukv[l].reshape(HEAD_D, MLA_HEADS, MLA_NOPE + MLA_V)
        wk = jnp.pad(ukv[:, :, :MLA_NOPE], ((0, 0), (0, 0), (0, HEAD_D - MLA_NOPE)))
        wk = wk.reshape(HEAD_D, 512).astype(BF16)
        wv = ukv[:, :, MLA_NOPE:].reshape(HEAD_D, 512).astype(BF16)
        qg = jnp.pad(mla_q_norm_g[l], (0, 512 - MLA_Q_LORA)).reshape(1, 512)
        q_a, k_a, v_a, q_b, kc, vc, kv_b, gates, u_pool, u_s5 = _in_proj(
            xf, row(attn_norm_g[l]), w_in_all, l, qg, row(mla_kv_norm_g[l]), wuq, wk, wv, qtabs, ktabs)

        y_a = _mla_attn(q_a, k_a, v_a, batch)

        pe = jnp.broadcast_to(nsa_cmp_pe[l].reshape(2, 1, CMP_BLOCK * HEAD_D),
                              (2, 8, CMP_BLOCK * HEAD_D)).astype(BF16)
        kcmp, vcmp = _nsa_compress(kc, vc, nsa_cmp_w1[l].astype(BF16), nsa_cmp_w2[l].astype(BF16),
                                   pe, batch)
        y_b = _nsa_attn(q_b, kcmp, vcmp, kv_b, gates, ovt, emat, batch)

        y_c = _pool(u_pool, pool_w[l].astype(BF16), row(pool_b[l]), row(pool_scale[l]), batch)

        wb, ar, ai, wc = _s5_params(s5_a_re[l], s5_a_im[l], s5_log_dt[l], s5_b_re[l], s5_b_im[l],
                                    s5_c_re[l], s5_c_im[l], batch)
        y_d = _s5(u_s5.reshape(batch, seq, W_GROUP), wb, ar, ai, wc, row(s5_d[l]),
                  s5_glu_w[l].astype(BF16), row(s5_glu_b[l])).reshape(t, W_GROUP)

        wr = jnp.concatenate([moe_w_group[l], moe_w_expert[l]], axis=1)
        wr = _pad_cols(wr, LANES)
        wr_hi = wr.astype(BF16)
        wr_lo = (wr - wr_hi.astype(F32)).astype(BF16)
        br = jnp.pad(jnp.concatenate([moe_b_group[l], moe_b_expert[l]]), (0, LANES - 36)).reshape(1, LANES)
        x1, h2, meta, counts = _out_proj(y_a, y_b, y_c, y_d, row(mix_norm_g[l]), w_out_all, l,
                                         xf, row(ffn_norm_g[l]), wr_hi, wr_lo, br)

        dest = _dest(meta, counts)
        dest1, dest2 = dest[:, 0], dest[:, 1]
        sched = _moe_schedule(counts, 2 * t, tmx)
        xs = _dispatch(dest1, dest2, h2)
        ys = _experts(sched, xs, moe_w_gate, moe_w_up, moe_w_down, l, tmx)
        xf = _combine(dest1, dest2, ys, meta, x1, row(final_norm_g), final=(l == depth - 1))

    return xf.reshape(batch, seq, d)
```

```python
import functools
import math

import numpy as np
import jax
import jax.numpy as jnp
from jax import lax
from jax.experimental import pallas as pl
from jax.experimental.pallas import tpu as pltpu

F32 = jnp.float32
BF16 = jnp.bfloat16

D_MODEL = 2048
SEQ = 2048
W_GROUP = 512
LANES = 128
ROW_CHUNKS = D_MODEL // LANES
ROW_PITCH = ROW_CHUNKS + 1
DMA_UNROLL = 4

ROPE_THETA = 500000.0
ROPE_HALF = 16
NEG_INF = -1.0e30
FORCE_SCORE = 1.0e4
EPS = 1e-6

MLA_HEADS = 4
MLA_Q_LORA = 448
MLA_NOPE = 96
MLA_V = 128
HEAD_D = 128

CMP_BLOCK = 32
CMP_STRIDE = 16
SEL_BLOCK = 64
N_SEL = 8
N_LOCAL = 2
WINDOW = 512
N_CMP_PAD = SEQ // CMP_STRIDE
N_BLK = SEQ // SEL_BLOCK

POOL_SIZES = (2, 4, 8, 16)
S5_GROUPS = 32
S5_CH = 16
S5_STATE = 64
S5_WIDTH = S5_GROUPS * S5_STATE

MOE_GROUPS = 4
MOE_EPG = 8
MOE_EXPERTS = 32
MOE_HIDDEN = 512

COL_CQ, COL_NQ, COL_POOL, COL_S5 = 0, 512, 1024, 1536
COL_CKV, COL_KR, COL_KV6, COL_GL = 2048, 2176, 2304, 3072
N_IN_PAD = 3200

VMEM_LIMIT = 56 * 1024 * 1024


def _cparams(sem, vmem=VMEM_LIMIT):
    return pltpu.CompilerParams(dimension_semantics=sem, vmem_limit_bytes=vmem)


def _rms(x, n=None):
    n = x.shape[-1] if n is None else n
    return x * lax.rsqrt(jnp.sum(x * x, axis=-1, keepdims=True) / n + EPS)


def _dot(a, b):
    return jnp.dot(a, b, preferred_element_type=F32)


def _dot_nt(a, b, precision=None):
    return lax.dot_general(a, b, (((1,), (1,)), ((), ())), preferred_element_type=F32,
                           precision=precision)


def _rope(x, c, sa, sb):
    return x * c + pltpu.roll(x, LANES - ROPE_HALF, 1) * sa + pltpu.roll(x, ROPE_HALF, 1) * sb


def _pack_rows(ref, x, n, mask=None, old=False):
    for s in range(ROW_CHUNKS):
        rows = pl.ds(s, n, stride=ROW_PITCH)
        w = x[:, s * LANES:(s + 1) * LANES]
        if mask is not None:
            w = jnp.where(mask, w, ref[rows, :] if old else 0.0)
        ref[rows, :] = w
    if not old:
        ref[pl.ds(ROW_CHUNKS, n, stride=ROW_PITCH), :] = jnp.zeros((n, LANES), F32)


def _unpack_rows(load, n):
    return [load(pl.ds(s, n, stride=ROW_PITCH)) for s in range(ROW_CHUNKS)]


_W_IN_SEGMENTS = ((COL_CQ, 0, 448), (COL_NQ, 608, 512), (COL_POOL, 1900, 512), (COL_S5, 2412, 512),
                  (COL_CKV, 448, 128), (COL_KR, 576, 32), (COL_KV6, 1120, 768), (COL_GL, 1888, 12))


def _w_in_prep_kernel(w_ref, o_ref):
    o_ref[0] = jnp.zeros(o_ref.shape[1:], BF16)
    for dst, src, width in _W_IN_SEGMENTS:
        o_ref[0, :, dst:dst + width] = w_ref[0, :, src:src + width].astype(BF16)


def _w_in_prep(w_in):
    depth, d, n = w_in.shape
    tk = 256
    return pl.pallas_call(
        _w_in_prep_kernel,
        grid=(depth, d // tk),
        in_specs=[pl.BlockSpec((1, tk, n), lambda l, k: (l, k, 0))],
        out_specs=pl.BlockSpec((1, tk, N_IN_PAD), lambda l, k: (l, k, 0)),
        out_shape=jax.ShapeDtypeStruct((depth, d, N_IN_PAD), BF16),
        compiler_params=_cparams(("parallel", "parallel")),
        name="w_in_prep",
    )(w_in)


def _in_proj_kernel(x_ref, g_ref, w_ref, qg_ref, kvg_ref, wuq_ref, wk_ref, wv_ref,
                    qc_ref, qsa_ref, qsb_ref, kc_ref, ksa_ref, ksb_ref,
                    qa_out, ka_out, va_out, qb_out, kcmp_out, vcmp_out, kvb_out, gate_out,
                    pool_out, s5_out):
    scale = 1.0 / math.sqrt(HEAD_D)
    h = (_rms(x_ref[...]) * g_ref[...]).astype(BF16)
    seg = lambda col, width: _dot(h, w_ref[:, col:col + width])
    heads = [slice(n * HEAD_D, (n + 1) * HEAD_D) for n in range(4)]

    qn = (_rms(seg(COL_CQ, 512), MLA_Q_LORA) * qg_ref[...]).astype(BF16)
    q = _dot(qn, wuq_ref[...])
    kvn = (_rms(seg(COL_CKV, LANES)) * kvg_ref[...]).astype(BF16)
    kn = _dot(kvn, wk_ref[...])
    va_out[...] = _dot(kvn, wv_ref[...]).astype(BF16)
    kc, ksa, ksb = kc_ref[...], ksa_ref[...], ksb_ref[...]
    kr = pltpu.roll(_rope(seg(COL_KR, LANES), kc, ksa, ksb), MLA_NOPE, 1)
    for sl in heads:
        qa_out[:, sl] = (_rope(q[:, sl], qc_ref[...], qsa_ref[...], qsb_ref[...]) * scale).astype(BF16)
        ka_out[:, sl] = (kn[:, sl] + kr).astype(BF16)

    qb = seg(COL_NQ, 512)
    for sl in heads:
        qb_out[:, sl] = (_rope(qb[:, sl], kc, ksa, ksb) * scale).astype(BF16)
    kv = seg(COL_KV6, 6 * HEAD_D)
    part = lambda n: kv[:, n * HEAD_D:(n + 1) * HEAD_D]
    kcmp_out[...] = _rope(part(0), kc, ksa, ksb)
    vcmp_out[...] = part(1)
    kvb_out[:, 0:128] = _rope(part(2), kc, ksa, ksb).astype(BF16)
    kvb_out[:, 128:256] = part(3).astype(BF16)
    kvb_out[:, 256:384] = _rope(part(4), kc, ksa, ksb).astype(BF16)
    kvb_out[:, 384:512] = part(5).astype(BF16)
    gate_out[...] = jax.nn.sigmoid(seg(COL_GL, LANES))

    pool_out[...] = seg(COL_POOL, W_GROUP).astype(BF16)
    s5_out[...] = seg(COL_S5, W_GROUP).astype(BF16)


def _in_proj(x, g, w_all, layer, qg, kvg, wuq, wk, wv, qtabs, ktabs):
    t = x.shape[0]
    tm = 512
    nsb = SEQ // tm
    tab = pl.BlockSpec((tm, LANES), lambda i: (i % nsb, 0))
    full = lambda shape: pl.BlockSpec(shape, lambda i: (0, 0))
    out = lambda width: pl.BlockSpec((tm, width), lambda i: (i, 0))
    widths = (512, 512, 512, 512, LANES, LANES, 512, LANES, W_GROUP, W_GROUP)
    dtypes = (BF16, BF16, BF16, BF16, F32, F32, BF16, F32, BF16, BF16)
    return pl.pallas_call(
        _in_proj_kernel,
        grid=(t // tm,),
        in_specs=[pl.BlockSpec((tm, D_MODEL), lambda i: (i, 0)),
                  full((1, D_MODEL)),
                  pl.BlockSpec((None, D_MODEL, N_IN_PAD), lambda i: (layer, 0, 0)),
                  full((1, 512)), full((1, LANES)),
                  full((512, 512)), full((LANES, 512)), full((LANES, 512)),
                  tab, tab, tab, tab, tab, tab],
        out_specs=[out(w) for w in widths],
        out_shape=[jax.ShapeDtypeStruct((t, w), d) for w, d in zip(widths, dtypes)],
        compiler_params=_cparams(("parallel",)),
        name="in_proj",
    )(x, g, w_all, qg, kvg, wuq, wk, wv, *qtabs, *ktabs)


def _fold_lanes(a, op):
    out = a[:, 0:LANES]
    for c in range(1, a.shape[1] // LANES):
        out = op(out, a[:, c * LANES:(c + 1) * LANES])
    return out


def _mla_attn_kernel(q_ref, k_ref, v_ref, o_ref, s_ref, mx_ref, acc_ref, *, tq, tk):
    i = pl.program_id(1)
    nfull = (i * tq) // tk
    t0 = pl.multiple_of(nfull * tk, tk)
    row = i * tq + lax.broadcasted_iota(jnp.int32, (tq, tk), 0)
    col = t0 + lax.broadcasted_iota(jnp.int32, (tq, tk), 1)
    heads = [slice(h * HEAD_D, (h + 1) * HEAD_D) for h in range(MLA_HEADS)]
    chunk = lambda j: pl.ds(pl.multiple_of(j * tk, tk), tk)

    for h, sl in enumerate(heads):
        s = jnp.where(col <= row, _dot_nt(q_ref[:, sl], k_ref[pl.ds(t0, tk), sl]), NEG_INF)
        s_ref[h, nfull] = s
        mx_ref[h] = _fold_lanes(s, jnp.maximum)

    def scores(j, carry):
        for h, sl in enumerate(heads):
            s = _dot_nt(q_ref[:, sl], k_ref[chunk(j), sl])
            s_ref[h, j] = s
            mx_ref[h] = jnp.maximum(mx_ref[h], _fold_lanes(s, jnp.maximum))
        return carry

    lax.fori_loop(0, nfull, scores, 0)
    for h in range(MLA_HEADS):
        m = jnp.max(mx_ref[h], axis=-1, keepdims=True)
        mx_ref[h] = jnp.broadcast_to(m, (tq, LANES))
        acc_ref[h] = jnp.zeros((tq, 2 * HEAD_D), F32)

    ones = jnp.ones((tk, LANES), BF16)

    def values(j, carry):
        for h, sl in enumerate(heads):
            m = mx_ref[h]
            s = s_ref[h, j]
            p = jnp.concatenate([jnp.exp((s[:, c * LANES:(c + 1) * LANES] - m).astype(BF16))
                                 for c in range(tk // LANES)], axis=1)
            v_aug = jnp.concatenate([v_ref[chunk(j), sl], ones], axis=1)
            acc_ref[h] = acc_ref[h] + _dot(p, v_aug)
        return carry

    lax.fori_loop(0, nfull + 1, values, 0)
    for h, sl in enumerate(heads):
        o_ref[:, sl] = acc_ref[h, :, 0:HEAD_D] / acc_ref[h, :, HEAD_D:2 * HEAD_D]


def _mla_attn(q, k, v, batch):
    tq, tk = 256, 512
    nq = SEQ // tq
    return pl.pallas_call(
        functools.partial(_mla_attn_kernel, tq=tq, tk=tk),
        grid=(batch, nq),
        in_specs=[pl.BlockSpec((tq, W_GROUP), lambda b, i: (b * nq + i, 0)),
                  pl.BlockSpec((SEQ, W_GROUP), lambda b, i: (b, 0)),
                  pl.BlockSpec((SEQ, W_GROUP), lambda b, i: (b, 0))],
        out_specs=pl.BlockSpec((tq, W_GROUP), lambda b, i: (b * nq + i, 0)),
        out_shape=jax.ShapeDtypeStruct((batch * SEQ, W_GROUP), F32),
        scratch_shapes=[pltpu.VMEM((MLA_HEADS, SEQ // tk, tq, tk), F32),
                        pltpu.VMEM((MLA_HEADS, tq, LANES), F32),
                        pltpu.VMEM((MLA_HEADS, tq, 2 * HEAD_D), F32)],
        compiler_params=_cparams(("parallel", "arbitrary")),
        name="mla_attn",
    )(q, k, v)


def _nsa_compress_kernel(xk_ref, xv_ref, w1_ref, w2_ref, pe_ref, k_out, v_out):
    for c, (x_ref, o_ref) in enumerate(((xk_ref, k_out), (xv_ref, v_out))):
        a = jnp.zeros((N_CMP_PAD, HEAD_D), F32)
        b = jnp.zeros((N_CMP_PAD, HEAD_D), F32)
        for r in range(CMP_STRIDE):
            x = x_ref[pl.ds(r, N_CMP_PAD, stride=CMP_STRIDE), :].astype(BF16)
            a = a + _dot(x, w1_ref[c, r * HEAD_D:(r + 1) * HEAD_D, :])
            b = b + _dot(x, w1_ref[c, (CMP_STRIDE + r) * HEAD_D:(CMP_STRIDE + r + 1) * HEAD_D, :])
        b = pltpu.roll(b, N_CMP_PAD - 1, 0)
        pe = _dot(pe_ref[c], w1_ref[c])[0:1, :]
        hid = jax.nn.gelu(a + b + pe)
        o_ref[0] = _dot(hid.astype(BF16), w2_ref[c]).astype(BF16)


def _nsa_compress(xk, xv, w1, w2, pe, batch):
    xspec = pl.BlockSpec((SEQ, HEAD_D), lambda b: (b, 0))
    ospec = pl.BlockSpec((1, N_CMP_PAD, HEAD_D), lambda b: (b, 0, 0))
    return pl.pallas_call(
        _nsa_compress_kernel,
        grid=(batch,),
        in_specs=[xspec, xspec,
                  pl.BlockSpec((2, CMP_BLOCK * HEAD_D, HEAD_D), lambda b: (0, 0, 0)),
                  pl.BlockSpec((2, HEAD_D, HEAD_D), lambda b: (0, 0, 0)),
                  pl.BlockSpec((2, 8, CMP_BLOCK * HEAD_D), lambda b: (0, 0, 0))],
        out_specs=[ospec, ospec],
        out_shape=[jax.ShapeDtypeStruct((batch, N_CMP_PAD, HEAD_D), BF16)] * 2,
        compiler_params=_cparams(("parallel",)),
        name="nsa_compress",
    )(xk, xv, w1, w2, pe)


def _softmax_rows(s):
    m = jnp.max(s, axis=-1, keepdims=True)
    p = jnp.exp(s - m)
    return p / jnp.sum(p, axis=-1, keepdims=True)


def _nsa_attn_kernel(q_ref, kc_ref, vc_ref, kv_ref, g_ref, ovt_ref, e_ref, o_ref,
                     m_ref, acc_ref, s_ref, *, tq, ck):
    i = pl.program_id(1)
    q0 = i * tq
    nh = 4
    qs = jnp.concatenate([q_ref[:, h * HEAD_D:(h + 1) * HEAD_D] for h in range(nh)], axis=0)
    qpos = q0 + lax.broadcasted_iota(jnp.int32, (tq, 1), 0)
    masked = lambda s, ok: (s.reshape(nh, tq, s.shape[1])
                            + jnp.where(ok, 0.0, NEG_INF)[None]).reshape(s.shape)

    n_idx = lax.broadcasted_iota(jnp.int32, (tq, N_CMP_PAD), 1)
    valid_c = n_idx * CMP_STRIDE + (CMP_BLOCK - 1) <= qpos
    pc = _softmax_rows(masked(_dot_nt(qs, kc_ref[0]), valid_c)).reshape(nh, tq, N_CMP_PAD)
    pc = jnp.where(valid_c[None], pc, 0.0)
    psum = pc[0] + pc[1] + pc[2] + pc[3]
    o_c = _dot(pc.reshape(nh * tq, N_CMP_PAD).astype(BF16), vc_ref[0])

    imp_t = _dot_nt(ovt_ref[...], psum, precision=lax.Precision.HIGHEST)
    kblk = lax.broadcasted_iota(jnp.int32, (N_BLK, tq), 0)
    cur = (q0 + lax.broadcasted_iota(jnp.int32, (N_BLK, tq), 1)) // SEL_BLOCK
    forced = (kblk == 0) | ((kblk <= cur) & (kblk > cur - N_LOCAL))
    score = jnp.where(forced, FORCE_SCORE, jnp.where(kblk <= cur, imp_t, -1.0))
    cnt = jnp.zeros((N_BLK, tq), F32)
    for j in range(N_BLK):
        sj = score[j:j + 1, :]
        beats = (sj > score) | ((sj == score) & (kblk > j))
        cnt = cnt + beats.astype(F32)
    sel = jnp.transpose((cnt < N_SEL).astype(F32)).astype(BF16)

    wlen = WINDOW + tq
    w0 = pl.multiple_of(jnp.maximum(q0 - WINDOW, 0), tq)
    sw = _dot_nt(qs, kv_ref[pl.ds(w0, wlen), 256:384])
    kpos = w0 + lax.broadcasted_iota(jnp.int32, (tq, wlen), 1)
    ok = (kpos <= qpos) & (kpos > qpos - WINDOW)
    sw = masked(sw, ok)
    pw = jnp.exp((sw - jnp.max(_fold_lanes(sw, jnp.maximum), axis=-1, keepdims=True)).astype(BF16))
    aug = lambda v: jnp.concatenate([v, jnp.ones(v.shape, BF16)], axis=1)
    ow = _dot(pw, aug(kv_ref[pl.ds(w0, wlen), 384:512]))
    o_w = ow[:, 0:HEAD_D] / ow[:, HEAD_D:2 * HEAD_D]

    nck = q0 // ck + 1
    m_ref[...] = jnp.full(m_ref.shape, NEG_INF, F32)

    def scores(c, carry):
        r0 = pl.multiple_of(c * ck, ck)
        chosen = _dot(sel, e_ref[c])
        kpos = r0 + lax.broadcasted_iota(jnp.int32, (tq, ck), 1)
        ok = (chosen > 0.5) & (kpos <= qpos)
        s = masked(_dot_nt(qs, kv_ref[pl.ds(r0, ck), 0:128]), ok)
        s_ref[c] = s
        m_ref[...] = jnp.maximum(m_ref[...], _fold_lanes(s, jnp.maximum))
        return carry

    lax.fori_loop(0, nck, scores, 0)
    m_ref[...] = jnp.broadcast_to(jnp.max(m_ref[...], axis=-1, keepdims=True), m_ref.shape)
    acc_ref[...] = jnp.zeros(acc_ref.shape, F32)

    def values(c, carry):
        r0 = pl.multiple_of(c * ck, ck)
        m = m_ref[...]
        s = s_ref[c]
        p = jnp.concatenate([jnp.exp((s[:, k * LANES:(k + 1) * LANES] - m).astype(BF16))
                             for k in range(ck // LANES)], axis=1)
        acc_ref[...] = acc_ref[...] + _dot(p, aug(kv_ref[pl.ds(r0, ck), 128:256]))
        return carry

    lax.fori_loop(0, nck, values, 0)
    o_s = acc_ref[:, 0:HEAD_D] / acc_ref[:, HEAD_D:2 * HEAD_D]

    g = g_ref[...]
    for h in range(nh):
        rs = slice(h * tq, (h + 1) * tq)
        o_ref[:, h * HEAD_D:(h + 1) * HEAD_D] = (
            g[:, 3 * h:3 * h + 1] * o_c[rs] + g[:, 3 * h + 1:3 * h + 2] * o_s[rs]
            + g[:, 3 * h + 2:3 * h + 3] * o_w[rs])


def _nsa_attn(q, kcmp, vcmp, kv, gates, ovt, emat, batch):
    tq, ck = 256, 512
    nq = SEQ // tq
    return pl.pallas_call(
        functools.partial(_nsa_attn_kernel, tq=tq, ck=ck),
        grid=(batch, nq),
        in_specs=[pl.BlockSpec((tq, 512), lambda b, i: (b * nq + i, 0)),
                  pl.BlockSpec((1, N_CMP_PAD, HEAD_D), lambda b, i: (b, 0, 0)),
                  pl.BlockSpec((1, N_CMP_PAD, HEAD_D), lambda b, i: (b, 0, 0)),
                  pl.BlockSpec((SEQ, 512), lambda b, i: (b, 0)),
                  pl.BlockSpec((tq, LANES), lambda b, i: (b * nq + i, 0)),
                  pl.BlockSpec((N_BLK, N_CMP_PAD), lambda b, i: (0, 0)),
                  pl.BlockSpec((SEQ // ck, N_BLK, ck), lambda b, i: (0, 0, 0))],
        out_specs=pl.BlockSpec((tq, 512), lambda b, i: (b * nq + i, 0)),
        out_shape=jax.ShapeDtypeStruct((batch * SEQ, 512), F32),
        scratch_shapes=[pltpu.VMEM((4 * tq, LANES), F32),
                        pltpu.VMEM((4 * tq, 2 * HEAD_D), F32),
                        pltpu.VMEM((SEQ // ck, 4 * tq, ck), F32)],
        compiler_params=_cparams(("parallel", "arbitrary")),
        name="nsa_attn",
    )(q, kcmp, vcmp, kv, gates, ovt, emat)


def _pool_kernel(u_ref, w_ref, b_ref, s_ref, o_ref, pad_ref):
    maxw = POOL_SIZES[-1]
    pad_ref[0:maxw, :] = jnp.zeros((maxw, W_GROUP), F32)
    pad_ref[maxw:maxw + SEQ, :] = u_ref[...].astype(F32)
    rc = 512
    for g, w in enumerate(POOL_SIZES):
        sl = slice(g * LANES, (g + 1) * LANES)
        for r in range(SEQ // rc):
            acc = pad_ref[maxw + r * rc:maxw + (r + 1) * rc, sl]
            tok = acc
            for j in range(1, w):
                acc = acc + pad_ref[maxw - j + r * rc:maxw - j + (r + 1) * rc, sl]
            t = r * rc + lax.broadcasted_iota(jnp.int32, (rc, 1), 0)
            cnt = jnp.minimum(t + 1, w).astype(F32)
            d = acc / cnt - tok
            y = _dot(d.astype(BF16), w_ref[g])
            o_ref[r * rc:(r + 1) * rc, sl] = (y + b_ref[:, sl]) * s_ref[:, sl]


def _pool(u, w, b, s, batch):
    return pl.pallas_call(
        _pool_kernel,
        grid=(batch,),
        in_specs=[pl.BlockSpec((SEQ, W_GROUP), lambda i: (i, 0)),
                  pl.BlockSpec((4, LANES, LANES), lambda i: (0, 0, 0)),
                  pl.BlockSpec((1, W_GROUP), lambda i: (0, 0)),
                  pl.BlockSpec((1, W_GROUP), lambda i: (0, 0))],
        out_specs=pl.BlockSpec((SEQ, W_GROUP), lambda i: (i, 0)),
        out_shape=jax.ShapeDtypeStruct((batch * SEQ, W_GROUP), F32),
        scratch_shapes=[pltpu.VMEM((SEQ + POOL_SIZES[-1], W_GROUP), F32)],
        compiler_params=_cparams(("parallel",)),
        name="pool",
    )(u, w, b, s)


def _s5_kernel(u_ref, wb_ref, ar_ref, ai_ref, wc_ref, d_ref, gw_ref, gb_ref, o_ref,
               bu_ref, st_ref, tm_ref, *, batch, tc):
    @pl.when(pl.program_id(0) == 0)
    def _():
        st_ref[...] = jnp.zeros(st_ref.shape, F32)

    nslab = W_GROUP // LANES
    for b in range(batch):
        for j in range(nslab):
            tm_ref[j, pl.ds(b, tc, stride=batch), :] = u_ref[b, :, j * LANES:(j + 1) * LANES].astype(F32)
    u = jnp.concatenate([tm_ref[j] for j in range(nslab)], axis=1)
    gl = 8 * S5_STATE
    ub = u.astype(BF16)
    for j in range(nslab):
        uj = ub[:, j * LANES:(j + 1) * LANES]
        for part in (0, S5_WIDTH):
            cs = slice(part + j * gl, part + (j + 1) * gl)
            bu_ref[:, cs] = _dot(uj, wb_ref[j * LANES:(j + 1) * LANES, cs])
    lc = 512
    unroll = 8
    for c in range(S5_WIDTH // lc):
        re = slice(c * lc, (c + 1) * lc)
        im = slice(S5_WIDTH + c * lc, S5_WIDTH + (c + 1) * lc)
        ar, ai = ar_ref[:, re], ai_ref[:, re]

        def body(tb, carry, re=re, im=im, ar=ar, ai=ai):
            xr, xi = carry
            for k in range(unroll):
                r0 = pl.multiple_of((tb * unroll + k) * batch, batch)
                nxr = ar * xr - ai * xi + bu_ref[pl.ds(r0, batch), re]
                nxi = ar * xi + ai * xr + bu_ref[pl.ds(r0, batch), im]
                bu_ref[pl.ds(r0, batch), re] = nxr
                bu_ref[pl.ds(r0, batch), im] = nxi
                xr, xi = nxr, nxi
            return xr, xi

        xr, xi = lax.fori_loop(0, tc // unroll, body, (st_ref[:, re], st_ref[:, im]))
        st_ref[:, re] = xr
        st_ref[:, im] = xi

    ys = []
    for j in range(nslab):
        osl = slice(j * LANES, (j + 1) * LANES)
        yj = 0.0
        for part in (0, S5_WIDTH):
            cs = slice(part + j * gl, part + (j + 1) * gl)
            yj = yj + _dot(bu_ref[:, cs].astype(BF16), wc_ref[cs, osl])
        ys.append(yj)
    y = jnp.concatenate(ys, axis=1) + d_ref[...] * u
    y = jax.nn.gelu(y)
    z = _dot(y.astype(BF16), gw_ref[...]) + gb_ref[...]
    o = y * jax.nn.sigmoid(z)
    for j in range(nslab):
        tm_ref[j] = o[:, j * LANES:(j + 1) * LANES]
    for b in range(batch):
        for j in range(nslab):
            o_ref[b, :, j * LANES:(j + 1) * LANES] = tm_ref[j, pl.ds(b, tc, stride=batch), :]


def _s5(u3, wb, ar, ai, wc, d, gw, gb):
    batch = u3.shape[0]
    tc = 64
    rows = tc * batch
    full = lambda shape: pl.BlockSpec(shape, lambda i: (0, 0))
    return pl.pallas_call(
        functools.partial(_s5_kernel, batch=batch, tc=tc),
        grid=(SEQ // tc,),
        in_specs=[pl.BlockSpec((batch, tc, W_GROUP), lambda i: (0, i, 0)),
                  full((W_GROUP, 2 * S5_WIDTH)), full((batch, S5_WIDTH)), full((batch, S5_WIDTH)),
                  full((2 * S5_WIDTH, W_GROUP)), full((1, W_GROUP)),
                  full((W_GROUP, W_GROUP)), full((1, W_GROUP))],
        out_specs=pl.BlockSpec((batch, tc, W_GROUP), lambda i: (0, i, 0)),
        out_shape=jax.ShapeDtypeStruct((batch, SEQ, W_GROUP), F32),
        scratch_shapes=[pltpu.VMEM((rows, 2 * S5_WIDTH), F32),
                        pltpu.VMEM((batch, 2 * S5_WIDTH), F32),
                        pltpu.VMEM((W_GROUP // LANES, rows, LANES), F32)],
        compiler_params=_cparams(("arbitrary",)),
        name="s5",
    )(u3, wb, ar, ai, wc, d, gw, gb)


def _out_proj_kernel(ya_ref, yb_ref, yc_ref, yd_ref, mg_ref, wo_ref, x_ref, fg_ref,
                     wrh_ref, wrl_ref, br_ref, x1_ref, h2_ref, lg_ref, *, tm):
    acc = x_ref[...]
    for gi, y_ref in enumerate((ya_ref, yb_ref, yc_ref, yd_ref)):
        sl = slice(gi * W_GROUP, (gi + 1) * W_GROUP)
        n = (_rms(y_ref[...]) * mg_ref[:, sl]).astype(BF16)
        acc = acc + _dot(n, wo_ref[sl, :])
    x1_ref[...] = acc
    h2 = _rms(acc) * fg_ref[...]
    hi = h2.astype(BF16)
    lo = (h2 - hi.astype(F32)).astype(BF16)
    lg_ref[...] = (_dot(hi, wrh_ref[...]) + _dot(hi, wrl_ref[...]) + _dot(lo, wrh_ref[...])
                   + br_ref[...])
    _pack_rows(h2_ref, h2, tm)


def _out_proj(ya, yb, yc, yd, mg, wo_all, layer, x, fg, wr_hi, wr_lo, br):
    t = x.shape[0]
    tm = 256
    yspec = pl.BlockSpec((tm, W_GROUP), lambda i: (i, 0))
    full = lambda shape: pl.BlockSpec(shape, lambda i: (0, 0))
    return pl.pallas_call(
        functools.partial(_out_proj_kernel, tm=tm),
        grid=(t // tm,),
        in_specs=[yspec, yspec, yspec, yspec, full((1, D_MODEL)),
                  pl.BlockSpec((None, D_MODEL, D_MODEL), lambda i: (layer, 0, 0)),
                  pl.BlockSpec((tm, D_MODEL), lambda i: (i, 0)), full((1, D_MODEL)),
                  full((D_MODEL, LANES)), full((D_MODEL, LANES)), full((1, LANES))],
        out_specs=[pl.BlockSpec((tm, D_MODEL), lambda i: (i, 0)),
                   pl.BlockSpec((tm * ROW_PITCH, LANES), lambda i: (i, 0)),
                   pl.BlockSpec((tm, LANES), lambda i: (i, 0))],
        out_shape=[jax.ShapeDtypeStruct((t, D_MODEL), F32),
                   jax.ShapeDtypeStruct((t * ROW_PITCH, LANES), F32),
                   jax.ShapeDtypeStruct((t, LANES), F32)],
        compiler_params=_cparams(("parallel",)),
        name="out_proj",
    )(ya, yb, yc, yd, mg, wo_all, x, fg, wr_hi, wr_lo, br)


def _route(lg, meta_ref, cnt_ref, carry_ref, tm):
    @pl.when(pl.program_id(0) == 0)
    def _():
        carry_ref[...] = jnp.zeros(carry_ref.shape, F32)

    lane = lax.broadcasted_iota(jnp.int32, (tm, LANES), 1)
    big = jnp.int32(1 << 20)
    rmax = lambda a: jnp.max(a, axis=-1, keepdims=True)
    rmin = lambda a: jnp.min(a, axis=-1, keepdims=True)
    rsum = lambda a: jnp.sum(a, axis=-1, keepdims=True)

    is_g = lane < MOE_GROUPS
    gl = jnp.where(is_g, lg, NEG_INF)
    gm = rmax(gl)
    p_top = 1.0 / rsum(jnp.where(is_g, jnp.exp(gl - gm), 0.0))
    g_top = rmin(jnp.where(is_g & (gl == gm), lane, big))

    is_e = (lane >= MOE_GROUPS) & (lane < MOE_GROUPS + MOE_EXPERTS) \
        & (((lane - MOE_GROUPS) // MOE_EPG) == g_top)
    el = jnp.where(is_e, lg, NEG_INF)
    ee = jnp.where(is_e, jnp.exp(el - rmax(el)), 0.0)
    p = jnp.where(is_e, ee / rsum(ee), -1.0)
    p1 = rmax(p)
    i1 = rmin(jnp.where(p == p1, lane, big))
    p_rest = jnp.where(lane == i1, -1.0, p)
    p2 = rmax(p_rest)
    i2 = rmin(jnp.where((p_rest == p2) & is_e & (lane != i1), lane, big))
    den = p1 + p2
    w1 = p_top * (p1 / den)
    w2 = p_top * (p2 / den)

    hit1, hit2 = lane == i1, lane == i2
    oh = (hit1 | hit2).astype(F32)
    r = lax.broadcasted_iota(jnp.int32, (tm, tm), 0)
    c = lax.broadcasted_iota(jnp.int32, (tm, tm), 1)
    before = _dot((c < r).astype(BF16), oh.astype(BF16)) + carry_ref[0:1, :]
    r1 = rsum(jnp.where(hit1, before, 0.0))
    r2 = rsum(jnp.where(hit2, before, 0.0))
    carry_ref[...] = carry_ref[...] + jnp.sum(oh, axis=0, keepdims=True)
    cnt_ref[...] = carry_ref[...]

    e1 = (i1 - MOE_GROUPS).astype(F32)
    e2 = (i2 - MOE_GROUPS).astype(F32)
    vals = (e1, e2, w1, w2, r1, r2)
    meta = jnp.zeros((tm, LANES), F32)
    for k, v in enumerate(vals):
        meta = jnp.where(lane == k, v, meta)
    meta_ref[...] = meta


def _router_kernel(lg_ref, meta_ref, cnt_ref, carry_ref, *, tm):
    _route(lg_ref[...], meta_ref, cnt_ref, carry_ref, tm)


def _router(logits):
    t = logits.shape[0]
    tm = 512
    return pl.pallas_call(
        functools.partial(_router_kernel, tm=tm),
        grid=(t // tm,),
        in_specs=[pl.BlockSpec((tm, LANES), lambda i: (i, 0))],
        out_specs=[pl.BlockSpec((tm, LANES), lambda i: (i, 0)),
                   pl.BlockSpec((8, LANES), lambda i: (i, 0))],
        out_shape=[jax.ShapeDtypeStruct((t, LANES), F32),
                   jax.ShapeDtypeStruct((t // tm * 8, LANES), F32)],
        scratch_shapes=[pltpu.VMEM((8, LANES), F32)],
        compiler_params=_cparams(("arbitrary",)),
        name="router",
    )(logits)


def _dest_kernel(meta_ref, cnt_ref, o_ref, *, tm):
    cnt = cnt_ref[...]
    hi = jnp.floor(cnt * (1.0 / 256.0))
    lo = cnt - 256.0 * hi
    r = lax.broadcasted_iota(jnp.int32, (LANES, LANES), 0)
    c = lax.broadcasted_iota(jnp.int32, (LANES, LANES), 1)
    before = (r < c).astype(BF16)
    start = (256.0 * _dot(hi.astype(BF16), before) + _dot(lo.astype(BF16), before))[0:1, :]
    meta = meta_ref[...]
    lane = lax.broadcasted_iota(jnp.int32, (tm, LANES), 1)
    rsum = lambda a: jnp.sum(a, axis=-1, keepdims=True)
    out = jnp.zeros((tm, LANES), F32)
    for k in range(2):
        e_lane = meta[:, k:k + 1].astype(jnp.int32) + MOE_GROUPS
        d = rsum(jnp.where(lane == e_lane, start, 0.0)) + meta[:, 4 + k:5 + k]
        out = jnp.where(lane == k, d, out)
    o_ref[...] = out.astype(jnp.int32)


def _dest(meta, counts):
    t = meta.shape[0]
    tm = 512
    last = counts.shape[0] // 8 - 1
    return pl.pallas_call(
        functools.partial(_dest_kernel, tm=tm),
        grid=(t // tm,),
        in_specs=[pl.BlockSpec((tm, LANES), lambda i: (i, 0)),
                  pl.BlockSpec((8, LANES), lambda i: (last, 0))],
        out_specs=pl.BlockSpec((tm, LANES), lambda i: (i, 0)),
        out_shape=jax.ShapeDtypeStruct((t, LANES), jnp.int32),
        compiler_params=_cparams(("parallel",)),
        name="dest",
    )(meta, counts)


def _dispatch_kernel(d1_ref, d2_ref, h_ref, xs_ref, sem, *, td):
    base = pl.program_id(0) * td

    def start(rb, carry):
        for k in range(DMA_UNROLL):
            r = rb * DMA_UNROLL + k
            src = h_ref.at[pl.ds(r * ROW_PITCH, ROW_PITCH)]
            for d_ref in (d1_ref, d2_ref):
                dst = xs_ref.at[pl.ds(d_ref[base + r] * ROW_PITCH, ROW_PITCH)]
                pltpu.make_async_copy(src, dst, sem).start()
        return carry

    lax.fori_loop(0, td // DMA_UNROLL, start, 0)
    for _ in range(2):
        pltpu.make_async_copy(h_ref, xs_ref.at[pl.ds(0, td * ROW_PITCH)], sem).wait()


def _dispatch(dest1, dest2, h2):
    t = dest1.shape[0]
    td = 512
    return pl.pallas_call(
        functools.partial(_dispatch_kernel, td=td),
        grid_spec=pltpu.PrefetchScalarGridSpec(
            num_scalar_prefetch=2,
            grid=(t // td,),
            in_specs=[pl.BlockSpec((td * ROW_PITCH, LANES), lambda i, *_: (i, 0))],
            out_specs=pl.BlockSpec(memory_space=pl.ANY),
            scratch_shapes=[pltpu.SemaphoreType.DMA(())]),
        out_shape=jax.ShapeDtypeStruct((2 * t * ROW_PITCH, LANES), F32),
        compiler_params=pltpu.CompilerParams(dimension_semantics=("arbitrary",),
                                             has_side_effects=True),
        name="dispatch",
    )(dest1, dest2, h2)


def _expert_kernel(vt_ref, ve_ref, vlo_ref, vhi_ref, vfirst_ref, vvalid_ref, vnew_ref, vnext_ref,
                   vslot_ref, xs_ref, wg_ref, wu_ref, wd_ref, ys_ref,
                   wgf_ref, wuf_ref, wdf_ref, wgb_ref, wub_ref, wdb_ref, sem, *, tmx, layer):
    v = pl.program_id(0)

    def weight_copies(e, slot):
        return [pltpu.make_async_copy(w_ref.at[layer, e], f_ref.at[slot], sem.at[slot])
                for w_ref, f_ref in ((wg_ref, wgf_ref), (wu_ref, wuf_ref), (wd_ref, wdf_ref))]

    @pl.when(vnew_ref[v] == 1)
    def _():
        slot = vslot_ref[v]

        @pl.when(v == 0)
        def _():
            for cp in weight_copies(ve_ref[v], slot):
                cp.start()

        for cp in weight_copies(ve_ref[v], slot):
            cp.wait()

        @pl.when(vnext_ref[v] >= 0)
        def _():
            for cp in weight_copies(vnext_ref[v], 1 - slot):
                cp.start()

        wgb_ref[...] = wgf_ref[slot].astype(BF16)
        wub_ref[...] = wuf_ref[slot].astype(BF16)
        wdb_ref[...] = wdf_ref[slot].astype(BF16)

    @pl.when(vvalid_ref[v] == 1)
    def _():
        x = jnp.concatenate(_unpack_rows(lambda rows: xs_ref[rows, :], tmx), axis=1).astype(BF16)
        a = _dot(x, wgb_ref[...])
        u = _dot(x, wub_ref[...])
        hid = (jax.nn.silu(a) * u).astype(BF16)
        y = _dot(hid, wdb_ref[...])
        rows = lax.broadcasted_iota(jnp.int32, (tmx, 1), 0)
        mine = (rows >= vlo_ref[v]) & (rows < vhi_ref[v])

        @pl.when(vfirst_ref[v] == 1)
        def _():
            _pack_rows(ys_ref, y, tmx, mask=mine)

        @pl.when(vfirst_ref[v] == 0)
        def _():
            _pack_rows(ys_ref, y, tmx, mask=mine, old=True)


def _experts(sched, xs, wg, wu, wd, layer, tmx):
    nvis = sched[0].shape[0]
    rows = xs.shape[0]
    xspec = pl.BlockSpec((tmx * ROW_PITCH, LANES), lambda v, vt, *_: (vt[v], 0))
    hbm = pl.BlockSpec(memory_space=pl.ANY)
    return pl.pallas_call(
        functools.partial(_expert_kernel, tmx=tmx, layer=layer),
        grid_spec=pltpu.PrefetchScalarGridSpec(
            num_scalar_prefetch=9,
            grid=(nvis,),
            in_specs=[xspec, hbm, hbm, hbm],
            out_specs=xspec,
            scratch_shapes=[pltpu.VMEM((2, D_MODEL, MOE_HIDDEN), F32),
                            pltpu.VMEM((2, D_MODEL, MOE_HIDDEN), F32),
                            pltpu.VMEM((2, MOE_HIDDEN, D_MODEL), F32),
                            pltpu.VMEM((D_MODEL, MOE_HIDDEN), BF16),
                            pltpu.VMEM((D_MODEL, MOE_HIDDEN), BF16),
                            pltpu.VMEM((MOE_HIDDEN, D_MODEL), BF16),
                            pltpu.SemaphoreType.DMA((2,))]),
        out_shape=jax.ShapeDtypeStruct((rows, LANES), F32),
        compiler_params=_cparams(("arbitrary",)),
        name="experts",
    )(*sched, xs, wg, wu, wd)


def _combine_kernel(d1_ref, d2_ref, ys_ref, meta_ref, x1_ref, fg_ref, o_ref,
                    b1_ref, b2_ref, sem, *, tc, final):
    i = pl.program_id(0)
    n = pl.num_programs(0)
    slot = i % 2

    def gather(tile, slot):
        def start(rb, carry):
            for k in range(DMA_UNROLL):
                r = rb * DMA_UNROLL + k
                dst = pl.ds(r * ROW_PITCH, ROW_CHUNKS)
                for d_ref, b_ref in ((d1_ref, b1_ref), (d2_ref, b2_ref)):
                    src = ys_ref.at[pl.ds(d_ref[tile * tc + r] * ROW_PITCH, ROW_CHUNKS)]
                    pltpu.make_async_copy(src, b_ref.at[slot, dst], sem.at[slot]).start()
            return carry

        lax.fori_loop(0, tc // DMA_UNROLL, start, 0)

    @pl.when(i == 0)
    def _():
        gather(0, 0)

    @pl.when(i + 1 < n)
    def _():
        gather(i + 1, 1 - slot)

    for b_ref in (b1_ref, b2_ref):
        pltpu.make_async_copy(ys_ref.at[pl.ds(0, tc * ROW_CHUNKS)],
                              b_ref.at[slot, pl.ds(0, tc * ROW_CHUNKS)], sem.at[slot]).wait()
    w1 = meta_ref[:, 2:3]
    w2 = meta_ref[:, 3:4]
    y1 = _unpack_rows(lambda rows: b1_ref[slot, rows, :], tc)
    y2 = _unpack_rows(lambda rows: b2_ref[slot, rows, :], tc)
    x2 = jnp.concatenate([x1_ref[:, c * LANES:(c + 1) * LANES] + (w1 * y1[c] + w2 * y2[c])
                          for c in range(ROW_CHUNKS)], axis=1)
    if final:
        x2 = _rms(x2) * fg_ref[...]
    o_ref[...] = x2


def _combine(dest1, dest2, ys, meta, x1, fg, final):
    t = x1.shape[0]
    tc = 256
    return pl.pallas_call(
        functools.partial(_combine_kernel, tc=tc, final=final),
        grid_spec=pltpu.PrefetchScalarGridSpec(
            num_scalar_prefetch=2,
            grid=(t // tc,),
            in_specs=[pl.BlockSpec(memory_space=pl.ANY),
                      pl.BlockSpec((tc, LANES), lambda i, *_: (i, 0)),
                      pl.BlockSpec((tc, D_MODEL), lambda i, *_: (i, 0)),
                      pl.BlockSpec((1, D_MODEL), lambda i, *_: (0, 0))],
            out_specs=pl.BlockSpec((tc, D_MODEL), lambda i, *_: (i, 0)),
            scratch_shapes=[pltpu.VMEM((2, tc * ROW_PITCH, LANES), F32),
                            pltpu.VMEM((2, tc * ROW_PITCH, LANES), F32),
                            pltpu.SemaphoreType.DMA((2,))]),
        out_shape=jax.ShapeDtypeStruct((t, D_MODEL), F32),
        compiler_params=_cparams(("arbitrary",)),
        name="combine",
    )(dest1, dest2, ys, meta, x1, fg)


def _rope_tables(r0):
    inv = ROPE_THETA ** (-jnp.arange(0, 2 * ROPE_HALF, 2, dtype=F32) / (2 * ROPE_HALF))
    ang = jnp.arange(SEQ, dtype=F32)[:, None] * inv[None, :]
    cos, sin = jnp.cos(ang), jnp.sin(ang)
    c = jnp.ones((SEQ, LANES), F32).at[:, r0:r0 + ROPE_HALF].set(cos)
    c = c.at[:, r0 + ROPE_HALF:r0 + 2 * ROPE_HALF].set(cos)
    sa = jnp.zeros((SEQ, LANES), F32).at[:, r0:r0 + ROPE_HALF].set(-sin)
    sb = jnp.zeros((SEQ, LANES), F32).at[:, r0 + ROPE_HALF:r0 + 2 * ROPE_HALF].set(sin)
    return c, sa, sb


def _pad_cols(w, width):
    return jnp.pad(w, ((0, 0), (0, width - w.shape[1])))


def _s5_params(a_re, a_im, log_dt, b_re, b_im, c_re, c_im, batch):
    dt = jnp.exp(log_dt)[:, None]
    mag = jnp.exp(a_re * dt)
    abar_r, abar_i = mag * jnp.cos(a_im * dt), mag * jnp.sin(a_im * dt)
    den = a_re * a_re + a_im * a_im
    nr, ni = abar_r - 1.0, abar_i
    coef_r = (nr * a_re + ni * a_im) / den
    coef_i = (ni * a_re - nr * a_im) / den
    bbar_r = coef_r[..., None] * b_re - coef_i[..., None] * b_im
    bbar_i = coef_r[..., None] * b_im + coef_i[..., None] * b_re
    eye = jnp.eye(S5_GROUPS, dtype=F32)
    blk_b = lambda m: jnp.einsum('gpc,gh->gchp', m, eye).reshape(W_GROUP, S5_WIDTH)
    wb = jnp.concatenate([blk_b(bbar_r), blk_b(bbar_i)], axis=1).astype(BF16)
    blk_c = lambda m: jnp.einsum('gcp,gh->gphc', m, eye).reshape(S5_WIDTH, W_GROUP)
    wc = jnp.concatenate([blk_c(c_re), blk_c(-c_im)], axis=0).astype(BF16)
    ar = jnp.broadcast_to(abar_r.reshape(1, S5_WIDTH), (batch, S5_WIDTH))
    ai = jnp.broadcast_to(abar_i.reshape(1, S5_WIDTH), (batch, S5_WIDTH))
    return wb, ar, ai, wc


def _nsa_consts(ck):
    c_start = np.arange(N_CMP_PAD) * CMP_STRIDE
    b_start = np.arange(N_BLK) * SEL_BLOCK
    ov = ((c_start[None, :] < b_start[:, None] + SEL_BLOCK)
          & (c_start[None, :] + CMP_BLOCK > b_start[:, None])).astype(np.float32)
    key_blk = np.arange(SEQ) // SEL_BLOCK
    e = (key_blk[None, :] == np.arange(N_BLK)[:, None]).astype(np.float32)
    e = e.reshape(N_BLK, SEQ // ck, ck).transpose(1, 0, 2)
    return jnp.asarray(ov), jnp.asarray(e, dtype=BF16)


def _lane_cumsum(v):
    r = lax.broadcasted_iota(jnp.int32, (LANES, LANES), 0)
    c = lax.broadcasted_iota(jnp.int32, (LANES, LANES), 1)
    incl = (r <= c).astype(BF16)
    hi = jnp.floor(v * (1.0 / 256.0))
    lo = v - 256.0 * hi
    return 256.0 * _dot(hi.astype(BF16), incl) + _dot(lo.astype(BF16), incl)


def _sched_kernel(cnt_ref, o_ref, *, tmx, nv):
    cnt = cnt_ref[...]
    ends = _lane_cumsum(cnt)
    offs = ends - cnt
    first = jnp.floor(offs * (1.0 / tmx))
    last = jnp.floor(jnp.maximum(ends - 1.0, 0.0) * (1.0 / tmx))
    nvis = jnp.where(cnt > 0.0, last - first + 1.0, 0.0)
    cumv = _lane_cumsum(nvis)
    row1 = lambda a: a[0:1, :]
    total = jnp.max(row1(cumv), axis=-1, keepdims=True)
    rsum = lambda a: jnp.sum(a, axis=-1, keepdims=True)
    v = lax.broadcasted_iota(jnp.int32, (nv, LANES), 0).astype(F32)
    lane = lax.broadcasted_iota(jnp.int32, (nv, LANES), 1).astype(F32)
    vc = jnp.minimum(v, total - 1.0)
    e_lane = rsum((row1(cumv) <= vc).astype(F32))
    hit = lane == e_lane
    pick = lambda a: rsum(jnp.where(hit, row1(a), 0.0))
    vt = pick(first) + vc[:, 0:1] - (pick(cumv) - pick(nvis))
    vlo = jnp.clip(pick(offs) - vt * tmx, 0.0, float(tmx))
    vhi = jnp.clip(pick(ends) - vt * tmx, 0.0, float(tmx))
    changed = lambda a: (v == 0.0) | (a != pltpu.roll(a, 1, 0))
    vfirst = changed(jnp.broadcast_to(vt, (nv, LANES))).astype(F32)
    vnew = changed(jnp.broadcast_to(e_lane, (nv, LANES))).astype(F32)
    vvalid = (v < total).astype(F32)
    nonempty = (cnt > 0.0).astype(F32)
    order = pick(_lane_cumsum(nonempty) - nonempty)
    vslot = order - 2.0 * jnp.floor(order * 0.5)
    far = float(1 << 20)
    nxt = jnp.min(jnp.where((row1(nonempty) > 0.0) & (lane > e_lane), lane, far), axis=-1, keepdims=True)
    vnext = jnp.where(nxt >= far, -1.0, nxt - MOE_GROUPS)
    out = jnp.zeros((nv, LANES), F32)
    cols = (vt, e_lane - MOE_GROUPS, vlo, vhi, vfirst, vvalid, vnew, vnext, vslot)
    for k, col in enumerate(cols):
        out = jnp.where(lane == k, col, out)
    o_ref[...] = out.astype(jnp.int32)


def _moe_schedule(counts, n_rows, tmx):
    nvis_max = n_rows // tmx + MOE_EXPERTS
    nv = 256
    last = counts.shape[0] // 8 - 1
    sched = pl.pallas_call(
        functools.partial(_sched_kernel, tmx=tmx, nv=nv),
        grid=(1,),
        in_specs=[pl.BlockSpec((8, LANES), lambda i: (last, 0))],
        out_specs=pl.BlockSpec((nv, LANES), lambda i: (0, 0)),
        out_shape=jax.ShapeDtypeStruct((nv, LANES), jnp.int32),
        name="sched",
    )(counts)
    return tuple(sched[:nvis_max, k] for k in range(9))


def kernel(x, attn_norm_g, w_in, mla_q_norm_g, mla_kv_norm_g, mla_w_uq, mla_w_ukv, nsa_cmp_pe, nsa_cmp_w1, nsa_cmp_w2, pool_w, pool_b, pool_scale, s5_a_re, s5_a_im, s5_log_dt, s5_b_re, s5_b_im, s5_c_re, s5_c_im, s5_d, s5_glu_w, s5_glu_b, mix_norm_g, w_out, ffn_norm_g, moe_w_group, moe_b_group, moe_w_expert, moe_b_expert, moe_w_gate, moe_w_up, moe_w_down, final_norm_g):
    batch, seq, d = x.shape
    depth = w_in.shape[0]
    t = batch * seq
    xf = x.reshape(t, d)
    qtabs = _rope_tables(MLA_NOPE)
    ktabs = _rope_tables(0)
    ovt, emat = _nsa_consts(512)
    row = lambda v: v.reshape(1, -1)
    tmx = 256
    w_in_all = _w_in_prep(w_in)
    w_out_all = w_out.astype(BF16)

    for l in range(depth):
        wuq = jnp.pad(mla_w_uq[l], ((0, 512 - MLA_Q_LORA), (0, 0))).astype(BF16)
        ukv = mla_w_ukv[l].reshape(HEAD_D, MLA_HEADS, MLA_NOPE + MLA_V)
        wk = jnp.pad(ukv[:, :, :MLA_NOPE], ((0, 0), (0, 0), (0, HEAD_D - MLA_NOPE)))
        wk = wk.reshape(HEAD_D, 512).astype(BF16)
        wv = ukv[:, :, MLA_NOPE:].reshape(HEAD_D, 512).astype(BF16)
        qg = jnp.pad(mla_q_norm_g[l], (0, 512 - MLA_Q_LORA)).reshape(1, 512)
        q_a, k_a, v_a, q_b, kc, vc, kv_b, gates, u_pool, u_s5 = _in_proj(
            xf, row(attn_norm_g[l]), w_in_all, l, qg, row(mla_kv_norm_g[l]), wuq, wk, wv, qtabs, ktabs)

        y_a = _mla_attn(q_a, k_a, v_a, batch)

        pe = jnp.broadcast_to(nsa_cmp_pe[l].reshape(2, 1, CMP_BLOCK * HEAD_D),
                              (2, 8, CMP_BLOCK * HEAD_D)).astype(BF16)
        kcmp, vcmp = _nsa_compress(kc, vc, nsa_cmp_w1[l].astype(BF16), nsa_cmp_w2[l].astype(BF16),
                                   pe, batch)
        y_b = _nsa_attn(q_b, kcmp, vcmp, kv_b, gates, ovt, emat, batch)

        y_c = _pool(u_pool, pool_w[l].astype(BF16), row(pool_b[l]), row(pool_scale[l]), batch)

        wb, ar, ai, wc = _s5_params(s5_a_re[l], s5_a_im[l], s5_log_dt[l], s5_b_re[l], s5_b_im[l],
                                    s5_c_re[l], s5_c_im[l], batch)
        y_d = _s5(u_s5.reshape(batch, seq, W_GROUP), wb, ar, ai, wc, row(s5_d[l]),
                  s5_glu_w[l].astype(BF16), row(s5_glu_b[l])).reshape(t, W_GROUP)

        wr = jnp.concatenate([moe_w_group[l], moe_w_expert[l]], axis=1)
        wr = _pad_cols(wr, LANES)
        wr_hi = wr.astype(BF16)
        wr_lo = (wr - wr_hi.astype(F32)).astype(BF16)
        br = jnp.pad(jnp.concatenate([moe_b_group[l], moe_b_expert[l]]), (0, LANES - 36)).reshape(1, LANES)
        x1, h2, logits = _out_proj(y_a, y_b, y_c, y_d, row(mix_norm_g[l]), w_out_all, l,
                                   xf, row(ffn_norm_g[l]), wr_hi, wr_lo, br)

        meta, counts = _router(logits)
        dest = _dest(meta, counts)
        dest1, dest2 = dest[:, 0], dest[:, 1]
        sched = _moe_schedule(counts, 2 * t, tmx)
        xs = _dispatch(dest1, dest2, h2)
        ys = _experts(sched, xs, moe_w_gate, moe_w_up, moe_w_down, l, tmx)
        xf = _combine(dest1, dest2, ys, meta, x1, row(final_norm_g), final=(l == depth - 1))

    return xf.reshape(batch, seq, d)
```

```python
import functools
import math

import numpy as np
import jax
import jax.numpy as jnp
from jax import lax
from jax.experimental import pallas as pl
from jax.experimental.pallas import tpu as pltpu

F32 = jnp.float32
BF16 = jnp.bfloat16

D_MODEL = 2048
SEQ = 2048
W_GROUP = 512
LANES = 128
ROW_CHUNKS = D_MODEL // LANES
ROW_PITCH = ROW_CHUNKS + 1
DMA_UNROLL = 4

ROPE_THETA = 500000.0
ROPE_HALF = 16
NEG_INF = -1.0e30
FORCE_SCORE = 1.0e4
EPS = 1e-6

MLA_HEADS = 4
MLA_Q_LORA = 448
MLA_NOPE = 96
MLA_V = 128
HEAD_D = 128

CMP_BLOCK = 32
CMP_STRIDE = 16
SEL_BLOCK = 64
N_SEL = 8
N_LOCAL = 2
WINDOW = 512
N_CMP_PAD = SEQ // CMP_STRIDE
N_BLK = SEQ // SEL_BLOCK

POOL_SIZES = (2, 4, 8, 16)
S5_GROUPS = 32
S5_CH = 16
S5_STATE = 64
S5_WIDTH = S5_GROUPS * S5_STATE

MOE_GROUPS = 4
MOE_EPG = 8
MOE_EXPERTS = 32
MOE_HIDDEN = 512

COL_CQ, COL_NQ, COL_POOL, COL_S5 = 0, 512, 1024, 1536
COL_CKV, COL_KR, COL_KV6, COL_GL = 2048, 2176, 2304, 3072
N_IN_PAD = 3200

VMEM_LIMIT = 56 * 1024 * 1024


def _cparams(sem, vmem=VMEM_LIMIT):
    return pltpu.CompilerParams(dimension_semantics=sem, vmem_limit_bytes=vmem)


def _rms(x, n=None):
    n = x.shape[-1] if n is None else n
    return x * lax.rsqrt(jnp.sum(x * x, axis=-1, keepdims=True) / n + EPS)


def _dot(a, b):
    return jnp.dot(a, b, preferred_element_type=F32)


def _dot_nt(a, b, precision=None):
    return lax.dot_general(a, b, (((1,), (1,)), ((), ())), preferred_element_type=F32,
                           precision=precision)


def _rope(x, c, sa, sb):
    return x * c + pltpu.roll(x, LANES - ROPE_HALF, 1) * sa + pltpu.roll(x, ROPE_HALF, 1) * sb


def _pack_rows(ref, x, n, mask=None, old=False):
    for s in range(ROW_CHUNKS):
        rows = pl.ds(s, n, stride=ROW_PITCH)
        w = x[:, s * LANES:(s + 1) * LANES]
        if mask is not None:
            w = jnp.where(mask, w, ref[rows, :] if old else 0.0)
        ref[rows, :] = w
    if not old:
        ref[pl.ds(ROW_CHUNKS, n, stride=ROW_PITCH), :] = jnp.zeros((n, LANES), F32)


def _unpack_rows(load, n):
    return [load(pl.ds(s, n, stride=ROW_PITCH)) for s in range(ROW_CHUNKS)]


_W_IN_SEGMENTS = ((COL_CQ, 0, 448), (COL_NQ, 608, 512), (COL_POOL, 1900, 512), (COL_S5, 2412, 512),
                  (COL_CKV, 448, 128), (COL_KR, 576, 32), (COL_KV6, 1120, 768), (COL_GL, 1888, 12))


def _w_in_prep_kernel(w_ref, o_ref):
    o_ref[0] = jnp.zeros(o_ref.shape[1:], BF16)
    for dst, src, width in _W_IN_SEGMENTS:
        o_ref[0, :, dst:dst + width] = w_ref[0, :, src:src + width].astype(BF16)


def _w_in_prep(w_in):
    depth, d, n = w_in.shape
    tk = 256
    return pl.pallas_call(
        _w_in_prep_kernel,
        grid=(depth, d // tk),
        in_specs=[pl.BlockSpec((1, tk, n), lambda l, k: (l, k, 0))],
        out_specs=pl.BlockSpec((1, tk, N_IN_PAD), lambda l, k: (l, k, 0)),
        out_shape=jax.ShapeDtypeStruct((depth, d, N_IN_PAD), BF16),
        compiler_params=_cparams(("parallel", "parallel")),
        name="w_in_prep",
    )(w_in)


def _in_proj_kernel(x_ref, g_ref, w_ref, qg_ref, kvg_ref, wuq_ref, wk_ref, wv_ref,
                    qc_ref, qsa_ref, qsb_ref, kc_ref, ksa_ref, ksb_ref,
                    qa_out, ka_out, va_out, qb_out, kcmp_out, vcmp_out, kvb_out, gate_out,
                    pool_out, s5_out):
    scale = 1.0 / math.sqrt(HEAD_D)
    h = (_rms(x_ref[...]) * g_ref[...]).astype(BF16)
    seg = lambda col, width: _dot(h, w_ref[:, col:col + width])
    heads = [slice(n * HEAD_D, (n + 1) * HEAD_D) for n in range(4)]

    qn = (_rms(seg(COL_CQ, 512), MLA_Q_LORA) * qg_ref[...]).astype(BF16)
    q = _dot(qn, wuq_ref[...])
    kvn = (_rms(seg(COL_CKV, LANES)) * kvg_ref[...]).astype(BF16)
    kn = _dot(kvn, wk_ref[...])
    va_out[...] = _dot(kvn, wv_ref[...]).astype(BF16)
    kc, ksa, ksb = kc_ref[...], ksa_ref[...], ksb_ref[...]
    kr = pltpu.roll(_rope(seg(COL_KR, LANES), kc, ksa, ksb), MLA_NOPE, 1)
    for sl in heads:
        qa_out[:, sl] = (_rope(q[:, sl], qc_ref[...], qsa_ref[...], qsb_ref[...]) * scale).astype(BF16)
        ka_out[:, sl] = (kn[:, sl] + kr).astype(BF16)

    qb = seg(COL_NQ, 512)
    for sl in heads:
        qb_out[:, sl] = (_rope(qb[:, sl], kc, ksa, ksb) * scale).astype(BF16)
    kv = seg(COL_KV6, 6 * HEAD_D)
    part = lambda n: kv[:, n * HEAD_D:(n + 1) * HEAD_D]
    kcmp_out[...] = _rope(part(0), kc, ksa, ksb)
    vcmp_out[...] = part(1)
    kvb_out[:, 0:128] = _rope(part(2), kc, ksa, ksb).astype(BF16)
    kvb_out[:, 128:256] = part(3).astype(BF16)
    kvb_out[:, 256:384] = _rope(part(4), kc, ksa, ksb).astype(BF16)
    kvb_out[:, 384:512] = part(5).astype(BF16)
    gate_out[...] = jax.nn.sigmoid(seg(COL_GL, LANES))

    pool_out[...] = seg(COL_POOL, W_GROUP).astype(BF16)
    s5_out[...] = seg(COL_S5, W_GROUP).astype(BF16)


def _in_proj(x, g, w_all, layer, qg, kvg, wuq, wk, wv, qtabs, ktabs):
    t = x.shape[0]
    tm = 512
    nsb = SEQ // tm
    tab = pl.BlockSpec((tm, LANES), lambda i: (i % nsb, 0))
    full = lambda shape: pl.BlockSpec(shape, lambda i: (0, 0))
    out = lambda width: pl.BlockSpec((tm, width), lambda i: (i, 0))
    widths = (512, 512, 512, 512, LANES, LANES, 512, LANES, W_GROUP, W_GROUP)
    dtypes = (BF16, BF16, BF16, BF16, F32, F32, BF16, F32, BF16, BF16)
    return pl.pallas_call(
        _in_proj_kernel,
        grid=(t // tm,),
        in_specs=[pl.BlockSpec((tm, D_MODEL), lambda i: (i, 0)),
                  full((1, D_MODEL)),
                  pl.BlockSpec((None, D_MODEL, N_IN_PAD), lambda i: (layer, 0, 0)),
                  full((1, 512)), full((1, LANES)),
                  full((512, 512)), full((LANES, 512)), full((LANES, 512)),
                  tab, tab, tab, tab, tab, tab],
        out_specs=[out(w) for w in widths],
        out_shape=[jax.ShapeDtypeStruct((t, w), d) for w, d in zip(widths, dtypes)],
        compiler_params=_cparams(("parallel",)),
        name="in_proj",
    )(x, g, w_all, qg, kvg, wuq, wk, wv, *qtabs, *ktabs)


def _fold_lanes(a, op):
    out = a[:, 0:LANES]
    for c in range(1, a.shape[1] // LANES):
        out = op(out, a[:, c * LANES:(c + 1) * LANES])
    return out


def _mla_attn_kernel(q_ref, k_ref, v_ref, o_ref, s_ref, mx_ref, acc_ref, *, tq, tk):
    i = pl.program_id(1)
    nfull = (i * tq) // tk
    t0 = pl.multiple_of(nfull * tk, tk)
    row = i * tq + lax.broadcasted_iota(jnp.int32, (tq, tk), 0)
    col = t0 + lax.broadcasted_iota(jnp.int32, (tq, tk), 1)
    heads = [slice(h * HEAD_D, (h + 1) * HEAD_D) for h in range(MLA_HEADS)]
    chunk = lambda j: pl.ds(pl.multiple_of(j * tk, tk), tk)

    for h, sl in enumerate(heads):
        s = jnp.where(col <= row, _dot_nt(q_ref[:, sl], k_ref[pl.ds(t0, tk), sl]), NEG_INF)
        s_ref[h, nfull] = s
        mx_ref[h] = _fold_lanes(s, jnp.maximum)

    def scores(j, carry):
        for h, sl in enumerate(heads):
            s = _dot_nt(q_ref[:, sl], k_ref[chunk(j), sl])
            s_ref[h, j] = s
            mx_ref[h] = jnp.maximum(mx_ref[h], _fold_lanes(s, jnp.maximum))
        return carry

    lax.fori_loop(0, nfull, scores, 0)
    for h in range(MLA_HEADS):
        m = jnp.max(mx_ref[h], axis=-1, keepdims=True)
        mx_ref[h] = jnp.broadcast_to(m, (tq, LANES))
        acc_ref[h] = jnp.zeros((tq, 2 * HEAD_D), F32)

    ones = jnp.ones((tk, LANES), BF16)

    def values(j, carry):
        for h, sl in enumerate(heads):
            m = mx_ref[h]
            s = s_ref[h, j]
            p = jnp.concatenate([jnp.exp((s[:, c * LANES:(c + 1) * LANES] - m).astype(BF16))
                                 for c in range(tk // LANES)], axis=1)
            v_aug = jnp.concatenate([v_ref[chunk(j), sl], ones], axis=1)
            acc_ref[h] = acc_ref[h] + _dot(p, v_aug)
        return carry

    lax.fori_loop(0, nfull + 1, values, 0)
    for h, sl in enumerate(heads):
        o_ref[:, sl] = acc_ref[h, :, 0:HEAD_D] / acc_ref[h, :, HEAD_D:2 * HEAD_D]


def _mla_attn(q, k, v, batch):
    tq, tk = 256, 512
    nq = SEQ // tq
    return pl.pallas_call(
        functools.partial(_mla_attn_kernel, tq=tq, tk=tk),
        grid=(batch, nq),
        in_specs=[pl.BlockSpec((tq, W_GROUP), lambda b, i: (b * nq + i, 0)),
                  pl.BlockSpec((SEQ, W_GROUP), lambda b, i: (b, 0)),
                  pl.BlockSpec((SEQ, W_GROUP), lambda b, i: (b, 0))],
        out_specs=pl.BlockSpec((tq, W_GROUP), lambda b, i: (b * nq + i, 0)),
        out_shape=jax.ShapeDtypeStruct((batch * SEQ, W_GROUP), F32),
        scratch_shapes=[pltpu.VMEM((MLA_HEADS, SEQ // tk, tq, tk), F32),
                        pltpu.VMEM((MLA_HEADS, tq, LANES), F32),
                        pltpu.VMEM((MLA_HEADS, tq, 2 * HEAD_D), F32)],
        compiler_params=_cparams(("parallel", "arbitrary")),
        name="mla_attn",
    )(q, k, v)


def _nsa_compress_kernel(xk_ref, xv_ref, w1_ref, w2_ref, pe_ref, k_out, v_out):
    for c, (x_ref, o_ref) in enumerate(((xk_ref, k_out), (xv_ref, v_out))):
        a = jnp.zeros((N_CMP_PAD, HEAD_D), F32)
        b = jnp.zeros((N_CMP_PAD, HEAD_D), F32)
        for r in range(CMP_STRIDE):
            x = x_ref[pl.ds(r, N_CMP_PAD, stride=CMP_STRIDE), :].astype(BF16)
            a = a + _dot(x, w1_ref[c, r * HEAD_D:(r + 1) * HEAD_D, :])
            b = b + _dot(x, w1_ref[c, (CMP_STRIDE + r) * HEAD_D:(CMP_STRIDE + r + 1) * HEAD_D, :])
        b = pltpu.roll(b, N_CMP_PAD - 1, 0)
        pe = _dot(pe_ref[c], w1_ref[c])[0:1, :]
        hid = jax.nn.gelu(a + b + pe)
        o_ref[0] = _dot(hid.astype(BF16), w2_ref[c]).astype(BF16)


def _nsa_compress(xk, xv, w1, w2, pe, batch):
    xspec = pl.BlockSpec((SEQ, HEAD_D), lambda b: (b, 0))
    ospec = pl.BlockSpec((1, N_CMP_PAD, HEAD_D), lambda b: (b, 0, 0))
    return pl.pallas_call(
        _nsa_compress_kernel,
        grid=(batch,),
        in_specs=[xspec, xspec,
                  pl.BlockSpec((2, CMP_BLOCK * HEAD_D, HEAD_D), lambda b: (0, 0, 0)),
                  pl.BlockSpec((2, HEAD_D, HEAD_D), lambda b: (0, 0, 0)),
                  pl.BlockSpec((2, 8, CMP_BLOCK * HEAD_D), lambda b: (0, 0, 0))],
        out_specs=[ospec, ospec],
        out_shape=[jax.ShapeDtypeStruct((batch, N_CMP_PAD, HEAD_D), BF16)] * 2,
        compiler_params=_cparams(("parallel",)),
        name="nsa_compress",
    )(xk, xv, w1, w2, pe)


def _softmax_rows(s):
    m = jnp.max(s, axis=-1, keepdims=True)
    p = jnp.exp(s - m)
    return p / jnp.sum(p, axis=-1, keepdims=True)


def _nsa_attn_kernel(q_ref, kc_ref, vc_ref, kv_ref, g_ref, ovt_ref, e_ref, o_ref,
                     m_ref, acc_ref, s_ref, *, tq, ck):
    i = pl.program_id(1)
    q0 = i * tq
    nh = 4
    qs = jnp.concatenate([q_ref[:, h * HEAD_D:(h + 1) * HEAD_D] for h in range(nh)], axis=0)
    qpos = q0 + lax.broadcasted_iota(jnp.int32, (tq, 1), 0)
    masked = lambda s, ok: (s.reshape(nh, tq, s.shape[1])
                            + jnp.where(ok, 0.0, NEG_INF)[None]).reshape(s.shape)

    n_idx = lax.broadcasted_iota(jnp.int32, (tq, N_CMP_PAD), 1)
    valid_c = n_idx * CMP_STRIDE + (CMP_BLOCK - 1) <= qpos
    pc = _softmax_rows(masked(_dot_nt(qs, kc_ref[0]), valid_c)).reshape(nh, tq, N_CMP_PAD)
    pc = jnp.where(valid_c[None], pc, 0.0)
    psum = pc[0] + pc[1] + pc[2] + pc[3]
    o_c = _dot(pc.reshape(nh * tq, N_CMP_PAD).astype(BF16), vc_ref[0])

    imp_t = _dot_nt(ovt_ref[...], psum, precision=lax.Precision.HIGHEST)
    kblk = lax.broadcasted_iota(jnp.int32, (N_BLK, tq), 0)
    cur = (q0 + lax.broadcasted_iota(jnp.int32, (N_BLK, tq), 1)) // SEL_BLOCK
    forced = (kblk == 0) | ((kblk <= cur) & (kblk > cur - N_LOCAL))
    score = jnp.where(forced, FORCE_SCORE, jnp.where(kblk <= cur, imp_t, -1.0))
    cnt = jnp.zeros((N_BLK, tq), F32)
    for j in range(N_BLK):
        sj = score[j:j + 1, :]
        beats = (sj > score) | ((sj == score) & (kblk > j))
        cnt = cnt + beats.astype(F32)
    sel = jnp.transpose((cnt < N_SEL).astype(F32)).astype(BF16)

    wlen = WINDOW + tq
    w0 = pl.multiple_of(jnp.maximum(q0 - WINDOW, 0), tq)
    sw = _dot_nt(qs, kv_ref[pl.ds(w0, wlen), 256:384])
    kpos = w0 + lax.broadcasted_iota(jnp.int32, (tq, wlen), 1)
    ok = (kpos <= qpos) & (kpos > qpos - WINDOW)
    sw = masked(sw, ok)
    pw = jnp.exp((sw - jnp.max(_fold_lanes(sw, jnp.maximum), axis=-1, keepdims=True)).astype(BF16))
    aug = lambda v: jnp.concatenate([v, jnp.ones(v.shape, BF16)], axis=1)
    ow = _dot(pw, aug(kv_ref[pl.ds(w0, wlen), 384:512]))
    o_w = ow[:, 0:HEAD_D] / ow[:, HEAD_D:2 * HEAD_D]

    nck = q0 // ck + 1
    m_ref[...] = jnp.full(m_ref.shape, NEG_INF, F32)

    def scores(c, carry):
        r0 = pl.multiple_of(c * ck, ck)
        chosen = _dot(sel, e_ref[c])
        kpos = r0 + lax.broadcasted_iota(jnp.int32, (tq, ck), 1)
        ok = (chosen > 0.5) & (kpos <= qpos)
        s = masked(_dot_nt(qs, kv_ref[pl.ds(r0, ck), 0:128]), ok)
        s_ref[c] = s
        m_ref[...] = jnp.maximum(m_ref[...], _fold_lanes(s, jnp.maximum))
        return carry

    lax.fori_loop(0, nck, scores, 0)
    m_ref[...] = jnp.broadcast_to(jnp.max(m_ref[...], axis=-1, keepdims=True), m_ref.shape)
    acc_ref[...] = jnp.zeros(acc_ref.shape, F32)

    def values(c, carry):
        r0 = pl.multiple_of(c * ck, ck)
        m = m_ref[...]
        s = s_ref[c]
        p = jnp.concatenate([jnp.exp((s[:, k * LANES:(k + 1) * LANES] - m).astype(BF16))
                             for k in range(ck // LANES)], axis=1)
        acc_ref[...] = acc_ref[...] + _dot(p, aug(kv_ref[pl.ds(r0, ck), 128:256]))
        return carry

    lax.fori_loop(0, nck, values, 0)
    o_s = acc_ref[:, 0:HEAD_D] / acc_ref[:, HEAD_D:2 * HEAD_D]

    g = g_ref[...]
    for h in range(nh):
        rs = slice(h * tq, (h + 1) * tq)
        o_ref[:, h * HEAD_D:(h + 1) * HEAD_D] = (
            g[:, 3 * h:3 * h + 1] * o_c[rs] + g[:, 3 * h + 1:3 * h + 2] * o_s[rs]
            + g[:, 3 * h + 2:3 * h + 3] * o_w[rs])


def _nsa_attn(q, kcmp, vcmp, kv, gates, ovt, emat, batch):
    tq, ck = 256, 512
    nq = SEQ // tq
    return pl.pallas_call(
        functools.partial(_nsa_attn_kernel, tq=tq, ck=ck),
        grid=(batch, nq),
        in_specs=[pl.BlockSpec((tq, 512), lambda b, i: (b * nq + i, 0)),
                  pl.BlockSpec((1, N_CMP_PAD, HEAD_D), lambda b, i: (b, 0, 0)),
                  pl.BlockSpec((1, N_CMP_PAD, HEAD_D), lambda b, i: (b, 0, 0)),
                  pl.BlockSpec((SEQ, 512), lambda b, i: (b, 0)),
                  pl.BlockSpec((tq, LANES), lambda b, i: (b * nq + i, 0)),
                  pl.BlockSpec((N_BLK, N_CMP_PAD), lambda b, i: (0, 0)),
                  pl.BlockSpec((SEQ // ck, N_BLK, ck), lambda b, i: (0, 0, 0))],
        out_specs=pl.BlockSpec((tq, 512), lambda b, i: (b * nq + i, 0)),
        out_shape=jax.ShapeDtypeStruct((batch * SEQ, 512), F32),
        scratch_shapes=[pltpu.VMEM((4 * tq, LANES), F32),
                        pltpu.VMEM((4 * tq, 2 * HEAD_D), F32),
                        pltpu.VMEM((SEQ // ck, 4 * tq, ck), F32)],
        compiler_params=_cparams(("parallel", "arbitrary")),
        name="nsa_attn",
    )(q, kcmp, vcmp, kv, gates, ovt, emat)


def _pool_kernel(u_ref, w_ref, b_ref, s_ref, o_ref, pad_ref):
    maxw = POOL_SIZES[-1]
    pad_ref[0:maxw, :] = jnp.zeros((maxw, W_GROUP), F32)
    pad_ref[maxw:maxw + SEQ, :] = u_ref[...].astype(F32)
    rc = 512
    for g, w in enumerate(POOL_SIZES):
        sl = slice(g * LANES, (g + 1) * LANES)
        for r in range(SEQ // rc):
            acc = pad_ref[maxw + r * rc:maxw + (r + 1) * rc, sl]
            tok = acc
            for j in range(1, w):
                acc = acc + pad_ref[maxw - j + r * rc:maxw - j + (r + 1) * rc, sl]
            t = r * rc + lax.broadcasted_iota(jnp.int32, (rc, 1), 0)
            cnt = jnp.minimum(t + 1, w).astype(F32)
            d = acc / cnt - tok
            y = _dot(d.astype(BF16), w_ref[g])
            o_ref[r * rc:(r + 1) * rc, sl] = (y + b_ref[:, sl]) * s_ref[:, sl]


def _pool(u, w, b, s, batch):
    return pl.pallas_call(
        _pool_kernel,
        grid=(batch,),
        in_specs=[pl.BlockSpec((SEQ, W_GROUP), lambda i: (i, 0)),
                  pl.BlockSpec((4, LANES, LANES), lambda i: (0, 0, 0)),
                  pl.BlockSpec((1, W_GROUP), lambda i: (0, 0)),
                  pl.BlockSpec((1, W_GROUP), lambda i: (0, 0))],
        out_specs=pl.BlockSpec((SEQ, W_GROUP), lambda i: (i, 0)),
        out_shape=jax.ShapeDtypeStruct((batch * SEQ, W_GROUP), F32),
        scratch_shapes=[pltpu.VMEM((SEQ + POOL_SIZES[-1], W_GROUP), F32)],
        compiler_params=_cparams(("parallel",)),
        name="pool",
    )(u, w, b, s)


def _s5_kernel(u_ref, wb_ref, ar_ref, ai_ref, wc_ref, d_ref, gw_ref, gb_ref, o_ref,
               bu_ref, st_ref, tm_ref, *, batch, tc):
    @pl.when(pl.program_id(0) == 0)
    def _():
        st_ref[...] = jnp.zeros(st_ref.shape, F32)

    nslab = W_GROUP // LANES
    for b in range(batch):
        for j in range(nslab):
            tm_ref[j, pl.ds(b, tc, stride=batch), :] = u_ref[b, :, j * LANES:(j + 1) * LANES].astype(F32)
    u = jnp.concatenate([tm_ref[j] for j in range(nslab)], axis=1)
    gl = 8 * S5_STATE
    ub = u.astype(BF16)
    for j in range(nslab):
        uj = ub[:, j * LANES:(j + 1) * LANES]
        for k, part in enumerate((0, S5_WIDTH)):
            cs = slice(part + j * gl, part + (j + 1) * gl)
            bu_ref[:, cs] = _dot(uj, wb_ref[j, :, k * gl:(k + 1) * gl])
    lc = 1024
    unroll = 8
    for c in range(S5_WIDTH // lc):
        re = slice(c * lc, (c + 1) * lc)
        im = slice(S5_WIDTH + c * lc, S5_WIDTH + (c + 1) * lc)
        ar, ai = ar_ref[:, re], ai_ref[:, re]

        def body(tb, carry, re=re, im=im, ar=ar, ai=ai):
            xr, xi = carry
            for k in range(unroll):
                r0 = pl.multiple_of((tb * unroll + k) * batch, batch)
                nxr = ar * xr - ai * xi + bu_ref[pl.ds(r0, batch), re]
                nxi = ar * xi + ai * xr + bu_ref[pl.ds(r0, batch), im]
                bu_ref[pl.ds(r0, batch), re] = nxr
                bu_ref[pl.ds(r0, batch), im] = nxi
                xr, xi = nxr, nxi
            return xr, xi

        xr, xi = lax.fori_loop(0, tc // unroll, body, (st_ref[:, re], st_ref[:, im]))
        st_ref[:, re] = xr
        st_ref[:, im] = xi

    ys = []
    for j in range(nslab):
        yj = 0.0
        for k, part in enumerate((0, S5_WIDTH)):
            cs = slice(part + j * gl, part + (j + 1) * gl)
            yj = yj + _dot(bu_ref[:, cs].astype(BF16), wc_ref[j, k * gl:(k + 1) * gl, :])
        ys.append(yj)
    y = jnp.concatenate(ys, axis=1) + d_ref[...] * u
    y = jax.nn.gelu(y)
    z = _dot(y.astype(BF16), gw_ref[...]) + gb_ref[...]
    o = y * jax.nn.sigmoid(z)
    for j in range(nslab):
        tm_ref[j] = o[:, j * LANES:(j + 1) * LANES]
    for b in range(batch):
        for j in range(nslab):
            o_ref[b, :, j * LANES:(j + 1) * LANES] = tm_ref[j, pl.ds(b, tc, stride=batch), :]


def _s5(u3, wb, ar, ai, wc, d, gw, gb):
    batch = u3.shape[0]
    tc = 64
    rows = tc * batch
    full = lambda shape: pl.BlockSpec(shape, lambda i: (0, 0))
    return pl.pallas_call(
        functools.partial(_s5_kernel, batch=batch, tc=tc),
        grid=(SEQ // tc,),
        in_specs=[pl.BlockSpec((batch, tc, W_GROUP), lambda i: (0, i, 0)),
                  pl.BlockSpec(wb.shape, lambda i: (0, 0, 0)), full((batch, S5_WIDTH)),
                  full((batch, S5_WIDTH)), pl.BlockSpec(wc.shape, lambda i: (0, 0, 0)),
                  full((1, W_GROUP)),
                  full((W_GROUP, W_GROUP)), full((1, W_GROUP))],
        out_specs=pl.BlockSpec((batch, tc, W_GROUP), lambda i: (0, i, 0)),
        out_shape=jax.ShapeDtypeStruct((batch, SEQ, W_GROUP), F32),
        scratch_shapes=[pltpu.VMEM((rows, 2 * S5_WIDTH), F32),
                        pltpu.VMEM((batch, 2 * S5_WIDTH), F32),
                        pltpu.VMEM((W_GROUP // LANES, rows, LANES), F32)],
        compiler_params=_cparams(("arbitrary",)),
        name="s5",
    )(u3, wb, ar, ai, wc, d, gw, gb)


def _out_proj_kernel(ya_ref, yb_ref, yc_ref, yd_ref, mg_ref, wo_ref, x_ref, fg_ref,
                     wrh_ref, wrl_ref, br_ref, x1_ref, h2_ref, lg_ref, *, tm):
    acc = x_ref[...]
    for gi, y_ref in enumerate((ya_ref, yb_ref, yc_ref, yd_ref)):
        sl = slice(gi * W_GROUP, (gi + 1) * W_GROUP)
        n = (_rms(y_ref[...]) * mg_ref[:, sl]).astype(BF16)
        acc = acc + _dot(n, wo_ref[sl, :])
    x1_ref[...] = acc
    h2 = _rms(acc) * fg_ref[...]
    hi = h2.astype(BF16)
    lo = (h2 - hi.astype(F32)).astype(BF16)
    lg_ref[...] = (_dot(hi, wrh_ref[...]) + _dot(hi, wrl_ref[...]) + _dot(lo, wrh_ref[...])
                   + br_ref[...])
    _pack_rows(h2_ref, h2, tm)


def _out_proj(ya, yb, yc, yd, mg, wo_all, layer, x, fg, wr_hi, wr_lo, br):
    t = x.shape[0]
    tm = 256
    yspec = pl.BlockSpec((tm, W_GROUP), lambda i: (i, 0))
    full = lambda shape: pl.BlockSpec(shape, lambda i: (0, 0))
    return pl.pallas_call(
        functools.partial(_out_proj_kernel, tm=tm),
        grid=(t // tm,),
        in_specs=[yspec, yspec, yspec, yspec, full((1, D_MODEL)),
                  pl.BlockSpec((None, D_MODEL, D_MODEL), lambda i: (layer, 0, 0)),
                  pl.BlockSpec((tm, D_MODEL), lambda i: (i, 0)), full((1, D_MODEL)),
                  full((D_MODEL, LANES)), full((D_MODEL, LANES)), full((1, LANES))],
        out_specs=[pl.BlockSpec((tm, D_MODEL), lambda i: (i, 0)),
                   pl.BlockSpec((tm * ROW_PITCH, LANES), lambda i: (i, 0)),
                   pl.BlockSpec((tm, LANES), lambda i: (i, 0))],
        out_shape=[jax.ShapeDtypeStruct((t, D_MODEL), F32),
                   jax.ShapeDtypeStruct((t * ROW_PITCH, LANES), F32),
                   jax.ShapeDtypeStruct((t, LANES), F32)],
        compiler_params=_cparams(("parallel",)),
        name="out_proj",
    )(ya, yb, yc, yd, mg, wo_all, x, fg, wr_hi, wr_lo, br)


ROUTE_ROWS = 40


def _route(lg, meta_ref, cnt_ref, col_ref, row_ref, tm):
    @pl.when(pl.program_id(0) == 0)
    def _():
        col_ref[...] = jnp.zeros(col_ref.shape, F32)
        row_ref[...] = jnp.zeros(row_ref.shape, F32)

    nr = ROUTE_ROWS
    lt = jnp.transpose(lg)[0:nr, :]
    row = lax.broadcasted_iota(jnp.int32, (nr, tm), 0)
    big = jnp.int32(1 << 20)
    cmax = lambda a: jnp.max(a, axis=0, keepdims=True)
    cmin = lambda a: jnp.min(a, axis=0, keepdims=True)
    csum = lambda a: jnp.sum(a, axis=0, keepdims=True)

    is_g = row < MOE_GROUPS
    gl = jnp.where(is_g, lt, NEG_INF)
    gm = cmax(gl)
    p_top = 1.0 / csum(jnp.where(is_g, jnp.exp(gl - gm), 0.0))
    g_top = cmin(jnp.where(is_g & (gl == gm), row, big))

    is_e = (row >= MOE_GROUPS) & (row < MOE_GROUPS + MOE_EXPERTS) \
        & (((row - MOE_GROUPS) // MOE_EPG) == g_top)
    el = jnp.where(is_e, lt, NEG_INF)
    ee = jnp.where(is_e, jnp.exp(el - cmax(el)), 0.0)
    p = jnp.where(is_e, ee / csum(ee), -1.0)
    p1 = cmax(p)
    i1 = cmin(jnp.where(p == p1, row, big))
    p_rest = jnp.where(row == i1, -1.0, p)
    p2 = cmax(p_rest)
    i2 = cmin(jnp.where((p_rest == p2) & is_e & (row != i1), row, big))
    den = p1 + p2
    w1 = p_top * (p1 / den)
    w2 = p_top * (p2 / den)

    hit1, hit2 = row == i1, row == i2
    oh = (hit1 | hit2).astype(BF16)
    r = lax.broadcasted_iota(jnp.int32, (tm, tm), 0)
    c = lax.broadcasted_iota(jnp.int32, (tm, tm), 1)
    before = _dot(oh, (r < c).astype(BF16)) + col_ref[:, 0:1]
    r1 = csum(jnp.where(hit1, before, 0.0))
    r2 = csum(jnp.where(hit2, before, 0.0))
    col_ref[...] = col_ref[...] + jnp.sum(oh.astype(F32), axis=1, keepdims=True)
    oh_all = jnp.concatenate([oh, jnp.zeros((LANES - nr, tm), BF16)], axis=0)
    row_ref[...] = row_ref[...] + _dot_nt(jnp.ones((8, tm), BF16), oh_all)
    cnt_ref[...] = row_ref[...]

    e1 = (i1 - MOE_GROUPS).astype(F32)
    e2 = (i2 - MOE_GROUPS).astype(F32)
    row8 = lax.broadcasted_iota(jnp.int32, (8, tm), 0)
    meta = jnp.zeros((8, tm), F32)
    for k, v in enumerate((e1, e2, w1, w2, r1, r2)):
        meta = jnp.where(row8 == k, v, meta)
    meta = jnp.concatenate([meta, jnp.zeros((LANES - 8, tm), F32)], axis=0)
    meta_ref[...] = jnp.transpose(meta)


def _router_kernel(lg_ref, meta_ref, cnt_ref, col_ref, row_ref, *, tm):
    _route(lg_ref[...], meta_ref, cnt_ref, col_ref, row_ref, tm)


def _router(logits):
    t = logits.shape[0]
    tm = 512
    return pl.pallas_call(
        functools.partial(_router_kernel, tm=tm),
        grid=(t // tm,),
        in_specs=[pl.BlockSpec((tm, LANES), lambda i: (i, 0))],
        out_specs=[pl.BlockSpec((tm, LANES), lambda i: (i, 0)),
                   pl.BlockSpec((8, LANES), lambda i: (i, 0))],
        out_shape=[jax.ShapeDtypeStruct((t, LANES), F32),
                   jax.ShapeDtypeStruct((t // tm * 8, LANES), F32)],
        scratch_shapes=[pltpu.VMEM((ROUTE_ROWS, LANES), F32), pltpu.VMEM((8, LANES), F32)],
        compiler_params=_cparams(("arbitrary",)),
        name="router",
    )(logits)


def _dest_kernel(meta_ref, cnt_ref, o_ref, *, tm):
    cnt = cnt_ref[...]
    start = (_lane_cumsum(cnt) - cnt)[0:1, :]
    nr = ROUTE_ROWS
    start_col = jnp.transpose(jnp.broadcast_to(start, (LANES, LANES)))[0:nr, 0:1]
    mt = jnp.transpose(meta_ref[...])
    row = lax.broadcasted_iota(jnp.int32, (nr, tm), 0)
    row8 = lax.broadcasted_iota(jnp.int32, (8, tm), 0)
    out = jnp.zeros((8, tm), F32)
    for k in range(2):
        e_row = mt[k:k + 1, :].astype(jnp.int32) + MOE_GROUPS
        d = jnp.sum(jnp.where(row == e_row, start_col, 0.0), axis=0, keepdims=True) + mt[4 + k:5 + k, :]
        out = jnp.where(row8 == k, d, out)
    o_ref[...] = out.astype(jnp.int32)


def _dest(meta, counts):
    t = meta.shape[0]
    tm = 512
    last = counts.shape[0] // 8 - 1
    return pl.pallas_call(
        functools.partial(_dest_kernel, tm=tm),
        grid=(t // tm,),
        in_specs=[pl.BlockSpec((tm, LANES), lambda i: (i, 0)),
                  pl.BlockSpec((8, LANES), lambda i: (last, 0))],
        out_specs=pl.BlockSpec((8, tm), lambda i: (0, i)),
        out_shape=jax.ShapeDtypeStruct((8, t), jnp.int32),
        compiler_params=_cparams(("parallel",)),
        name="dest",
    )(meta, counts)


def _dispatch_kernel(d1_ref, d2_ref, h_ref, xs_ref, sem, *, td):
    base = pl.program_id(0) * td

    def start(rb, carry):
        for k in range(DMA_UNROLL):
            r = rb * DMA_UNROLL + k
            src = h_ref.at[pl.ds(r * ROW_PITCH, ROW_PITCH)]
            for d_ref in (d1_ref, d2_ref):
                dst = xs_ref.at[pl.ds(d_ref[base + r] * ROW_PITCH, ROW_PITCH)]
                pltpu.make_async_copy(src, dst, sem).start()
        return carry

    lax.fori_loop(0, td // DMA_UNROLL, start, 0)
    for _ in range(2):
        pltpu.make_async_copy(h_ref, xs_ref.at[pl.ds(0, td * ROW_PITCH)], sem).wait()


def _dispatch(dest1, dest2, h2):
    t = dest1.shape[0]
    td = 512
    return pl.pallas_call(
        functools.partial(_dispatch_kernel, td=td),
        grid_spec=pltpu.PrefetchScalarGridSpec(
            num_scalar_prefetch=2,
            grid=(t // td,),
            in_specs=[pl.BlockSpec((td * ROW_PITCH, LANES), lambda i, *_: (i, 0))],
            out_specs=pl.BlockSpec(memory_space=pl.ANY),
            scratch_shapes=[pltpu.SemaphoreType.DMA(())]),
        out_shape=jax.ShapeDtypeStruct((2 * t * ROW_PITCH, LANES), F32),
        compiler_params=pltpu.CompilerParams(dimension_semantics=("arbitrary",),
                                             has_side_effects=True),
        name="dispatch",
    )(dest1, dest2, h2)


def _expert_kernel(vt_ref, ve_ref, vlo_ref, vhi_ref, vfirst_ref, vvalid_ref, vnew_ref, vnext_ref,
                   vslot_ref, xs_ref, wg_ref, wu_ref, wd_ref, ys_ref,
                   wgf_ref, wuf_ref, wdf_ref, wgb_ref, wub_ref, wdb_ref, sem, *, tmx, layer):
    v = pl.program_id(0)

    def weight_copies(e, slot):
        return [pltpu.make_async_copy(w_ref.at[layer, e], f_ref.at[slot], sem.at[slot])
                for w_ref, f_ref in ((wg_ref, wgf_ref), (wu_ref, wuf_ref), (wd_ref, wdf_ref))]

    @pl.when(vnew_ref[v] == 1)
    def _():
        slot = vslot_ref[v]

        @pl.when(v == 0)
        def _():
            for cp in weight_copies(ve_ref[v], slot):
                cp.start()

        for cp in weight_copies(ve_ref[v], slot):
            cp.wait()

        @pl.when(vnext_ref[v] >= 0)
        def _():
            for cp in weight_copies(vnext_ref[v], 1 - slot):
                cp.start()

        wgb_ref[...] = wgf_ref[slot].astype(BF16)
        wub_ref[...] = wuf_ref[slot].astype(BF16)
        wdb_ref[...] = wdf_ref[slot].astype(BF16)

    @pl.when(vvalid_ref[v] == 1)
    def _():
        x = jnp.concatenate(_unpack_rows(lambda rows: xs_ref[rows, :], tmx), axis=1).astype(BF16)
        a = _dot(x, wgb_ref[...])
        u = _dot(x, wub_ref[...])
        hid = (jax.nn.silu(a) * u).astype(BF16)
        y = _dot(hid, wdb_ref[...])
        rows = lax.broadcasted_iota(jnp.int32, (tmx, 1), 0)
        mine = (rows >= vlo_ref[v]) & (rows < vhi_ref[v])

        @pl.when(vfirst_ref[v] == 1)
        def _():
            _pack_rows(ys_ref, y, tmx, mask=mine)

        @pl.when(vfirst_ref[v] == 0)
        def _():
            _pack_rows(ys_ref, y, tmx, mask=mine, old=True)


def _experts(sched, xs, wg, wu, wd, layer, tmx):
    nvis = sched[0].shape[0]
    rows = xs.shape[0]
    xspec = pl.BlockSpec((tmx * ROW_PITCH, LANES), lambda v, vt, *_: (vt[v], 0))
    hbm = pl.BlockSpec(memory_space=pl.ANY)
    return pl.pallas_call(
        functools.partial(_expert_kernel, tmx=tmx, layer=layer),
        grid_spec=pltpu.PrefetchScalarGridSpec(
            num_scalar_prefetch=9,
            grid=(nvis,),
            in_specs=[xspec, hbm, hbm, hbm],
            out_specs=xspec,
            scratch_shapes=[pltpu.VMEM((2, D_MODEL, MOE_HIDDEN), F32),
                            pltpu.VMEM((2, D_MODEL, MOE_HIDDEN), F32),
                            pltpu.VMEM((2, MOE_HIDDEN, D_MODEL), F32),
                            pltpu.VMEM((D_MODEL, MOE_HIDDEN), BF16),
                            pltpu.VMEM((D_MODEL, MOE_HIDDEN), BF16),
                            pltpu.VMEM((MOE_HIDDEN, D_MODEL), BF16),
                            pltpu.SemaphoreType.DMA((2,))]),
        out_shape=jax.ShapeDtypeStruct((rows, LANES), F32),
        compiler_params=_cparams(("arbitrary",)),
        name="experts",
    )(*sched, xs, wg, wu, wd)


def _combine_kernel(d1_ref, d2_ref, ys_ref, meta_ref, x1_ref, fg_ref, o_ref,
                    b1_ref, b2_ref, sem, *, tc, final):
    i = pl.program_id(0)
    n = pl.num_programs(0)
    slot = i % 2

    def gather(tile, slot):
        def start(rb, carry):
            for k in range(DMA_UNROLL):
                r = rb * DMA_UNROLL + k
                dst = pl.ds(r * ROW_PITCH, ROW_CHUNKS)
                for d_ref, b_ref in ((d1_ref, b1_ref), (d2_ref, b2_ref)):
                    src = ys_ref.at[pl.ds(d_ref[tile * tc + r] * ROW_PITCH, ROW_CHUNKS)]
                    pltpu.make_async_copy(src, b_ref.at[slot, dst], sem.at[slot]).start()
            return carry

        lax.fori_loop(0, tc // DMA_UNROLL, start, 0)

    @pl.when(i == 0)
    def _():
        gather(0, 0)

    @pl.when(i + 1 < n)
    def _():
        gather(i + 1, 1 - slot)

    for b_ref in (b1_ref, b2_ref):
        pltpu.make_async_copy(ys_ref.at[pl.ds(0, tc * ROW_CHUNKS)],
                              b_ref.at[slot, pl.ds(0, tc * ROW_CHUNKS)], sem.at[slot]).wait()
    w1 = meta_ref[:, 2:3]
    w2 = meta_ref[:, 3:4]
    y1 = _unpack_rows(lambda rows: b1_ref[slot, rows, :], tc)
    y2 = _unpack_rows(lambda rows: b2_ref[slot, rows, :], tc)
    x2 = jnp.concatenate([x1_ref[:, c * LANES:(c + 1) * LANES] + (w1 * y1[c] + w2 * y2[c])
                          for c in range(ROW_CHUNKS)], axis=1)
    if final:
        x2 = _rms(x2) * fg_ref[...]
    o_ref[...] = x2


def _combine(dest1, dest2, ys, meta, x1, fg, final):
    t = x1.shape[0]
    tc = 256
    return pl.pallas_call(
        functools.partial(_combine_kernel, tc=tc, final=final),
        grid_spec=pltpu.PrefetchScalarGridSpec(
            num_scalar_prefetch=2,
            grid=(t // tc,),
            in_specs=[pl.BlockSpec(memory_space=pl.ANY),
                      pl.BlockSpec((tc, LANES), lambda i, *_: (i, 0)),
                      pl.BlockSpec((tc, D_MODEL), lambda i, *_: (i, 0)),
                      pl.BlockSpec((1, D_MODEL), lambda i, *_: (0, 0))],
            out_specs=pl.BlockSpec((tc, D_MODEL), lambda i, *_: (i, 0)),
            scratch_shapes=[pltpu.VMEM((2, tc * ROW_PITCH, LANES), F32),
                            pltpu.VMEM((2, tc * ROW_PITCH, LANES), F32),
                            pltpu.SemaphoreType.DMA((2,))]),
        out_shape=jax.ShapeDtypeStruct((t, D_MODEL), F32),
        compiler_params=_cparams(("arbitrary",)),
        name="combine",
    )(dest1, dest2, ys, meta, x1, fg)


def _rope_tables(r0):
    inv = ROPE_THETA ** (-jnp.arange(0, 2 * ROPE_HALF, 2, dtype=F32) / (2 * ROPE_HALF))
    ang = jnp.arange(SEQ, dtype=F32)[:, None] * inv[None, :]
    cos, sin = jnp.cos(ang), jnp.sin(ang)
    c = jnp.ones((SEQ, LANES), F32).at[:, r0:r0 + ROPE_HALF].set(cos)
    c = c.at[:, r0 + ROPE_HALF:r0 + 2 * ROPE_HALF].set(cos)
    sa = jnp.zeros((SEQ, LANES), F32).at[:, r0:r0 + ROPE_HALF].set(-sin)
    sb = jnp.zeros((SEQ, LANES), F32).at[:, r0 + ROPE_HALF:r0 + 2 * ROPE_HALF].set(sin)
    return c, sa, sb


def _pad_cols(w, width):
    return jnp.pad(w, ((0, 0), (0, width - w.shape[1])))


def _s5_params(a_re, a_im, log_dt, b_re, b_im, c_re, c_im, batch):
    dt = jnp.exp(log_dt)[:, None]
    mag = jnp.exp(a_re * dt)
    abar_r, abar_i = mag * jnp.cos(a_im * dt), mag * jnp.sin(a_im * dt)
    den = a_re * a_re + a_im * a_im
    nr, ni = abar_r - 1.0, abar_i
    coef_r = (nr * a_re + ni * a_im) / den
    coef_i = (ni * a_re - nr * a_im) / den
    bbar_r = coef_r[..., None] * b_re - coef_i[..., None] * b_im
    bbar_i = coef_r[..., None] * b_im + coef_i[..., None] * b_re
    nslab, gps = W_GROUP // LANES, LANES // S5_CH
    eye = jnp.eye(gps, dtype=F32)
    slab = lambda m: m.reshape((nslab, gps) + m.shape[1:])
    blk_b = lambda m: jnp.einsum('jgpc,gh->jgchp', slab(m), eye).reshape(nslab, LANES, gps * S5_STATE)
    wb = jnp.concatenate([blk_b(bbar_r), blk_b(bbar_i)], axis=2).astype(BF16)
    blk_c = lambda m: jnp.einsum('jgcp,gh->jgphc', slab(m), eye).reshape(nslab, gps * S5_STATE, LANES)
    wc = jnp.concatenate([blk_c(c_re), blk_c(-c_im)], axis=1).astype(BF16)
    ar = jnp.broadcast_to(abar_r.reshape(1, S5_WIDTH), (batch, S5_WIDTH))
    ai = jnp.broadcast_to(abar_i.reshape(1, S5_WIDTH), (batch, S5_WIDTH))
    return wb, ar, ai, wc


def _nsa_consts(ck):
    c_start = np.arange(N_CMP_PAD) * CMP_STRIDE
    b_start = np.arange(N_BLK) * SEL_BLOCK
    ov = ((c_start[None, :] < b_start[:, None] + SEL_BLOCK)
          & (c_start[None, :] + CMP_BLOCK > b_start[:, None])).astype(np.float32)
    key_blk = np.arange(SEQ) // SEL_BLOCK
    e = (key_blk[None, :] == np.arange(N_BLK)[:, None]).astype(np.float32)
    e = e.reshape(N_BLK, SEQ // ck, ck).transpose(1, 0, 2)
    return jnp.asarray(ov), jnp.asarray(e, dtype=BF16)


def _lane_cumsum(v):
    r = lax.broadcasted_iota(jnp.int32, (LANES, LANES), 0)
    c = lax.broadcasted_iota(jnp.int32, (LANES, LANES), 1)
    incl = (r <= c).astype(BF16)
    hi = jnp.floor(v * (1.0 / 256.0))
    lo = v - 256.0 * hi
    return 256.0 * _dot(hi.astype(BF16), incl) + _dot(lo.astype(BF16), incl)


def _sched_kernel(cnt_ref, o_ref, *, tmx, nv):
    cnt = cnt_ref[...]
    ends = _lane_cumsum(cnt)
    offs = ends - cnt
    first = jnp.floor(offs * (1.0 / tmx))
    last = jnp.floor(jnp.maximum(ends - 1.0, 0.0) * (1.0 / tmx))
    nvis = jnp.where(cnt > 0.0, last - first + 1.0, 0.0)
    cumv = _lane_cumsum(nvis)
    row1 = lambda a: a[0:1, :]
    total = jnp.max(row1(cumv), axis=-1, keepdims=True)
    rsum = lambda a: jnp.sum(a, axis=-1, keepdims=True)
    v = lax.broadcasted_iota(jnp.int32, (nv, LANES), 0).astype(F32)
    lane = lax.broadcasted_iota(jnp.int32, (nv, LANES), 1).astype(F32)
    vc = jnp.minimum(v, total - 1.0)
    e_lane = rsum((row1(cumv) <= vc).astype(F32))
    hit = lane == e_lane
    pick = lambda a: rsum(jnp.where(hit, row1(a), 0.0))
    vt = pick(first) + vc[:, 0:1] - (pick(cumv) - pick(nvis))
    vlo = jnp.clip(pick(offs) - vt * tmx, 0.0, float(tmx))
    vhi = jnp.clip(pick(ends) - vt * tmx, 0.0, float(tmx))
    changed = lambda a: (v == 0.0) | (a != pltpu.roll(a, 1, 0))
    vfirst = changed(jnp.broadcast_to(vt, (nv, LANES))).astype(F32)
    vnew = changed(jnp.broadcast_to(e_lane, (nv, LANES))).astype(F32)
    vvalid = (v < total).astype(F32)
    nonempty = (cnt > 0.0).astype(F32)
    order = pick(_lane_cumsum(nonempty) - nonempty)
    vslot = order - 2.0 * jnp.floor(order * 0.5)
    far = float(1 << 20)
    nxt = jnp.min(jnp.where((row1(nonempty) > 0.0) & (lane > e_lane), lane, far), axis=-1, keepdims=True)
    vnext = jnp.where(nxt >= far, -1.0, nxt - MOE_GROUPS)
    out = jnp.zeros((nv, LANES), F32)
    cols = (vt, e_lane - MOE_GROUPS, vlo, vhi, vfirst, vvalid, vnew, vnext, vslot)
    for k, col in enumerate(cols):
        out = jnp.where(lane == k, col, out)
    o_ref[...] = out.astype(jnp.int32)


def _moe_schedule(counts, n_rows, tmx):
    nvis_max = n_rows // tmx + MOE_EXPERTS
    nv = 256
    last = counts.shape[0] // 8 - 1
    sched = pl.pallas_call(
        functools.partial(_sched_kernel, tmx=tmx, nv=nv),
        grid=(1,),
        in_specs=[pl.BlockSpec((8, LANES), lambda i: (last, 0))],
        out_specs=pl.BlockSpec((nv, LANES), lambda i: (0, 0)),
        out_shape=jax.ShapeDtypeStruct((nv, LANES), jnp.int32),
        name="sched",
    )(counts)
    return tuple(sched[:nvis_max, k] for k in range(9))


def kernel(x, attn_norm_g, w_in, mla_q_norm_g, mla_kv_norm_g, mla_w_uq, mla_w_ukv, nsa_cmp_pe, nsa_cmp_w1, nsa_cmp_w2, pool_w, pool_b, pool_scale, s5_a_re, s5_a_im, s5_log_dt, s5_b_re, s5_b_im, s5_c_re, s5_c_im, s5_d, s5_glu_w, s5_glu_b, mix_norm_g, w_out, ffn_norm_g, moe_w_group, moe_b_group, moe_w_expert, moe_b_expert, moe_w_gate, moe_w_up, moe_w_down, final_norm_g):
    batch, seq, d = x.shape
    depth = w_in.shape[0]
    t = batch * seq
    xf = x.reshape(t, d)
    qtabs = _rope_tables(MLA_NOPE)
    ktabs = _rope_tables(0)
    ovt, emat = _nsa_consts(512)
    row = lambda v: v.reshape(1, -1)
    tmx = 256
    w_in_all = _w_in_prep(w_in)
    w_out_all = w_out.astype(BF16)

    for l in range(depth):
        wuq = jnp.pad(mla_w_uq[l], ((0, 512 - MLA_Q_LORA), (0, 0))).astype(BF16)
        ukv = mla_w_ukv[l].reshape(HEAD_D, MLA_HEADS, MLA_NOPE + MLA_V)
        wk = jnp.pad(ukv[:, :, :MLA_NOPE], ((0, 0), (0, 0), (0, HEAD_D - MLA_NOPE)))
        wk = wk.reshape(HEAD_D, 512).astype(BF16)
        wv = ukv[:, :, MLA_NOPE:].reshape(HEAD_D, 512).astype(BF16)
        qg = jnp.pad(mla_q_norm_g[l], (0, 512 - MLA_Q_LORA)).reshape(1, 512)
        q_a, k_a, v_a, q_b, kc, vc, kv_b, gates, u_pool, u_s5 = _in_proj(
            xf, row(attn_norm_g[l]), w_in_all, l, qg, row(mla_kv_norm_g[l]), wuq, wk, wv, qtabs, ktabs)

        y_a = _mla_attn(q_a, k_a, v_a, batch)

        pe = jnp.broadcast_to(nsa_cmp_pe[l].reshape(2, 1, CMP_BLOCK * HEAD_D),
                              (2, 8, CMP_BLOCK * HEAD_D)).astype(BF16)
        kcmp, vcmp = _nsa_compress(kc, vc, nsa_cmp_w1[l].astype(BF16), nsa_cmp_w2[l].astype(BF16),
                                   pe, batch)
        y_b = _nsa_attn(q_b, kcmp, vcmp, kv_b, gates, ovt, emat, batch)

        y_c = _pool(u_pool, pool_w[l].astype(BF16), row(pool_b[l]), row(pool_scale[l]), batch)

        wb, ar, ai, wc = _s5_params(s5_a_re[l], s5_a_im[l], s5_log_dt[l], s5_b_re[l], s5_b_im[l],
                                    s5_c_re[l], s5_c_im[l], batch)
        y_d = _s5(u_s5.reshape(batch, seq, W_GROUP), wb, ar, ai, wc, row(s5_d[l]),
                  s5_glu_w[l].astype(BF16), row(s5_glu_b[l])).reshape(t, W_GROUP)

        wr = jnp.concatenate([moe_w_group[l], moe_w_expert[l]], axis=1)
        wr = _pad_cols(wr, LANES)
        wr_hi = wr.astype(BF16)
        wr_lo = (wr - wr_hi.astype(F32)).astype(BF16)
        br = jnp.pad(jnp.concatenate([moe_b_group[l], moe_b_expert[l]]), (0, LANES - 36)).reshape(1, LANES)
        x1, h2, logits = _out_proj(y_a, y_b, y_c, y_d, row(mix_norm_g[l]), w_out_all, l,
                                   xf, row(ffn_norm_g[l]), wr_hi, wr_lo, br)

        meta, counts = _router(logits)
        dest = _dest(meta, counts)
        dest1, dest2 = dest[0], dest[1]
        sched = _moe_schedule(counts, 2 * t, tmx)
        xs = _dispatch(dest1, dest2, h2)
        ys = _experts(sched, xs, moe_w_gate, moe_w_up, moe_w_down, l, tmx)
        xf = _combine(dest1, dest2, ys, meta, x1, row(final_norm_g), final=(l == depth - 1))

    return xf.reshape(batch, seq, d)
```

```python
import functools
import math

import numpy as np
import jax
import jax.numpy as jnp
from jax import lax
from jax.experimental import pallas as pl
from jax.experimental.pallas import tpu as pltpu

F32 = jnp.float32
BF16 = jnp.bfloat16

D_MODEL = 2048
SEQ = 2048
W_GROUP = 512
LANES = 128
ROW_CHUNKS = D_MODEL // LANES
ROW_PITCH = ROW_CHUNKS + 1
DMA_UNROLL = 4

ROPE_THETA = 500000.0
ROPE_HALF = 16
NEG_INF = -1.0e30
FORCE_SCORE = 1.0e4
EPS = 1e-6

MLA_HEADS = 4
MLA_Q_LORA = 448
MLA_NOPE = 96
MLA_V = 128
HEAD_D = 128

CMP_BLOCK = 32
CMP_STRIDE = 16
SEL_BLOCK = 64
N_SEL = 8
N_LOCAL = 2
WINDOW = 512
N_CMP_PAD = SEQ // CMP_STRIDE
N_BLK = SEQ // SEL_BLOCK

POOL_SIZES = (2, 4, 8, 16)
S5_GROUPS = 32
S5_CH = 16
S5_STATE = 64
S5_WIDTH = S5_GROUPS * S5_STATE

MOE_GROUPS = 4
MOE_EPG = 8
MOE_EXPERTS = 32
MOE_HIDDEN = 512

COL_CQ, COL_NQ, COL_POOL, COL_S5 = 0, 512, 1024, 1536
COL_CKV, COL_KR, COL_KV6, COL_GL = 2048, 2176, 2304, 3072
N_IN_PAD = 3200

VMEM_LIMIT = 56 * 1024 * 1024


def _cparams(sem, vmem=VMEM_LIMIT):
    return pltpu.CompilerParams(dimension_semantics=sem, vmem_limit_bytes=vmem)


def _rms(x, n=None):
    n = x.shape[-1] if n is None else n
    return x * lax.rsqrt(jnp.sum(x * x, axis=-1, keepdims=True) / n + EPS)


def _dot(a, b):
    return jnp.dot(a, b, preferred_element_type=F32)


def _dot_nt(a, b, precision=None):
    return lax.dot_general(a, b, (((1,), (1,)), ((), ())), preferred_element_type=F32,
                           precision=precision)


def _rope(x, c, sa, sb):
    return x * c + pltpu.roll(x, LANES - ROPE_HALF, 1) * sa + pltpu.roll(x, ROPE_HALF, 1) * sb


def _pack_rows(ref, x, n, mask=None, old=False):
    for s in range(ROW_CHUNKS):
        rows = pl.ds(s, n, stride=ROW_PITCH)
        w = x[:, s * LANES:(s + 1) * LANES]
        if mask is not None:
            w = jnp.where(mask, w, ref[rows, :] if old else 0.0)
        ref[rows, :] = w
    if not old:
        ref[pl.ds(ROW_CHUNKS, n, stride=ROW_PITCH), :] = jnp.zeros((n, LANES), F32)


def _unpack_rows(load, n):
    return [load(pl.ds(s, n, stride=ROW_PITCH)) for s in range(ROW_CHUNKS)]


_W_IN_SEGMENTS = ((COL_CQ, 0, 448), (COL_NQ, 608, 512), (COL_POOL, 1900, 512), (COL_S5, 2412, 512),
                  (COL_CKV, 448, 128), (COL_KR, 576, 32), (COL_KV6, 1120, 768), (COL_GL, 1888, 12))


def _w_in_prep(w_in):
    wt = jnp.swapaxes(w_in, 1, 2)
    parts, at = [], 0
    for dst, src, width in sorted(_W_IN_SEGMENTS):
        if dst > at:
            parts.append(jnp.zeros((wt.shape[0], dst - at, wt.shape[2]), BF16))
        parts.append(wt[:, src:src + width, :].astype(BF16))
        at = dst + width
    parts.append(jnp.zeros((wt.shape[0], N_IN_PAD - at, wt.shape[2]), BF16))
    return jnp.concatenate(parts, axis=1)


def _in_proj_kernel(x_ref, g_ref, w_ref, qg_ref, kvg_ref, wuq_ref, wk_ref, wv_ref,
                    qc_ref, qsa_ref, qsb_ref, kc_ref, ksa_ref, ksb_ref,
                    qa_out, ka_out, va_out, qb_out, kcmp_out, vcmp_out, kvb_out, gate_out,
                    pool_out, s5_out):
    scale = 1.0 / math.sqrt(HEAD_D)
    h = (_rms(x_ref[...]) * g_ref[...]).astype(BF16)
    seg = lambda col, width: _dot_nt(h, w_ref[col:col + width, :])
    heads = [slice(n * HEAD_D, (n + 1) * HEAD_D) for n in range(4)]

    qn = (_rms(seg(COL_CQ, 512), MLA_Q_LORA) * qg_ref[...]).astype(BF16)
    q = _dot(qn, wuq_ref[...])
    kvn = (_rms(seg(COL_CKV, LANES)) * kvg_ref[...]).astype(BF16)
    kn = _dot(kvn, wk_ref[...])
    va_out[...] = _dot(kvn, wv_ref[...]).astype(BF16)
    kc, ksa, ksb = kc_ref[...], ksa_ref[...], ksb_ref[...]
    kr = pltpu.roll(_rope(seg(COL_KR, LANES), kc, ksa, ksb), MLA_NOPE, 1)
    for sl in heads:
        qa_out[:, sl] = (_rope(q[:, sl], qc_ref[...], qsa_ref[...], qsb_ref[...]) * scale).astype(BF16)
        ka_out[:, sl] = (kn[:, sl] + kr).astype(BF16)

    qb = seg(COL_NQ, 512)
    for sl in heads:
        qb_out[:, sl] = (_rope(qb[:, sl], kc, ksa, ksb) * scale).astype(BF16)
    kv = seg(COL_KV6, 6 * HEAD_D)
    part = lambda n: kv[:, n * HEAD_D:(n + 1) * HEAD_D]
    kcmp_out[...] = _rope(part(0), kc, ksa, ksb)
    vcmp_out[...] = part(1)
    kvb_out[:, 0:128] = _rope(part(2), kc, ksa, ksb).astype(BF16)
    kvb_out[:, 128:256] = part(3).astype(BF16)
    kvb_out[:, 256:384] = _rope(part(4), kc, ksa, ksb).astype(BF16)
    kvb_out[:, 384:512] = part(5).astype(BF16)
    gate_out[...] = jax.nn.sigmoid(seg(COL_GL, LANES))

    pool_out[...] = seg(COL_POOL, W_GROUP).astype(BF16)
    s5_out[...] = seg(COL_S5, W_GROUP).astype(BF16)


def _in_proj(x, g, w_all, layer, qg, kvg, wuq, wk, wv, qtabs, ktabs):
    t = x.shape[0]
    tm = 512
    nsb = SEQ // tm
    tab = pl.BlockSpec((tm, LANES), lambda i: (i % nsb, 0))
    full = lambda shape: pl.BlockSpec(shape, lambda i: (0, 0))
    out = lambda width: pl.BlockSpec((tm, width), lambda i: (i, 0))
    widths = (512, 512, 512, 512, LANES, LANES, 512, LANES, W_GROUP, W_GROUP)
    dtypes = (BF16, BF16, BF16, BF16, F32, F32, BF16, F32, BF16, BF16)
    return pl.pallas_call(
        _in_proj_kernel,
        grid=(t // tm,),
        in_specs=[pl.BlockSpec((tm, D_MODEL), lambda i: (i, 0)),
                  full((1, D_MODEL)),
                  pl.BlockSpec((None, N_IN_PAD, D_MODEL), lambda i: (layer, 0, 0)),
                  full((1, 512)), full((1, LANES)),
                  full((512, 512)), full((LANES, 512)), full((LANES, 512)),
                  tab, tab, tab, tab, tab, tab],
        out_specs=[out(w) for w in widths],
        out_shape=[jax.ShapeDtypeStruct((t, w), d) for w, d in zip(widths, dtypes)],
        compiler_params=_cparams(("parallel",)),
        name="in_proj",
    )(x, g, w_all, qg, kvg, wuq, wk, wv, *qtabs, *ktabs)


def _fold_lanes(a, op):
    out = a[:, 0:LANES]
    for c in range(1, a.shape[1] // LANES):
        out = op(out, a[:, c * LANES:(c + 1) * LANES])
    return out


def _mla_attn_kernel(q_ref, k_ref, v_ref, o_ref, s_ref, mx_ref, acc_ref, *, tq, tk):
    i = pl.program_id(1)
    nfull = (i * tq) // tk
    t0 = pl.multiple_of(nfull * tk, tk)
    row = i * tq + lax.broadcasted_iota(jnp.int32, (tq, tk), 0)
    col = t0 + lax.broadcasted_iota(jnp.int32, (tq, tk), 1)
    heads = [slice(h * HEAD_D, (h + 1) * HEAD_D) for h in range(MLA_HEADS)]
    chunk = lambda j: pl.ds(pl.multiple_of(j * tk, tk), tk)

    for h, sl in enumerate(heads):
        s = jnp.where(col <= row, _dot_nt(q_ref[:, sl], k_ref[pl.ds(t0, tk), sl]), NEG_INF)
        s_ref[h, nfull] = s
        mx_ref[h] = _fold_lanes(s, jnp.maximum)

    def scores(j, carry):
        for h, sl in enumerate(heads):
            s = _dot_nt(q_ref[:, sl], k_ref[chunk(j), sl])
            s_ref[h, j] = s
            mx_ref[h] = jnp.maximum(mx_ref[h], _fold_lanes(s, jnp.maximum))
        return carry

    lax.fori_loop(0, nfull, scores, 0)
    for h in range(MLA_HEADS):
        m = jnp.max(mx_ref[h], axis=-1, keepdims=True)
        mx_ref[h] = jnp.broadcast_to(m, (tq, LANES))
        acc_ref[h] = jnp.zeros((tq, 2 * HEAD_D), F32)

    ones = jnp.ones((tk, LANES), BF16)

    def values(j, carry):
        for h, sl in enumerate(heads):
            m = mx_ref[h]
            s = s_ref[h, j]
            p = jnp.concatenate([jnp.exp((s[:, c * LANES:(c + 1) * LANES] - m).astype(BF16))
                                 for c in range(tk // LANES)], axis=1)
            v_aug = jnp.concatenate([v_ref[chunk(j), sl], ones], axis=1)
            acc_ref[h] = acc_ref[h] + _dot(p, v_aug)
        return carry

    lax.fori_loop(0, nfull + 1, values, 0)
    for h, sl in enumerate(heads):
        o_ref[:, sl] = acc_ref[h, :, 0:HEAD_D] / acc_ref[h, :, HEAD_D:2 * HEAD_D]


def _mla_attn(q, k, v, batch):
    tq, tk = 256, 512
    nq = SEQ // tq
    return pl.pallas_call(
        functools.partial(_mla_attn_kernel, tq=tq, tk=tk),
        grid=(batch, nq),
        in_specs=[pl.BlockSpec((tq, W_GROUP), lambda b, i: (b * nq + i, 0)),
                  pl.BlockSpec((SEQ, W_GROUP), lambda b, i: (b, 0)),
                  pl.BlockSpec((SEQ, W_GROUP), lambda b, i: (b, 0))],
        out_specs=pl.BlockSpec((tq, W_GROUP), lambda b, i: (b * nq + i, 0)),
        out_shape=jax.ShapeDtypeStruct((batch * SEQ, W_GROUP), F32),
        scratch_shapes=[pltpu.VMEM((MLA_HEADS, SEQ // tk, tq, tk), F32),
                        pltpu.VMEM((MLA_HEADS, tq, LANES), F32),
                        pltpu.VMEM((MLA_HEADS, tq, 2 * HEAD_D), F32)],
        compiler_params=_cparams(("parallel", "arbitrary")),
        name="mla_attn",
    )(q, k, v)


def _nsa_compress_kernel(xk_ref, xv_ref, w1_ref, w2_ref, pe_ref, k_out, v_out):
    for c, (x_ref, o_ref) in enumerate(((xk_ref, k_out), (xv_ref, v_out))):
        a = jnp.zeros((N_CMP_PAD, HEAD_D), F32)
        b = jnp.zeros((N_CMP_PAD, HEAD_D), F32)
        for r in range(CMP_STRIDE):
            x = x_ref[pl.ds(r, N_CMP_PAD, stride=CMP_STRIDE), :].astype(BF16)
            a = a + _dot(x, w1_ref[c, r * HEAD_D:(r + 1) * HEAD_D, :])
            b = b + _dot(x, w1_ref[c, (CMP_STRIDE + r) * HEAD_D:(CMP_STRIDE + r + 1) * HEAD_D, :])
        b = pltpu.roll(b, N_CMP_PAD - 1, 0)
        pe = _dot(pe_ref[c], w1_ref[c])[0:1, :]
        hid = jax.nn.gelu(a + b + pe)
        o_ref[0] = _dot(hid.astype(BF16), w2_ref[c]).astype(BF16)


def _nsa_compress(xk, xv, w1, w2, pe, batch):
    xspec = pl.BlockSpec((SEQ, HEAD_D), lambda b: (b, 0))
    ospec = pl.BlockSpec((1, N_CMP_PAD, HEAD_D), lambda b: (b, 0, 0))
    return pl.pallas_call(
        _nsa_compress_kernel,
        grid=(batch,),
        in_specs=[xspec, xspec,
                  pl.BlockSpec((2, CMP_BLOCK * HEAD_D, HEAD_D), lambda b: (0, 0, 0)),
                  pl.BlockSpec((2, HEAD_D, HEAD_D), lambda b: (0, 0, 0)),
                  pl.BlockSpec((2, 8, CMP_BLOCK * HEAD_D), lambda b: (0, 0, 0))],
        out_specs=[ospec, ospec],
        out_shape=[jax.ShapeDtypeStruct((batch, N_CMP_PAD, HEAD_D), BF16)] * 2,
        compiler_params=_cparams(("parallel",)),
        name="nsa_compress",
    )(xk, xv, w1, w2, pe)


def _softmax_rows(s):
    m = jnp.max(s, axis=-1, keepdims=True)
    p = jnp.exp(s - m)
    return p / jnp.sum(p, axis=-1, keepdims=True)


def _nsa_attn_kernel(q_ref, kc_ref, vc_ref, kv_ref, g_ref, ovt_ref, e_ref, o_ref,
                     m_ref, acc_ref, s_ref, *, tq, ck):
    i = pl.program_id(1)
    q0 = i * tq
    nh = 4
    qs = jnp.concatenate([q_ref[:, h * HEAD_D:(h + 1) * HEAD_D] for h in range(nh)], axis=0)
    qpos = q0 + lax.broadcasted_iota(jnp.int32, (tq, 1), 0)
    masked = lambda s, ok: (s.reshape(nh, tq, s.shape[1])
                            + jnp.where(ok, 0.0, NEG_INF)[None]).reshape(s.shape)

    n_idx = lax.broadcasted_iota(jnp.int32, (tq, N_CMP_PAD), 1)
    valid_c = n_idx * CMP_STRIDE + (CMP_BLOCK - 1) <= qpos
    pc = _softmax_rows(masked(_dot_nt(qs, kc_ref[0]), valid_c)).reshape(nh, tq, N_CMP_PAD)
    pc = jnp.where(valid_c[None], pc, 0.0)
    psum = pc[0] + pc[1] + pc[2] + pc[3]
    o_c = _dot(pc.reshape(nh * tq, N_CMP_PAD).astype(BF16), vc_ref[0])

    imp_t = _dot_nt(ovt_ref[...], psum, precision=lax.Precision.HIGHEST)
    kblk = lax.broadcasted_iota(jnp.int32, (N_BLK, tq), 0)
    cur = (q0 + lax.broadcasted_iota(jnp.int32, (N_BLK, tq), 1)) // SEL_BLOCK
    forced = (kblk == 0) | ((kblk <= cur) & (kblk > cur - N_LOCAL))
    score = jnp.where(forced, FORCE_SCORE, jnp.where(kblk <= cur, imp_t, -1.0))
    cnt = jnp.zeros((N_BLK, tq), F32)
    for j in range(N_BLK):
        sj = score[j:j + 1, :]
        beats = (sj > score) | ((sj == score) & (kblk > j))
        cnt = cnt + beats.astype(F32)
    sel = jnp.transpose((cnt < N_SEL).astype(F32)).astype(BF16)

    wlen = WINDOW + tq
    w0 = pl.multiple_of(jnp.maximum(q0 - WINDOW, 0), tq)
    sw = _dot_nt(qs, kv_ref[pl.ds(w0, wlen), 256:384])
    kpos = w0 + lax.broadcasted_iota(jnp.int32, (tq, wlen), 1)
    ok = (kpos <= qpos) & (kpos > qpos - WINDOW)
    sw = masked(sw, ok)
    pw = jnp.exp((sw - jnp.max(_fold_lanes(sw, jnp.maximum), axis=-1, keepdims=True)).astype(BF16))
    aug = lambda v: jnp.concatenate([v, jnp.ones(v.shape, BF16)], axis=1)
    ow = _dot(pw, aug(kv_ref[pl.ds(w0, wlen), 384:512]))
    o_w = ow[:, 0:HEAD_D] / ow[:, HEAD_D:2 * HEAD_D]

    nck = q0 // ck + 1
    m_ref[...] = jnp.full(m_ref.shape, NEG_INF, F32)

    def scores(c, carry):
        r0 = pl.multiple_of(c * ck, ck)
        chosen = _dot(sel, e_ref[c])
        kpos = r0 + lax.broadcasted_iota(jnp.int32, (tq, ck), 1)
        ok = (chosen > 0.5) & (kpos <= qpos)
        s = masked(_dot_nt(qs, kv_ref[pl.ds(r0, ck), 0:128]), ok)
        s_ref[c] = s
        m_ref[...] = jnp.maximum(m_ref[...], _fold_lanes(s, jnp.maximum))
        return carry

    lax.fori_loop(0, nck, scores, 0)
    m_ref[...] = jnp.broadcast_to(jnp.max(m_ref[...], axis=-1, keepdims=True), m_ref.shape)
    acc_ref[...] = jnp.zeros(acc_ref.shape, F32)

    def values(c, carry):
        r0 = pl.multiple_of(c * ck, ck)
        m = m_ref[...]
        s = s_ref[c]
        p = jnp.concatenate([jnp.exp((s[:, k * LANES:(k + 1) * LANES] - m).astype(BF16))
                             for k in range(ck // LANES)], axis=1)
        acc_ref[...] = acc_ref[...] + _dot(p, aug(kv_ref[pl.ds(r0, ck), 128:256]))
        return carry

    lax.fori_loop(0, nck, values, 0)
    o_s = acc_ref[:, 0:HEAD_D] / acc_ref[:, HEAD_D:2 * HEAD_D]

    g = g_ref[...]
    for h in range(nh):
        rs = slice(h * tq, (h + 1) * tq)
        o_ref[:, h * HEAD_D:(h + 1) * HEAD_D] = (
            g[:, 3 * h:3 * h + 1] * o_c[rs] + g[:, 3 * h + 1:3 * h + 2] * o_s[rs]
            + g[:, 3 * h + 2:3 * h + 3] * o_w[rs])


def _nsa_attn(q, kcmp, vcmp, kv, gates, ovt, emat, batch):
    tq, ck = 256, 512
    nq = SEQ // tq
    return pl.pallas_call(
        functools.partial(_nsa_attn_kernel, tq=tq, ck=ck),
        grid=(batch, nq),
        in_specs=[pl.BlockSpec((tq, 512), lambda b, i: (b * nq + i, 0)),
                  pl.BlockSpec((1, N_CMP_PAD, HEAD_D), lambda b, i: (b, 0, 0)),
                  pl.BlockSpec((1, N_CMP_PAD, HEAD_D), lambda b, i: (b, 0, 0)),
                  pl.BlockSpec((SEQ, 512), lambda b, i: (b, 0)),
                  pl.BlockSpec((tq, LANES), lambda b, i: (b * nq + i, 0)),
                  pl.BlockSpec((N_BLK, N_CMP_PAD), lambda b, i: (0, 0)),
                  pl.BlockSpec((SEQ // ck, N_BLK, ck), lambda b, i: (0, 0, 0))],
        out_specs=pl.BlockSpec((tq, 512), lambda b, i: (b * nq + i, 0)),
        out_shape=jax.ShapeDtypeStruct((batch * SEQ, 512), F32),
        scratch_shapes=[pltpu.VMEM((4 * tq, LANES), F32),
                        pltpu.VMEM((4 * tq, 2 * HEAD_D), F32),
                        pltpu.VMEM((SEQ // ck, 4 * tq, ck), F32)],
        compiler_params=_cparams(("parallel", "arbitrary")),
        name="nsa_attn",
    )(q, kcmp, vcmp, kv, gates, ovt, emat)


def _pool_kernel(u_ref, w_ref, b_ref, s_ref, o_ref, pad_ref):
    maxw = POOL_SIZES[-1]
    pad_ref[0:maxw, :] = jnp.zeros((maxw, W_GROUP), F32)
    pad_ref[maxw:maxw + SEQ, :] = u_ref[...].astype(F32)
    rc = 512
    for g, w in enumerate(POOL_SIZES):
        sl = slice(g * LANES, (g + 1) * LANES)
        for r in range(SEQ // rc):
            acc = pad_ref[maxw + r * rc:maxw + (r + 1) * rc, sl]
            tok = acc
            for j in range(1, w):
                acc = acc + pad_ref[maxw - j + r * rc:maxw - j + (r + 1) * rc, sl]
            t = r * rc + lax.broadcasted_iota(jnp.int32, (rc, 1), 0)
            cnt = jnp.minimum(t + 1, w).astype(F32)
            d = acc / cnt - tok
            y = _dot(d.astype(BF16), w_ref[g])
            o_ref[r * rc:(r + 1) * rc, sl] = (y + b_ref[:, sl]) * s_ref[:, sl]


def _pool(u, w, b, s, batch):
    return pl.pallas_call(
        _pool_kernel,
        grid=(batch,),
        in_specs=[pl.BlockSpec((SEQ, W_GROUP), lambda i: (i, 0)),
                  pl.BlockSpec((4, LANES, LANES), lambda i: (0, 0, 0)),
                  pl.BlockSpec((1, W_GROUP), lambda i: (0, 0)),
                  pl.BlockSpec((1, W_GROUP), lambda i: (0, 0))],
        out_specs=pl.BlockSpec((SEQ, W_GROUP), lambda i: (i, 0)),
        out_shape=jax.ShapeDtypeStruct((batch * SEQ, W_GROUP), F32),
        scratch_shapes=[pltpu.VMEM((SEQ + POOL_SIZES[-1], W_GROUP), F32)],
        compiler_params=_cparams(("parallel",)),
        name="pool",
    )(u, w, b, s)


def _s5_kernel(u_ref, wb_ref, ar_ref, ai_ref, wc_ref, d_ref, gw_ref, gb_ref, o_ref,
               bu_ref, st_ref, tm_ref, *, batch, tc):
    @pl.when(pl.program_id(0) == 0)
    def _():
        st_ref[...] = jnp.zeros(st_ref.shape, F32)

    nslab = W_GROUP // LANES
    for b in range(batch):
        for j in range(nslab):
            tm_ref[j, pl.ds(b, tc, stride=batch), :] = u_ref[b, :, j * LANES:(j + 1) * LANES].astype(F32)
    u = jnp.concatenate([tm_ref[j] for j in range(nslab)], axis=1)
    gl = 8 * S5_STATE
    ub = u.astype(BF16)
    for j in range(nslab):
        uj = ub[:, j * LANES:(j + 1) * LANES]
        for k, part in enumerate((0, S5_WIDTH)):
            cs = slice(part + j * gl, part + (j + 1) * gl)
            bu_ref[:, cs] = _dot(uj, wb_ref[j, :, k * gl:(k + 1) * gl])
    lc = 1024
    unroll = 8
    for c in range(S5_WIDTH // lc):
        re = slice(c * lc, (c + 1) * lc)
        im = slice(S5_WIDTH + c * lc, S5_WIDTH + (c + 1) * lc)
        ar, ai = ar_ref[:, re], ai_ref[:, re]

        def body(tb, carry, re=re, im=im, ar=ar, ai=ai):
            xr, xi = carry
            for k in range(unroll):
                r0 = pl.multiple_of((tb * unroll + k) * batch, batch)
                nxr = ar * xr - ai * xi + bu_ref[pl.ds(r0, batch), re]
                nxi = ar * xi + ai * xr + bu_ref[pl.ds(r0, batch), im]
                bu_ref[pl.ds(r0, batch), re] = nxr
                bu_ref[pl.ds(r0, batch), im] = nxi
                xr, xi = nxr, nxi
            return xr, xi

        xr, xi = lax.fori_loop(0, tc // unroll, body, (st_ref[:, re], st_ref[:, im]))
        st_ref[:, re] = xr
        st_ref[:, im] = xi

    ys = []
    for j in range(nslab):
        yj = 0.0
        for k, part in enumerate((0, S5_WIDTH)):
            cs = slice(part + j * gl, part + (j + 1) * gl)
            yj = yj + _dot(bu_ref[:, cs].astype(BF16), wc_ref[j, k * gl:(k + 1) * gl, :])
        ys.append(yj)
    y = jnp.concatenate(ys, axis=1) + d_ref[...] * u
    y = jax.nn.gelu(y)
    z = _dot(y.astype(BF16), gw_ref[...]) + gb_ref[...]
    o = y * jax.nn.sigmoid(z)
    for j in range(nslab):
        tm_ref[j] = o[:, j * LANES:(j + 1) * LANES]
    for b in range(batch):
        for j in range(nslab):
            o_ref[b, :, j * LANES:(j + 1) * LANES] = tm_ref[j, pl.ds(b, tc, stride=batch), :]


def _s5(u3, wb, ar, ai, wc, d, gw, gb):
    batch = u3.shape[0]
    tc = 64
    rows = tc * batch
    full = lambda shape: pl.BlockSpec(shape, lambda i: (0, 0))
    return pl.pallas_call(
        functools.partial(_s5_kernel, batch=batch, tc=tc),
        grid=(SEQ // tc,),
        in_specs=[pl.BlockSpec((batch, tc, W_GROUP), lambda i: (0, i, 0)),
                  pl.BlockSpec(wb.shape, lambda i: (0, 0, 0)), full((batch, S5_WIDTH)),
                  full((batch, S5_WIDTH)), pl.BlockSpec(wc.shape, lambda i: (0, 0, 0)),
                  full((1, W_GROUP)),
                  full((W_GROUP, W_GROUP)), full((1, W_GROUP))],
        out_specs=pl.BlockSpec((batch, tc, W_GROUP), lambda i: (0, i, 0)),
        out_shape=jax.ShapeDtypeStruct((batch, SEQ, W_GROUP), F32),
        scratch_shapes=[pltpu.VMEM((rows, 2 * S5_WIDTH), F32),
                        pltpu.VMEM((batch, 2 * S5_WIDTH), F32),
                        pltpu.VMEM((W_GROUP // LANES, rows, LANES), F32)],
        compiler_params=_cparams(("arbitrary",)),
        name="s5",
    )(u3, wb, ar, ai, wc, d, gw, gb)


def _out_proj_kernel(ya_ref, yb_ref, yc_ref, yd_ref, mg_ref, wo_ref, x_ref, fg_ref,
                     wrh_ref, wrl_ref, br_ref, x1_ref, h2_ref, lg_ref, *, tm):
    acc = x_ref[...]
    for gi, y_ref in enumerate((ya_ref, yb_ref, yc_ref, yd_ref)):
        sl = slice(gi * W_GROUP, (gi + 1) * W_GROUP)
        n = (_rms(y_ref[...]) * mg_ref[:, sl]).astype(BF16)
        acc = acc + _dot(n, wo_ref[sl, :])
    x1_ref[...] = acc
    h2 = _rms(acc) * fg_ref[...]
    hi = h2.astype(BF16)
    lo = (h2 - hi.astype(F32)).astype(BF16)
    lg_ref[...] = (_dot(hi, wrh_ref[...]) + _dot(hi, wrl_ref[...]) + _dot(lo, wrh_ref[...])
                   + br_ref[...])
    _pack_rows(h2_ref, h2, tm)


def _out_proj(ya, yb, yc, yd, mg, wo_all, layer, x, fg, wr_hi, wr_lo, br):
    t = x.shape[0]
    tm = 256
    yspec = pl.BlockSpec((tm, W_GROUP), lambda i: (i, 0))
    full = lambda shape: pl.BlockSpec(shape, lambda i: (0, 0))
    return pl.pallas_call(
        functools.partial(_out_proj_kernel, tm=tm),
        grid=(t // tm,),
        in_specs=[yspec, yspec, yspec, yspec, full((1, D_MODEL)),
                  pl.BlockSpec((None, D_MODEL, D_MODEL), lambda i: (layer, 0, 0)),
                  pl.BlockSpec((tm, D_MODEL), lambda i: (i, 0)), full((1, D_MODEL)),
                  full((D_MODEL, LANES)), full((D_MODEL, LANES)), full((1, LANES))],
        out_specs=[pl.BlockSpec((tm, D_MODEL), lambda i: (i, 0)),
                   pl.BlockSpec((tm * ROW_PITCH, LANES), lambda i: (i, 0)),
                   pl.BlockSpec((tm, LANES), lambda i: (i, 0))],
        out_shape=[jax.ShapeDtypeStruct((t, D_MODEL), F32),
                   jax.ShapeDtypeStruct((t * ROW_PITCH, LANES), F32),
                   jax.ShapeDtypeStruct((t, LANES), F32)],
        compiler_params=_cparams(("parallel",)),
        name="out_proj",
    )(ya, yb, yc, yd, mg, wo_all, x, fg, wr_hi, wr_lo, br)


ROUTE_ROWS = 40


def _route(lg, meta_ref, metat_ref, cnt_ref, col_ref, row_ref, tm):
    @pl.when(pl.program_id(0) == 0)
    def _():
        col_ref[...] = jnp.zeros(col_ref.shape, F32)
        row_ref[...] = jnp.zeros(row_ref.shape, F32)

    nr = ROUTE_ROWS
    lt = jnp.transpose(lg)[0:nr, :]
    row = lax.broadcasted_iota(jnp.int32, (nr, tm), 0)
    big = jnp.int32(1 << 20)
    cmax = lambda a: jnp.max(a, axis=0, keepdims=True)
    cmin = lambda a: jnp.min(a, axis=0, keepdims=True)
    csum = lambda a: jnp.sum(a, axis=0, keepdims=True)

    is_g = row < MOE_GROUPS
    gl = jnp.where(is_g, lt, NEG_INF)
    gm = cmax(gl)
    p_top = 1.0 / csum(jnp.where(is_g, jnp.exp(gl - gm), 0.0))
    g_top = cmin(jnp.where(is_g & (gl == gm), row, big))

    is_e = (row >= MOE_GROUPS) & (row < MOE_GROUPS + MOE_EXPERTS) \
        & (((row - MOE_GROUPS) // MOE_EPG) == g_top)
    el = jnp.where(is_e, lt, NEG_INF)
    ee = jnp.where(is_e, jnp.exp(el - cmax(el)), 0.0)
    p = jnp.where(is_e, ee / csum(ee), -1.0)
    p1 = cmax(p)
    i1 = cmin(jnp.where(p == p1, row, big))
    p_rest = jnp.where(row == i1, -1.0, p)
    p2 = cmax(p_rest)
    i2 = cmin(jnp.where((p_rest == p2) & is_e & (row != i1), row, big))
    den = p1 + p2
    w1 = p_top * (p1 / den)
    w2 = p_top * (p2 / den)

    hit1, hit2 = row == i1, row == i2
    oh = (hit1 | hit2).astype(BF16)
    r = lax.broadcasted_iota(jnp.int32, (tm, tm), 0)
    c = lax.broadcasted_iota(jnp.int32, (tm, tm), 1)
    before = _dot(oh, (r < c).astype(BF16)) + col_ref[:, 0:1]
    r1 = csum(jnp.where(hit1, before, 0.0))
    r2 = csum(jnp.where(hit2, before, 0.0))
    col_ref[...] = col_ref[...] + jnp.sum(oh.astype(F32), axis=1, keepdims=True)
    oh_all = jnp.concatenate([oh, jnp.zeros((LANES - nr, tm), BF16)], axis=0)
    row_ref[...] = row_ref[...] + _dot_nt(jnp.ones((8, tm), BF16), oh_all)
    cnt_ref[...] = row_ref[...]

    e1 = (i1 - MOE_GROUPS).astype(F32)
    e2 = (i2 - MOE_GROUPS).astype(F32)
    row8 = lax.broadcasted_iota(jnp.int32, (8, tm), 0)
    meta = jnp.zeros((8, tm), F32)
    for k, v in enumerate((e1, e2, w1, w2, r1, r2)):
        meta = jnp.where(row8 == k, v, meta)
    metat_ref[...] = meta
    meta = jnp.concatenate([meta, jnp.zeros((LANES - 8, tm), F32)], axis=0)
    meta_ref[...] = jnp.transpose(meta)


def _router_kernel(lg_ref, meta_ref, metat_ref, cnt_ref, col_ref, row_ref, *, tm):
    _route(lg_ref[...], meta_ref, metat_ref, cnt_ref, col_ref, row_ref, tm)


def _router(logits):
    t = logits.shape[0]
    tm = 512
    return pl.pallas_call(
        functools.partial(_router_kernel, tm=tm),
        grid=(t // tm,),
        in_specs=[pl.BlockSpec((tm, LANES), lambda i: (i, 0))],
        out_specs=[pl.BlockSpec((tm, LANES), lambda i: (i, 0)),
                   pl.BlockSpec((8, tm), lambda i: (0, i)),
                   pl.BlockSpec((8, LANES), lambda i: (i, 0))],
        out_shape=[jax.ShapeDtypeStruct((t, LANES), F32),
                   jax.ShapeDtypeStruct((8, t), F32),
                   jax.ShapeDtypeStruct((t // tm * 8, LANES), F32)],
        scratch_shapes=[pltpu.VMEM((ROUTE_ROWS, LANES), F32), pltpu.VMEM((8, LANES), F32)],
        compiler_params=_cparams(("arbitrary",)),
        name="router",
    )(logits)


def _dest_kernel(metat_ref, cnt_ref, o_ref, *, tm):
    cnt = cnt_ref[...]
    start = (_lane_cumsum(cnt) - cnt)[0:1, :]
    nr = ROUTE_ROWS
    start_col = jnp.transpose(jnp.broadcast_to(start, (LANES, LANES)))[0:nr, 0:1]
    mt = metat_ref[...]
    row = lax.broadcasted_iota(jnp.int32, (nr, tm), 0)
    row8 = lax.broadcasted_iota(jnp.int32, (8, tm), 0)
    out = jnp.zeros((8, tm), F32)
    for k in range(2):
        e_row = mt[k:k + 1, :].astype(jnp.int32) + MOE_GROUPS
        d = jnp.sum(jnp.where(row == e_row, start_col, 0.0), axis=0, keepdims=True) + mt[4 + k:5 + k, :]
        out = jnp.where(row8 == k, d, out)
    o_ref[...] = out.astype(jnp.int32)


def _dest(meta_t, counts):
    t = meta_t.shape[1]
    tm = 512
    last = counts.shape[0] // 8 - 1
    return pl.pallas_call(
        functools.partial(_dest_kernel, tm=tm),
        grid=(t // tm,),
        in_specs=[pl.BlockSpec((8, tm), lambda i: (0, i)),
                  pl.BlockSpec((8, LANES), lambda i: (last, 0))],
        out_specs=pl.BlockSpec((8, tm), lambda i: (0, i)),
        out_shape=jax.ShapeDtypeStruct((8, t), jnp.int32),
        compiler_params=_cparams(("parallel",)),
        name="dest",
    )(meta_t, counts)


def _dispatch_kernel(d1_ref, d2_ref, h_ref, xs_ref, sem, *, td):
    base = pl.program_id(0) * td

    def start(rb, carry):
        for k in range(DMA_UNROLL):
            r = rb * DMA_UNROLL + k
            src = h_ref.at[pl.ds(r * ROW_PITCH, ROW_PITCH)]
            for d_ref in (d1_ref, d2_ref):
                dst = xs_ref.at[pl.ds(d_ref[base + r] * ROW_PITCH, ROW_PITCH)]
                pltpu.make_async_copy(src, dst, sem).start()
        return carry

    lax.fori_loop(0, td // DMA_UNROLL, start, 0)
    for _ in range(2):
        pltpu.make_async_copy(h_ref, xs_ref.at[pl.ds(0, td * ROW_PITCH)], sem).wait()


def _dispatch(dest1, dest2, h2):
    t = dest1.shape[0]
    td = 512
    return pl.pallas_call(
        functools.partial(_dispatch_kernel, td=td),
        grid_spec=pltpu.PrefetchScalarGridSpec(
            num_scalar_prefetch=2,
            grid=(t // td,),
            in_specs=[pl.BlockSpec((td * ROW_PITCH, LANES), lambda i, *_: (i, 0))],
            out_specs=pl.BlockSpec(memory_space=pl.ANY),
            scratch_shapes=[pltpu.SemaphoreType.DMA(())]),
        out_shape=jax.ShapeDtypeStruct((2 * t * ROW_PITCH, LANES), F32),
        compiler_params=pltpu.CompilerParams(dimension_semantics=("arbitrary",),
                                             has_side_effects=True),
        name="dispatch",
    )(dest1, dest2, h2)


def _expert_kernel(vt_ref, ve_ref, vlo_ref, vhi_ref, vfirst_ref, vvalid_ref, vnew_ref, vnext_ref,
                   vslot_ref, xs_ref, wg_ref, wu_ref, wd_ref, ys_ref,
                   wgf_ref, wuf_ref, wdf_ref, wgb_ref, wub_ref, wdb_ref, sem, *, tmx, layer):
    v = pl.program_id(0)

    def weight_copies(e, slot):
        return [pltpu.make_async_copy(w_ref.at[layer, e], f_ref.at[slot], sem.at[slot])
                for w_ref, f_ref in ((wg_ref, wgf_ref), (wu_ref, wuf_ref), (wd_ref, wdf_ref))]

    @pl.when(vnew_ref[v] == 1)
    def _():
        slot = vslot_ref[v]

        @pl.when(v == 0)
        def _():
            for cp in weight_copies(ve_ref[v], slot):
                cp.start()

        for cp in weight_copies(ve_ref[v], slot):
            cp.wait()

        @pl.when(vnext_ref[v] >= 0)
        def _():
            for cp in weight_copies(vnext_ref[v], 1 - slot):
                cp.start()

        wgb_ref[...] = wgf_ref[slot].astype(BF16)
        wub_ref[...] = wuf_ref[slot].astype(BF16)
        wdb_ref[...] = wdf_ref[slot].astype(BF16)

    @pl.when(vvalid_ref[v] == 1)
    def _():
        x = jnp.concatenate(_unpack_rows(lambda rows: xs_ref[rows, :], tmx), axis=1).astype(BF16)
        a = _dot(x, wgb_ref[...])
        u = _dot(x, wub_ref[...])
        hid = (jax.nn.silu(a) * u).astype(BF16)
        y = _dot(hid, wdb_ref[...])
        rows = lax.broadcasted_iota(jnp.int32, (tmx, 1), 0)
        mine = (rows >= vlo_ref[v]) & (rows < vhi_ref[v])

        @pl.when(vfirst_ref[v] == 1)
        def _():
            _pack_rows(ys_ref, y, tmx, mask=mine)

        @pl.when(vfirst_ref[v] == 0)
        def _():
            _pack_rows(ys_ref, y, tmx, mask=mine, old=True)


def _experts(sched, xs, wg, wu, wd, layer, tmx):
    nvis = sched[0].shape[0]
    rows = xs.shape[0]
    xspec = pl.BlockSpec((tmx * ROW_PITCH, LANES), lambda v, vt, *_: (vt[v], 0))
    hbm = pl.BlockSpec(memory_space=pl.ANY)
    return pl.pallas_call(
        functools.partial(_expert_kernel, tmx=tmx, layer=layer),
        grid_spec=pltpu.PrefetchScalarGridSpec(
            num_scalar_prefetch=9,
            grid=(nvis,),
            in_specs=[xspec, hbm, hbm, hbm],
            out_specs=xspec,
            scratch_shapes=[pltpu.VMEM((2, D_MODEL, MOE_HIDDEN), F32),
                            pltpu.VMEM((2, D_MODEL, MOE_HIDDEN), F32),
                            pltpu.VMEM((2, MOE_HIDDEN, D_MODEL), F32),
                            pltpu.VMEM((D_MODEL, MOE_HIDDEN), BF16),
                            pltpu.VMEM((D_MODEL, MOE_HIDDEN), BF16),
                            pltpu.VMEM((MOE_HIDDEN, D_MODEL), BF16),
                            pltpu.SemaphoreType.DMA((2,))]),
        out_shape=jax.ShapeDtypeStruct((rows, LANES), F32),
        compiler_params=_cparams(("arbitrary",)),
        name="experts",
    )(*sched, xs, wg, wu, wd)


def _combine_kernel(d1_ref, d2_ref, ys_ref, meta_ref, x1_ref, fg_ref, o_ref,
                    b1_ref, b2_ref, sem, *, tc, final):
    i = pl.program_id(0)
    n = pl.num_programs(0)
    slot = i % 2

    def gather(tile, slot):
        def start(rb, carry):
            for k in range(DMA_UNROLL):
                r = rb * DMA_UNROLL + k
                dst = pl.ds(r * ROW_PITCH, ROW_CHUNKS)
                for d_ref, b_ref in ((d1_ref, b1_ref), (d2_ref, b2_ref)):
                    src = ys_ref.at[pl.ds(d_ref[tile * tc + r] * ROW_PITCH, ROW_CHUNKS)]
                    pltpu.make_async_copy(src, b_ref.at[slot, dst], sem.at[slot]).start()
            return carry

        lax.fori_loop(0, tc // DMA_UNROLL, start, 0)

    @pl.when(i == 0)
    def _():
        gather(0, 0)

    @pl.when(i + 1 < n)
    def _():
        gather(i + 1, 1 - slot)

    for b_ref in (b1_ref, b2_ref):
        pltpu.make_async_copy(ys_ref.at[pl.ds(0, tc * ROW_CHUNKS)],
                              b_ref.at[slot, pl.ds(0, tc * ROW_CHUNKS)], sem.at[slot]).wait()
    w1 = meta_ref[:, 2:3]
    w2 = meta_ref[:, 3:4]
    y1 = _unpack_rows(lambda rows: b1_ref[slot, rows, :], tc)
    y2 = _unpack_rows(lambda rows: b2_ref[slot, rows, :], tc)
    x2 = jnp.concatenate([x1_ref[:, c * LANES:(c + 1) * LANES] + (w1 * y1[c] + w2 * y2[c])
                          for c in range(ROW_CHUNKS)], axis=1)
    if final:
        x2 = _rms(x2) * fg_ref[...]
    o_ref[...] = x2


def _combine(dest1, dest2, ys, meta, x1, fg, final):
    t = x1.shape[0]
    tc = 256
    return pl.pallas_call(
        functools.partial(_combine_kernel, tc=tc, final=final),
        grid_spec=pltpu.PrefetchScalarGridSpec(
            num_scalar_prefetch=2,
            grid=(t // tc,),
            in_specs=[pl.BlockSpec(memory_space=pl.ANY),
                      pl.BlockSpec((tc, LANES), lambda i, *_: (i, 0)),
                      pl.BlockSpec((tc, D_MODEL), lambda i, *_: (i, 0)),
                      pl.BlockSpec((1, D_MODEL), lambda i, *_: (0, 0))],
            out_specs=pl.BlockSpec((tc, D_MODEL), lambda i, *_: (i, 0)),
            scratch_shapes=[pltpu.VMEM((2, tc * ROW_PITCH, LANES), F32),
                            pltpu.VMEM((2, tc * ROW_PITCH, LANES), F32),
                            pltpu.SemaphoreType.DMA((2,))]),
        out_shape=jax.ShapeDtypeStruct((t, D_MODEL), F32),
        compiler_params=_cparams(("arbitrary",)),
        name="combine",
    )(dest1, dest2, ys, meta, x1, fg)


def _rope_tables(r0):
    inv = ROPE_THETA ** (-jnp.arange(0, 2 * ROPE_HALF, 2, dtype=F32) / (2 * ROPE_HALF))
    ang = jnp.arange(SEQ, dtype=F32)[:, None] * inv[None, :]
    cos, sin = jnp.cos(ang), jnp.sin(ang)
    c = jnp.ones((SEQ, LANES), F32).at[:, r0:r0 + ROPE_HALF].set(cos)
    c = c.at[:, r0 + ROPE_HALF:r0 + 2 * ROPE_HALF].set(cos)
    sa = jnp.zeros((SEQ, LANES), F32).at[:, r0:r0 + ROPE_HALF].set(-sin)
    sb = jnp.zeros((SEQ, LANES), F32).at[:, r0 + ROPE_HALF:r0 + 2 * ROPE_HALF].set(sin)
    return c, sa, sb


def _pad_cols(w, width):
    return jnp.pad(w, ((0, 0), (0, width - w.shape[1])))


def _s5_params(a_re, a_im, log_dt, b_re, b_im, c_re, c_im, batch):
    dt = jnp.exp(log_dt)[:, None]
    mag = jnp.exp(a_re * dt)
    abar_r, abar_i = mag * jnp.cos(a_im * dt), mag * jnp.sin(a_im * dt)
    den = a_re * a_re + a_im * a_im
    nr, ni = abar_r - 1.0, abar_i
    coef_r = (nr * a_re + ni * a_im) / den
    coef_i = (ni * a_re - nr * a_im) / den
    bbar_r = coef_r[..., None] * b_re - coef_i[..., None] * b_im
    bbar_i = coef_r[..., None] * b_im + coef_i[..., None] * b_re
    nslab, gps = W_GROUP // LANES, LANES // S5_CH
    eye = jnp.eye(gps, dtype=F32)
    slab = lambda m: m.reshape((nslab, gps) + m.shape[1:])
    blk_b = lambda m: jnp.einsum('jgpc,gh->jgchp', slab(m), eye).reshape(nslab, LANES, gps * S5_STATE)
    wb = jnp.concatenate([blk_b(bbar_r), blk_b(bbar_i)], axis=2).astype(BF16)
    blk_c = lambda m: jnp.einsum('jgcp,gh->jgphc', slab(m), eye).reshape(nslab, gps * S5_STATE, LANES)
    wc = jnp.concatenate([blk_c(c_re), blk_c(-c_im)], axis=1).astype(BF16)
    ar = jnp.broadcast_to(abar_r.reshape(1, S5_WIDTH), (batch, S5_WIDTH))
    ai = jnp.broadcast_to(abar_i.reshape(1, S5_WIDTH), (batch, S5_WIDTH))
    return wb, ar, ai, wc


def _nsa_consts(ck):
    c_start = np.arange(N_CMP_PAD) * CMP_STRIDE
    b_start = np.arange(N_BLK) * SEL_BLOCK
    ov = ((c_start[None, :] < b_start[:, None] + SEL_BLOCK)
          & (c_start[None, :] + CMP_BLOCK > b_start[:, None])).astype(np.float32)
    key_blk = np.arange(SEQ) // SEL_BLOCK
    e = (key_blk[None, :] == np.arange(N_BLK)[:, None]).astype(np.float32)
    e = e.reshape(N_BLK, SEQ // ck, ck).transpose(1, 0, 2)
    return jnp.asarray(ov), jnp.asarray(e, dtype=BF16)


def _lane_cumsum(v):
    r = lax.broadcasted_iota(jnp.int32, (LANES, LANES), 0)
    c = lax.broadcasted_iota(jnp.int32, (LANES, LANES), 1)
    incl = (r <= c).astype(BF16)
    hi = jnp.floor(v * (1.0 / 256.0))
    lo = v - 256.0 * hi
    return 256.0 * _dot(hi.astype(BF16), incl) + _dot(lo.astype(BF16), incl)


def _sched_kernel(cnt_ref, o_ref, *, tmx, nv):
    cnt = cnt_ref[...]
    ends = _lane_cumsum(cnt)
    offs = ends - cnt
    first = jnp.floor(offs * (1.0 / tmx))
    last = jnp.floor(jnp.maximum(ends - 1.0, 0.0) * (1.0 / tmx))
    nvis = jnp.where(cnt > 0.0, last - first + 1.0, 0.0)
    cumv = _lane_cumsum(nvis)
    row1 = lambda a: a[0:1, :]
    total = jnp.max(row1(cumv), axis=-1, keepdims=True)
    rsum = lambda a: jnp.sum(a, axis=-1, keepdims=True)
    v = lax.broadcasted_iota(jnp.int32, (nv, LANES), 0).astype(F32)
    lane = lax.broadcasted_iota(jnp.int32, (nv, LANES), 1).astype(F32)
    vc = jnp.minimum(v, total - 1.0)
    e_lane = rsum((row1(cumv) <= vc).astype(F32))
    hit = lane == e_lane
    pick = lambda a: rsum(jnp.where(hit, row1(a), 0.0))
    vt = pick(first) + vc[:, 0:1] - (pick(cumv) - pick(nvis))
    vlo = jnp.clip(pick(offs) - vt * tmx, 0.0, float(tmx))
    vhi = jnp.clip(pick(ends) - vt * tmx, 0.0, float(tmx))
    changed = lambda a: (v == 0.0) | (a != pltpu.roll(a, 1, 0))
    vfirst = changed(jnp.broadcast_to(vt, (nv, LANES))).astype(F32)
    vnew = changed(jnp.broadcast_to(e_lane, (nv, LANES))).astype(F32)
    vvalid = (v < total).astype(F32)
    nonempty = (cnt > 0.0).astype(F32)
    order = pick(_lane_cumsum(nonempty) - nonempty)
    vslot = order - 2.0 * jnp.floor(order * 0.5)
    far = float(1 << 20)
    nxt = jnp.min(jnp.where((row1(nonempty) > 0.0) & (lane > e_lane), lane, far), axis=-1, keepdims=True)
    vnext = jnp.where(nxt >= far, -1.0, nxt - MOE_GROUPS)
    out = jnp.zeros((nv, LANES), F32)
    cols = (vt, e_lane - MOE_GROUPS, vlo, vhi, vfirst, vvalid, vnew, vnext, vslot)
    for k, col in enumerate(cols):
        out = jnp.where(lane == k, col, out)
    o_ref[...] = out.astype(jnp.int32)


def _moe_schedule(counts, n_rows, tmx):
    nvis_max = n_rows // tmx + MOE_EXPERTS
    nv = 256
    last = counts.shape[0] // 8 - 1
    sched = pl.pallas_call(
        functools.partial(_sched_kernel, tmx=tmx, nv=nv),
        grid=(1,),
        in_specs=[pl.BlockSpec((8, LANES), lambda i: (last, 0))],
        out_specs=pl.BlockSpec((nv, LANES), lambda i: (0, 0)),
        out_shape=jax.ShapeDtypeStruct((nv, LANES), jnp.int32),
        name="sched",
    )(counts)
    return tuple(sched[:nvis_max, k] for k in range(9))


def kernel(x, attn_norm_g, w_in, mla_q_norm_g, mla_kv_norm_g, mla_w_uq, mla_w_ukv, nsa_cmp_pe, nsa_cmp_w1, nsa_cmp_w2, pool_w, pool_b, pool_scale, s5_a_re, s5_a_im, s5_log_dt, s5_b_re, s5_b_im, s5_c_re, s5_c_im, s5_d, s5_glu_w, s5_glu_b, mix_norm_g, w_out, ffn_norm_g, moe_w_group, moe_b_group, moe_w_expert, moe_b_expert, moe_w_gate, moe_w_up, moe_w_down, final_norm_g):
    batch, seq, d = x.shape
    depth = w_in.shape[0]
    t = batch * seq
    xf = x.reshape(t, d)
    qtabs = _rope_tables(MLA_NOPE)
    ktabs = _rope_tables(0)
    ovt, emat = _nsa_consts(512)
    row = lambda v: v.reshape(1, -1)
    tmx = 256
    w_in_all = _w_in_prep(w_in)
    w_out_all = w_out.astype(BF16)

    for l in range(depth):
        wuq = jnp.pad(mla_w_uq[l], ((0, 512 - MLA_Q_LORA), (0, 0))).astype(BF16)
        ukv = mla_w_ukv[l].reshape(HEAD_D, MLA_HEADS, MLA_NOPE + MLA_V)
        wk = jnp.pad(ukv[:, :, :MLA_NOPE], ((0, 0), (0, 0), (0, HEAD_D - MLA_NOPE)))
        wk = wk.reshape(HEAD_D, 512).astype(BF16)
        wv = ukv[:, :, MLA_NOPE:].reshape(HEAD_D, 512).astype(BF16)
        qg = jnp.pad(mla_q_norm_g[l], (0, 512 - MLA_Q_LORA)).reshape(1, 512)
        q_a, k_a, v_a, q_b, kc, vc, kv_b, gates, u_pool, u_s5 = _in_proj(
            xf, row(attn_norm_g[l]), w_in_all, l, qg, row(mla_kv_norm_g[l]), wuq, wk, wv, qtabs, ktabs)

        y_a = _mla_attn(q_a, k_a, v_a, batch)

        pe = jnp.broadcast_to(nsa_cmp_pe[l].reshape(2, 1, CMP_BLOCK * HEAD_D),
                              (2, 8, CMP_BLOCK * HEAD_D)).astype(BF16)
        kcmp, vcmp = _nsa_compress(kc, vc, nsa_cmp_w1[l].astype(BF16), nsa_cmp_w2[l].astype(BF16),
                                   pe, batch)
        y_b = _nsa_attn(q_b, kcmp, vcmp, kv_b, gates, ovt, emat, batch)

        y_c = _pool(u_pool, pool_w[l].astype(BF16), row(pool_b[l]), row(pool_scale[l]), batch)

        wb, ar, ai, wc = _s5_params(s5_a_re[l], s5_a_im[l], s5_log_dt[l], s5_b_re[l], s5_b_im[l],
                                    s5_c_re[l], s5_c_im[l], batch)
        y_d = _s5(u_s5.reshape(batch, seq, W_GROUP), wb, ar, ai, wc, row(s5_d[l]),
                  s5_glu_w[l].astype(BF16), row(s5_glu_b[l])).reshape(t, W_GROUP)

        wr = jnp.concatenate([moe_w_group[l], moe_w_expert[l]], axis=1)
        wr = _pad_cols(wr, LANES)
        wr_hi = wr.astype(BF16)
        wr_lo = (wr - wr_hi.astype(F32)).astype(BF16)
        br = jnp.pad(jnp.concatenate([moe_b_group[l], moe_b_expert[l]]), (0, LANES - 36)).reshape(1, LANES)
        x1, h2, logits = _out_proj(y_a, y_b, y_c, y_d, row(mix_norm_g[l]), w_out_all, l,
                                   xf, row(ffn_norm_g[l]), wr_hi, wr_lo, br)

        meta, meta_t, counts = _router(logits)
        dest = _dest(meta_t, counts)
        dest1, dest2 = dest[0], dest[1]
        sched = _moe_schedule(counts, 2 * t, tmx)
        xs = _dispatch(dest1, dest2, h2)
        ys = _experts(sched, xs, moe_w_gate, moe_w_up, moe_w_down, l, tmx)
        xf = _combine(dest1, dest2, ys, meta, x1, row(final_norm_g), final=(l == depth - 1))

    return xf.reshape(batch, seq, d)
```

```python
import functools
import math

import numpy as np
import jax
import jax.numpy as jnp
from jax import lax
from jax.experimental import pallas as pl
from jax.experimental.pallas import tpu as pltpu

F32 = jnp.float32
BF16 = jnp.bfloat16

D_MODEL = 2048
SEQ = 2048
W_GROUP = 512
LANES = 128
ROW_CHUNKS = D_MODEL // LANES
ROW_PITCH = ROW_CHUNKS + 1
DMA_UNROLL = 4

ROPE_THETA = 500000.0
ROPE_HALF = 16
NEG_INF = -1.0e30
FORCE_SCORE = 1.0e4
EPS = 1e-6

MLA_HEADS = 4
MLA_Q_LORA = 448
MLA_NOPE = 96
MLA_V = 128
HEAD_D = 128

CMP_BLOCK = 32
CMP_STRIDE = 16
SEL_BLOCK = 64
N_SEL = 8
N_LOCAL = 2
WINDOW = 512
N_CMP_PAD = SEQ // CMP_STRIDE
N_BLK = SEQ // SEL_BLOCK

POOL_SIZES = (2, 4, 8, 16)
S5_GROUPS = 32
S5_CH = 16
S5_STATE = 64
S5_WIDTH = S5_GROUPS * S5_STATE

MOE_GROUPS = 4
MOE_EPG = 8
MOE_EXPERTS = 32
MOE_HIDDEN = 512

COL_CQ, COL_NQ, COL_POOL, COL_S5 = 0, 512, 1024, 1536
COL_CKV, COL_KR, COL_KV6, COL_GL = 2048, 2176, 2304, 3072
N_IN_PAD = 3200

VMEM_LIMIT = 56 * 1024 * 1024


def _cparams(sem, vmem=VMEM_LIMIT):
    return pltpu.CompilerParams(dimension_semantics=sem, vmem_limit_bytes=vmem)


def _rms(x, n=None):
    n = x.shape[-1] if n is None else n
    return x * lax.rsqrt(jnp.sum(x * x, axis=-1, keepdims=True) / n + EPS)


def _dot(a, b):
    return jnp.dot(a, b, preferred_element_type=F32)


def _dot_nt(a, b, precision=None):
    return lax.dot_general(a, b, (((1,), (1,)), ((), ())), preferred_element_type=F32,
                           precision=precision)


def _rope(x, c, sa, sb):
    return x * c + pltpu.roll(x, LANES - ROPE_HALF, 1) * sa + pltpu.roll(x, ROPE_HALF, 1) * sb


def _pack_rows(ref, x, n, mask=None, old=False):
    for s in range(ROW_CHUNKS):
        rows = pl.ds(s, n, stride=ROW_PITCH)
        w = x[:, s * LANES:(s + 1) * LANES]
        if mask is not None:
            w = jnp.where(mask, w, ref[rows, :] if old else 0.0)
        ref[rows, :] = w
    if not old:
        ref[pl.ds(ROW_CHUNKS, n, stride=ROW_PITCH), :] = jnp.zeros((n, LANES), F32)


def _unpack_rows(load, n):
    return [load(pl.ds(s, n, stride=ROW_PITCH)) for s in range(ROW_CHUNKS)]


_W_IN_SEGMENTS = ((COL_CQ, 0, 448), (COL_NQ, 608, 512), (COL_POOL, 1900, 512), (COL_S5, 2412, 512),
                  (COL_CKV, 448, 128), (COL_KR, 576, 32), (COL_KV6, 1120, 768), (COL_GL, 1888, 12))


def _w_in_prep_kernel(w_ref, o_ref):
    o_ref[0] = jnp.zeros(o_ref.shape[1:], BF16)
    for dst, src, width in _W_IN_SEGMENTS:
        o_ref[0, :, dst:dst + width] = w_ref[0, :, src:src + width].astype(BF16)


def _w_in_prep(w_in):
    depth, d, n = w_in.shape
    tk = 256
    return pl.pallas_call(
        _w_in_prep_kernel,
        grid=(depth, d // tk),
        in_specs=[pl.BlockSpec((1, tk, n), lambda l, k: (l, k, 0))],
        out_specs=pl.BlockSpec((1, tk, N_IN_PAD), lambda l, k: (l, k, 0)),
        out_shape=jax.ShapeDtypeStruct((depth, d, N_IN_PAD), BF16),
        compiler_params=_cparams(("parallel", "parallel")),
        name="w_in_prep",
    )(w_in)


def _in_proj_kernel(x_ref, g_ref, w_ref, qg_ref, kvg_ref, wuq_ref, wk_ref, wv_ref,
                    qc_ref, qsa_ref, qsb_ref, kc_ref, ksa_ref, ksb_ref,
                    qa_out, ka_out, va_out, qb_out, kcmp_out, vcmp_out, kvb_out, gate_out,
                    pool_out, s5_out):
    scale = 1.0 / math.sqrt(HEAD_D)
    h = (_rms(x_ref[...]) * g_ref[...]).astype(BF16)
    seg = lambda col, width: _dot(h, w_ref[:, col:col + width])
    heads = [slice(n * HEAD_D, (n + 1) * HEAD_D) for n in range(4)]

    qn = (_rms(seg(COL_CQ, 512), MLA_Q_LORA) * qg_ref[...]).astype(BF16)
    q = _dot(qn, wuq_ref[...])
    kvn = (_rms(seg(COL_CKV, LANES)) * kvg_ref[...]).astype(BF16)
    kn = _dot(kvn, wk_ref[...])
    va_out[...] = _dot(kvn, wv_ref[...]).astype(BF16)
    kc, ksa, ksb = kc_ref[...], ksa_ref[...], ksb_ref[...]
    kr = pltpu.roll(_rope(seg(COL_KR, LANES), kc, ksa, ksb), MLA_NOPE, 1)
    for sl in heads:
        qa_out[:, sl] = (_rope(q[:, sl], qc_ref[...], qsa_ref[...], qsb_ref[...]) * scale).astype(BF16)
        ka_out[:, sl] = (kn[:, sl] + kr).astype(BF16)

    qb = seg(COL_NQ, 512)
    for sl in heads:
        qb_out[:, sl] = (_rope(qb[:, sl], kc, ksa, ksb) * scale).astype(BF16)
    kv = seg(COL_KV6, 6 * HEAD_D)
    part = lambda n: kv[:, n * HEAD_D:(n + 1) * HEAD_D]
    kcmp_out[...] = _rope(part(0), kc, ksa, ksb)
    vcmp_out[...] = part(1)
    kvb_out[:, 0:128] = _rope(part(2), kc, ksa, ksb).astype(BF16)
    kvb_out[:, 128:256] = part(3).astype(BF16)
    kvb_out[:, 256:384] = _rope(part(4), kc, ksa, ksb).astype(BF16)
    kvb_out[:, 384:512] = part(5).astype(BF16)
    gate_out[...] = jax.nn.sigmoid(seg(COL_GL, LANES))

    pool_out[...] = seg(COL_POOL, W_GROUP).astype(BF16)
    s5_out[...] = seg(COL_S5, W_GROUP).astype(BF16)


def _in_proj(x, g, w_all, layer, qg, kvg, wuq, wk, wv, qtabs, ktabs):
    t = x.shape[0]
    tm = 512
    nsb = SEQ // tm
    tab = pl.BlockSpec((tm, LANES), lambda i: (i % nsb, 0))
    full = lambda shape: pl.BlockSpec(shape, lambda i: (0, 0))
    out = lambda width: pl.BlockSpec((tm, width), lambda i: (i, 0))
    widths = (512, 512, 512, 512, LANES, LANES, 512, LANES, W_GROUP, W_GROUP)
    dtypes = (BF16, BF16, BF16, BF16, F32, F32, BF16, F32, BF16, BF16)
    return pl.pallas_call(
        _in_proj_kernel,
        grid=(t // tm,),
        in_specs=[pl.BlockSpec((tm, D_MODEL), lambda i: (i, 0)),
                  full((1, D_MODEL)),
                  pl.BlockSpec((None, D_MODEL, N_IN_PAD), lambda i: (layer, 0, 0)),
                  full((1, 512)), full((1, LANES)),
                  full((512, 512)), full((LANES, 512)), full((LANES, 512)),
                  tab, tab, tab, tab, tab, tab],
        out_specs=[out(w) for w in widths],
        out_shape=[jax.ShapeDtypeStruct((t, w), d) for w, d in zip(widths, dtypes)],
        compiler_params=_cparams(("parallel",)),
        name="in_proj",
    )(x, g, w_all, qg, kvg, wuq, wk, wv, *qtabs, *ktabs)


def _fold_lanes(a, op):
    out = a[:, 0:LANES]
    for c in range(1, a.shape[1] // LANES):
        out = op(out, a[:, c * LANES:(c + 1) * LANES])
    return out


def _mla_attn_kernel(q_ref, k_ref, v_ref, o_ref, s_ref, mx_ref, acc_ref, *, tq, tk):
    i = pl.program_id(1)
    nfull = (i * tq) // tk
    t0 = pl.multiple_of(nfull * tk, tk)
    row = i * tq + lax.broadcasted_iota(jnp.int32, (tq, tk), 0)
    col = t0 + lax.broadcasted_iota(jnp.int32, (tq, tk), 1)
    heads = [slice(h * HEAD_D, (h + 1) * HEAD_D) for h in range(MLA_HEADS)]
    chunk = lambda j: pl.ds(pl.multiple_of(j * tk, tk), tk)

    for h, sl in enumerate(heads):
        s = jnp.where(col <= row, _dot_nt(q_ref[:, sl], k_ref[pl.ds(t0, tk), sl]), NEG_INF)
        s_ref[h, nfull] = s
        mx_ref[h] = _fold_lanes(s, jnp.maximum)

    def scores(j, carry):
        for h, sl in enumerate(heads):
            s = _dot_nt(q_ref[:, sl], k_ref[chunk(j), sl])
            s_ref[h, j] = s
            mx_ref[h] = jnp.maximum(mx_ref[h], _fold_lanes(s, jnp.maximum))
        return carry

    lax.fori_loop(0, nfull, scores, 0)
    for h in range(MLA_HEADS):
        m = jnp.max(mx_ref[h], axis=-1, keepdims=True)
        mx_ref[h] = jnp.broadcast_to(m, (tq, LANES))
        acc_ref[h] = jnp.zeros((tq, 2 * HEAD_D), F32)

    ones = jnp.ones((tk, LANES), BF16)

    def values(j, carry):
        for h, sl in enumerate(heads):
            m = mx_ref[h]
            s = s_ref[h, j]
            p = jnp.concatenate([jnp.exp((s[:, c * LANES:(c + 1) * LANES] - m).astype(BF16))
                                 for c in range(tk // LANES)], axis=1)
            v_aug = jnp.concatenate([v_ref[chunk(j), sl], ones], axis=1)
            acc_ref[h] = acc_ref[h] + _dot(p, v_aug)
        return carry

    lax.fori_loop(0, nfull + 1, values, 0)
    for h, sl in enumerate(heads):
        o_ref[:, sl] = (acc_ref[h, :, 0:HEAD_D] / acc_ref[h, :, HEAD_D:2 * HEAD_D]).astype(BF16)


def _mla_attn(q, k, v, batch):
    tq, tk = 256, 512
    nq = SEQ // tq
    return pl.pallas_call(
        functools.partial(_mla_attn_kernel, tq=tq, tk=tk),
        grid=(batch, nq),
        in_specs=[pl.BlockSpec((tq, W_GROUP), lambda b, i: (b * nq + i, 0)),
                  pl.BlockSpec((SEQ, W_GROUP), lambda b, i: (b, 0)),
                  pl.BlockSpec((SEQ, W_GROUP), lambda b, i: (b, 0))],
        out_specs=pl.BlockSpec((tq, W_GROUP), lambda b, i: (b * nq + i, 0)),
        out_shape=jax.ShapeDtypeStruct((batch * SEQ, W_GROUP), BF16),
        scratch_shapes=[pltpu.VMEM((MLA_HEADS, SEQ // tk, tq, tk), F32),
                        pltpu.VMEM((MLA_HEADS, tq, LANES), F32),
                        pltpu.VMEM((MLA_HEADS, tq, 2 * HEAD_D), F32)],
        compiler_params=_cparams(("parallel", "arbitrary")),
        name="mla_attn",
    )(q, k, v)


def _nsa_compress_kernel(xk_ref, xv_ref, w1_ref, w2_ref, pe_ref, k_out, v_out):
    for c, (x_ref, o_ref) in enumerate(((xk_ref, k_out), (xv_ref, v_out))):
        a = jnp.zeros((N_CMP_PAD, HEAD_D), F32)
        b = jnp.zeros((N_CMP_PAD, HEAD_D), F32)
        for r in range(CMP_STRIDE):
            x = x_ref[pl.ds(r, N_CMP_PAD, stride=CMP_STRIDE), :].astype(BF16)
            a = a + _dot(x, w1_ref[c, r * HEAD_D:(r + 1) * HEAD_D, :])
            b = b + _dot(x, w1_ref[c, (CMP_STRIDE + r) * HEAD_D:(CMP_STRIDE + r + 1) * HEAD_D, :])
        b = pltpu.roll(b, N_CMP_PAD - 1, 0)
        pe = _dot(pe_ref[c], w1_ref[c])[0:1, :]
        hid = jax.nn.gelu(a + b + pe)
        o_ref[0] = _dot(hid.astype(BF16), w2_ref[c]).astype(BF16)


def _nsa_compress(xk, xv, w1, w2, pe, batch):
    xspec = pl.BlockSpec((SEQ, HEAD_D), lambda b: (b, 0))
    ospec = pl.BlockSpec((1, N_CMP_PAD, HEAD_D), lambda b: (b, 0, 0))
    return pl.pallas_call(
        _nsa_compress_kernel,
        grid=(batch,),
        in_specs=[xspec, xspec,
                  pl.BlockSpec((2, CMP_BLOCK * HEAD_D, HEAD_D), lambda b: (0, 0, 0)),
                  pl.BlockSpec((2, HEAD_D, HEAD_D), lambda b: (0, 0, 0)),
                  pl.BlockSpec((2, 8, CMP_BLOCK * HEAD_D), lambda b: (0, 0, 0))],
        out_specs=[ospec, ospec],
        out_shape=[jax.ShapeDtypeStruct((batch, N_CMP_PAD, HEAD_D), BF16)] * 2,
        compiler_params=_cparams(("parallel",)),
        name="nsa_compress",
    )(xk, xv, w1, w2, pe)


def _softmax_rows(s):
    m = jnp.max(s, axis=-1, keepdims=True)
    p = jnp.exp(s - m)
    return p / jnp.sum(p, axis=-1, keepdims=True)


def _nsa_attn_kernel(q_ref, kc_ref, vc_ref, kv_ref, g_ref, ovt_ref, e_ref, o_ref,
                     m_ref, acc_ref, s_ref, *, tq, ck):
    i = pl.program_id(1)
    q0 = i * tq
    nh = 4
    qs = jnp.concatenate([q_ref[:, h * HEAD_D:(h + 1) * HEAD_D] for h in range(nh)], axis=0)
    qpos = q0 + lax.broadcasted_iota(jnp.int32, (tq, 1), 0)
    masked = lambda s, ok: (s.reshape(nh, tq, s.shape[1])
                            + jnp.where(ok, 0.0, NEG_INF)[None]).reshape(s.shape)

    n_idx = lax.broadcasted_iota(jnp.int32, (tq, N_CMP_PAD), 1)
    valid_c = n_idx * CMP_STRIDE + (CMP_BLOCK - 1) <= qpos
    pc = _softmax_rows(masked(_dot_nt(qs, kc_ref[0]), valid_c)).reshape(nh, tq, N_CMP_PAD)
    pc = jnp.where(valid_c[None], pc, 0.0)
    psum = pc[0] + pc[1] + pc[2] + pc[3]
    o_c = _dot(pc.reshape(nh * tq, N_CMP_PAD).astype(BF16), vc_ref[0])

    imp_t = _dot_nt(ovt_ref[...], psum, precision=lax.Precision.HIGHEST)
    kblk = lax.broadcasted_iota(jnp.int32, (N_BLK, tq), 0)
    cur = (q0 + lax.broadcasted_iota(jnp.int32, (N_BLK, tq), 1)) // SEL_BLOCK
    forced = (kblk == 0) | ((kblk <= cur) & (kblk > cur - N_LOCAL))
    score = jnp.where(forced, FORCE_SCORE, jnp.where(kblk <= cur, imp_t, -1.0))
    cnt = jnp.zeros((N_BLK, tq), F32)
    for j in range(N_BLK):
        sj = score[j:j + 1, :]
        beats = (sj > score) | ((sj == score) & (kblk > j))
        cnt = cnt + beats.astype(F32)
    sel = jnp.transpose((cnt < N_SEL).astype(F32)).astype(BF16)

    wlen = WINDOW + tq
    w0 = pl.multiple_of(jnp.maximum(q0 - WINDOW, 0), tq)
    sw = _dot_nt(qs, kv_ref[pl.ds(w0, wlen), 256:384])
    kpos = w0 + lax.broadcasted_iota(jnp.int32, (tq, wlen), 1)
    ok = (kpos <= qpos) & (kpos > qpos - WINDOW)
    sw = masked(sw, ok)
    pw = jnp.exp((sw - jnp.max(_fold_lanes(sw, jnp.maximum), axis=-1, keepdims=True)).astype(BF16))
    aug = lambda v: jnp.concatenate([v, jnp.ones(v.shape, BF16)], axis=1)
    ow = _dot(pw, aug(kv_ref[pl.ds(w0, wlen), 384:512]))
    o_w = ow[:, 0:HEAD_D] / ow[:, HEAD_D:2 * HEAD_D]

    nck = q0 // ck + 1
    m_ref[...] = jnp.full(m_ref.shape, NEG_INF, F32)

    def scores(c, carry):
        r0 = pl.multiple_of(c * ck, ck)
        chosen = _dot(sel, e_ref[c])
        kpos = r0 + lax.broadcasted_iota(jnp.int32, (tq, ck), 1)
        ok = (chosen > 0.5) & (kpos <= qpos)
        s = masked(_dot_nt(qs, kv_ref[pl.ds(r0, ck), 0:128]), ok)
        s_ref[c] = s
        m_ref[...] = jnp.maximum(m_ref[...], _fold_lanes(s, jnp.maximum))
        return carry

    lax.fori_loop(0, nck, scores, 0)
    m_ref[...] = jnp.broadcast_to(jnp.max(m_ref[...], axis=-1, keepdims=True), m_ref.shape)
    acc_ref[...] = jnp.zeros(acc_ref.shape, F32)

    def values(c, carry):
        r0 = pl.multiple_of(c * ck, ck)
        m = m_ref[...]
        s = s_ref[c]
        p = jnp.concatenate([jnp.exp((s[:, k * LANES:(k + 1) * LANES] - m).astype(BF16))
                             for k in range(ck // LANES)], axis=1)
        acc_ref[...] = acc_ref[...] + _dot(p, aug(kv_ref[pl.ds(r0, ck), 128:256]))
        return carry

    lax.fori_loop(0, nck, values, 0)
    o_s = acc_ref[:, 0:HEAD_D] / acc_ref[:, HEAD_D:2 * HEAD_D]

    g = g_ref[...]
    for h in range(nh):
        rs = slice(h * tq, (h + 1) * tq)
        o_ref[:, h * HEAD_D:(h + 1) * HEAD_D] = (
            g[:, 3 * h:3 * h + 1] * o_c[rs] + g[:, 3 * h + 1:3 * h + 2] * o_s[rs]
            + g[:, 3 * h + 2:3 * h + 3] * o_w[rs]).astype(BF16)


def _nsa_attn(q, kcmp, vcmp, kv, gates, ovt, emat, batch):
    tq, ck = 256, 512
    nq = SEQ // tq
    return pl.pallas_call(
        functools.partial(_nsa_attn_kernel, tq=tq, ck=ck),
        grid=(batch, nq),
        in_specs=[pl.BlockSpec((tq, 512), lambda b, i: (b * nq + i, 0)),
                  pl.BlockSpec((1, N_CMP_PAD, HEAD_D), lambda b, i: (b, 0, 0)),
                  pl.BlockSpec((1, N_CMP_PAD, HEAD_D), lambda b, i: (b, 0, 0)),
                  pl.BlockSpec((SEQ, 512), lambda b, i: (b, 0)),
                  pl.BlockSpec((tq, LANES), lambda b, i: (b * nq + i, 0)),
                  pl.BlockSpec((N_BLK, N_CMP_PAD), lambda b, i: (0, 0)),
                  pl.BlockSpec((SEQ // ck, N_BLK, ck), lambda b, i: (0, 0, 0))],
        out_specs=pl.BlockSpec((tq, 512), lambda b, i: (b * nq + i, 0)),
        out_shape=jax.ShapeDtypeStruct((batch * SEQ, 512), BF16),
        scratch_shapes=[pltpu.VMEM((4 * tq, LANES), F32),
                        pltpu.VMEM((4 * tq, 2 * HEAD_D), F32),
                        pltpu.VMEM((SEQ // ck, 4 * tq, ck), F32)],
        compiler_params=_cparams(("parallel", "arbitrary")),
        name="nsa_attn",
    )(q, kcmp, vcmp, kv, gates, ovt, emat)


def _pool_kernel(u_ref, w_ref, b_ref, s_ref, o_ref, pad_ref):
    maxw = POOL_SIZES[-1]
    pad_ref[0:maxw, :] = jnp.zeros((maxw, W_GROUP), F32)
    pad_ref[maxw:maxw + SEQ, :] = u_ref[...].astype(F32)
    rc = 512
    for g, w in enumerate(POOL_SIZES):
        sl = slice(g * LANES, (g + 1) * LANES)
        for r in range(SEQ // rc):
            acc = pad_ref[maxw + r * rc:maxw + (r + 1) * rc, sl]
            tok = acc
            for j in range(1, w):
                acc = acc + pad_ref[maxw - j + r * rc:maxw - j + (r + 1) * rc, sl]
            t = r * rc + lax.broadcasted_iota(jnp.int32, (rc, 1), 0)
            cnt = jnp.minimum(t + 1, w).astype(F32)
            d = acc / cnt - tok
            y = _dot(d.astype(BF16), w_ref[g])
            o_ref[r * rc:(r + 1) * rc, sl] = ((y + b_ref[:, sl]) * s_ref[:, sl]).astype(BF16)


def _pool(u, w, b, s, batch):
    return pl.pallas_call(
        _pool_kernel,
        grid=(batch,),
        in_specs=[pl.BlockSpec((SEQ, W_GROUP), lambda i: (i, 0)),
                  pl.BlockSpec((4, LANES, LANES), lambda i: (0, 0, 0)),
                  pl.BlockSpec((1, W_GROUP), lambda i: (0, 0)),
                  pl.BlockSpec((1, W_GROUP), lambda i: (0, 0))],
        out_specs=pl.BlockSpec((SEQ, W_GROUP), lambda i: (i, 0)),
        out_shape=jax.ShapeDtypeStruct((batch * SEQ, W_GROUP), BF16),
        scratch_shapes=[pltpu.VMEM((SEQ + POOL_SIZES[-1], W_GROUP), F32)],
        compiler_params=_cparams(("parallel",)),
        name="pool",
    )(u, w, b, s)


def _s5_kernel(u_ref, wb_ref, ar_ref, ai_ref, wc_ref, d_ref, gw_ref, gb_ref, o_ref,
               bu_ref, st_ref, tm_ref, *, batch, tc):
    @pl.when(pl.program_id(0) == 0)
    def _():
        st_ref[...] = jnp.zeros(st_ref.shape, F32)

    nslab = W_GROUP // LANES
    for b in range(batch):
        for j in range(nslab):
            tm_ref[j, pl.ds(b, tc, stride=batch), :] = u_ref[b, :, j * LANES:(j + 1) * LANES].astype(F32)
    u = jnp.concatenate([tm_ref[j] for j in range(nslab)], axis=1)
    gl = 8 * S5_STATE
    ub = u.astype(BF16)
    for j in range(nslab):
        uj = ub[:, j * LANES:(j + 1) * LANES]
        for k, part in enumerate((0, S5_WIDTH)):
            cs = slice(part + j * gl, part + (j + 1) * gl)
            bu_ref[:, cs] = _dot(uj, wb_ref[j, :, k * gl:(k + 1) * gl])
    lc = 1024
    unroll = 8
    for c in range(S5_WIDTH // lc):
        re = slice(c * lc, (c + 1) * lc)
        im = slice(S5_WIDTH + c * lc, S5_WIDTH + (c + 1) * lc)
        ar, ai = ar_ref[:, re], ai_ref[:, re]

        def body(tb, carry, re=re, im=im, ar=ar, ai=ai):
            xr, xi = carry
            for k in range(unroll):
                r0 = pl.multiple_of((tb * unroll + k) * batch, batch)
                nxr = ar * xr - ai * xi + bu_ref[pl.ds(r0, batch), re]
                nxi = ar * xi + ai * xr + bu_ref[pl.ds(r0, batch), im]
                bu_ref[pl.ds(r0, batch), re] = nxr
                bu_ref[pl.ds(r0, batch), im] = nxi
                xr, xi = nxr, nxi
            return xr, xi

        xr, xi = lax.fori_loop(0, tc // unroll, body, (st_ref[:, re], st_ref[:, im]))
        st_ref[:, re] = xr
        st_ref[:, im] = xi

    ys = []
    for j in range(nslab):
        yj = 0.0
        for k, part in enumerate((0, S5_WIDTH)):
            cs = slice(part + j * gl, part + (j + 1) * gl)
            yj = yj + _dot(bu_ref[:, cs].astype(BF16), wc_ref[j, k * gl:(k + 1) * gl, :])
        ys.append(yj)
    y = jnp.concatenate(ys, axis=1) + d_ref[...] * u
    y = jax.nn.gelu(y)
    z = _dot(y.astype(BF16), gw_ref[...]) + gb_ref[...]
    o = y * jax.nn.sigmoid(z)
    for j in range(nslab):
        tm_ref[j] = o[:, j * LANES:(j + 1) * LANES]
    for b in range(batch):
        for j in range(nslab):
            o_ref[b, :, j * LANES:(j + 1) * LANES] = tm_ref[j, pl.ds(b, tc, stride=batch), :].astype(BF16)


def _s5(u3, wb, ar, ai, wc, d, gw, gb):
    batch = u3.shape[0]
    tc = 64
    rows = tc * batch
    full = lambda shape: pl.BlockSpec(shape, lambda i: (0, 0))
    return pl.pallas_call(
        functools.partial(_s5_kernel, batch=batch, tc=tc),
        grid=(SEQ // tc,),
        in_specs=[pl.BlockSpec((batch, tc, W_GROUP), lambda i: (0, i, 0)),
                  pl.BlockSpec(wb.shape, lambda i: (0, 0, 0)), full((batch, S5_WIDTH)),
                  full((batch, S5_WIDTH)), pl.BlockSpec(wc.shape, lambda i: (0, 0, 0)),
                  full((1, W_GROUP)),
                  full((W_GROUP, W_GROUP)), full((1, W_GROUP))],
        out_specs=pl.BlockSpec((batch, tc, W_GROUP), lambda i: (0, i, 0)),
        out_shape=jax.ShapeDtypeStruct((batch, SEQ, W_GROUP), BF16),
        scratch_shapes=[pltpu.VMEM((rows, 2 * S5_WIDTH), F32),
                        pltpu.VMEM((batch, 2 * S5_WIDTH), F32),
                        pltpu.VMEM((W_GROUP // LANES, rows, LANES), F32)],
        compiler_params=_cparams(("arbitrary",)),
        name="s5",
    )(u3, wb, ar, ai, wc, d, gw, gb)


def _out_proj_kernel(ya_ref, yb_ref, yc_ref, yd_ref, mg_ref, wo_ref, x_ref, fg_ref,
                     wrh_ref, wrl_ref, br_ref, x1_ref, h2_ref, lg_ref, *, tm):
    acc = x_ref[...]
    for gi, y_ref in enumerate((ya_ref, yb_ref, yc_ref, yd_ref)):
        sl = slice(gi * W_GROUP, (gi + 1) * W_GROUP)
        n = (_rms(y_ref[...].astype(F32)) * mg_ref[:, sl]).astype(BF16)
        acc = acc + _dot(n, wo_ref[sl, :])
    x1_ref[...] = acc
    h2 = _rms(acc) * fg_ref[...]
    hi = h2.astype(BF16)
    lo = (h2 - hi.astype(F32)).astype(BF16)
    lg_ref[...] = (_dot(hi, wrh_ref[...]) + _dot(hi, wrl_ref[...]) + _dot(lo, wrh_ref[...])
                   + br_ref[...])
    _pack_rows(h2_ref, h2, tm)


def _out_proj(ya, yb, yc, yd, mg, wo_all, layer, x, fg, wr_hi, wr_lo, br):
    t = x.shape[0]
    tm = 256
    yspec = pl.BlockSpec((tm, W_GROUP), lambda i: (i, 0))
    full = lambda shape: pl.BlockSpec(shape, lambda i: (0, 0))
    return pl.pallas_call(
        functools.partial(_out_proj_kernel, tm=tm),
        grid=(t // tm,),
        in_specs=[yspec, yspec, yspec, yspec, full((1, D_MODEL)),
                  pl.BlockSpec((None, D_MODEL, D_MODEL), lambda i: (layer, 0, 0)),
                  pl.BlockSpec((tm, D_MODEL), lambda i: (i, 0)), full((1, D_MODEL)),
                  full((D_MODEL, LANES)), full((D_MODEL, LANES)), full((1, LANES))],
        out_specs=[pl.BlockSpec((tm, D_MODEL), lambda i: (i, 0)),
                   pl.BlockSpec((tm * ROW_PITCH, LANES), lambda i: (i, 0)),
                   pl.BlockSpec((tm, LANES), lambda i: (i, 0))],
        out_shape=[jax.ShapeDtypeStruct((t, D_MODEL), F32),
                   jax.ShapeDtypeStruct((t * ROW_PITCH, LANES), F32),
                   jax.ShapeDtypeStruct((t, LANES), F32)],
        compiler_params=_cparams(("parallel",)),
        name="out_proj",
    )(ya, yb, yc, yd, mg, wo_all, x, fg, wr_hi, wr_lo, br)


ROUTE_ROWS = 40


def _route(lg, meta_ref, metat_ref, cnt_ref, col_ref, row_ref, tm):
    @pl.when(pl.program_id(0) == 0)
    def _():
        col_ref[...] = jnp.zeros(col_ref.shape, F32)
        row_ref[...] = jnp.zeros(row_ref.shape, F32)

    nr = ROUTE_ROWS
    lt = jnp.transpose(lg)[0:nr, :]
    row = lax.broadcasted_iota(jnp.int32, (nr, tm), 0)
    big = jnp.int32(1 << 20)
    cmax = lambda a: jnp.max(a, axis=0, keepdims=True)
    cmin = lambda a: jnp.min(a, axis=0, keepdims=True)
    csum = lambda a: jnp.sum(a, axis=0, keepdims=True)

    is_g = row < MOE_GROUPS
    gl = jnp.where(is_g, lt, NEG_INF)
    gm = cmax(gl)
    p_top = 1.0 / csum(jnp.where(is_g, jnp.exp(gl - gm), 0.0))
    g_top = cmin(jnp.where(is_g & (gl == gm), row, big))

    is_e = (row >= MOE_GROUPS) & (row < MOE_GROUPS + MOE_EXPERTS) \
        & (((row - MOE_GROUPS) // MOE_EPG) == g_top)
    el = jnp.where(is_e, lt, NEG_INF)
    ee = jnp.where(is_e, jnp.exp(el - cmax(el)), 0.0)
    p = jnp.where(is_e, ee / csum(ee), -1.0)
    p1 = cmax(p)
    i1 = cmin(jnp.where(p == p1, row, big))
    p_rest = jnp.where(row == i1, -1.0, p)
    p2 = cmax(p_rest)
    i2 = cmin(jnp.where((p_rest == p2) & is_e & (row != i1), row, big))
    den = p1 + p2
    w1 = p_top * (p1 / den)
    w2 = p_top * (p2 / den)

    hit1, hit2 = row == i1, row == i2
    oh = (hit1 | hit2).astype(BF16)
    r = lax.broadcasted_iota(jnp.int32, (tm, tm), 0)
    c = lax.broadcasted_iota(jnp.int32, (tm, tm), 1)
    before = _dot(oh, (r < c).astype(BF16)) + col_ref[:, 0:1]
    r1 = csum(jnp.where(hit1, before, 0.0))
    r2 = csum(jnp.where(hit2, before, 0.0))
    col_ref[...] = col_ref[...] + jnp.sum(oh.astype(F32), axis=1, keepdims=True)
    oh_all = jnp.concatenate([oh, jnp.zeros((LANES - nr, tm), BF16)], axis=0)
    row_ref[...] = row_ref[...] + _dot_nt(jnp.ones((8, tm), BF16), oh_all)
    cnt_ref[...] = row_ref[...]

    e1 = (i1 - MOE_GROUPS).astype(F32)
    e2 = (i2 - MOE_GROUPS).astype(F32)
    row8 = lax.broadcasted_iota(jnp.int32, (8, tm), 0)
    meta = jnp.zeros((8, tm), F32)
    for k, v in enumerate((e1, e2, w1, w2, r1, r2)):
        meta = jnp.where(row8 == k, v, meta)
    metat_ref[...] = meta
    meta = jnp.concatenate([meta, jnp.zeros((LANES - 8, tm), F32)], axis=0)
    meta_ref[...] = jnp.transpose(meta)


def _router_kernel(lg_ref, meta_ref, metat_ref, cnt_ref, col_ref, row_ref, *, tm):
    _route(lg_ref[...], meta_ref, metat_ref, cnt_ref, col_ref, row_ref, tm)


def _router(logits):
    t = logits.shape[0]
    tm = 512
    return pl.pallas_call(
        functools.partial(_router_kernel, tm=tm),
        grid=(t // tm,),
        in_specs=[pl.BlockSpec((tm, LANES), lambda i: (i, 0))],
        out_specs=[pl.BlockSpec((tm, LANES), lambda i: (i, 0)),
                   pl.BlockSpec((8, tm), lambda i: (0, i)),
                   pl.BlockSpec((8, LANES), lambda i: (i, 0))],
        out_shape=[jax.ShapeDtypeStruct((t, LANES), F32),
                   jax.ShapeDtypeStruct((8, t), F32),
                   jax.ShapeDtypeStruct((t // tm * 8, LANES), F32)],
        scratch_shapes=[pltpu.VMEM((ROUTE_ROWS, LANES), F32), pltpu.VMEM((8, LANES), F32)],
        compiler_params=_cparams(("arbitrary",)),
        name="router",
    )(logits)


def _dest_kernel(metat_ref, cnt_ref, o_ref, *, tm):
    cnt = cnt_ref[...]
    start = (_lane_cumsum(cnt) - cnt)[0:1, :]
    nr = ROUTE_ROWS
    start_col = jnp.transpose(jnp.broadcast_to(start, (LANES, LANES)))[0:nr, 0:1]
    mt = metat_ref[...]
    row = lax.broadcasted_iota(jnp.int32, (nr, tm), 0)
    row8 = lax.broadcasted_iota(jnp.int32, (8, tm), 0)
    out = jnp.zeros((8, tm), F32)
    for k in range(2):
        e_row = mt[k:k + 1, :].astype(jnp.int32) + MOE_GROUPS
        d = jnp.sum(jnp.where(row == e_row, start_col, 0.0), axis=0, keepdims=True) + mt[4 + k:5 + k, :]
        out = jnp.where(row8 == k, d, out)
    o_ref[...] = out.astype(jnp.int32)


def _dest(meta_t, counts):
    t = meta_t.shape[1]
    tm = 512
    last = counts.shape[0] // 8 - 1
    return pl.pallas_call(
        functools.partial(_dest_kernel, tm=tm),
        grid=(t // tm,),
        in_specs=[pl.BlockSpec((8, tm), lambda i: (0, i)),
                  pl.BlockSpec((8, LANES), lambda i: (last, 0))],
        out_specs=pl.BlockSpec((8, tm), lambda i: (0, i)),
        out_shape=jax.ShapeDtypeStruct((8, t), jnp.int32),
        compiler_params=_cparams(("parallel",)),
        name="dest",
    )(meta_t, counts)


def _dispatch_kernel(d1_ref, d2_ref, h_ref, xs_ref, sem, *, td):
    base = pl.program_id(0) * td

    def start(rb, carry):
        for k in range(DMA_UNROLL):
            r = rb * DMA_UNROLL + k
            src = h_ref.at[pl.ds(r * ROW_PITCH, ROW_PITCH)]
            for d_ref in (d1_ref, d2_ref):
                dst = xs_ref.at[pl.ds(d_ref[base + r] * ROW_PITCH, ROW_PITCH)]
                pltpu.make_async_copy(src, dst, sem).start()
        return carry

    lax.fori_loop(0, td // DMA_UNROLL, start, 0)
    for _ in range(2):
        pltpu.make_async_copy(h_ref, xs_ref.at[pl.ds(0, td * ROW_PITCH)], sem).wait()


def _dispatch(dest1, dest2, h2):
    t = dest1.shape[0]
    td = 1024
    return pl.pallas_call(
        functools.partial(_dispatch_kernel, td=td),
        grid_spec=pltpu.PrefetchScalarGridSpec(
            num_scalar_prefetch=2,
            grid=(t // td,),
            in_specs=[pl.BlockSpec((td * ROW_PITCH, LANES), lambda i, *_: (i, 0))],
            out_specs=pl.BlockSpec(memory_space=pl.ANY),
            scratch_shapes=[pltpu.SemaphoreType.DMA(())]),
        out_shape=jax.ShapeDtypeStruct((2 * t * ROW_PITCH, LANES), F32),
        compiler_params=pltpu.CompilerParams(dimension_semantics=("arbitrary",),
                                             has_side_effects=True),
        name="dispatch",
    )(dest1, dest2, h2)


def _expert_kernel(vt_ref, ve_ref, vlo_ref, vhi_ref, vfirst_ref, vvalid_ref, vnew_ref, vnext_ref,
                   vslot_ref, xs_ref, wg_ref, wu_ref, wd_ref, ys_ref,
                   wgf_ref, wuf_ref, wdf_ref, wgb_ref, wub_ref, wdb_ref, sem, *, tmx, layer):
    v = pl.program_id(0)

    def weight_copies(e, slot):
        return [pltpu.make_async_copy(w_ref.at[layer, e], f_ref.at[slot], sem.at[slot])
                for w_ref, f_ref in ((wg_ref, wgf_ref), (wu_ref, wuf_ref), (wd_ref, wdf_ref))]

    @pl.when(vnew_ref[v] == 1)
    def _():
        slot = vslot_ref[v]

        @pl.when(v == 0)
        def _():
            for cp in weight_copies(ve_ref[v], slot):
                cp.start()

        for cp in weight_copies(ve_ref[v], slot):
            cp.wait()

        @pl.when(vnext_ref[v] >= 0)
        def _():
            for cp in weight_copies(vnext_ref[v], 1 - slot):
                cp.start()

        wgb_ref[...] = wgf_ref[slot].astype(BF16)
        wub_ref[...] = wuf_ref[slot].astype(BF16)
        wdb_ref[...] = wdf_ref[slot].astype(BF16)

    @pl.when(vvalid_ref[v] == 1)
    def _():
        x = jnp.concatenate(_unpack_rows(lambda rows: xs_ref[rows, :], tmx), axis=1).astype(BF16)
        a = _dot(x, wgb_ref[...])
        u = _dot(x, wub_ref[...])
        hid = (jax.nn.silu(a) * u).astype(BF16)
        y = _dot(hid, wdb_ref[...])
        rows = lax.broadcasted_iota(jnp.int32, (tmx, 1), 0)
        mine = (rows >= vlo_ref[v]) & (rows < vhi_ref[v])

        @pl.when(vfirst_ref[v] == 1)
        def _():
            _pack_rows(ys_ref, y, tmx, mask=mine)

        @pl.when(vfirst_ref[v] == 0)
        def _():
            _pack_rows(ys_ref, y, tmx, mask=mine, old=True)


def _experts(sched, xs, wg, wu, wd, layer, tmx):
    nvis = sched[0].shape[0]
    rows = xs.shape[0]
    xspec = pl.BlockSpec((tmx * ROW_PITCH, LANES), lambda v, vt, *_: (vt[v], 0))
    hbm = pl.BlockSpec(memory_space=pl.ANY)
    return pl.pallas_call(
        functools.partial(_expert_kernel, tmx=tmx, layer=layer),
        grid_spec=pltpu.PrefetchScalarGridSpec(
            num_scalar_prefetch=9,
            grid=(nvis,),
            in_specs=[xspec, hbm, hbm, hbm],
            out_specs=xspec,
            scratch_shapes=[pltpu.VMEM((2, D_MODEL, MOE_HIDDEN), F32),
                            pltpu.VMEM((2, D_MODEL, MOE_HIDDEN), F32),
                            pltpu.VMEM((2, MOE_HIDDEN, D_MODEL), F32),
                            pltpu.VMEM((D_MODEL, MOE_HIDDEN), BF16),
                            pltpu.VMEM((D_MODEL, MOE_HIDDEN), BF16),
                            pltpu.VMEM((MOE_HIDDEN, D_MODEL), BF16),
                            pltpu.SemaphoreType.DMA((2,))]),
        out_shape=jax.ShapeDtypeStruct((rows, LANES), F32),
        compiler_params=_cparams(("arbitrary",)),
        name="experts",
    )(*sched, xs, wg, wu, wd)


def _combine_kernel(d1_ref, d2_ref, ys_ref, meta_ref, x1_ref, fg_ref, o_ref,
                    b1_ref, b2_ref, sem, *, tc, final):
    i = pl.program_id(0)
    n = pl.num_programs(0)
    slot = i % 2

    def gather(tile, slot):
        def start(rb, carry):
            for k in range(DMA_UNROLL):
                r = rb * DMA_UNROLL + k
                dst = pl.ds(r * ROW_PITCH, ROW_CHUNKS)
                for d_ref, b_ref in ((d1_ref, b1_ref), (d2_ref, b2_ref)):
                    src = ys_ref.at[pl.ds(d_ref[tile * tc + r] * ROW_PITCH, ROW_CHUNKS)]
                    pltpu.make_async_copy(src, b_ref.at[slot, dst], sem.at[slot]).start()
            return carry

        lax.fori_loop(0, tc // DMA_UNROLL, start, 0)

    @pl.when(i == 0)
    def _():
        gather(0, 0)

    @pl.when(i + 1 < n)
    def _():
        gather(i + 1, 1 - slot)

    for b_ref in (b1_ref, b2_ref):
        pltpu.make_async_copy(ys_ref.at[pl.ds(0, tc * ROW_CHUNKS)],
                              b_ref.at[slot, pl.ds(0, tc * ROW_CHUNKS)], sem.at[slot]).wait()
    w1 = meta_ref[:, 2:3]
    w2 = meta_ref[:, 3:4]
    y1 = _unpack_rows(lambda rows: b1_ref[slot, rows, :], tc)
    y2 = _unpack_rows(lambda rows: b2_ref[slot, rows, :], tc)
    x2 = jnp.concatenate([x1_ref[:, c * LANES:(c + 1) * LANES] + (w1 * y1[c] + w2 * y2[c])
                          for c in range(ROW_CHUNKS)], axis=1)
    if final:
        x2 = _rms(x2) * fg_ref[...]
    o_ref[...] = x2


def _combine(dest1, dest2, ys, meta, x1, fg, final):
    t = x1.shape[0]
    tc = 512
    return pl.pallas_call(
        functools.partial(_combine_kernel, tc=tc, final=final),
        grid_spec=pltpu.PrefetchScalarGridSpec(
            num_scalar_prefetch=2,
            grid=(t // tc,),
            in_specs=[pl.BlockSpec(memory_space=pl.ANY),
                      pl.BlockSpec((tc, LANES), lambda i, *_: (i, 0)),
                      pl.BlockSpec((tc, D_MODEL), lambda i, *_: (i, 0)),
                      pl.BlockSpec((1, D_MODEL), lambda i, *_: (0, 0))],
            out_specs=pl.BlockSpec((tc, D_MODEL), lambda i, *_: (i, 0)),
            scratch_shapes=[pltpu.VMEM((2, tc * ROW_PITCH, LANES), F32),
                            pltpu.VMEM((2, tc * ROW_PITCH, LANES), F32),
                            pltpu.SemaphoreType.DMA((2,))]),
        out_shape=jax.ShapeDtypeStruct((t, D_MODEL), F32),
        compiler_params=_cparams(("arbitrary",)),
        name="combine",
    )(dest1, dest2, ys, meta, x1, fg)


def _rope_tables(r0):
    inv = ROPE_THETA ** (-jnp.arange(0, 2 * ROPE_HALF, 2, dtype=F32) / (2 * ROPE_HALF))
    ang = jnp.arange(SEQ, dtype=F32)[:, None] * inv[None, :]
    cos, sin = jnp.cos(ang), jnp.sin(ang)
    c = jnp.ones((SEQ, LANES), F32).at[:, r0:r0 + ROPE_HALF].set(cos)
    c = c.at[:, r0 + ROPE_HALF:r0 + 2 * ROPE_HALF].set(cos)
    sa = jnp.zeros((SEQ, LANES), F32).at[:, r0:r0 + ROPE_HALF].set(-sin)
    sb = jnp.zeros((SEQ, LANES), F32).at[:, r0 + ROPE_HALF:r0 + 2 * ROPE_HALF].set(sin)
    return c, sa, sb


def _pad_cols(w, width):
    return jnp.pad(w, ((0, 0), (0, width - w.shape[1])))


def _s5_params(a_re, a_im, log_dt, b_re, b_im, c_re, c_im, batch):
    dt = jnp.exp(log_dt)[:, None]
    mag = jnp.exp(a_re * dt)
    abar_r, abar_i = mag * jnp.cos(a_im * dt), mag * jnp.sin(a_im * dt)
    den = a_re * a_re + a_im * a_im
    nr, ni = abar_r - 1.0, abar_i
    coef_r = (nr * a_re + ni * a_im) / den
    coef_i = (ni * a_re - nr * a_im) / den
    bbar_r = coef_r[..., None] * b_re - coef_i[..., None] * b_im
    bbar_i = coef_r[..., None] * b_im + coef_i[..., None] * b_re
    nslab, gps = W_GROUP // LANES, LANES // S5_CH
    eye = jnp.eye(gps, dtype=F32)
    slab = lambda m: m.reshape((nslab, gps) + m.shape[1:])
    blk_b = lambda m: jnp.einsum('jgpc,gh->jgchp', slab(m), eye).reshape(nslab, LANES, gps * S5_STATE)
    wb = jnp.concatenate([blk_b(bbar_r), blk_b(bbar_i)], axis=2).astype(BF16)
    blk_c = lambda m: jnp.einsum('jgcp,gh->jgphc', slab(m), eye).reshape(nslab, gps * S5_STATE, LANES)
    wc = jnp.concatenate([blk_c(c_re), blk_c(-c_im)], axis=1).astype(BF16)
    ar = jnp.broadcast_to(abar_r.reshape(1, S5_WIDTH), (batch, S5_WIDTH))
    ai = jnp.broadcast_to(abar_i.reshape(1, S5_WIDTH), (batch, S5_WIDTH))
    return wb, ar, ai, wc


def _nsa_consts(ck):
    c_start = np.arange(N_CMP_PAD) * CMP_STRIDE
    b_start = np.arange(N_BLK) * SEL_BLOCK
    ov = ((c_start[None, :] < b_start[:, None] + SEL_BLOCK)
          & (c_start[None, :] + CMP_BLOCK > b_start[:, None])).astype(np.float32)
    key_blk = np.arange(SEQ) // SEL_BLOCK
    e = (key_blk[None, :] == np.arange(N_BLK)[:, None]).astype(np.float32)
    e = e.reshape(N_BLK, SEQ // ck, ck).transpose(1, 0, 2)
    return jnp.asarray(ov), jnp.asarray(e, dtype=BF16)


def _lane_cumsum(v):
    r = lax.broadcasted_iota(jnp.int32, (LANES, LANES), 0)
    c = lax.broadcasted_iota(jnp.int32, (LANES, LANES), 1)
    incl = (r <= c).astype(BF16)
    hi = jnp.floor(v * (1.0 / 256.0))
    lo = v - 256.0 * hi
    return 256.0 * _dot(hi.astype(BF16), incl) + _dot(lo.astype(BF16), incl)


def _sched_kernel(cnt_ref, o_ref, *, tmx, nv):
    cnt = cnt_ref[...]
    ends = _lane_cumsum(cnt)
    offs = ends - cnt
    first = jnp.floor(offs * (1.0 / tmx))
    last = jnp.floor(jnp.maximum(ends - 1.0, 0.0) * (1.0 / tmx))
    nvis = jnp.where(cnt > 0.0, last - first + 1.0, 0.0)
    cumv = _lane_cumsum(nvis)
    row1 = lambda a: a[0:1, :]
    total = jnp.max(row1(cumv), axis=-1, keepdims=True)
    rsum = lambda a: jnp.sum(a, axis=-1, keepdims=True)
    v = lax.broadcasted_iota(jnp.int32, (nv, LANES), 0).astype(F32)
    lane = lax.broadcasted_iota(jnp.int32, (nv, LANES), 1).astype(F32)
    vc = jnp.minimum(v, total - 1.0)
    e_lane = rsum((row1(cumv) <= vc).astype(F32))
    hit = lane == e_lane
    pick = lambda a: rsum(jnp.where(hit, row1(a), 0.0))
    vt = pick(first) + vc[:, 0:1] - (pick(cumv) - pick(nvis))
    vlo = jnp.clip(pick(offs) - vt * tmx, 0.0, float(tmx))
    vhi = jnp.clip(pick(ends) - vt * tmx, 0.0, float(tmx))
    changed = lambda a: (v == 0.0) | (a != pltpu.roll(a, 1, 0))
    vfirst = changed(jnp.broadcast_to(vt, (nv, LANES))).astype(F32)
    vnew = changed(jnp.broadcast_to(e_lane, (nv, LANES))).astype(F32)
    vvalid = (v < total).astype(F32)
    nonempty = (cnt > 0.0).astype(F32)
    order = pick(_lane_cumsum(nonempty) - nonempty)
    vslot = order - 2.0 * jnp.floor(order * 0.5)
    far = float(1 << 20)
    nxt = jnp.min(jnp.where((row1(nonempty) > 0.0) & (lane > e_lane), lane, far), axis=-1, keepdims=True)
    vnext = jnp.where(nxt >= far, -1.0, nxt - MOE_GROUPS)
    out = jnp.zeros((nv, LANES), F32)
    cols = (vt, e_lane - MOE_GROUPS, vlo, vhi, vfirst, vvalid, vnew, vnext, vslot)
    for k, col in enumerate(cols):
        out = jnp.where(lane == k, col, out)
    o_ref[...] = out.astype(jnp.int32)


def _moe_schedule(counts, n_rows, tmx):
    nvis_max = n_rows // tmx + MOE_EXPERTS
    nv = 256
    last = counts.shape[0] // 8 - 1
    sched = pl.pallas_call(
        functools.partial(_sched_kernel, tmx=tmx, nv=nv),
        grid=(1,),
        in_specs=[pl.BlockSpec((8, LANES), lambda i: (last, 0))],
        out_specs=pl.BlockSpec((nv, LANES), lambda i: (0, 0)),
        out_shape=jax.ShapeDtypeStruct((nv, LANES), jnp.int32),
        name="sched",
    )(counts)
    return tuple(sched[:nvis_max, k] for k in range(9))


def kernel(x, attn_norm_g, w_in, mla_q_norm_g, mla_kv_norm_g, mla_w_uq, mla_w_ukv, nsa_cmp_pe, nsa_cmp_w1, nsa_cmp_w2, pool_w, pool_b, pool_scale, s5_a_re, s5_a_im, s5_log_dt, s5_b_re, s5_b_im, s5_c_re, s5_c_im, s5_d, s5_glu_w, s5_glu_b, mix_norm_g, w_out, ffn_norm_g, moe_w_group, moe_b_group, moe_w_expert, moe_b_expert, moe_w_gate, moe_w_up, moe_w_down, final_norm_g):
    batch, seq, d = x.shape
    depth = w_in.shape[0]
    t = batch * seq
    xf = x.reshape(t, d)
    qtabs = _rope_tables(MLA_NOPE)
    ktabs = _rope_tables(0)
    ovt, emat = _nsa_consts(512)
    row = lambda v: v.reshape(1, -1)
    tmx = 256
    w_in_all = _w_in_prep(w_in)
    w_out_all = w_out.astype(BF16)

    for l in range(depth):
        wuq = jnp.pad(mla_w_uq[l], ((0, 512 - MLA_Q_LORA), (0, 0))).astype(BF16)
        ukv = mla_w_ukv[l].reshape(HEAD_D, MLA_HEADS, MLA_NOPE + MLA_V)
        wk = jnp.pad(ukv[:, :, :MLA_NOPE], ((0, 0), (0, 0), (0, HEAD_D - MLA_NOPE)))
        wk = wk.reshape(HEAD_D, 512).astype(BF16)
        wv = ukv[:, :, MLA_NOPE:].reshape(HEAD_D, 512).astype(BF16)
        qg = jnp.pad(mla_q_norm_g[l], (0, 512 - MLA_Q_LORA)).reshape(1, 512)
        q_a, k_a, v_a, q_b, kc, vc, kv_b, gates, u_pool, u_s5 = _in_proj(
            xf, row(attn_norm_g[l]), w_in_all, l, qg, row(mla_kv_norm_g[l]), wuq, wk, wv, qtabs, ktabs)

        y_a = _mla_attn(q_a, k_a, v_a, batch)

        pe = jnp.broadcast_to(nsa_cmp_pe[l].reshape(2, 1, CMP_BLOCK * HEAD_D),
                              (2, 8, CMP_BLOCK * HEAD_D)).astype(BF16)
        kcmp, vcmp = _nsa_compress(kc, vc, nsa_cmp_w1[l].astype(BF16), nsa_cmp_w2[l].astype(BF16),
                                   pe, batch)
        y_b = _nsa_attn(q_b, kcmp, vcmp, kv_b, gates, ovt, emat, batch)

        y_c = _pool(u_pool, pool_w[l].astype(BF16), row(pool_b[l]), row(pool_scale[l]), batch)

        wb, ar, ai, wc = _s5_params(s5_a_re[l], s5_a_im[l], s5_log_dt[l], s5_b_re[l], s5_b_im[l],
                                    s5_c_re[l], s5_c_im[l], batch)
        y_d = _s5(u_s5.reshape(batch, seq, W_GROUP), wb, ar, ai, wc, row(s5_d[l]),
                  s5_glu_w[l].astype(BF16), row(s5_glu_b[l])).reshape(t, W_GROUP)

        wr = jnp.concatenate([moe_w_group[l], moe_w_expert[l]], axis=1)
        wr = _pad_cols(wr, LANES)
        wr_hi = wr.astype(BF16)
        wr_lo = (wr - wr_hi.astype(F32)).astype(BF16)
        br = jnp.pad(jnp.concatenate([moe_b_group[l], moe_b_expert[l]]), (0, LANES - 36)).reshape(1, LANES)
        x1, h2, logits = _out_proj(y_a, y_b, y_c, y_d, row(mix_norm_g[l]), w_out_all, l,
                                   xf, row(ffn_norm_g[l]), wr_hi, wr_lo, br)

        meta, meta_t, counts = _router(logits)
        dest = _dest(meta_t, counts)
        dest1, dest2 = dest[0], dest[1]
        sched = _moe_schedule(counts, 2 * t, tmx)
        xs = _dispatch(dest1, dest2, h2)
        ys = _experts(sched, xs, moe_w_gate, moe_w_up, moe_w_down, l, tmx)
        xf = _combine(dest1, dest2, ys, meta, x1, row(final_norm_g), final=(l == depth - 1))

    return xf.reshape(batch, seq, d)
```

```python
import functools
import math

import numpy as np
import jax
import jax.numpy as jnp
from jax import lax
from jax.experimental import pallas as pl
from jax.experimental.pallas import tpu as pltpu

F32 = jnp.float32
BF16 = jnp.bfloat16

D_MODEL = 2048
SEQ = 2048
W_GROUP = 512
LANES = 128
ROW_CHUNKS = D_MODEL // LANES
ROW_PITCH = ROW_CHUNKS + 1
DMA_UNROLL = 4

ROPE_THETA = 500000.0
ROPE_HALF = 16
NEG_INF = -1.0e30
FORCE_SCORE = 1.0e4
EPS = 1e-6

MLA_HEADS = 4
MLA_Q_LORA = 448
MLA_NOPE = 96
MLA_V = 128
HEAD_D = 128

CMP_BLOCK = 32
CMP_STRIDE = 16
SEL_BLOCK = 64
N_SEL = 8
N_LOCAL = 2
WINDOW = 512
N_CMP_PAD = SEQ // CMP_STRIDE
N_BLK = SEQ // SEL_BLOCK

POOL_SIZES = (2, 4, 8, 16)
S5_GROUPS = 32
S5_CH = 16
S5_STATE = 64
S5_WIDTH = S5_GROUPS * S5_STATE

MOE_GROUPS = 4
MOE_EPG = 8
MOE_EXPERTS = 32
MOE_HIDDEN = 512

COL_CQ, COL_NQ, COL_POOL, COL_S5 = 0, 512, 1024, 1536
COL_CKV, COL_KR, COL_KV6, COL_GL = 2048, 2176, 2304, 3072
N_IN_PAD = 3200

VMEM_LIMIT = 56 * 1024 * 1024


def _cparams(sem, vmem=VMEM_LIMIT):
    return pltpu.CompilerParams(dimension_semantics=sem, vmem_limit_bytes=vmem)


def _rms(x, n=None):
    n = x.shape[-1] if n is None else n
    return x * lax.rsqrt(jnp.sum(x * x, axis=-1, keepdims=True) / n + EPS)


def _dot(a, b):
    return jnp.dot(a, b, preferred_element_type=F32)


def _dot_nt(a, b, precision=None):
    return lax.dot_general(a, b, (((1,), (1,)), ((), ())), preferred_element_type=F32,
                           precision=precision)


def _rope(x, c, sa, sb):
    return x * c + pltpu.roll(x, LANES - ROPE_HALF, 1) * sa + pltpu.roll(x, ROPE_HALF, 1) * sb


def _pack_rows(ref, x, n, mask=None, old=False):
    for s in range(ROW_CHUNKS):
        rows = pl.ds(s, n, stride=ROW_PITCH)
        w = x[:, s * LANES:(s + 1) * LANES]
        if mask is not None:
            w = jnp.where(mask, w, ref[rows, :] if old else 0.0)
        ref[rows, :] = w
    if not old:
        ref[pl.ds(ROW_CHUNKS, n, stride=ROW_PITCH), :] = jnp.zeros((n, LANES), F32)


def _unpack_rows(load, n):
    return [load(pl.ds(s, n, stride=ROW_PITCH)) for s in range(ROW_CHUNKS)]


_W_IN_SEGMENTS = ((COL_CQ, 0, 448), (COL_NQ, 608, 512), (COL_POOL, 1900, 512), (COL_S5, 2412, 512),
                  (COL_CKV, 448, 128), (COL_KR, 576, 32), (COL_KV6, 1120, 768), (COL_GL, 1888, 12))


def _w_in_prep_kernel(w_ref, o_ref):
    o_ref[0] = jnp.zeros(o_ref.shape[1:], BF16)
    for dst, src, width in _W_IN_SEGMENTS:
        o_ref[0, :, dst:dst + width] = w_ref[0, :, src:src + width].astype(BF16)


def _w_in_prep(w_in):
    depth, d, n = w_in.shape
    tk = 256
    return pl.pallas_call(
        _w_in_prep_kernel,
        grid=(depth, d // tk),
        in_specs=[pl.BlockSpec((1, tk, n), lambda l, k: (l, k, 0))],
        out_specs=pl.BlockSpec((1, tk, N_IN_PAD), lambda l, k: (l, k, 0)),
        out_shape=jax.ShapeDtypeStruct((depth, d, N_IN_PAD), BF16),
        compiler_params=_cparams(("parallel", "parallel")),
        name="w_in_prep",
    )(w_in)


def _in_proj_kernel(x_ref, g_ref, w_ref, qg_ref, kvg_ref, wuq_ref, wk_ref, wv_ref,
                    qc_ref, qsa_ref, qsb_ref, kc_ref, ksa_ref, ksb_ref,
                    qa_out, ka_out, va_out, qb_out, kcmp_out, vcmp_out, kvb_out, gate_out,
                    pool_out, s5_out):
    scale = 1.0 / math.sqrt(HEAD_D)
    h = (_rms(x_ref[...]) * g_ref[...]).astype(BF16)
    seg = lambda col, width: _dot(h, w_ref[:, col:col + width])
    heads = [slice(n * HEAD_D, (n + 1) * HEAD_D) for n in range(4)]

    qn = (_rms(seg(COL_CQ, 512), MLA_Q_LORA) * qg_ref[...]).astype(BF16)
    q = _dot(qn, wuq_ref[...])
    kvn = (_rms(seg(COL_CKV, LANES)) * kvg_ref[...]).astype(BF16)
    kn = _dot(kvn, wk_ref[...])
    va_out[...] = _dot(kvn, wv_ref[...]).astype(BF16)
    kc, ksa, ksb = kc_ref[...], ksa_ref[...], ksb_ref[...]
    kr = pltpu.roll(_rope(seg(COL_KR, LANES), kc, ksa, ksb), MLA_NOPE, 1)
    for sl in heads:
        qa_out[:, sl] = (_rope(q[:, sl], qc_ref[...], qsa_ref[...], qsb_ref[...]) * scale).astype(BF16)
        ka_out[:, sl] = (kn[:, sl] + kr).astype(BF16)

    qb = seg(COL_NQ, 512)
    for sl in heads:
        qb_out[:, sl] = (_rope(qb[:, sl], kc, ksa, ksb) * scale).astype(BF16)
    kv = seg(COL_KV6, 6 * HEAD_D)
    part = lambda n: kv[:, n * HEAD_D:(n + 1) * HEAD_D]
    kcmp_out[...] = _rope(part(0), kc, ksa, ksb)
    vcmp_out[...] = part(1)
    kvb_out[:, 0:128] = _rope(part(2), kc, ksa, ksb).astype(BF16)
    kvb_out[:, 128:256] = part(3).astype(BF16)
    kvb_out[:, 256:384] = _rope(part(4), kc, ksa, ksb).astype(BF16)
    kvb_out[:, 384:512] = part(5).astype(BF16)
    gate_out[...] = jax.nn.sigmoid(seg(COL_GL, LANES))

    pool_out[...] = seg(COL_POOL, W_GROUP).astype(BF16)
    s5_out[...] = seg(COL_S5, W_GROUP).astype(BF16)


def _in_proj(x, g, w_all, layer, qg, kvg, wuq, wk, wv, qtabs, ktabs):
    t = x.shape[0]
    tm = 256
    nsb = SEQ // tm
    tab = pl.BlockSpec((tm, LANES), lambda i: (i % nsb, 0))
    full = lambda shape: pl.BlockSpec(shape, lambda i: (0, 0))
    out = lambda width: pl.BlockSpec((tm, width), lambda i: (i, 0))
    widths = (512, 512, 512, 512, LANES, LANES, 512, LANES, W_GROUP, W_GROUP)
    dtypes = (BF16, BF16, BF16, BF16, F32, F32, BF16, F32, BF16, BF16)
    return pl.pallas_call(
        _in_proj_kernel,
        grid=(t // tm,),
        in_specs=[pl.BlockSpec((tm, D_MODEL), lambda i: (i, 0)),
                  full((1, D_MODEL)),
                  pl.BlockSpec((None, D_MODEL, N_IN_PAD), lambda i: (layer, 0, 0)),
                  full((1, 512)), full((1, LANES)),
                  full((512, 512)), full((LANES, 512)), full((LANES, 512)),
                  tab, tab, tab, tab, tab, tab],
        out_specs=[out(w) for w in widths],
        out_shape=[jax.ShapeDtypeStruct((t, w), d) for w, d in zip(widths, dtypes)],
        compiler_params=_cparams(("parallel",)),
        name="in_proj",
    )(x, g, w_all, qg, kvg, wuq, wk, wv, *qtabs, *ktabs)


def _fold_lanes(a, op):
    out = a[:, 0:LANES]
    for c in range(1, a.shape[1] // LANES):
        out = op(out, a[:, c * LANES:(c + 1) * LANES])
    return out


def _mla_attn_kernel(q_ref, k_ref, v_ref, o_ref, s_ref, mx_ref, acc_ref, *, tq, tk):
    i = pl.program_id(1)
    nfull = (i * tq) // tk
    t0 = pl.multiple_of(nfull * tk, tk)
    row = i * tq + lax.broadcasted_iota(jnp.int32, (tq, tk), 0)
    col = t0 + lax.broadcasted_iota(jnp.int32, (tq, tk), 1)
    heads = [slice(h * HEAD_D, (h + 1) * HEAD_D) for h in range(MLA_HEADS)]
    chunk = lambda j: pl.ds(pl.multiple_of(j * tk, tk), tk)

    for h, sl in enumerate(heads):
        s = jnp.where(col <= row, _dot_nt(q_ref[:, sl], k_ref[pl.ds(t0, tk), sl]), NEG_INF)
        s_ref[h, nfull] = s
        mx_ref[h] = _fold_lanes(s, jnp.maximum)

    def scores(j, carry):
        for h, sl in enumerate(heads):
            s = _dot_nt(q_ref[:, sl], k_ref[chunk(j), sl])
            s_ref[h, j] = s
            mx_ref[h] = jnp.maximum(mx_ref[h], _fold_lanes(s, jnp.maximum))
        return carry

    lax.fori_loop(0, nfull, scores, 0)
    for h in range(MLA_HEADS):
        m = jnp.max(mx_ref[h], axis=-1, keepdims=True)
        mx_ref[h] = jnp.broadcast_to(m, (tq, LANES))
        acc_ref[h] = jnp.zeros((tq, 2 * HEAD_D), F32)

    ones = jnp.ones((tk, LANES), BF16)

    def values(j, carry):
        for h, sl in enumerate(heads):
            m = mx_ref[h]
            s = s_ref[h, j]
            p = jnp.concatenate([jnp.exp((s[:, c * LANES:(c + 1) * LANES] - m).astype(BF16))
                                 for c in range(tk // LANES)], axis=1)
            v_aug = jnp.concatenate([v_ref[chunk(j), sl], ones], axis=1)
            acc_ref[h] = acc_ref[h] + _dot(p, v_aug)
        return carry

    lax.fori_loop(0, nfull + 1, values, 0)
    for h, sl in enumerate(heads):
        o_ref[:, sl] = (acc_ref[h, :, 0:HEAD_D] / acc_ref[h, :, HEAD_D:2 * HEAD_D]).astype(BF16)


def _mla_attn(q, k, v, batch):
    tq, tk = 512, 512
    nq = SEQ // tq
    return pl.pallas_call(
        functools.partial(_mla_attn_kernel, tq=tq, tk=tk),
        grid=(batch, nq),
        in_specs=[pl.BlockSpec((tq, W_GROUP), lambda b, i: (b * nq + i, 0)),
                  pl.BlockSpec((SEQ, W_GROUP), lambda b, i: (b, 0)),
                  pl.BlockSpec((SEQ, W_GROUP), lambda b, i: (b, 0))],
        out_specs=pl.BlockSpec((tq, W_GROUP), lambda b, i: (b * nq + i, 0)),
        out_shape=jax.ShapeDtypeStruct((batch * SEQ, W_GROUP), BF16),
        scratch_shapes=[pltpu.VMEM((MLA_HEADS, SEQ // tk, tq, tk), F32),
                        pltpu.VMEM((MLA_HEADS, tq, LANES), F32),
                        pltpu.VMEM((MLA_HEADS, tq, 2 * HEAD_D), F32)],
        compiler_params=_cparams(("parallel", "arbitrary")),
        name="mla_attn",
    )(q, k, v)


def _nsa_compress_kernel(xk_ref, xv_ref, w1_ref, w2_ref, pe_ref, k_out, v_out):
    for c, (x_ref, o_ref) in enumerate(((xk_ref, k_out), (xv_ref, v_out))):
        a = jnp.zeros((N_CMP_PAD, HEAD_D), F32)
        b = jnp.zeros((N_CMP_PAD, HEAD_D), F32)
        for r in range(CMP_STRIDE):
            x = x_ref[pl.ds(r, N_CMP_PAD, stride=CMP_STRIDE), :].astype(BF16)
            a = a + _dot(x, w1_ref[c, r * HEAD_D:(r + 1) * HEAD_D, :])
            b = b + _dot(x, w1_ref[c, (CMP_STRIDE + r) * HEAD_D:(CMP_STRIDE + r + 1) * HEAD_D, :])
        b = pltpu.roll(b, N_CMP_PAD - 1, 0)
        pe = _dot(pe_ref[c], w1_ref[c])[0:1, :]
        hid = jax.nn.gelu(a + b + pe)
        o_ref[0] = _dot(hid.astype(BF16), w2_ref[c]).astype(BF16)


def _nsa_compress(xk, xv, w1, w2, pe, batch):
    xspec = pl.BlockSpec((SEQ, HEAD_D), lambda b: (b, 0))
    ospec = pl.BlockSpec((1, N_CMP_PAD, HEAD_D), lambda b: (b, 0, 0))
    return pl.pallas_call(
        _nsa_compress_kernel,
        grid=(batch,),
        in_specs=[xspec, xspec,
                  pl.BlockSpec((2, CMP_BLOCK * HEAD_D, HEAD_D), lambda b: (0, 0, 0)),
                  pl.BlockSpec((2, HEAD_D, HEAD_D), lambda b: (0, 0, 0)),
                  pl.BlockSpec((2, 8, CMP_BLOCK * HEAD_D), lambda b: (0, 0, 0))],
        out_specs=[ospec, ospec],
        out_shape=[jax.ShapeDtypeStruct((batch, N_CMP_PAD, HEAD_D), BF16)] * 2,
        compiler_params=_cparams(("parallel",)),
        name="nsa_compress",
    )(xk, xv, w1, w2, pe)


def _softmax_rows(s):
    m = jnp.max(s, axis=-1, keepdims=True)
    p = jnp.exp(s - m)
    return p / jnp.sum(p, axis=-1, keepdims=True)


def _nsa_attn_kernel(q_ref, kc_ref, vc_ref, kv_ref, g_ref, ovt_ref, e_ref, o_ref,
                     m_ref, acc_ref, s_ref, *, tq, ck):
    i = pl.program_id(1)
    q0 = i * tq
    nh = 4
    qs = jnp.concatenate([q_ref[:, h * HEAD_D:(h + 1) * HEAD_D] for h in range(nh)], axis=0)
    qpos = q0 + lax.broadcasted_iota(jnp.int32, (tq, 1), 0)
    masked = lambda s, ok: (s.reshape(nh, tq, s.shape[1])
                            + jnp.where(ok, 0.0, NEG_INF)[None]).reshape(s.shape)

    n_idx = lax.broadcasted_iota(jnp.int32, (tq, N_CMP_PAD), 1)
    valid_c = n_idx * CMP_STRIDE + (CMP_BLOCK - 1) <= qpos
    pc = _softmax_rows(masked(_dot_nt(qs, kc_ref[0]), valid_c)).reshape(nh, tq, N_CMP_PAD)
    pc = jnp.where(valid_c[None], pc, 0.0)
    psum = pc[0] + pc[1] + pc[2] + pc[3]
    o_c = _dot(pc.reshape(nh * tq, N_CMP_PAD).astype(BF16), vc_ref[0])

    imp_t = _dot_nt(ovt_ref[...], psum, precision=lax.Precision.HIGHEST)
    kblk = lax.broadcasted_iota(jnp.int32, (N_BLK, tq), 0)
    cur = (q0 + lax.broadcasted_iota(jnp.int32, (N_BLK, tq), 1)) // SEL_BLOCK
    forced = (kblk == 0) | ((kblk <= cur) & (kblk > cur - N_LOCAL))
    score = jnp.where(forced, FORCE_SCORE, jnp.where(kblk <= cur, imp_t, -1.0))
    cnt = jnp.zeros((N_BLK, tq), F32)
    for j in range(N_BLK):
        sj = score[j:j + 1, :]
        beats = (sj > score) | ((sj == score) & (kblk > j))
        cnt = cnt + beats.astype(F32)
    sel = jnp.transpose((cnt < N_SEL).astype(F32)).astype(BF16)

    wlen = WINDOW + tq
    w0 = pl.multiple_of(jnp.maximum(q0 - WINDOW, 0), tq)
    sw = _dot_nt(qs, kv_ref[pl.ds(w0, wlen), 256:384])
    kpos = w0 + lax.broadcasted_iota(jnp.int32, (tq, wlen), 1)
    ok = (kpos <= qpos) & (kpos > qpos - WINDOW)
    sw = masked(sw, ok)
    pw = jnp.exp((sw - jnp.max(_fold_lanes(sw, jnp.maximum), axis=-1, keepdims=True)).astype(BF16))
    aug = lambda v: jnp.concatenate([v, jnp.ones(v.shape, BF16)], axis=1)
    ow = _dot(pw, aug(kv_ref[pl.ds(w0, wlen), 384:512]))
    o_w = ow[:, 0:HEAD_D] / ow[:, HEAD_D:2 * HEAD_D]

    nck = q0 // ck + 1
    m_ref[...] = jnp.full(m_ref.shape, NEG_INF, F32)

    def scores(c, carry):
        r0 = pl.multiple_of(c * ck, ck)
        chosen = _dot(sel, e_ref[c])
        kpos = r0 + lax.broadcasted_iota(jnp.int32, (tq, ck), 1)
        ok = (chosen > 0.5) & (kpos <= qpos)
        s = masked(_dot_nt(qs, kv_ref[pl.ds(r0, ck), 0:128]), ok)
        s_ref[c] = s
        m_ref[...] = jnp.maximum(m_ref[...], _fold_lanes(s, jnp.maximum))
        return carry

    lax.fori_loop(0, nck, scores, 0)
    m_ref[...] = jnp.broadcast_to(jnp.max(m_ref[...], axis=-1, keepdims=True), m_ref.shape)
    acc_ref[...] = jnp.zeros(acc_ref.shape, F32)

    def values(c, carry):
        r0 = pl.multiple_of(c * ck, ck)
        m = m_ref[...]
        s = s_ref[c]
        p = jnp.concatenate([jnp.exp((s[:, k * LANES:(k + 1) * LANES] - m).astype(BF16))
                             for k in range(ck // LANES)], axis=1)
        acc_ref[...] = acc_ref[...] + _dot(p, aug(kv_ref[pl.ds(r0, ck), 128:256]))
        return carry

    lax.fori_loop(0, nck, values, 0)
    o_s = acc_ref[:, 0:HEAD_D] / acc_ref[:, HEAD_D:2 * HEAD_D]

    g = g_ref[...]
    for h in range(nh):
        rs = slice(h * tq, (h + 1) * tq)
        o_ref[:, h * HEAD_D:(h + 1) * HEAD_D] = (
            g[:, 3 * h:3 * h + 1] * o_c[rs] + g[:, 3 * h + 1:3 * h + 2] * o_s[rs]
            + g[:, 3 * h + 2:3 * h + 3] * o_w[rs]).astype(BF16)


def _nsa_attn(q, kcmp, vcmp, kv, gates, ovt, emat, batch):
    tq, ck = 256, 512
    nq = SEQ // tq
    return pl.pallas_call(
        functools.partial(_nsa_attn_kernel, tq=tq, ck=ck),
        grid=(batch, nq),
        in_specs=[pl.BlockSpec((tq, 512), lambda b, i: (b * nq + i, 0)),
                  pl.BlockSpec((1, N_CMP_PAD, HEAD_D), lambda b, i: (b, 0, 0)),
                  pl.BlockSpec((1, N_CMP_PAD, HEAD_D), lambda b, i: (b, 0, 0)),
                  pl.BlockSpec((SEQ, 512), lambda b, i: (b, 0)),
                  pl.BlockSpec((tq, LANES), lambda b, i: (b * nq + i, 0)),
                  pl.BlockSpec((N_BLK, N_CMP_PAD), lambda b, i: (0, 0)),
                  pl.BlockSpec((SEQ // ck, N_BLK, ck), lambda b, i: (0, 0, 0))],
        out_specs=pl.BlockSpec((tq, 512), lambda b, i: (b * nq + i, 0)),
        out_shape=jax.ShapeDtypeStruct((batch * SEQ, 512), BF16),
        scratch_shapes=[pltpu.VMEM((4 * tq, LANES), F32),
                        pltpu.VMEM((4 * tq, 2 * HEAD_D), F32),
                        pltpu.VMEM((SEQ // ck, 4 * tq, ck), F32)],
        compiler_params=_cparams(("parallel", "arbitrary")),
        name="nsa_attn",
    )(q, kcmp, vcmp, kv, gates, ovt, emat)


def _pool_kernel(u_ref, w_ref, b_ref, s_ref, o_ref, pad_ref):
    maxw = POOL_SIZES[-1]
    pad_ref[0:maxw, :] = jnp.zeros((maxw, W_GROUP), F32)
    pad_ref[maxw:maxw + SEQ, :] = u_ref[...].astype(F32)
    rc = 512
    for g, w in enumerate(POOL_SIZES):
        sl = slice(g * LANES, (g + 1) * LANES)
        for r in range(SEQ // rc):
            acc = pad_ref[maxw + r * rc:maxw + (r + 1) * rc, sl]
            tok = acc
            for j in range(1, w):
                acc = acc + pad_ref[maxw - j + r * rc:maxw - j + (r + 1) * rc, sl]
            t = r * rc + lax.broadcasted_iota(jnp.int32, (rc, 1), 0)
            cnt = jnp.minimum(t + 1, w).astype(F32)
            d = acc / cnt - tok
            y = _dot(d.astype(BF16), w_ref[g])
            o_ref[r * rc:(r + 1) * rc, sl] = ((y + b_ref[:, sl]) * s_ref[:, sl]).astype(BF16)


def _pool(u, w, b, s, batch):
    return pl.pallas_call(
        _pool_kernel,
        grid=(batch,),
        in_specs=[pl.BlockSpec((SEQ, W_GROUP), lambda i: (i, 0)),
                  pl.BlockSpec((4, LANES, LANES), lambda i: (0, 0, 0)),
                  pl.BlockSpec((1, W_GROUP), lambda i: (0, 0)),
                  pl.BlockSpec((1, W_GROUP), lambda i: (0, 0))],
        out_specs=pl.BlockSpec((SEQ, W_GROUP), lambda i: (i, 0)),
        out_shape=jax.ShapeDtypeStruct((batch * SEQ, W_GROUP), BF16),
        scratch_shapes=[pltpu.VMEM((SEQ + POOL_SIZES[-1], W_GROUP), F32)],
        compiler_params=_cparams(("parallel",)),
        name="pool",
    )(u, w, b, s)


def _s5_kernel(u_ref, wb_ref, ar_ref, ai_ref, wc_ref, d_ref, gw_ref, gb_ref, o_ref,
               bu_ref, st_ref, tm_ref, *, batch, tc):
    @pl.when(pl.program_id(0) == 0)
    def _():
        st_ref[...] = jnp.zeros(st_ref.shape, F32)

    nslab = W_GROUP // LANES
    for b in range(batch):
        for j in range(nslab):
            tm_ref[j, pl.ds(b, tc, stride=batch), :] = u_ref[b, :, j * LANES:(j + 1) * LANES].astype(F32)
    u = jnp.concatenate([tm_ref[j] for j in range(nslab)], axis=1)
    gl = 8 * S5_STATE
    ub = u.astype(BF16)
    for j in range(nslab):
        uj = ub[:, j * LANES:(j + 1) * LANES]
        for k, part in enumerate((0, S5_WIDTH)):
            cs = slice(part + j * gl, part + (j + 1) * gl)
            bu_ref[:, cs] = _dot(uj, wb_ref[j, :, k * gl:(k + 1) * gl])
    lc = 1024
    unroll = 8
    for c in range(S5_WIDTH // lc):
        re = slice(c * lc, (c + 1) * lc)
        im = slice(S5_WIDTH + c * lc, S5_WIDTH + (c + 1) * lc)
        ar, ai = ar_ref[:, re], ai_ref[:, re]

        def body(tb, carry, re=re, im=im, ar=ar, ai=ai):
            xr, xi = carry
            for k in range(unroll):
                r0 = pl.multiple_of((tb * unroll + k) * batch, batch)
                nxr = ar * xr - ai * xi + bu_ref[pl.ds(r0, batch), re]
                nxi = ar * xi + ai * xr + bu_ref[pl.ds(r0, batch), im]
                bu_ref[pl.ds(r0, batch), re] = nxr
                bu_ref[pl.ds(r0, batch), im] = nxi
                xr, xi = nxr, nxi
            return xr, xi

        xr, xi = lax.fori_loop(0, tc // unroll, body, (st_ref[:, re], st_ref[:, im]))
        st_ref[:, re] = xr
        st_ref[:, im] = xi

    ys = []
    for j in range(nslab):
        yj = 0.0
        for k, part in enumerate((0, S5_WIDTH)):
            cs = slice(part + j * gl, part + (j + 1) * gl)
            yj = yj + _dot(bu_ref[:, cs].astype(BF16), wc_ref[j, k * gl:(k + 1) * gl, :])
        ys.append(yj)
    y = jnp.concatenate(ys, axis=1) + d_ref[...] * u
    y = jax.nn.gelu(y)
    z = _dot(y.astype(BF16), gw_ref[...]) + gb_ref[...]
    o = y * jax.nn.sigmoid(z)
    for j in range(nslab):
        tm_ref[j] = o[:, j * LANES:(j + 1) * LANES]
    for b in range(batch):
        for j in range(nslab):
            o_ref[b, :, j * LANES:(j + 1) * LANES] = tm_ref[j, pl.ds(b, tc, stride=batch), :].astype(BF16)


def _s5(u3, wb, ar, ai, wc, d, gw, gb):
    batch = u3.shape[0]
    tc = 64
    rows = tc * batch
    full = lambda shape: pl.BlockSpec(shape, lambda i: (0, 0))
    return pl.pallas_call(
        functools.partial(_s5_kernel, batch=batch, tc=tc),
        grid=(SEQ // tc,),
        in_specs=[pl.BlockSpec((batch, tc, W_GROUP), lambda i: (0, i, 0)),
                  pl.BlockSpec(wb.shape, lambda i: (0, 0, 0)), full((batch, S5_WIDTH)),
                  full((batch, S5_WIDTH)), pl.BlockSpec(wc.shape, lambda i: (0, 0, 0)),
                  full((1, W_GROUP)),
                  full((W_GROUP, W_GROUP)), full((1, W_GROUP))],
        out_specs=pl.BlockSpec((batch, tc, W_GROUP), lambda i: (0, i, 0)),
        out_shape=jax.ShapeDtypeStruct((batch, SEQ, W_GROUP), BF16),
        scratch_shapes=[pltpu.VMEM((rows, 2 * S5_WIDTH), F32),
                        pltpu.VMEM((batch, 2 * S5_WIDTH), F32),
                        pltpu.VMEM((W_GROUP // LANES, rows, LANES), F32)],
        compiler_params=_cparams(("arbitrary",)),
        name="s5",
    )(u3, wb, ar, ai, wc, d, gw, gb)


def _out_proj_kernel(ya_ref, yb_ref, yc_ref, yd_ref, mg_ref, wo_ref, x_ref, fg_ref,
                     wrh_ref, wrl_ref, br_ref, x1_ref, h2_ref, lg_ref, *, tm):
    acc = x_ref[...]
    for gi, y_ref in enumerate((ya_ref, yb_ref, yc_ref, yd_ref)):
        sl = slice(gi * W_GROUP, (gi + 1) * W_GROUP)
        n = (_rms(y_ref[...].astype(F32)) * mg_ref[:, sl]).astype(BF16)
        acc = acc + _dot(n, wo_ref[sl, :])
    x1_ref[...] = acc
    h2 = _rms(acc) * fg_ref[...]
    hi = h2.astype(BF16)
    lo = (h2 - hi.astype(F32)).astype(BF16)
    lg_ref[...] = (_dot(hi, wrh_ref[...]) + _dot(hi, wrl_ref[...]) + _dot(lo, wrh_ref[...])
                   + br_ref[...])
    _pack_rows(h2_ref, h2, tm)


def _out_proj(ya, yb, yc, yd, mg, wo_all, layer, x, fg, wr_hi, wr_lo, br):
    t = x.shape[0]
    tm = 256
    yspec = pl.BlockSpec((tm, W_GROUP), lambda i: (i, 0))
    full = lambda shape: pl.BlockSpec(shape, lambda i: (0, 0))
    return pl.pallas_call(
        functools.partial(_out_proj_kernel, tm=tm),
        grid=(t // tm,),
        in_specs=[yspec, yspec, yspec, yspec, full((1, D_MODEL)),
                  pl.BlockSpec((None, D_MODEL, D_MODEL), lambda i: (layer, 0, 0)),
                  pl.BlockSpec((tm, D_MODEL), lambda i: (i, 0)), full((1, D_MODEL)),
                  full((D_MODEL, LANES)), full((D_MODEL, LANES)), full((1, LANES))],
        out_specs=[pl.BlockSpec((tm, D_MODEL), lambda i: (i, 0)),
                   pl.BlockSpec((tm * ROW_PITCH, LANES), lambda i: (i, 0)),
                   pl.BlockSpec((tm, LANES), lambda i: (i, 0))],
        out_shape=[jax.ShapeDtypeStruct((t, D_MODEL), F32),
                   jax.ShapeDtypeStruct((t * ROW_PITCH, LANES), F32),
                   jax.ShapeDtypeStruct((t, LANES), F32)],
        compiler_params=_cparams(("parallel",)),
        name="out_proj",
    )(ya, yb, yc, yd, mg, wo_all, x, fg, wr_hi, wr_lo, br)


ROUTE_ROWS = 40


def _route(lg, meta_ref, metat_ref, cnt_ref, col_ref, row_ref, tm):
    @pl.when(pl.program_id(0) == 0)
    def _():
        col_ref[...] = jnp.zeros(col_ref.shape, F32)
        row_ref[...] = jnp.zeros(row_ref.shape, F32)

    nr = ROUTE_ROWS
    lt = jnp.transpose(lg)[0:nr, :]
    row = lax.broadcasted_iota(jnp.int32, (nr, tm), 0)
    big = jnp.int32(1 << 20)
    cmax = lambda a: jnp.max(a, axis=0, keepdims=True)
    cmin = lambda a: jnp.min(a, axis=0, keepdims=True)
    csum = lambda a: jnp.sum(a, axis=0, keepdims=True)

    is_g = row < MOE_GROUPS
    gl = jnp.where(is_g, lt, NEG_INF)
    gm = cmax(gl)
    p_top = 1.0 / csum(jnp.where(is_g, jnp.exp(gl - gm), 0.0))
    g_top = cmin(jnp.where(is_g & (gl == gm), row, big))

    is_e = (row >= MOE_GROUPS) & (row < MOE_GROUPS + MOE_EXPERTS) \
        & (((row - MOE_GROUPS) // MOE_EPG) == g_top)
    el = jnp.where(is_e, lt, NEG_INF)
    ee = jnp.where(is_e, jnp.exp(el - cmax(el)), 0.0)
    p = jnp.where(is_e, ee / csum(ee), -1.0)
    p1 = cmax(p)
    i1 = cmin(jnp.where(p == p1, row, big))
    p_rest = jnp.where(row == i1, -1.0, p)
    p2 = cmax(p_rest)
    i2 = cmin(jnp.where((p_rest == p2) & is_e & (row != i1), row, big))
    den = p1 + p2
    w1 = p_top * (p1 / den)
    w2 = p_top * (p2 / den)

    hit1, hit2 = row == i1, row == i2
    oh = (hit1 | hit2).astype(BF16)
    r = lax.broadcasted_iota(jnp.int32, (tm, tm), 0)
    c = lax.broadcasted_iota(jnp.int32, (tm, tm), 1)
    before = _dot(oh, (r < c).astype(BF16)) + col_ref[:, 0:1]
    r1 = csum(jnp.where(hit1, before, 0.0))
    r2 = csum(jnp.where(hit2, before, 0.0))
    col_ref[...] = col_ref[...] + jnp.sum(oh.astype(F32), axis=1, keepdims=True)
    oh_all = jnp.concatenate([oh, jnp.zeros((LANES - nr, tm), BF16)], axis=0)
    row_ref[...] = row_ref[...] + _dot_nt(jnp.ones((8, tm), BF16), oh_all)
    cnt_ref[...] = row_ref[...]

    e1 = (i1 - MOE_GROUPS).astype(F32)
    e2 = (i2 - MOE_GROUPS).astype(F32)
    row8 = lax.broadcasted_iota(jnp.int32, (8, tm), 0)
    meta = jnp.zeros((8, tm), F32)
    for k, v in enumerate((e1, e2, w1, w2, r1, r2)):
        meta = jnp.where(row8 == k, v, meta)
    metat_ref[...] = meta
    meta = jnp.concatenate([meta, jnp.zeros((LANES - 8, tm), F32)], axis=0)
    meta_ref[...] = jnp.transpose(meta)


def _router_kernel(lg_ref, meta_ref, metat_ref, cnt_ref, col_ref, row_ref, *, tm):
    _route(lg_ref[...], meta_ref, metat_ref, cnt_ref, col_ref, row_ref, tm)


def _router(logits):
    t = logits.shape[0]
    tm = 512
    return pl.pallas_call(
        functools.partial(_router_kernel, tm=tm),
        grid=(t // tm,),
        in_specs=[pl.BlockSpec((tm, LANES), lambda i: (i, 0))],
        out_specs=[pl.BlockSpec((tm, LANES), lambda i: (i, 0)),
                   pl.BlockSpec((8, tm), lambda i: (0, i)),
                   pl.BlockSpec((8, LANES), lambda i: (i, 0))],
        out_shape=[jax.ShapeDtypeStruct((t, LANES), F32),
                   jax.ShapeDtypeStruct((8, t), F32),
                   jax.ShapeDtypeStruct((t // tm * 8, LANES), F32)],
        scratch_shapes=[pltpu.VMEM((ROUTE_ROWS, LANES), F32), pltpu.VMEM((8, LANES), F32)],
        compiler_params=_cparams(("arbitrary",)),
        name="router",
    )(logits)


def _dest_kernel(metat_ref, cnt_ref, o_ref, *, tm):
    cnt = cnt_ref[...]
    start = (_lane_cumsum(cnt) - cnt)[0:1, :]
    nr = ROUTE_ROWS
    start_col = jnp.transpose(jnp.broadcast_to(start, (LANES, LANES)))[0:nr, 0:1]
    mt = metat_ref[...]
    row = lax.broadcasted_iota(jnp.int32, (nr, tm), 0)
    row8 = lax.broadcasted_iota(jnp.int32, (8, tm), 0)
    out = jnp.zeros((8, tm), F32)
    for k in range(2):
        e_row = mt[k:k + 1, :].astype(jnp.int32) + MOE_GROUPS
        d = jnp.sum(jnp.where(row == e_row, start_col, 0.0), axis=0, keepdims=True) + mt[4 + k:5 + k, :]
        out = jnp.where(row8 == k, d, out)
    o_ref[...] = out.astype(jnp.int32)


def _dest(meta_t, counts):
    t = meta_t.shape[1]
    tm = 512
    last = counts.shape[0] // 8 - 1
    return pl.pallas_call(
        functools.partial(_dest_kernel, tm=tm),
        grid=(t // tm,),
        in_specs=[pl.BlockSpec((8, tm), lambda i: (0, i)),
                  pl.BlockSpec((8, LANES), lambda i: (last, 0))],
        out_specs=pl.BlockSpec((8, tm), lambda i: (0, i)),
        out_shape=jax.ShapeDtypeStruct((8, t), jnp.int32),
        compiler_params=_cparams(("parallel",)),
        name="dest",
    )(meta_t, counts)


def _dispatch_kernel(d1_ref, d2_ref, h_ref, xs_ref, sem, *, td):
    base = pl.program_id(0) * td

    def start(rb, carry):
        for k in range(DMA_UNROLL):
            r = rb * DMA_UNROLL + k
            src = h_ref.at[pl.ds(r * ROW_PITCH, ROW_PITCH)]
            for d_ref in (d1_ref, d2_ref):
                dst = xs_ref.at[pl.ds(d_ref[base + r] * ROW_PITCH, ROW_PITCH)]
                pltpu.make_async_copy(src, dst, sem).start()
        return carry

    lax.fori_loop(0, td // DMA_UNROLL, start, 0)
    for _ in range(2):
        pltpu.make_async_copy(h_ref, xs_ref.at[pl.ds(0, td * ROW_PITCH)], sem).wait()


def _dispatch(dest1, dest2, h2):
    t = dest1.shape[0]
    td = 1024
    return pl.pallas_call(
        functools.partial(_dispatch_kernel, td=td),
        grid_spec=pltpu.PrefetchScalarGridSpec(
            num_scalar_prefetch=2,
            grid=(t // td,),
            in_specs=[pl.BlockSpec((td * ROW_PITCH, LANES), lambda i, *_: (i, 0))],
            out_specs=pl.BlockSpec(memory_space=pl.ANY),
            scratch_shapes=[pltpu.SemaphoreType.DMA(())]),
        out_shape=jax.ShapeDtypeStruct((2 * t * ROW_PITCH, LANES), F32),
        compiler_params=pltpu.CompilerParams(dimension_semantics=("arbitrary",),
                                             has_side_effects=True),
        name="dispatch",
    )(dest1, dest2, h2)


def _expert_kernel(vt_ref, ve_ref, vlo_ref, vhi_ref, vfirst_ref, vvalid_ref, vnew_ref, vnext_ref,
                   vslot_ref, xs_ref, wg_ref, wu_ref, wd_ref, ys_ref,
                   wgf_ref, wuf_ref, wdf_ref, wgb_ref, wub_ref, wdb_ref, sem, *, tmx, layer):
    v = pl.program_id(0)

    def weight_copies(e, slot):
        return [pltpu.make_async_copy(w_ref.at[layer, e], f_ref.at[slot], sem.at[slot])
                for w_ref, f_ref in ((wg_ref, wgf_ref), (wu_ref, wuf_ref), (wd_ref, wdf_ref))]

    @pl.when(vnew_ref[v] == 1)
    def _():
        slot = vslot_ref[v]

        @pl.when(v == 0)
        def _():
            for cp in weight_copies(ve_ref[v], slot):
                cp.start()

        for cp in weight_copies(ve_ref[v], slot):
            cp.wait()

        @pl.when(vnext_ref[v] >= 0)
        def _():
            for cp in weight_copies(vnext_ref[v], 1 - slot):
                cp.start()

        wgb_ref[...] = wgf_ref[slot].astype(BF16)
        wub_ref[...] = wuf_ref[slot].astype(BF16)
        wdb_ref[...] = wdf_ref[slot].astype(BF16)

    @pl.when(vvalid_ref[v] == 1)
    def _():
        x = jnp.concatenate(_unpack_rows(lambda rows: xs_ref[rows, :], tmx), axis=1).astype(BF16)
        a = _dot(x, wgb_ref[...])
        u = _dot(x, wub_ref[...])
        hid = (jax.nn.silu(a) * u).astype(BF16)
        y = _dot(hid, wdb_ref[...])
        rows = lax.broadcasted_iota(jnp.int32, (tmx, 1), 0)
        mine = (rows >= vlo_ref[v]) & (rows < vhi_ref[v])

        @pl.when(vfirst_ref[v] == 1)
        def _():
            _pack_rows(ys_ref, y, tmx, mask=mine)

        @pl.when(vfirst_ref[v] == 0)
        def _():
            _pack_rows(ys_ref, y, tmx, mask=mine, old=True)


def _experts(sched, xs, wg, wu, wd, layer, tmx):
    nvis = sched[0].shape[0]
    rows = xs.shape[0]
    xspec = pl.BlockSpec((tmx * ROW_PITCH, LANES), lambda v, vt, *_: (vt[v], 0))
    hbm = pl.BlockSpec(memory_space=pl.ANY)
    return pl.pallas_call(
        functools.partial(_expert_kernel, tmx=tmx, layer=layer),
        grid_spec=pltpu.PrefetchScalarGridSpec(
            num_scalar_prefetch=9,
            grid=(nvis,),
            in_specs=[xspec, hbm, hbm, hbm],
            out_specs=xspec,
            scratch_shapes=[pltpu.VMEM((2, D_MODEL, MOE_HIDDEN), F32),
                            pltpu.VMEM((2, D_MODEL, MOE_HIDDEN), F32),
                            pltpu.VMEM((2, MOE_HIDDEN, D_MODEL), F32),
                            pltpu.VMEM((D_MODEL, MOE_HIDDEN), BF16),
                            pltpu.VMEM((D_MODEL, MOE_HIDDEN), BF16),
                            pltpu.VMEM((MOE_HIDDEN, D_MODEL), BF16),
                            pltpu.SemaphoreType.DMA((2,))]),
        out_shape=jax.ShapeDtypeStruct((rows, LANES), F32),
        compiler_params=_cparams(("arbitrary",)),
        name="experts",
    )(*sched, xs, wg, wu, wd)


def _combine_kernel(d1_ref, d2_ref, ys_ref, meta_ref, x1_ref, fg_ref, o_ref,
                    b1_ref, b2_ref, sem, *, tc, final):
    i = pl.program_id(0)
    n = pl.num_programs(0)
    slot = i % 2

    def gather(tile, slot):
        def start(rb, carry):
            for k in range(DMA_UNROLL):
                r = rb * DMA_UNROLL + k
                dst = pl.ds(r * ROW_PITCH, ROW_CHUNKS)
                for d_ref, b_ref in ((d1_ref, b1_ref), (d2_ref, b2_ref)):
                    src = ys_ref.at[pl.ds(d_ref[tile * tc + r] * ROW_PITCH, ROW_CHUNKS)]
                    pltpu.make_async_copy(src, b_ref.at[slot, dst], sem.at[slot]).start()
            return carry

        lax.fori_loop(0, tc // DMA_UNROLL, start, 0)

    @pl.when(i == 0)
    def _():
        gather(0, 0)

    @pl.when(i + 1 < n)
    def _():
        gather(i + 1, 1 - slot)

    for b_ref in (b1_ref, b2_ref):
        pltpu.make_async_copy(ys_ref.at[pl.ds(0, tc * ROW_CHUNKS)],
                              b_ref.at[slot, pl.ds(0, tc * ROW_CHUNKS)], sem.at[slot]).wait()
    w1 = meta_ref[:, 2:3]
    w2 = meta_ref[:, 3:4]
    y1 = _unpack_rows(lambda rows: b1_ref[slot, rows, :], tc)
    y2 = _unpack_rows(lambda rows: b2_ref[slot, rows, :], tc)
    x2 = jnp.concatenate([x1_ref[:, c * LANES:(c + 1) * LANES] + (w1 * y1[c] + w2 * y2[c])
                          for c in range(ROW_CHUNKS)], axis=1)
    if final:
        x2 = _rms(x2) * fg_ref[...]
    o_ref[...] = x2


def _combine(dest1, dest2, ys, meta, x1, fg, final):
    t = x1.shape[0]
    tc = 256
    return pl.pallas_call(
        functools.partial(_combine_kernel, tc=tc, final=final),
        grid_spec=pltpu.PrefetchScalarGridSpec(
            num_scalar_prefetch=2,
            grid=(t // tc,),
            in_specs=[pl.BlockSpec(memory_space=pl.ANY),
                      pl.BlockSpec((tc, LANES), lambda i, *_: (i, 0)),
                      pl.BlockSpec((tc, D_MODEL), lambda i, *_: (i, 0)),
                      pl.BlockSpec((1, D_MODEL), lambda i, *_: (0, 0))],
            out_specs=pl.BlockSpec((tc, D_MODEL), lambda i, *_: (i, 0)),
            scratch_shapes=[pltpu.VMEM((2, tc * ROW_PITCH, LANES), F32),
                            pltpu.VMEM((2, tc * ROW_PITCH, LANES), F32),
                            pltpu.SemaphoreType.DMA((2,))]),
        out_shape=jax.ShapeDtypeStruct((t, D_MODEL), F32),
        compiler_params=_cparams(("arbitrary",)),
        name="combine",
    )(dest1, dest2, ys, meta, x1, fg)


def _rope_tables(r0):
    inv = ROPE_THETA ** (-jnp.arange(0, 2 * ROPE_HALF, 2, dtype=F32) / (2 * ROPE_HALF))
    ang = jnp.arange(SEQ, dtype=F32)[:, None] * inv[None, :]
    cos, sin = jnp.cos(ang), jnp.sin(ang)
    c = jnp.ones((SEQ, LANES), F32).at[:, r0:r0 + ROPE_HALF].set(cos)
    c = c.at[:, r0 + ROPE_HALF:r0 + 2 * ROPE_HALF].set(cos)
    sa = jnp.zeros((SEQ, LANES), F32).at[:, r0:r0 + ROPE_HALF].set(-sin)
    sb = jnp.zeros((SEQ, LANES), F32).at[:, r0 + ROPE_HALF:r0 + 2 * ROPE_HALF].set(sin)
    return c, sa, sb


def _pad_cols(w, width):
    return jnp.pad(w, ((0, 0), (0, width - w.shape[1])))


def _s5_params(a_re, a_im, log_dt, b_re, b_im, c_re, c_im, batch):
    dt = jnp.exp(log_dt)[:, None]
    mag = jnp.exp(a_re * dt)
    abar_r, abar_i = mag * jnp.cos(a_im * dt), mag * jnp.sin(a_im * dt)
    den = a_re * a_re + a_im * a_im
    nr, ni = abar_r - 1.0, abar_i
    coef_r = (nr * a_re + ni * a_im) / den
    coef_i = (ni * a_re - nr * a_im) / den
    bbar_r = coef_r[..., None] * b_re - coef_i[..., None] * b_im
    bbar_i = coef_r[..., None] * b_im + coef_i[..., None] * b_re
    nslab, gps = W_GROUP // LANES, LANES // S5_CH
    eye = jnp.eye(gps, dtype=F32)
    slab = lambda m: m.reshape((nslab, gps) + m.shape[1:])
    blk_b = lambda m: jnp.einsum('jgpc,gh->jgchp', slab(m), eye).reshape(nslab, LANES, gps * S5_STATE)
    wb = jnp.concatenate([blk_b(bbar_r), blk_b(bbar_i)], axis=2).astype(BF16)
    blk_c = lambda m: jnp.einsum('jgcp,gh->jgphc', slab(m), eye).reshape(nslab, gps * S5_STATE, LANES)
    wc = jnp.concatenate([blk_c(c_re), blk_c(-c_im)], axis=1).astype(BF16)
    ar = jnp.broadcast_to(abar_r.reshape(1, S5_WIDTH), (batch, S5_WIDTH))
    ai = jnp.broadcast_to(abar_i.reshape(1, S5_WIDTH), (batch, S5_WIDTH))
    return wb, ar, ai, wc


def _nsa_consts(ck):
    c_start = np.arange(N_CMP_PAD) * CMP_STRIDE
    b_start = np.arange(N_BLK) * SEL_BLOCK
    ov = ((c_start[None, :] < b_start[:, None] + SEL_BLOCK)
          & (c_start[None, :] + CMP_BLOCK > b_start[:, None])).astype(np.float32)
    key_blk = np.arange(SEQ) // SEL_BLOCK
    e = (key_blk[None, :] == np.arange(N_BLK)[:, None]).astype(np.float32)
    e = e.reshape(N_BLK, SEQ // ck, ck).transpose(1, 0, 2)
    return jnp.asarray(ov), jnp.asarray(e, dtype=BF16)


def _lane_cumsum(v):
    r = lax.broadcasted_iota(jnp.int32, (LANES, LANES), 0)
    c = lax.broadcasted_iota(jnp.int32, (LANES, LANES), 1)
    incl = (r <= c).astype(BF16)
    hi = jnp.floor(v * (1.0 / 256.0))
    lo = v - 256.0 * hi
    return 256.0 * _dot(hi.astype(BF16), incl) + _dot(lo.astype(BF16), incl)


def _sched_kernel(cnt_ref, o_ref, *, tmx, nv):
    cnt = cnt_ref[...]
    ends = _lane_cumsum(cnt)
    offs = ends - cnt
    first = jnp.floor(offs * (1.0 / tmx))
    last = jnp.floor(jnp.maximum(ends - 1.0, 0.0) * (1.0 / tmx))
    nvis = jnp.where(cnt > 0.0, last - first + 1.0, 0.0)
    cumv = _lane_cumsum(nvis)
    row1 = lambda a: a[0:1, :]
    total = jnp.max(row1(cumv), axis=-1, keepdims=True)
    rsum = lambda a: jnp.sum(a, axis=-1, keepdims=True)
    v = lax.broadcasted_iota(jnp.int32, (nv, LANES), 0).astype(F32)
    lane = lax.broadcasted_iota(jnp.int32, (nv, LANES), 1).astype(F32)
    vc = jnp.minimum(v, total - 1.0)
    e_lane = rsum((row1(cumv) <= vc).astype(F32))
    hit = lane == e_lane
    pick = lambda a: rsum(jnp.where(hit, row1(a), 0.0))
    vt = pick(first) + vc[:, 0:1] - (pick(cumv) - pick(nvis))
    vlo = jnp.clip(pick(offs) - vt * tmx, 0.0, float(tmx))
    vhi = jnp.clip(pick(ends) - vt * tmx, 0.0, float(tmx))
    changed = lambda a: (v == 0.0) | (a != pltpu.roll(a, 1, 0))
    vfirst = changed(jnp.broadcast_to(vt, (nv, LANES))).astype(F32)
    vnew = changed(jnp.broadcast_to(e_lane, (nv, LANES))).astype(F32)
    vvalid = (v < total).astype(F32)
    nonempty = (cnt > 0.0).astype(F32)
    order = pick(_lane_cumsum(nonempty) - nonempty)
    vslot = order - 2.0 * jnp.floor(order * 0.5)
    far = float(1 << 20)
    nxt = jnp.min(jnp.where((row1(nonempty) > 0.0) & (lane > e_lane), lane, far), axis=-1, keepdims=True)
    vnext = jnp.where(nxt >= far, -1.0, nxt - MOE_GROUPS)
    out = jnp.zeros((nv, LANES), F32)
    cols = (vt, e_lane - MOE_GROUPS, vlo, vhi, vfirst, vvalid, vnew, vnext, vslot)
    for k, col in enumerate(cols):
        out = jnp.where(lane == k, col, out)
    o_ref[...] = out.astype(jnp.int32)


def _moe_schedule(counts, n_rows, tmx):
    nvis_max = n_rows // tmx + MOE_EXPERTS
    nv = 256
    last = counts.shape[0] // 8 - 1
    sched = pl.pallas_call(
        functools.partial(_sched_kernel, tmx=tmx, nv=nv),
        grid=(1,),
        in_specs=[pl.BlockSpec((8, LANES), lambda i: (last, 0))],
        out_specs=pl.BlockSpec((nv, LANES), lambda i: (0, 0)),
        out_shape=jax.ShapeDtypeStruct((nv, LANES), jnp.int32),
        name="sched",
    )(counts)
    return tuple(sched[:nvis_max, k] for k in range(9))


def kernel(x, attn_norm_g, w_in, mla_q_norm_g, mla_kv_norm_g, mla_w_uq, mla_w_ukv, nsa_cmp_pe, nsa_cmp_w1, nsa_cmp_w2, pool_w, pool_b, pool_scale, s5_a_re, s5_a_im, s5_log_dt, s5_b_re, s5_b_im, s5_c_re, s5_c_im, s5_d, s5_glu_w, s5_glu_b, mix_norm_g, w_out, ffn_norm_g, moe_w_group, moe_b_group, moe_w_expert, moe_b_expert, moe_w_gate, moe_w_up, moe_w_down, final_norm_g):
    batch, seq, d = x.shape
    depth = w_in.shape[0]
    t = batch * seq
    xf = x.reshape(t, d)
    qtabs = _rope_tables(MLA_NOPE)
    ktabs = _rope_tables(0)
    ovt, emat = _nsa_consts(512)
    row = lambda v: v.reshape(1, -1)
    tmx = 256
    w_in_all = _w_in_prep(w_in)
    w_out_all = w_out.astype(BF16)

    for l in range(depth):
        wuq = jnp.pad(mla_w_uq[l], ((0, 512 - MLA_Q_LORA), (0, 0))).astype(BF16)
        ukv = mla_w_ukv[l].reshape(HEAD_D, MLA_HEADS, MLA_NOPE + MLA_V)
        wk = jnp.pad(ukv[:, :, :MLA_NOPE], ((0, 0), (0, 0), (0, HEAD_D - MLA_NOPE)))
        wk = wk.reshape(HEAD_D, 512).astype(BF16)
        wv = ukv[:, :, MLA_NOPE:].reshape(HEAD_D, 512).astype(BF16)
        qg = jnp.pad(mla_q_norm_g[l], (0, 512 - MLA_Q_LORA)).reshape(1, 512)
        q_a, k_a, v_a, q_b, kc, vc, kv_b, gates, u_pool, u_s5 = _in_proj(
            xf, row(attn_norm_g[l]), w_in_all, l, qg, row(mla_kv_norm_g[l]), wuq, wk, wv, qtabs, ktabs)

        y_a = _mla_attn(q_a, k_a, v_a, batch)

        pe = jnp.broadcast_to(nsa_cmp_pe[l].reshape(2, 1, CMP_BLOCK * HEAD_D),
                              (2, 8, CMP_BLOCK * HEAD_D)).astype(BF16)
        kcmp, vcmp = _nsa_compress(kc, vc, nsa_cmp_w1[l].astype(BF16), nsa_cmp_w2[l].astype(BF16),
                                   pe, batch)
        y_b = _nsa_attn(q_b, kcmp, vcmp, kv_b, gates, ovt, emat, batch)

        y_c = _pool(u_pool, pool_w[l].astype(BF16), row(pool_b[l]), row(pool_scale[l]), batch)

        wb, ar, ai, wc = _s5_params(s5_a_re[l], s5_a_im[l], s5_log_dt[l], s5_b_re[l], s5_b_im[l],
                                    s5_c_re[l], s5_c_im[l], batch)
        y_d = _s5(u_s5.reshape(batch, seq, W_GROUP), wb, ar, ai, wc, row(s5_d[l]),
                  s5_glu_w[l].astype(BF16), row(s5_glu_b[l])).reshape(t, W_GROUP)

        wr = jnp.concatenate([moe_w_group[l], moe_w_expert[l]], axis=1)
        wr = _pad_cols(wr, LANES)
        wr_hi = wr.astype(BF16)
        wr_lo = (wr - wr_hi.astype(F32)).astype(BF16)
        br = jnp.pad(jnp.concatenate([moe_b_group[l], moe_b_expert[l]]), (0, LANES - 36)).reshape(1, LANES)
        x1, h2, logits = _out_proj(y_a, y_b, y_c, y_d, row(mix_norm_g[l]), w_out_all, l,
                                   xf, row(ffn_norm_g[l]), wr_hi, wr_lo, br)

        meta, meta_t, counts = _router(logits)
        dest = _dest(meta_t, counts)
        dest1, dest2 = dest[0], dest[1]
        sched = _moe_schedule(counts, 2 * t, tmx)
        xs = _dispatch(dest1, dest2, h2)
        ys = _experts(sched, xs, moe_w_gate, moe_w_up, moe_w_down, l, tmx)
        xf = _combine(dest1, dest2, ys, meta, x1, row(final_norm_g), final=(l == depth - 1))

    return xf.reshape(batch, seq, d)
```

```python
import functools
import math

import numpy as np
import jax
import jax.numpy as jnp
from jax import lax
from jax.experimental import pallas as pl
from jax.experimental.pallas import tpu as pltpu

F32 = jnp.float32
BF16 = jnp.bfloat16

D_MODEL = 2048
SEQ = 2048
W_GROUP = 512
LANES = 128
ROW_CHUNKS = D_MODEL // LANES
ROW_PITCH = ROW_CHUNKS + 1
DMA_UNROLL = 4

ROPE_THETA = 500000.0
ROPE_HALF = 16
NEG_INF = -1.0e30
FORCE_SCORE = 1.0e4
EPS = 1e-6

MLA_HEADS = 4
MLA_Q_LORA = 448
MLA_NOPE = 96
MLA_V = 128
HEAD_D = 128

CMP_BLOCK = 32
CMP_STRIDE = 16
SEL_BLOCK = 64
N_SEL = 8
N_LOCAL = 2
WINDOW = 512
N_CMP_PAD = SEQ // CMP_STRIDE
N_BLK = SEQ // SEL_BLOCK

POOL_SIZES = (2, 4, 8, 16)
S5_GROUPS = 32
S5_CH = 16
S5_STATE = 64
S5_WIDTH = S5_GROUPS * S5_STATE

MOE_GROUPS = 4
MOE_EPG = 8
MOE_EXPERTS = 32
MOE_HIDDEN = 512

COL_CQ, COL_NQ, COL_POOL, COL_S5 = 0, 512, 1024, 1536
COL_CKV, COL_KR, COL_KV6, COL_GL = 2048, 2176, 2304, 3072
N_IN_PAD = 3200

VMEM_LIMIT = 56 * 1024 * 1024


def _cparams(sem, vmem=VMEM_LIMIT):
    return pltpu.CompilerParams(dimension_semantics=sem, vmem_limit_bytes=vmem)


def _rms(x, n=None):
    n = x.shape[-1] if n is None else n
    return x * lax.rsqrt(jnp.sum(x * x, axis=-1, keepdims=True) / n + EPS)


def _dot(a, b):
    return jnp.dot(a, b, preferred_element_type=F32)


def _dot_nt(a, b, precision=None):
    return lax.dot_general(a, b, (((1,), (1,)), ((), ())), preferred_element_type=F32,
                           precision=precision)


def _rope(x, c, sa, sb):
    return x * c + pltpu.roll(x, LANES - ROPE_HALF, 1) * sa + pltpu.roll(x, ROPE_HALF, 1) * sb


def _pack_rows(ref, x, n, mask=None, old=False):
    for s in range(ROW_CHUNKS):
        rows = pl.ds(s, n, stride=ROW_PITCH)
        w = x[:, s * LANES:(s + 1) * LANES]
        if mask is not None:
            w = jnp.where(mask, w, ref[rows, :] if old else 0.0)
        ref[rows, :] = w
    if not old:
        ref[pl.ds(ROW_CHUNKS, n, stride=ROW_PITCH), :] = jnp.zeros((n, LANES), F32)


def _unpack_rows(load, n):
    return [load(pl.ds(s, n, stride=ROW_PITCH)) for s in range(ROW_CHUNKS)]


_W_IN_SEGMENTS = ((COL_CQ, 0, 448), (COL_NQ, 608, 512), (COL_POOL, 1900, 512), (COL_S5, 2412, 512),
                  (COL_CKV, 448, 128), (COL_KR, 576, 32), (COL_KV6, 1120, 768), (COL_GL, 1888, 12))


def _w_in_prep_kernel(w_ref, o_ref):
    o_ref[0] = jnp.zeros(o_ref.shape[1:], BF16)
    for dst, src, width in _W_IN_SEGMENTS:
        o_ref[0, :, dst:dst + width] = w_ref[0, :, src:src + width].astype(BF16)


def _w_in_prep(w_in):
    depth, d, n = w_in.shape
    tk = 256
    return pl.pallas_call(
        _w_in_prep_kernel,
        grid=(depth, d // tk),
        in_specs=[pl.BlockSpec((1, tk, n), lambda l, k: (l, k, 0))],
        out_specs=pl.BlockSpec((1, tk, N_IN_PAD), lambda l, k: (l, k, 0)),
        out_shape=jax.ShapeDtypeStruct((depth, d, N_IN_PAD), BF16),
        compiler_params=_cparams(("parallel", "parallel")),
        name="w_in_prep",
    )(w_in)


def _in_proj_kernel(x_ref, g_ref, w_ref, qg_ref, kvg_ref, wuq_ref, wk_ref, wv_ref,
                    qc_ref, qsa_ref, qsb_ref, kc_ref, ksa_ref, ksb_ref,
                    qa_out, ka_out, va_out, qb_out, kcmp_out, vcmp_out, kvb_out, gate_out,
                    pool_out, s5_out):
    scale = 1.0 / math.sqrt(HEAD_D)
    h = (_rms(x_ref[...]) * g_ref[...]).astype(BF16)
    seg = lambda col, width: _dot(h, w_ref[:, col:col + width])
    heads = [slice(n * HEAD_D, (n + 1) * HEAD_D) for n in range(4)]

    qn = (_rms(seg(COL_CQ, 512), MLA_Q_LORA) * qg_ref[...]).astype(BF16)
    q = _dot(qn, wuq_ref[...])
    kvn = (_rms(seg(COL_CKV, LANES)) * kvg_ref[...]).astype(BF16)
    kn = _dot(kvn, wk_ref[...])
    va_out[...] = _dot(kvn, wv_ref[...]).astype(BF16)
    kc, ksa, ksb = kc_ref[...], ksa_ref[...], ksb_ref[...]
    kr = pltpu.roll(_rope(seg(COL_KR, LANES), kc, ksa, ksb), MLA_NOPE, 1)
    for sl in heads:
        qa_out[:, sl] = (_rope(q[:, sl], qc_ref[...], qsa_ref[...], qsb_ref[...]) * scale).astype(BF16)
        ka_out[:, sl] = (kn[:, sl] + kr).astype(BF16)

    qb = seg(COL_NQ, 512)
    for sl in heads:
        qb_out[:, sl] = (_rope(qb[:, sl], kc, ksa, ksb) * scale).astype(BF16)
    kv = seg(COL_KV6, 6 * HEAD_D)
    part = lambda n: kv[:, n * HEAD_D:(n + 1) * HEAD_D]
    kcmp_out[...] = _rope(part(0), kc, ksa, ksb)
    vcmp_out[...] = part(1)
    kvb_out[:, 0:128] = _rope(part(2), kc, ksa, ksb).astype(BF16)
    kvb_out[:, 128:256] = part(3).astype(BF16)
    kvb_out[:, 256:384] = _rope(part(4), kc, ksa, ksb).astype(BF16)
    kvb_out[:, 384:512] = part(5).astype(BF16)
    gate_out[...] = jax.nn.sigmoid(seg(COL_GL, LANES))

    pool_out[...] = seg(COL_POOL, W_GROUP).astype(BF16)
    s5_out[...] = seg(COL_S5, W_GROUP).astype(BF16)


def _in_proj(x, g, w_all, layer, qg, kvg, wuq, wk, wv, qtabs, ktabs):
    t = x.shape[0]
    tm = 256
    nsb = SEQ // tm
    tab = pl.BlockSpec((tm, LANES), lambda i: (i % nsb, 0))
    full = lambda shape: pl.BlockSpec(shape, lambda i: (0, 0))
    out = lambda width: pl.BlockSpec((tm, width), lambda i: (i, 0))
    widths = (512, 512, 512, 512, LANES, LANES, 512, LANES, W_GROUP, W_GROUP)
    dtypes = (BF16, BF16, BF16, BF16, F32, F32, BF16, F32, BF16, BF16)
    return pl.pallas_call(
        _in_proj_kernel,
        grid=(t // tm,),
        in_specs=[pl.BlockSpec((tm, D_MODEL), lambda i: (i, 0)),
                  full((1, D_MODEL)),
                  pl.BlockSpec((None, D_MODEL, N_IN_PAD), lambda i: (layer, 0, 0)),
                  full((1, 512)), full((1, LANES)),
                  full((512, 512)), full((LANES, 512)), full((LANES, 512)),
                  tab, tab, tab, tab, tab, tab],
        out_specs=[out(w) for w in widths],
        out_shape=[jax.ShapeDtypeStruct((t, w), d) for w, d in zip(widths, dtypes)],
        compiler_params=_cparams(("parallel",)),
        name="in_proj",
    )(x, g, w_all, qg, kvg, wuq, wk, wv, *qtabs, *ktabs)


def _fold_lanes(a, op):
    out = a[:, 0:LANES]
    for c in range(1, a.shape[1] // LANES):
        out = op(out, a[:, c * LANES:(c + 1) * LANES])
    return out


def _mla_attn_kernel(q_ref, k_ref, v_ref, o_ref, s_ref, mx_ref, acc_ref, *, tq, tk):
    i = pl.program_id(1)
    nfull = (i * tq) // tk
    t0 = pl.multiple_of(nfull * tk, tk)
    row = i * tq + lax.broadcasted_iota(jnp.int32, (tq, tk), 0)
    col = t0 + lax.broadcasted_iota(jnp.int32, (tq, tk), 1)
    heads = [slice(h * HEAD_D, (h + 1) * HEAD_D) for h in range(MLA_HEADS)]
    chunk = lambda j: pl.ds(pl.multiple_of(j * tk, tk), tk)

    for h, sl in enumerate(heads):
        s = jnp.where(col <= row, _dot_nt(q_ref[:, sl], k_ref[pl.ds(t0, tk), sl]), NEG_INF)
        s_ref[h, nfull] = s
        mx_ref[h] = _fold_lanes(s, jnp.maximum)

    def scores(j, carry):
        for h, sl in enumerate(heads):
            s = _dot_nt(q_ref[:, sl], k_ref[chunk(j), sl])
            s_ref[h, j] = s
            mx_ref[h] = jnp.maximum(mx_ref[h], _fold_lanes(s, jnp.maximum))
        return carry

    lax.fori_loop(0, nfull, scores, 0)
    for h in range(MLA_HEADS):
        m = jnp.max(mx_ref[h], axis=-1, keepdims=True)
        mx_ref[h] = jnp.broadcast_to(m, (tq, LANES))
        acc_ref[h] = jnp.zeros((tq, 2 * HEAD_D), F32)

    ones = jnp.ones((tk, LANES), BF16)

    def values(j, carry):
        for h, sl in enumerate(heads):
            m = mx_ref[h]
            s = s_ref[h, j]
            p = jnp.concatenate([jnp.exp((s[:, c * LANES:(c + 1) * LANES] - m).astype(BF16))
                                 for c in range(tk // LANES)], axis=1)
            v_aug = jnp.concatenate([v_ref[chunk(j), sl], ones], axis=1)
            acc_ref[h] = acc_ref[h] + _dot(p, v_aug)
        return carry

    lax.fori_loop(0, nfull + 1, values, 0)
    for h, sl in enumerate(heads):
        o_ref[:, sl] = (acc_ref[h, :, 0:HEAD_D] / acc_ref[h, :, HEAD_D:2 * HEAD_D]).astype(BF16)


def _mla_attn(q, k, v, batch):
    tq, tk = 512, 512
    nq = SEQ // tq
    return pl.pallas_call(
        functools.partial(_mla_attn_kernel, tq=tq, tk=tk),
        grid=(batch, nq),
        in_specs=[pl.BlockSpec((tq, W_GROUP), lambda b, i: (b * nq + i, 0)),
                  pl.BlockSpec((SEQ, W_GROUP), lambda b, i: (b, 0)),
                  pl.BlockSpec((SEQ, W_GROUP), lambda b, i: (b, 0))],
        out_specs=pl.BlockSpec((tq, W_GROUP), lambda b, i: (b * nq + i, 0)),
        out_shape=jax.ShapeDtypeStruct((batch * SEQ, W_GROUP), BF16),
        scratch_shapes=[pltpu.VMEM((MLA_HEADS, SEQ // tk, tq, tk), F32),
                        pltpu.VMEM((MLA_HEADS, tq, LANES), F32),
                        pltpu.VMEM((MLA_HEADS, tq, 2 * HEAD_D), F32)],
        compiler_params=_cparams(("parallel", "arbitrary")),
        name="mla_attn",
    )(q, k, v)


def _nsa_compress_kernel(xk_ref, xv_ref, w1_ref, w2_ref, pe_ref, k_out, v_out):
    for c, (x_ref, o_ref) in enumerate(((xk_ref, k_out), (xv_ref, v_out))):
        a = jnp.zeros((N_CMP_PAD, HEAD_D), F32)
        b = jnp.zeros((N_CMP_PAD, HEAD_D), F32)
        for r in range(CMP_STRIDE):
            x = x_ref[pl.ds(r, N_CMP_PAD, stride=CMP_STRIDE), :].astype(BF16)
            a = a + _dot(x, w1_ref[c, r * HEAD_D:(r + 1) * HEAD_D, :])
            b = b + _dot(x, w1_ref[c, (CMP_STRIDE + r) * HEAD_D:(CMP_STRIDE + r + 1) * HEAD_D, :])
        b = pltpu.roll(b, N_CMP_PAD - 1, 0)
        pe = _dot(pe_ref[c], w1_ref[c])[0:1, :]
        hid = jax.nn.gelu(a + b + pe)
        o_ref[0] = _dot(hid.astype(BF16), w2_ref[c]).astype(BF16)


def _nsa_compress(xk, xv, w1, w2, pe, batch):
    xspec = pl.BlockSpec((SEQ, HEAD_D), lambda b: (b, 0))
    ospec = pl.BlockSpec((1, N_CMP_PAD, HEAD_D), lambda b: (b, 0, 0))
    return pl.pallas_call(
        _nsa_compress_kernel,
        grid=(batch,),
        in_specs=[xspec, xspec,
                  pl.BlockSpec((2, CMP_BLOCK * HEAD_D, HEAD_D), lambda b: (0, 0, 0)),
                  pl.BlockSpec((2, HEAD_D, HEAD_D), lambda b: (0, 0, 0)),
                  pl.BlockSpec((2, 8, CMP_BLOCK * HEAD_D), lambda b: (0, 0, 0))],
        out_specs=[ospec, ospec],
        out_shape=[jax.ShapeDtypeStruct((batch, N_CMP_PAD, HEAD_D), BF16)] * 2,
        compiler_params=_cparams(("parallel",)),
        name="nsa_compress",
    )(xk, xv, w1, w2, pe)


def _softmax_rows(s):
    m = jnp.max(s, axis=-1, keepdims=True)
    p = jnp.exp(s - m)
    return p / jnp.sum(p, axis=-1, keepdims=True)


def _nsa_attn_kernel(q_ref, kc_ref, vc_ref, kv_ref, g_ref, ovt_ref, e_ref, o_ref,
                     m_ref, acc_ref, s_ref, *, tq, ck):
    i = pl.program_id(1)
    q0 = i * tq
    nh = 4
    qs = jnp.concatenate([q_ref[:, h * HEAD_D:(h + 1) * HEAD_D] for h in range(nh)], axis=0)
    qpos = q0 + lax.broadcasted_iota(jnp.int32, (tq, 1), 0)
    masked = lambda s, ok: (s.reshape(nh, tq, s.shape[1])
                            + jnp.where(ok, 0.0, NEG_INF)[None]).reshape(s.shape)

    n_idx = lax.broadcasted_iota(jnp.int32, (tq, N_CMP_PAD), 1)
    valid_c = n_idx * CMP_STRIDE + (CMP_BLOCK - 1) <= qpos
    pc = _softmax_rows(masked(_dot_nt(qs, kc_ref[0]), valid_c)).reshape(nh, tq, N_CMP_PAD)
    pc = jnp.where(valid_c[None], pc, 0.0)
    psum = pc[0] + pc[1] + pc[2] + pc[3]
    o_c = _dot(pc.reshape(nh * tq, N_CMP_PAD).astype(BF16), vc_ref[0])

    imp_t = _dot_nt(ovt_ref[...], psum, precision=lax.Precision.HIGHEST)
    kblk = lax.broadcasted_iota(jnp.int32, (N_BLK, tq), 0)
    cur = (q0 + lax.broadcasted_iota(jnp.int32, (N_BLK, tq), 1)) // SEL_BLOCK
    forced = (kblk == 0) | ((kblk <= cur) & (kblk > cur - N_LOCAL))
    score = jnp.where(forced, FORCE_SCORE, jnp.where(kblk <= cur, imp_t, -1.0))
    cnt = jnp.zeros((N_BLK, tq), F32)
    for j in range(N_BLK):
        sj = score[j:j + 1, :]
        beats = (sj > score) | ((sj == score) & (kblk > j))
        cnt = cnt + beats.astype(F32)
    sel = jnp.transpose((cnt < N_SEL).astype(F32)).astype(BF16)

    wlen = WINDOW + tq
    w0 = pl.multiple_of(jnp.maximum(q0 - WINDOW, 0), tq)
    sw = _dot_nt(qs, kv_ref[pl.ds(w0, wlen), 256:384])
    kpos = w0 + lax.broadcasted_iota(jnp.int32, (tq, wlen), 1)
    ok = (kpos <= qpos) & (kpos > qpos - WINDOW)
    sw = masked(sw, ok)
    pw = jnp.exp((sw - jnp.max(_fold_lanes(sw, jnp.maximum), axis=-1, keepdims=True)).astype(BF16))
    aug = lambda v: jnp.concatenate([v, jnp.ones(v.shape, BF16)], axis=1)
    ow = _dot(pw, aug(kv_ref[pl.ds(w0, wlen), 384:512]))
    o_w = ow[:, 0:HEAD_D] / ow[:, HEAD_D:2 * HEAD_D]

    nck = q0 // ck + 1
    m_ref[...] = jnp.full(m_ref.shape, NEG_INF, F32)

    def scores(c, carry):
        r0 = pl.multiple_of(c * ck, ck)
        chosen = _dot(sel, e_ref[c])
        kpos = r0 + lax.broadcasted_iota(jnp.int32, (tq, ck), 1)
        ok = (chosen > 0.5) & (kpos <= qpos)
        s = masked(_dot_nt(qs, kv_ref[pl.ds(r0, ck), 0:128]), ok)
        s_ref[c] = s
        m_ref[...] = jnp.maximum(m_ref[...], _fold_lanes(s, jnp.maximum))
        return carry

    lax.fori_loop(0, nck, scores, 0)
    m_ref[...] = jnp.broadcast_to(jnp.max(m_ref[...], axis=-1, keepdims=True), m_ref.shape)
    acc_ref[...] = jnp.zeros(acc_ref.shape, F32)

    def values(c, carry):
        r0 = pl.multiple_of(c * ck, ck)
        m = m_ref[...]
        s = s_ref[c]
        p = jnp.concatenate([jnp.exp((s[:, k * LANES:(k + 1) * LANES] - m).astype(BF16))
                             for k in range(ck // LANES)], axis=1)
        acc_ref[...] = acc_ref[...] + _dot(p, aug(kv_ref[pl.ds(r0, ck), 128:256]))
        return carry

    lax.fori_loop(0, nck, values, 0)
    o_s = acc_ref[:, 0:HEAD_D] / acc_ref[:, HEAD_D:2 * HEAD_D]

    g = g_ref[...]
    for h in range(nh):
        rs = slice(h * tq, (h + 1) * tq)
        o_ref[:, h * HEAD_D:(h + 1) * HEAD_D] = (
            g[:, 3 * h:3 * h + 1] * o_c[rs] + g[:, 3 * h + 1:3 * h + 2] * o_s[rs]
            + g[:, 3 * h + 2:3 * h + 3] * o_w[rs]).astype(BF16)


def _nsa_attn(q, kcmp, vcmp, kv, gates, ovt, emat, batch):
    tq, ck = 512, 512
    nq = SEQ // tq
    return pl.pallas_call(
        functools.partial(_nsa_attn_kernel, tq=tq, ck=ck),
        grid=(batch, nq),
        in_specs=[pl.BlockSpec((tq, 512), lambda b, i: (b * nq + i, 0)),
                  pl.BlockSpec((1, N_CMP_PAD, HEAD_D), lambda b, i: (b, 0, 0)),
                  pl.BlockSpec((1, N_CMP_PAD, HEAD_D), lambda b, i: (b, 0, 0)),
                  pl.BlockSpec((SEQ, 512), lambda b, i: (b, 0)),
                  pl.BlockSpec((tq, LANES), lambda b, i: (b * nq + i, 0)),
                  pl.BlockSpec((N_BLK, N_CMP_PAD), lambda b, i: (0, 0)),
                  pl.BlockSpec((SEQ // ck, N_BLK, ck), lambda b, i: (0, 0, 0))],
        out_specs=pl.BlockSpec((tq, 512), lambda b, i: (b * nq + i, 0)),
        out_shape=jax.ShapeDtypeStruct((batch * SEQ, 512), BF16),
        scratch_shapes=[pltpu.VMEM((4 * tq, LANES), F32),
                        pltpu.VMEM((4 * tq, 2 * HEAD_D), F32),
                        pltpu.VMEM((SEQ // ck, 4 * tq, ck), F32)],
        compiler_params=_cparams(("parallel", "arbitrary")),
        name="nsa_attn",
    )(q, kcmp, vcmp, kv, gates, ovt, emat)


def _pool_kernel(u_ref, w_ref, b_ref, s_ref, o_ref, pad_ref):
    maxw = POOL_SIZES[-1]
    pad_ref[0:maxw, :] = jnp.zeros((maxw, W_GROUP), F32)
    pad_ref[maxw:maxw + SEQ, :] = u_ref[...].astype(F32)
    rc = 512
    for g, w in enumerate(POOL_SIZES):
        sl = slice(g * LANES, (g + 1) * LANES)
        for r in range(SEQ // rc):
            acc = pad_ref[maxw + r * rc:maxw + (r + 1) * rc, sl]
            tok = acc
            for j in range(1, w):
                acc = acc + pad_ref[maxw - j + r * rc:maxw - j + (r + 1) * rc, sl]
            t = r * rc + lax.broadcasted_iota(jnp.int32, (rc, 1), 0)
            cnt = jnp.minimum(t + 1, w).astype(F32)
            d = acc / cnt - tok
            y = _dot(d.astype(BF16), w_ref[g])
            o_ref[r * rc:(r + 1) * rc, sl] = ((y + b_ref[:, sl]) * s_ref[:, sl]).astype(BF16)


def _pool(u, w, b, s, batch):
    return pl.pallas_call(
        _pool_kernel,
        grid=(batch,),
        in_specs=[pl.BlockSpec((SEQ, W_GROUP), lambda i: (i, 0)),
                  pl.BlockSpec((4, LANES, LANES), lambda i: (0, 0, 0)),
                  pl.BlockSpec((1, W_GROUP), lambda i: (0, 0)),
                  pl.BlockSpec((1, W_GROUP), lambda i: (0, 0))],
        out_specs=pl.BlockSpec((SEQ, W_GROUP), lambda i: (i, 0)),
        out_shape=jax.ShapeDtypeStruct((batch * SEQ, W_GROUP), BF16),
        scratch_shapes=[pltpu.VMEM((SEQ + POOL_SIZES[-1], W_GROUP), F32)],
        compiler_params=_cparams(("parallel",)),
        name="pool",
    )(u, w, b, s)


def _s5_kernel(u_ref, wb_ref, ar_ref, ai_ref, wc_ref, d_ref, gw_ref, gb_ref, o_ref,
               bu_ref, st_ref, tm_ref, *, batch, tc):
    @pl.when(pl.program_id(0) == 0)
    def _():
        st_ref[...] = jnp.zeros(st_ref.shape, F32)

    nslab = W_GROUP // LANES
    for b in range(batch):
        for j in range(nslab):
            tm_ref[j, pl.ds(b, tc, stride=batch), :] = u_ref[b, :, j * LANES:(j + 1) * LANES].astype(F32)
    u = jnp.concatenate([tm_ref[j] for j in range(nslab)], axis=1)
    gl = 8 * S5_STATE
    ub = u.astype(BF16)
    for j in range(nslab):
        uj = ub[:, j * LANES:(j + 1) * LANES]
        for k, part in enumerate((0, S5_WIDTH)):
            cs = slice(part + j * gl, part + (j + 1) * gl)
            bu_ref[:, cs] = _dot(uj, wb_ref[j, :, k * gl:(k + 1) * gl])
    lc = 1024
    unroll = 8
    for c in range(S5_WIDTH // lc):
        re = slice(c * lc, (c + 1) * lc)
        im = slice(S5_WIDTH + c * lc, S5_WIDTH + (c + 1) * lc)
        ar, ai = ar_ref[:, re], ai_ref[:, re]

        def body(tb, carry, re=re, im=im, ar=ar, ai=ai):
            xr, xi = carry
            for k in range(unroll):
                r0 = pl.multiple_of((tb * unroll + k) * batch, batch)
                nxr = ar * xr - ai * xi + bu_ref[pl.ds(r0, batch), re]
                nxi = ar * xi + ai * xr + bu_ref[pl.ds(r0, batch), im]
                bu_ref[pl.ds(r0, batch), re] = nxr
                bu_ref[pl.ds(r0, batch), im] = nxi
                xr, xi = nxr, nxi
            return xr, xi

        xr, xi = lax.fori_loop(0, tc // unroll, body, (st_ref[:, re], st_ref[:, im]))
        st_ref[:, re] = xr
        st_ref[:, im] = xi

    ys = []
    for j in range(nslab):
        yj = 0.0
        for k, part in enumerate((0, S5_WIDTH)):
            cs = slice(part + j * gl, part + (j + 1) * gl)
            yj = yj + _dot(bu_ref[:, cs].astype(BF16), wc_ref[j, k * gl:(k + 1) * gl, :])
        ys.append(yj)
    y = jnp.concatenate(ys, axis=1) + d_ref[...] * u
    y = jax.nn.gelu(y)
    z = _dot(y.astype(BF16), gw_ref[...]) + gb_ref[...]
    o = y * jax.nn.sigmoid(z)
    for j in range(nslab):
        tm_ref[j] = o[:, j * LANES:(j + 1) * LANES]
    for b in range(batch):
        for j in range(nslab):
            o_ref[b, :, j * LANES:(j + 1) * LANES] = tm_ref[j, pl.ds(b, tc, stride=batch), :].astype(BF16)


def _s5(u3, wb, ar, ai, wc, d, gw, gb):
    batch = u3.shape[0]
    tc = 64
    rows = tc * batch
    full = lambda shape: pl.BlockSpec(shape, lambda i: (0, 0))
    return pl.pallas_call(
        functools.partial(_s5_kernel, batch=batch, tc=tc),
        grid=(SEQ // tc,),
        in_specs=[pl.BlockSpec((batch, tc, W_GROUP), lambda i: (0, i, 0)),
                  pl.BlockSpec(wb.shape, lambda i: (0, 0, 0)), full((batch, S5_WIDTH)),
                  full((batch, S5_WIDTH)), pl.BlockSpec(wc.shape, lambda i: (0, 0, 0)),
                  full((1, W_GROUP)),
                  full((W_GROUP, W_GROUP)), full((1, W_GROUP))],
        out_specs=pl.BlockSpec((batch, tc, W_GROUP), lambda i: (0, i, 0)),
        out_shape=jax.ShapeDtypeStruct((batch, SEQ, W_GROUP), BF16),
        scratch_shapes=[pltpu.VMEM((rows, 2 * S5_WIDTH), F32),
                        pltpu.VMEM((batch, 2 * S5_WIDTH), F32),
                        pltpu.VMEM((W_GROUP // LANES, rows, LANES), F32)],
        compiler_params=_cparams(("arbitrary",)),
        name="s5",
    )(u3, wb, ar, ai, wc, d, gw, gb)


def _out_proj_kernel(ya_ref, yb_ref, yc_ref, yd_ref, mg_ref, wo_ref, x_ref, fg_ref,
                     wrh_ref, wrl_ref, br_ref, x1_ref, h2_ref, lg_ref, *, tm):
    acc = x_ref[...]
    for gi, y_ref in enumerate((ya_ref, yb_ref, yc_ref, yd_ref)):
        sl = slice(gi * W_GROUP, (gi + 1) * W_GROUP)
        n = (_rms(y_ref[...].astype(F32)) * mg_ref[:, sl]).astype(BF16)
        acc = acc + _dot(n, wo_ref[sl, :])
    x1_ref[...] = acc
    h2 = _rms(acc) * fg_ref[...]
    hi = h2.astype(BF16)
    lo = (h2 - hi.astype(F32)).astype(BF16)
    lg_ref[...] = (_dot(hi, wrh_ref[...]) + _dot(hi, wrl_ref[...]) + _dot(lo, wrh_ref[...])
                   + br_ref[...])
    _pack_rows(h2_ref, h2, tm)


def _out_proj(ya, yb, yc, yd, mg, wo_all, layer, x, fg, wr_hi, wr_lo, br):
    t = x.shape[0]
    tm = 512
    yspec = pl.BlockSpec((tm, W_GROUP), lambda i: (i, 0))
    full = lambda shape: pl.BlockSpec(shape, lambda i: (0, 0))
    return pl.pallas_call(
        functools.partial(_out_proj_kernel, tm=tm),
        grid=(t // tm,),
        in_specs=[yspec, yspec, yspec, yspec, full((1, D_MODEL)),
                  pl.BlockSpec((None, D_MODEL, D_MODEL), lambda i: (layer, 0, 0)),
                  pl.BlockSpec((tm, D_MODEL), lambda i: (i, 0)), full((1, D_MODEL)),
                  full((D_MODEL, LANES)), full((D_MODEL, LANES)), full((1, LANES))],
        out_specs=[pl.BlockSpec((tm, D_MODEL), lambda i: (i, 0)),
                   pl.BlockSpec((tm * ROW_PITCH, LANES), lambda i: (i, 0)),
                   pl.BlockSpec((tm, LANES), lambda i: (i, 0))],
        out_shape=[jax.ShapeDtypeStruct((t, D_MODEL), F32),
                   jax.ShapeDtypeStruct((t * ROW_PITCH, LANES), F32),
                   jax.ShapeDtypeStruct((t, LANES), F32)],
        compiler_params=_cparams(("parallel",)),
        name="out_proj",
    )(ya, yb, yc, yd, mg, wo_all, x, fg, wr_hi, wr_lo, br)


ROUTE_ROWS = 40


def _route(lg, meta_ref, metat_ref, cnt_ref, col_ref, row_ref, tm):
    @pl.when(pl.program_id(0) == 0)
    def _():
        col_ref[...] = jnp.zeros(col_ref.shape, F32)
        row_ref[...] = jnp.zeros(row_ref.shape, F32)

    nr = ROUTE_ROWS
    lt = jnp.transpose(lg)[0:nr, :]
    row = lax.broadcasted_iota(jnp.int32, (nr, tm), 0)
    big = jnp.int32(1 << 20)
    cmax = lambda a: jnp.max(a, axis=0, keepdims=True)
    cmin = lambda a: jnp.min(a, axis=0, keepdims=True)
    csum = lambda a: jnp.sum(a, axis=0, keepdims=True)

    is_g = row < MOE_GROUPS
    gl = jnp.where(is_g, lt, NEG_INF)
    gm = cmax(gl)
    p_top = 1.0 / csum(jnp.where(is_g, jnp.exp(gl - gm), 0.0))
    g_top = cmin(jnp.where(is_g & (gl == gm), row, big))

    is_e = (row >= MOE_GROUPS) & (row < MOE_GROUPS + MOE_EXPERTS) \
        & (((row - MOE_GROUPS) // MOE_EPG) == g_top)
    el = jnp.where(is_e, lt, NEG_INF)
    ee = jnp.where(is_e, jnp.exp(el - cmax(el)), 0.0)
    p = jnp.where(is_e, ee / csum(ee), -1.0)
    p1 = cmax(p)
    i1 = cmin(jnp.where(p == p1, row, big))
    p_rest = jnp.where(row == i1, -1.0, p)
    p2 = cmax(p_rest)
    i2 = cmin(jnp.where((p_rest == p2) & is_e & (row != i1), row, big))
    den = p1 + p2
    w1 = p_top * (p1 / den)
    w2 = p_top * (p2 / den)

    hit1, hit2 = row == i1, row == i2
    oh = (hit1 | hit2).astype(BF16)
    r = lax.broadcasted_iota(jnp.int32, (tm, tm), 0)
    c = lax.broadcasted_iota(jnp.int32, (tm, tm), 1)
    before = _dot(oh, (r < c).astype(BF16)) + col_ref[:, 0:1]
    r1 = csum(jnp.where(hit1, before, 0.0))
    r2 = csum(jnp.where(hit2, before, 0.0))
    col_ref[...] = col_ref[...] + jnp.sum(oh.astype(F32), axis=1, keepdims=True)
    oh_all = jnp.concatenate([oh, jnp.zeros((LANES - nr, tm), BF16)], axis=0)
    row_ref[...] = row_ref[...] + _dot_nt(jnp.ones((8, tm), BF16), oh_all)
    cnt_ref[...] = row_ref[...]

    e1 = (i1 - MOE_GROUPS).astype(F32)
    e2 = (i2 - MOE_GROUPS).astype(F32)
    row8 = lax.broadcasted_iota(jnp.int32, (8, tm), 0)
    meta = jnp.zeros((8, tm), F32)
    for k, v in enumerate((e1, e2, w1, w2, r1, r2)):
        meta = jnp.where(row8 == k, v, meta)
    metat_ref[...] = meta
    meta = jnp.concatenate([meta, jnp.zeros((LANES - 8, tm), F32)], axis=0)
    meta_ref[...] = jnp.transpose(meta)


def _router_kernel(lg_ref, meta_ref, metat_ref, cnt_ref, col_ref, row_ref, *, tm):
    _route(lg_ref[...], meta_ref, metat_ref, cnt_ref, col_ref, row_ref, tm)


def _router(logits):
    t = logits.shape[0]
    tm = 512
    return pl.pallas_call(
        functools.partial(_router_kernel, tm=tm),
        grid=(t // tm,),
        in_specs=[pl.BlockSpec((tm, LANES), lambda i: (i, 0))],
        out_specs=[pl.BlockSpec((tm, LANES), lambda i: (i, 0)),
                   pl.BlockSpec((8, tm), lambda i: (0, i)),
                   pl.BlockSpec((8, LANES), lambda i: (i, 0))],
        out_shape=[jax.ShapeDtypeStruct((t, LANES), F32),
                   jax.ShapeDtypeStruct((8, t), F32),
                   jax.ShapeDtypeStruct((t // tm * 8, LANES), F32)],
        scratch_shapes=[pltpu.VMEM((ROUTE_ROWS, LANES), F32), pltpu.VMEM((8, LANES), F32)],
        compiler_params=_cparams(("arbitrary",)),
        name="router",
    )(logits)


def _dest_kernel(metat_ref, cnt_ref, o_ref, *, tm):
    cnt = cnt_ref[...]
    start = (_lane_cumsum(cnt) - cnt)[0:1, :]
    nr = ROUTE_ROWS
    start_col = jnp.transpose(jnp.broadcast_to(start, (LANES, LANES)))[0:nr, 0:1]
    mt = metat_ref[...]
    row = lax.broadcasted_iota(jnp.int32, (nr, tm), 0)
    row8 = lax.broadcasted_iota(jnp.int32, (8, tm), 0)
    out = jnp.zeros((8, tm), F32)
    for k in range(2):
        e_row = mt[k:k + 1, :].astype(jnp.int32) + MOE_GROUPS
        d = jnp.sum(jnp.where(row == e_row, start_col, 0.0), axis=0, keepdims=True) + mt[4 + k:5 + k, :]
        out = jnp.where(row8 == k, d, out)
    o_ref[...] = out.astype(jnp.int32)


def _dest(meta_t, counts):
    t = meta_t.shape[1]
    tm = 512
    last = counts.shape[0] // 8 - 1
    return pl.pallas_call(
        functools.partial(_dest_kernel, tm=tm),
        grid=(t // tm,),
        in_specs=[pl.BlockSpec((8, tm), lambda i: (0, i)),
                  pl.BlockSpec((8, LANES), lambda i: (last, 0))],
        out_specs=pl.BlockSpec((8, tm), lambda i: (0, i)),
        out_shape=jax.ShapeDtypeStruct((8, t), jnp.int32),
        compiler_params=_cparams(("parallel",)),
        name="dest",
    )(meta_t, counts)


def _dispatch_kernel(d1_ref, d2_ref, h_ref, xs_ref, sem, *, td):
    base = pl.program_id(0) * td

    def start(rb, carry):
        for k in range(DMA_UNROLL):
            r = rb * DMA_UNROLL + k
            src = h_ref.at[pl.ds(r * ROW_PITCH, ROW_PITCH)]
            for d_ref in (d1_ref, d2_ref):
                dst = xs_ref.at[pl.ds(d_ref[base + r] * ROW_PITCH, ROW_PITCH)]
                pltpu.make_async_copy(src, dst, sem).start()
        return carry

    lax.fori_loop(0, td // DMA_UNROLL, start, 0)
    for _ in range(2):
        pltpu.make_async_copy(h_ref, xs_ref.at[pl.ds(0, td * ROW_PITCH)], sem).wait()


def _dispatch(dest1, dest2, h2):
    t = dest1.shape[0]
    td = 1024
    return pl.pallas_call(
        functools.partial(_dispatch_kernel, td=td),
        grid_spec=pltpu.PrefetchScalarGridSpec(
            num_scalar_prefetch=2,
            grid=(t // td,),
            in_specs=[pl.BlockSpec((td * ROW_PITCH, LANES), lambda i, *_: (i, 0))],
            out_specs=pl.BlockSpec(memory_space=pl.ANY),
            scratch_shapes=[pltpu.SemaphoreType.DMA(())]),
        out_shape=jax.ShapeDtypeStruct((2 * t * ROW_PITCH, LANES), F32),
        compiler_params=pltpu.CompilerParams(dimension_semantics=("arbitrary",),
                                             has_side_effects=True),
        name="dispatch",
    )(dest1, dest2, h2)


def _expert_kernel(vt_ref, ve_ref, vlo_ref, vhi_ref, vfirst_ref, vvalid_ref, vnew_ref, vnext_ref,
                   vslot_ref, xs_ref, wg_ref, wu_ref, wd_ref, ys_ref,
                   wgf_ref, wuf_ref, wdf_ref, wgb_ref, wub_ref, wdb_ref, sem, *, tmx, layer):
    v = pl.program_id(0)

    def weight_copies(e, slot):
        return [pltpu.make_async_copy(w_ref.at[layer, e], f_ref.at[slot], sem.at[slot])
                for w_ref, f_ref in ((wg_ref, wgf_ref), (wu_ref, wuf_ref), (wd_ref, wdf_ref))]

    @pl.when(vnew_ref[v] == 1)
    def _():
        slot = vslot_ref[v]

        @pl.when(v == 0)
        def _():
            for cp in weight_copies(ve_ref[v], slot):
                cp.start()

        for cp in weight_copies(ve_ref[v], slot):
            cp.wait()

        @pl.when(vnext_ref[v] >= 0)
        def _():
            for cp in weight_copies(vnext_ref[v], 1 - slot):
                cp.start()

        wgb_ref[...] = wgf_ref[slot].astype(BF16)
        wub_ref[...] = wuf_ref[slot].astype(BF16)
        wdb_ref[...] = wdf_ref[slot].astype(BF16)

    @pl.when(vvalid_ref[v] == 1)
    def _():
        x = jnp.concatenate(_unpack_rows(lambda rows: xs_ref[rows, :], tmx), axis=1).astype(BF16)
        a = _dot(x, wgb_ref[...])
        u = _dot(x, wub_ref[...])
        hid = (jax.nn.silu(a) * u).astype(BF16)
        y = _dot(hid, wdb_ref[...])
        rows = lax.broadcasted_iota(jnp.int32, (tmx, 1), 0)
        mine = (rows >= vlo_ref[v]) & (rows < vhi_ref[v])

        @pl.when(vfirst_ref[v] == 1)
        def _():
            _pack_rows(ys_ref, y, tmx, mask=mine)

        @pl.when(vfirst_ref[v] == 0)
        def _():
            _pack_rows(ys_ref, y, tmx, mask=mine, old=True)


def _experts(sched, xs, wg, wu, wd, layer, tmx):
    nvis = sched[0].shape[0]
    rows = xs.shape[0]
    xspec = pl.BlockSpec((tmx * ROW_PITCH, LANES), lambda v, vt, *_: (vt[v], 0))
    hbm = pl.BlockSpec(memory_space=pl.ANY)
    return pl.pallas_call(
        functools.partial(_expert_kernel, tmx=tmx, layer=layer),
        grid_spec=pltpu.PrefetchScalarGridSpec(
            num_scalar_prefetch=9,
            grid=(nvis,),
            in_specs=[xspec, hbm, hbm, hbm],
            out_specs=xspec,
            scratch_shapes=[pltpu.VMEM((2, D_MODEL, MOE_HIDDEN), F32),
                            pltpu.VMEM((2, D_MODEL, MOE_HIDDEN), F32),
                            pltpu.VMEM((2, MOE_HIDDEN, D_MODEL), F32),
                            pltpu.VMEM((D_MODEL, MOE_HIDDEN), BF16),
                            pltpu.VMEM((D_MODEL, MOE_HIDDEN), BF16),
                            pltpu.VMEM((MOE_HIDDEN, D_MODEL), BF16),
                            pltpu.SemaphoreType.DMA((2,))]),
        out_shape=jax.ShapeDtypeStruct((rows, LANES), F32),
        compiler_params=_cparams(("arbitrary",)),
        name="experts",
    )(*sched, xs, wg, wu, wd)


def _combine_kernel(d1_ref, d2_ref, ys_ref, meta_ref, x1_ref, fg_ref, o_ref,
                    b1_ref, b2_ref, sem, *, tc, final):
    i = pl.program_id(0)
    n = pl.num_programs(0)
    slot = i % 2

    def gather(tile, slot):
        def start(rb, carry):
            for k in range(DMA_UNROLL):
                r = rb * DMA_UNROLL + k
                dst = pl.ds(r * ROW_PITCH, ROW_CHUNKS)
                for d_ref, b_ref in ((d1_ref, b1_ref), (d2_ref, b2_ref)):
                    src = ys_ref.at[pl.ds(d_ref[tile * tc + r] * ROW_PITCH, ROW_CHUNKS)]
                    pltpu.make_async_copy(src, b_ref.at[slot, dst], sem.at[slot]).start()
            return carry

        lax.fori_loop(0, tc // DMA_UNROLL, start, 0)

    @pl.when(i == 0)
    def _():
        gather(0, 0)

    @pl.when(i + 1 < n)
    def _():
        gather(i + 1, 1 - slot)

    for b_ref in (b1_ref, b2_ref):
        pltpu.make_async_copy(ys_ref.at[pl.ds(0, tc * ROW_CHUNKS)],
                              b_ref.at[slot, pl.ds(0, tc * ROW_CHUNKS)], sem.at[slot]).wait()
    w1 = meta_ref[:, 2:3]
    w2 = meta_ref[:, 3:4]
    y1 = _unpack_rows(lambda rows: b1_ref[slot, rows, :], tc)
    y2 = _unpack_rows(lambda rows: b2_ref[slot, rows, :], tc)
    x2 = jnp.concatenate([x1_ref[:, c * LANES:(c + 1) * LANES] + (w1 * y1[c] + w2 * y2[c])
                          for c in range(ROW_CHUNKS)], axis=1)
    if final:
        x2 = _rms(x2) * fg_ref[...]
    o_ref[...] = x2


def _combine(dest1, dest2, ys, meta, x1, fg, final):
    t = x1.shape[0]
    tc = 256
    return pl.pallas_call(
        functools.partial(_combine_kernel, tc=tc, final=final),
        grid_spec=pltpu.PrefetchScalarGridSpec(
            num_scalar_prefetch=2,
            grid=(t // tc,),
            in_specs=[pl.BlockSpec(memory_space=pl.ANY),
                      pl.BlockSpec((tc, LANES), lambda i, *_: (i, 0)),
                      pl.BlockSpec((tc, D_MODEL), lambda i, *_: (i, 0)),
                      pl.BlockSpec((1, D_MODEL), lambda i, *_: (0, 0))],
            out_specs=pl.BlockSpec((tc, D_MODEL), lambda i, *_: (i, 0)),
            scratch_shapes=[pltpu.VMEM((2, tc * ROW_PITCH, LANES), F32),
                            pltpu.VMEM((2, tc * ROW_PITCH, LANES), F32),
                            pltpu.SemaphoreType.DMA((2,))]),
        out_shape=jax.ShapeDtypeStruct((t, D_MODEL), F32),
        compiler_params=_cparams(("arbitrary",)),
        name="combine",
    )(dest1, dest2, ys, meta, x1, fg)


def _rope_tables(r0):
    inv = ROPE_THETA ** (-jnp.arange(0, 2 * ROPE_HALF, 2, dtype=F32) / (2 * ROPE_HALF))
    ang = jnp.arange(SEQ, dtype=F32)[:, None] * inv[None, :]
    cos, sin = jnp.cos(ang), jnp.sin(ang)
    c = jnp.ones((SEQ, LANES), F32).at[:, r0:r0 + ROPE_HALF].set(cos)
    c = c.at[:, r0 + ROPE_HALF:r0 + 2 * ROPE_HALF].set(cos)
    sa = jnp.zeros((SEQ, LANES), F32).at[:, r0:r0 + ROPE_HALF].set(-sin)
    sb = jnp.zeros((SEQ, LANES), F32).at[:, r0 + ROPE_HALF:r0 + 2 * ROPE_HALF].set(sin)
    return c, sa, sb


def _pad_cols(w, width):
    return jnp.pad(w, ((0, 0), (0, width - w.shape[1])))


def _s5_params(a_re, a_im, log_dt, b_re, b_im, c_re, c_im, batch):
    dt = jnp.exp(log_dt)[:, None]
    mag = jnp.exp(a_re * dt)
    abar_r, abar_i = mag * jnp.cos(a_im * dt), mag * jnp.sin(a_im * dt)
    den = a_re * a_re + a_im * a_im
    nr, ni = abar_r - 1.0, abar_i
    coef_r = (nr * a_re + ni * a_im) / den
    coef_i = (ni * a_re - nr * a_im) / den
    bbar_r = coef_r[..., None] * b_re - coef_i[..., None] * b_im
    bbar_i = coef_r[..., None] * b_im + coef_i[..., None] * b_re
    nslab, gps = W_GROUP // LANES, LANES // S5_CH
    eye = jnp.eye(gps, dtype=F32)
    slab = lambda m: m.reshape((nslab, gps) + m.shape[1:])
    blk_b = lambda m: jnp.einsum('jgpc,gh->jgchp', slab(m), eye).reshape(nslab, LANES, gps * S5_STATE)
    wb = jnp.concatenate([blk_b(bbar_r), blk_b(bbar_i)], axis=2).astype(BF16)
    blk_c = lambda m: jnp.einsum('jgcp,gh->jgphc', slab(m), eye).reshape(nslab, gps * S5_STATE, LANES)
    wc = jnp.concatenate([blk_c(c_re), blk_c(-c_im)], axis=1).astype(BF16)
    ar = jnp.broadcast_to(abar_r.reshape(1, S5_WIDTH), (batch, S5_WIDTH))
    ai = jnp.broadcast_to(abar_i.reshape(1, S5_WIDTH), (batch, S5_WIDTH))
    return wb, ar, ai, wc


def _nsa_consts(ck):
    c_start = np.arange(N_CMP_PAD) * CMP_STRIDE
    b_start = np.arange(N_BLK) * SEL_BLOCK
    ov = ((c_start[None, :] < b_start[:, None] + SEL_BLOCK)
          & (c_start[None, :] + CMP_BLOCK > b_start[:, None])).astype(np.float32)
    key_blk = np.arange(SEQ) // SEL_BLOCK
    e = (key_blk[None, :] == np.arange(N_BLK)[:, None]).astype(np.float32)
    e = e.reshape(N_BLK, SEQ // ck, ck).transpose(1, 0, 2)
    return jnp.asarray(ov), jnp.asarray(e, dtype=BF16)


def _lane_cumsum(v):
    r = lax.broadcasted_iota(jnp.int32, (LANES, LANES), 0)
    c = lax.broadcasted_iota(jnp.int32, (LANES, LANES), 1)
    incl = (r <= c).astype(BF16)
    hi = jnp.floor(v * (1.0 / 256.0))
    lo = v - 256.0 * hi
    return 256.0 * _dot(hi.astype(BF16), incl) + _dot(lo.astype(BF16), incl)


def _sched_kernel(cnt_ref, o_ref, *, tmx, nv):
    cnt = cnt_ref[...]
    ends = _lane_cumsum(cnt)
    offs = ends - cnt
    first = jnp.floor(offs * (1.0 / tmx))
    last = jnp.floor(jnp.maximum(ends - 1.0, 0.0) * (1.0 / tmx))
    nvis = jnp.where(cnt > 0.0, last - first + 1.0, 0.0)
    cumv = _lane_cumsum(nvis)
    row1 = lambda a: a[0:1, :]
    total = jnp.max(row1(cumv), axis=-1, keepdims=True)
    rsum = lambda a: jnp.sum(a, axis=-1, keepdims=True)
    v = lax.broadcasted_iota(jnp.int32, (nv, LANES), 0).astype(F32)
    lane = lax.broadcasted_iota(jnp.int32, (nv, LANES), 1).astype(F32)
    vc = jnp.minimum(v, total - 1.0)
    e_lane = rsum((row1(cumv) <= vc).astype(F32))
    hit = lane == e_lane
    pick = lambda a: rsum(jnp.where(hit, row1(a), 0.0))
    vt = pick(first) + vc[:, 0:1] - (pick(cumv) - pick(nvis))
    vlo = jnp.clip(pick(offs) - vt * tmx, 0.0, float(tmx))
    vhi = jnp.clip(pick(ends) - vt * tmx, 0.0, float(tmx))
    changed = lambda a: (v == 0.0) | (a != pltpu.roll(a, 1, 0))
    vfirst = changed(jnp.broadcast_to(vt, (nv, LANES))).astype(F32)
    vnew = changed(jnp.broadcast_to(e_lane, (nv, LANES))).astype(F32)
    vvalid = (v < total).astype(F32)
    nonempty = (cnt > 0.0).astype(F32)
    order = pick(_lane_cumsum(nonempty) - nonempty)
    vslot = order - 2.0 * jnp.floor(order * 0.5)
    far = float(1 << 20)
    nxt = jnp.min(jnp.where((row1(nonempty) > 0.0) & (lane > e_lane), lane, far), axis=-1, keepdims=True)
    vnext = jnp.where(nxt >= far, -1.0, nxt - MOE_GROUPS)
    out = jnp.zeros((nv, LANES), F32)
    cols = (vt, e_lane - MOE_GROUPS, vlo, vhi, vfirst, vvalid, vnew, vnext, vslot)
    for k, col in enumerate(cols):
        out = jnp.where(lane == k, col, out)
    o_ref[...] = out.astype(jnp.int32)


def _moe_schedule(counts, n_rows, tmx):
    nvis_max = n_rows // tmx + MOE_EXPERTS
    nv = 256
    last = counts.shape[0] // 8 - 1
    sched = pl.pallas_call(
        functools.partial(_sched_kernel, tmx=tmx, nv=nv),
        grid=(1,),
        in_specs=[pl.BlockSpec((8, LANES), lambda i: (last, 0))],
        out_specs=pl.BlockSpec((nv, LANES), lambda i: (0, 0)),
        out_shape=jax.ShapeDtypeStruct((nv, LANES), jnp.int32),
        name="sched",
    )(counts)
    return tuple(sched[:nvis_max, k] for k in range(9))


def kernel(x, attn_norm_g, w_in, mla_q_norm_g, mla_kv_norm_g, mla_w_uq, mla_w_ukv, nsa_cmp_pe, nsa_cmp_w1, nsa_cmp_w2, pool_w, pool_b, pool_scale, s5_a_re, s5_a_im, s5_log_dt, s5_b_re, s5_b_im, s5_c_re, s5_c_im, s5_d, s5_glu_w, s5_glu_b, mix_norm_g, w_out, ffn_norm_g, moe_w_group, moe_b_group, moe_w_expert, moe_b_expert, moe_w_gate, moe_w_up, moe_w_down, final_norm_g):
    batch, seq, d = x.shape
    depth = w_in.shape[0]
    t = batch * seq
    xf = x.reshape(t, d)
    qtabs = _rope_tables(MLA_NOPE)
    ktabs = _rope_tables(0)
    ovt, emat = _nsa_consts(512)
    row = lambda v: v.reshape(1, -1)
    tmx = 256
    w_in_all = _w_in_prep(w_in)
    w_out_all = w_out.astype(BF16)

    for l in range(depth):
        wuq = jnp.pad(mla_w_uq[l], ((0, 512 - MLA_Q_LORA), (0, 0))).astype(BF16)
        ukv = mla_w_ukv[l].reshape(HEAD_D, MLA_HEADS, MLA_NOPE + MLA_V)
        wk = jnp.pad(ukv[:, :, :MLA_NOPE], ((0, 0), (0, 0), (0, HEAD_D - MLA_NOPE)))
        wk = wk.reshape(HEAD_D, 512).astype(BF16)
        wv = ukv[:, :, MLA_NOPE:].reshape(HEAD_D, 512).astype(BF16)
        qg = jnp.pad(mla_q_norm_g[l], (0, 512 - MLA_Q_LORA)).reshape(1, 512)
        q_a, k_a, v_a, q_b, kc, vc, kv_b, gates, u_pool, u_s5 = _in_proj(
            xf, row(attn_norm_g[l]), w_in_all, l, qg, row(mla_kv_norm_g[l]), wuq, wk, wv, qtabs, ktabs)

        y_a = _mla_attn(q_a, k_a, v_a, batch)

        pe = jnp.broadcast_to(nsa_cmp_pe[l].reshape(2, 1, CMP_BLOCK * HEAD_D),
                              (2, 8, CMP_BLOCK * HEAD_D)).astype(BF16)
        kcmp, vcmp = _nsa_compress(kc, vc, nsa_cmp_w1[l].astype(BF16), nsa_cmp_w2[l].astype(BF16),
                                   pe, batch)
        y_b = _nsa_attn(q_b, kcmp, vcmp, kv_b, gates, ovt, emat, batch)

        y_c = _pool(u_pool, pool_w[l].astype(BF16), row(pool_b[l]), row(pool_scale[l]), batch)

        wb, ar, ai, wc = _s5_params(s5_a_re[l], s5_a_im[l], s5_log_dt[l], s5_b_re[l], s5_b_im[l],
                                    s5_c_re[l], s5_c_im[l], batch)
        y_d = _s5(u_s5.reshape(batch, seq, W_GROUP), wb, ar, ai, wc, row(s5_d[l]),
                  s5_glu_w[l].astype(BF16), row(s5_glu_b[l])).reshape(t, W_GROUP)

        wr = jnp.concatenate([moe_w_group[l], moe_w_expert[l]], axis=1)
        wr = _pad_cols(wr, LANES)
        wr_hi = wr.astype(BF16)
        wr_lo = (wr - wr_hi.astype(F32)).astype(BF16)
        br = jnp.pad(jnp.concatenate([moe_b_group[l], moe_b_expert[l]]), (0, LANES - 36)).reshape(1, LANES)
        x1, h2, logits = _out_proj(y_a, y_b, y_c, y_d, row(mix_norm_g[l]), w_out_all, l,
                                   xf, row(ffn_norm_g[l]), wr_hi, wr_lo, br)

        meta, meta_t, counts = _router(logits)
        dest = _dest(meta_t, counts)
        dest1, dest2 = dest[0], dest[1]
        sched = _moe_schedule(counts, 2 * t, tmx)
        xs = _dispatch(dest1, dest2, h2)
        ys = _experts(sched, xs, moe_w_gate, moe_w_up, moe_w_down, l, tmx)
        xf = _combine(dest1, dest2, ys, meta, x1, row(final_norm_g), final=(l == depth - 1))

    return xf.reshape(batch, seq, d)
```

```python
import functools
import math

import numpy as np
import jax
import jax.numpy as jnp
from jax import lax
from jax.experimental import pallas as pl
from jax.experimental.pallas import tpu as pltpu

F32 = jnp.float32
BF16 = jnp.bfloat16

D_MODEL = 2048
SEQ = 2048
W_GROUP = 512
LANES = 128
ROW_CHUNKS = D_MODEL // LANES
ROW_PITCH = ROW_CHUNKS + 1
DMA_UNROLL = 4

ROPE_THETA = 500000.0
ROPE_HALF = 16
NEG_INF = -1.0e30
FORCE_SCORE = 1.0e4
EPS = 1e-6

MLA_HEADS = 4
MLA_Q_LORA = 448
MLA_NOPE = 96
MLA_V = 128
HEAD_D = 128

CMP_BLOCK = 32
CMP_STRIDE = 16
SEL_BLOCK = 64
N_SEL = 8
N_LOCAL = 2
WINDOW = 512
N_CMP_PAD = SEQ // CMP_STRIDE
N_BLK = SEQ // SEL_BLOCK

POOL_SIZES = (2, 4, 8, 16)
S5_GROUPS = 32
S5_CH = 16
S5_STATE = 64
S5_WIDTH = S5_GROUPS * S5_STATE

MOE_GROUPS = 4
MOE_EPG = 8
MOE_EXPERTS = 32
MOE_HIDDEN = 512

COL_CQ, COL_NQ, COL_POOL, COL_S5 = 0, 512, 1024, 1536
COL_CKV, COL_KR, COL_KV6, COL_GL = 2048, 2176, 2304, 3072
N_IN_PAD = 3200

VMEM_LIMIT = 56 * 1024 * 1024


def _cparams(sem, vmem=VMEM_LIMIT):
    return pltpu.CompilerParams(dimension_semantics=sem, vmem_limit_bytes=vmem)


def _rms(x, n=None):
    n = x.shape[-1] if n is None else n
    return x * lax.rsqrt(jnp.sum(x * x, axis=-1, keepdims=True) / n + EPS)


def _dot(a, b):
    return jnp.dot(a, b, preferred_element_type=F32)


def _dot_nt(a, b, precision=None):
    return lax.dot_general(a, b, (((1,), (1,)), ((), ())), preferred_element_type=F32,
                           precision=precision)


def _rope(x, c, sa, sb):
    return x * c + pltpu.roll(x, LANES - ROPE_HALF, 1) * sa + pltpu.roll(x, ROPE_HALF, 1) * sb


def _pack_rows(ref, x, n, mask=None, old=False):
    for s in range(ROW_CHUNKS):
        rows = pl.ds(s, n, stride=ROW_PITCH)
        w = x[:, s * LANES:(s + 1) * LANES]
        if mask is not None:
            w = jnp.where(mask, w, ref[rows, :] if old else 0.0)
        ref[rows, :] = w
    if not old:
        ref[pl.ds(ROW_CHUNKS, n, stride=ROW_PITCH), :] = jnp.zeros((n, LANES), F32)


def _unpack_rows(load, n):
    return [load(pl.ds(s, n, stride=ROW_PITCH)) for s in range(ROW_CHUNKS)]


_W_IN_SEGMENTS = ((COL_CQ, 0, 448), (COL_NQ, 608, 512), (COL_POOL, 1900, 512), (COL_S5, 2412, 512),
                  (COL_CKV, 448, 128), (COL_KR, 576, 32), (COL_KV6, 1120, 768), (COL_GL, 1888, 12))


def _w_in_prep_kernel(w_ref, o_ref):
    o_ref[0] = jnp.zeros(o_ref.shape[1:], BF16)
    for dst, src, width in _W_IN_SEGMENTS:
        o_ref[0, :, dst:dst + width] = w_ref[0, :, src:src + width].astype(BF16)


def _w_in_prep(w_in):
    depth, d, n = w_in.shape
    tk = 256
    return pl.pallas_call(
        _w_in_prep_kernel,
        grid=(depth, d // tk),
        in_specs=[pl.BlockSpec((1, tk, n), lambda l, k: (l, k, 0))],
        out_specs=pl.BlockSpec((1, tk, N_IN_PAD), lambda l, k: (l, k, 0)),
        out_shape=jax.ShapeDtypeStruct((depth, d, N_IN_PAD), BF16),
        compiler_params=_cparams(("parallel", "parallel")),
        name="w_in_prep",
    )(w_in)


def _in_proj_kernel(x_ref, g_ref, w_ref, qg_ref, kvg_ref, wuq_ref, wk_ref, wv_ref,
                    qc_ref, qsa_ref, qsb_ref, kc_ref, ksa_ref, ksb_ref,
                    qa_out, ka_out, va_out, qb_out, kcmp_out, vcmp_out, kvb_out, gate_out,
                    pool_out, s5_out):
    scale = 1.0 / math.sqrt(HEAD_D)
    h = (_rms(x_ref[...]) * g_ref[...]).astype(BF16)
    seg = lambda col, width: _dot(h, w_ref[:, col:col + width])
    heads = [slice(n * HEAD_D, (n + 1) * HEAD_D) for n in range(4)]

    qn = (_rms(seg(COL_CQ, 512), MLA_Q_LORA) * qg_ref[...]).astype(BF16)
    q = _dot(qn, wuq_ref[...])
    kvn = (_rms(seg(COL_CKV, LANES)) * kvg_ref[...]).astype(BF16)
    kn = _dot(kvn, wk_ref[...])
    va_out[...] = _dot(kvn, wv_ref[...]).astype(BF16)
    kc, ksa, ksb = kc_ref[...], ksa_ref[...], ksb_ref[...]
    kr = pltpu.roll(_rope(seg(COL_KR, LANES), kc, ksa, ksb), MLA_NOPE, 1)
    for sl in heads:
        qa_out[:, sl] = (_rope(q[:, sl], qc_ref[...], qsa_ref[...], qsb_ref[...]) * scale).astype(BF16)
        ka_out[:, sl] = (kn[:, sl] + kr).astype(BF16)

    qb = seg(COL_NQ, 512)
    for sl in heads:
        qb_out[:, sl] = (_rope(qb[:, sl], kc, ksa, ksb) * scale).astype(BF16)
    kv = seg(COL_KV6, 6 * HEAD_D)
    part = lambda n: kv[:, n * HEAD_D:(n + 1) * HEAD_D]
    kcmp_out[...] = _rope(part(0), kc, ksa, ksb)
    vcmp_out[...] = part(1)
    kvb_out[:, 0:128] = _rope(part(2), kc, ksa, ksb).astype(BF16)
    kvb_out[:, 128:256] = part(3).astype(BF16)
    kvb_out[:, 256:384] = _rope(part(4), kc, ksa, ksb).astype(BF16)
    kvb_out[:, 384:512] = part(5).astype(BF16)
    gate_out[...] = jax.nn.sigmoid(seg(COL_GL, LANES))

    pool_out[...] = seg(COL_POOL, W_GROUP).astype(BF16)
    s5_out[...] = seg(COL_S5, W_GROUP).astype(BF16)


def _in_proj(x, g, w_all, layer, qg, kvg, wuq, wk, wv, qtabs, ktabs):
    t = x.shape[0]
    tm = 256
    nsb = SEQ // tm
    tab = pl.BlockSpec((tm, LANES), lambda i: (i % nsb, 0))
    full = lambda shape: pl.BlockSpec(shape, lambda i: (0, 0))
    out = lambda width: pl.BlockSpec((tm, width), lambda i: (i, 0))
    widths = (512, 512, 512, 512, LANES, LANES, 512, LANES, W_GROUP, W_GROUP)
    dtypes = (BF16, BF16, BF16, BF16, F32, F32, BF16, F32, BF16, BF16)
    return pl.pallas_call(
        _in_proj_kernel,
        grid=(t // tm,),
        in_specs=[pl.BlockSpec((tm, D_MODEL), lambda i: (i, 0)),
                  full((1, D_MODEL)),
                  pl.BlockSpec((None, D_MODEL, N_IN_PAD), lambda i: (layer, 0, 0)),
                  full((1, 512)), full((1, LANES)),
                  full((512, 512)), full((LANES, 512)), full((LANES, 512)),
                  tab, tab, tab, tab, tab, tab],
        out_specs=[out(w) for w in widths],
        out_shape=[jax.ShapeDtypeStruct((t, w), d) for w, d in zip(widths, dtypes)],
        compiler_params=_cparams(("parallel",)),
        name="in_proj",
    )(x, g, w_all, qg, kvg, wuq, wk, wv, *qtabs, *ktabs)


def _fold_lanes(a, op):
    out = a[:, 0:LANES]
    for c in range(1, a.shape[1] // LANES):
        out = op(out, a[:, c * LANES:(c + 1) * LANES])
    return out


def _mla_attn_kernel(q_ref, k_ref, v_ref, o_ref, s_ref, mx_ref, acc_ref, *, tq, tk):
    i = pl.program_id(1)
    nfull = (i * tq) // tk
    t0 = pl.multiple_of(nfull * tk, tk)
    row = i * tq + lax.broadcasted_iota(jnp.int32, (tq, tk), 0)
    col = t0 + lax.broadcasted_iota(jnp.int32, (tq, tk), 1)
    heads = [slice(h * HEAD_D, (h + 1) * HEAD_D) for h in range(MLA_HEADS)]
    chunk = lambda j: pl.ds(pl.multiple_of(j * tk, tk), tk)

    for h, sl in enumerate(heads):
        s = jnp.where(col <= row, _dot_nt(q_ref[:, sl], k_ref[pl.ds(t0, tk), sl]), NEG_INF)
        s_ref[h, nfull] = s
        mx_ref[h] = _fold_lanes(s, jnp.maximum)

    def scores(j, carry):
        for h, sl in enumerate(heads):
            s = _dot_nt(q_ref[:, sl], k_ref[chunk(j), sl])
            s_ref[h, j] = s
            mx_ref[h] = jnp.maximum(mx_ref[h], _fold_lanes(s, jnp.maximum))
        return carry

    lax.fori_loop(0, nfull, scores, 0)
    for h in range(MLA_HEADS):
        m = jnp.max(mx_ref[h], axis=-1, keepdims=True)
        mx_ref[h] = jnp.broadcast_to(m, (tq, LANES))
        acc_ref[h] = jnp.zeros((tq, 2 * HEAD_D), F32)

    ones = jnp.ones((tk, LANES), BF16)

    def values(j, carry):
        for h, sl in enumerate(heads):
            m = mx_ref[h]
            s = s_ref[h, j]
            p = jnp.concatenate([jnp.exp((s[:, c * LANES:(c + 1) * LANES] - m).astype(BF16))
                                 for c in range(tk // LANES)], axis=1)
            v_aug = jnp.concatenate([v_ref[chunk(j), sl], ones], axis=1)
            acc_ref[h] = acc_ref[h] + _dot(p, v_aug)
        return carry

    lax.fori_loop(0, nfull + 1, values, 0)
    for h, sl in enumerate(heads):
        o_ref[:, sl] = (acc_ref[h, :, 0:HEAD_D] / acc_ref[h, :, HEAD_D:2 * HEAD_D]).astype(BF16)


def _mla_attn(q, k, v, batch):
    tq, tk = 512, 512
    nq = SEQ // tq
    return pl.pallas_call(
        functools.partial(_mla_attn_kernel, tq=tq, tk=tk),
        grid=(batch, nq),
        in_specs=[pl.BlockSpec((tq, W_GROUP), lambda b, i: (b * nq + i, 0)),
                  pl.BlockSpec((SEQ, W_GROUP), lambda b, i: (b, 0)),
                  pl.BlockSpec((SEQ, W_GROUP), lambda b, i: (b, 0))],
        out_specs=pl.BlockSpec((tq, W_GROUP), lambda b, i: (b * nq + i, 0)),
        out_shape=jax.ShapeDtypeStruct((batch * SEQ, W_GROUP), BF16),
        scratch_shapes=[pltpu.VMEM((MLA_HEADS, SEQ // tk, tq, tk), F32),
                        pltpu.VMEM((MLA_HEADS, tq, LANES), F32),
                        pltpu.VMEM((MLA_HEADS, tq, 2 * HEAD_D), F32)],
        compiler_params=_cparams(("parallel", "arbitrary")),
        name="mla_attn",
    )(q, k, v)


def _nsa_compress_kernel(xk_ref, xv_ref, w1_ref, w2_ref, pe_ref, k_out, v_out):
    for c, (x_ref, o_ref) in enumerate(((xk_ref, k_out), (xv_ref, v_out))):
        a = jnp.zeros((N_CMP_PAD, HEAD_D), F32)
        b = jnp.zeros((N_CMP_PAD, HEAD_D), F32)
        for r in range(CMP_STRIDE):
            x = x_ref[pl.ds(r, N_CMP_PAD, stride=CMP_STRIDE), :].astype(BF16)
            a = a + _dot(x, w1_ref[c, r * HEAD_D:(r + 1) * HEAD_D, :])
            b = b + _dot(x, w1_ref[c, (CMP_STRIDE + r) * HEAD_D:(CMP_STRIDE + r + 1) * HEAD_D, :])
        b = pltpu.roll(b, N_CMP_PAD - 1, 0)
        pe = _dot(pe_ref[c], w1_ref[c])[0:1, :]
        hid = jax.nn.gelu(a + b + pe)
        o_ref[0] = _dot(hid.astype(BF16), w2_ref[c]).astype(BF16)


def _nsa_compress(xk, xv, w1, w2, pe, batch):
    xspec = pl.BlockSpec((SEQ, HEAD_D), lambda b: (b, 0))
    ospec = pl.BlockSpec((1, N_CMP_PAD, HEAD_D), lambda b: (b, 0, 0))
    return pl.pallas_call(
        _nsa_compress_kernel,
        grid=(batch,),
        in_specs=[xspec, xspec,
                  pl.BlockSpec((2, CMP_BLOCK * HEAD_D, HEAD_D), lambda b: (0, 0, 0)),
                  pl.BlockSpec((2, HEAD_D, HEAD_D), lambda b: (0, 0, 0)),
                  pl.BlockSpec((2, 8, CMP_BLOCK * HEAD_D), lambda b: (0, 0, 0))],
        out_specs=[ospec, ospec],
        out_shape=[jax.ShapeDtypeStruct((batch, N_CMP_PAD, HEAD_D), BF16)] * 2,
        compiler_params=_cparams(("parallel",)),
        name="nsa_compress",
    )(xk, xv, w1, w2, pe)


def _softmax_rows(s):
    m = jnp.max(s, axis=-1, keepdims=True)
    p = jnp.exp(s - m)
    return p / jnp.sum(p, axis=-1, keepdims=True)


def _nsa_attn_kernel(q_ref, kc_ref, vc_ref, kv_ref, g_ref, ovt_ref, e_ref, o_ref,
                     m_ref, acc_ref, s_ref, *, tq, ck):
    i = pl.program_id(1)
    q0 = i * tq
    nh = 4
    qs = jnp.concatenate([q_ref[:, h * HEAD_D:(h + 1) * HEAD_D] for h in range(nh)], axis=0)
    qpos = q0 + lax.broadcasted_iota(jnp.int32, (tq, 1), 0)
    masked = lambda s, ok: (s.reshape(nh, tq, s.shape[1])
                            + jnp.where(ok, 0.0, NEG_INF)[None]).reshape(s.shape)

    n_idx = lax.broadcasted_iota(jnp.int32, (tq, N_CMP_PAD), 1)
    valid_c = n_idx * CMP_STRIDE + (CMP_BLOCK - 1) <= qpos
    pc = _softmax_rows(masked(_dot_nt(qs, kc_ref[0]), valid_c)).reshape(nh, tq, N_CMP_PAD)
    pc = jnp.where(valid_c[None], pc, 0.0)
    psum = pc[0] + pc[1] + pc[2] + pc[3]
    o_c = _dot(pc.reshape(nh * tq, N_CMP_PAD).astype(BF16), vc_ref[0])

    imp_t = _dot_nt(ovt_ref[...], psum, precision=lax.Precision.HIGHEST)
    kblk = lax.broadcasted_iota(jnp.int32, (N_BLK, tq), 0)
    cur = (q0 + lax.broadcasted_iota(jnp.int32, (N_BLK, tq), 1)) // SEL_BLOCK
    forced = (kblk == 0) | ((kblk <= cur) & (kblk > cur - N_LOCAL))
    score = jnp.where(forced, FORCE_SCORE, jnp.where(kblk <= cur, imp_t, -1.0))
    cnt = jnp.zeros((N_BLK, tq), F32)
    for j in range(N_BLK):
        sj = score[j:j + 1, :]
        beats = (sj > score) | ((sj == score) & (kblk > j))
        cnt = cnt + beats.astype(F32)
    sel = jnp.transpose((cnt < N_SEL).astype(F32)).astype(BF16)

    wlen = WINDOW + tq
    w0 = pl.multiple_of(jnp.maximum(q0 - WINDOW, 0), tq)
    sw = _dot_nt(qs, kv_ref[pl.ds(w0, wlen), 256:384])
    kpos = w0 + lax.broadcasted_iota(jnp.int32, (tq, wlen), 1)
    ok = (kpos <= qpos) & (kpos > qpos - WINDOW)
    sw = masked(sw, ok)
    pw = jnp.exp((sw - jnp.max(_fold_lanes(sw, jnp.maximum), axis=-1, keepdims=True)).astype(BF16))
    aug = lambda v: jnp.concatenate([v, jnp.ones(v.shape, BF16)], axis=1)
    ow = _dot(pw, aug(kv_ref[pl.ds(w0, wlen), 384:512]))
    o_w = ow[:, 0:HEAD_D] / ow[:, HEAD_D:2 * HEAD_D]

    nck = q0 // ck + 1
    m_ref[...] = jnp.full(m_ref.shape, NEG_INF, F32)

    def scores(c, carry):
        r0 = pl.multiple_of(c * ck, ck)
        chosen = _dot(sel, e_ref[c])
        kpos = r0 + lax.broadcasted_iota(jnp.int32, (tq, ck), 1)
        ok = (chosen > 0.5) & (kpos <= qpos)
        s = masked(_dot_nt(qs, kv_ref[pl.ds(r0, ck), 0:128]), ok)
        s_ref[c] = s
        m_ref[...] = jnp.maximum(m_ref[...], _fold_lanes(s, jnp.maximum))
        return carry

    lax.fori_loop(0, nck, scores, 0)
    m_ref[...] = jnp.broadcast_to(jnp.max(m_ref[...], axis=-1, keepdims=True), m_ref.shape)
    acc_ref[...] = jnp.zeros(acc_ref.shape, F32)

    def values(c, carry):
        r0 = pl.multiple_of(c * ck, ck)
        m = m_ref[...]
        s = s_ref[c]
        p = jnp.concatenate([jnp.exp((s[:, k * LANES:(k + 1) * LANES] - m).astype(BF16))
                             for k in range(ck // LANES)], axis=1)
        acc_ref[...] = acc_ref[...] + _dot(p, aug(kv_ref[pl.ds(r0, ck), 128:256]))
        return carry

    lax.fori_loop(0, nck, values, 0)
    o_s = acc_ref[:, 0:HEAD_D] / acc_ref[:, HEAD_D:2 * HEAD_D]

    g = g_ref[...]
    for h in range(nh):
        rs = slice(h * tq, (h + 1) * tq)
        o_ref[:, h * HEAD_D:(h + 1) * HEAD_D] = (
            g[:, 3 * h:3 * h + 1] * o_c[rs] + g[:, 3 * h + 1:3 * h + 2] * o_s[rs]
            + g[:, 3 * h + 2:3 * h + 3] * o_w[rs]).astype(BF16)


def _nsa_attn(q, kcmp, vcmp, kv, gates, ovt, emat, batch):
    tq, ck = 512, 512
    nq = SEQ // tq
    return pl.pallas_call(
        functools.partial(_nsa_attn_kernel, tq=tq, ck=ck),
        grid=(batch, nq),
        in_specs=[pl.BlockSpec((tq, 512), lambda b, i: (b * nq + i, 0)),
                  pl.BlockSpec((1, N_CMP_PAD, HEAD_D), lambda b, i: (b, 0, 0)),
                  pl.BlockSpec((1, N_CMP_PAD, HEAD_D), lambda b, i: (b, 0, 0)),
                  pl.BlockSpec((SEQ, 512), lambda b, i: (b, 0)),
                  pl.BlockSpec((tq, LANES), lambda b, i: (b * nq + i, 0)),
                  pl.BlockSpec((N_BLK, N_CMP_PAD), lambda b, i: (0, 0)),
                  pl.BlockSpec((SEQ // ck, N_BLK, ck), lambda b, i: (0, 0, 0))],
        out_specs=pl.BlockSpec((tq, 512), lambda b, i: (b * nq + i, 0)),
        out_shape=jax.ShapeDtypeStruct((batch * SEQ, 512), BF16),
        scratch_shapes=[pltpu.VMEM((4 * tq, LANES), F32),
                        pltpu.VMEM((4 * tq, 2 * HEAD_D), F32),
                        pltpu.VMEM((SEQ // ck, 4 * tq, ck), F32)],
        compiler_params=_cparams(("parallel", "arbitrary")),
        name="nsa_attn",
    )(q, kcmp, vcmp, kv, gates, ovt, emat)


def _pool_kernel(u_ref, w_ref, b_ref, s_ref, o_ref, pad_ref):
    maxw = POOL_SIZES[-1]
    pad_ref[0:maxw, :] = jnp.zeros((maxw, W_GROUP), F32)
    pad_ref[maxw:maxw + SEQ, :] = u_ref[...].astype(F32)
    rc = 512
    for g, w in enumerate(POOL_SIZES):
        sl = slice(g * LANES, (g + 1) * LANES)
        for r in range(SEQ // rc):
            acc = pad_ref[maxw + r * rc:maxw + (r + 1) * rc, sl]
            tok = acc
            for j in range(1, w):
                acc = acc + pad_ref[maxw - j + r * rc:maxw - j + (r + 1) * rc, sl]
            t = r * rc + lax.broadcasted_iota(jnp.int32, (rc, 1), 0)
            cnt = jnp.minimum(t + 1, w).astype(F32)
            d = acc / cnt - tok
            y = _dot(d.astype(BF16), w_ref[g])
            o_ref[r * rc:(r + 1) * rc, sl] = ((y + b_ref[:, sl]) * s_ref[:, sl]).astype(BF16)


def _pool(u, w, b, s, batch):
    return pl.pallas_call(
        _pool_kernel,
        grid=(batch,),
        in_specs=[pl.BlockSpec((SEQ, W_GROUP), lambda i: (i, 0)),
                  pl.BlockSpec((4, LANES, LANES), lambda i: (0, 0, 0)),
                  pl.BlockSpec((1, W_GROUP), lambda i: (0, 0)),
                  pl.BlockSpec((1, W_GROUP), lambda i: (0, 0))],
        out_specs=pl.BlockSpec((SEQ, W_GROUP), lambda i: (i, 0)),
        out_shape=jax.ShapeDtypeStruct((batch * SEQ, W_GROUP), BF16),
        scratch_shapes=[pltpu.VMEM((SEQ + POOL_SIZES[-1], W_GROUP), F32)],
        compiler_params=_cparams(("parallel",)),
        name="pool",
    )(u, w, b, s)


def _s5_kernel(u_ref, wb_ref, ar_ref, ai_ref, wc_ref, d_ref, gw_ref, gb_ref, o_ref,
               bu_ref, st_ref, tm_ref, *, batch, tc):
    @pl.when(pl.program_id(0) == 0)
    def _():
        st_ref[...] = jnp.zeros(st_ref.shape, F32)

    nslab = W_GROUP // LANES
    for b in range(batch):
        for j in range(nslab):
            tm_ref[j, pl.ds(b, tc, stride=batch), :] = u_ref[b, :, j * LANES:(j + 1) * LANES].astype(F32)
    u = jnp.concatenate([tm_ref[j] for j in range(nslab)], axis=1)
    gl = 8 * S5_STATE
    ub = u.astype(BF16)
    for j in range(nslab):
        uj = ub[:, j * LANES:(j + 1) * LANES]
        for k, part in enumerate((0, S5_WIDTH)):
            cs = slice(part + j * gl, part + (j + 1) * gl)
            bu_ref[:, cs] = _dot(uj, wb_ref[j, :, k * gl:(k + 1) * gl])
    lc = 1024
    unroll = 8
    for c in range(S5_WIDTH // lc):
        re = slice(c * lc, (c + 1) * lc)
        im = slice(S5_WIDTH + c * lc, S5_WIDTH + (c + 1) * lc)
        ar, ai = ar_ref[:, re], ai_ref[:, re]

        def body(tb, carry, re=re, im=im, ar=ar, ai=ai):
            xr, xi = carry
            for k in range(unroll):
                r0 = pl.multiple_of((tb * unroll + k) * batch, batch)
                nxr = ar * xr - ai * xi + bu_ref[pl.ds(r0, batch), re]
                nxi = ar * xi + ai * xr + bu_ref[pl.ds(r0, batch), im]
                bu_ref[pl.ds(r0, batch), re] = nxr
                bu_ref[pl.ds(r0, batch), im] = nxi
                xr, xi = nxr, nxi
            return xr, xi

        xr, xi = lax.fori_loop(0, tc // unroll, body, (st_ref[:, re], st_ref[:, im]))
        st_ref[:, re] = xr
        st_ref[:, im] = xi

    ys = []
    for j in range(nslab):
        yj = 0.0
        for k, part in enumerate((0, S5_WIDTH)):
            cs = slice(part + j * gl, part + (j + 1) * gl)
            yj = yj + _dot(bu_ref[:, cs].astype(BF16), wc_ref[j, k * gl:(k + 1) * gl, :])
        ys.append(yj)
    y = jnp.concatenate(ys, axis=1) + d_ref[...] * u
    y = jax.nn.gelu(y)
    z = _dot(y.astype(BF16), gw_ref[...]) + gb_ref[...]
    o = y * jax.nn.sigmoid(z)
    for j in range(nslab):
        tm_ref[j] = o[:, j * LANES:(j + 1) * LANES]
    for b in range(batch):
        for j in range(nslab):
            o_ref[b, :, j * LANES:(j + 1) * LANES] = tm_ref[j, pl.ds(b, tc, stride=batch), :].astype(BF16)


def _s5(u3, wb, ar, ai, wc, d, gw, gb):
    batch = u3.shape[0]
    tc = 64
    rows = tc * batch
    full = lambda shape: pl.BlockSpec(shape, lambda i: (0, 0))
    return pl.pallas_call(
        functools.partial(_s5_kernel, batch=batch, tc=tc),
        grid=(SEQ // tc,),
        in_specs=[pl.BlockSpec((batch, tc, W_GROUP), lambda i: (0, i, 0)),
                  pl.BlockSpec(wb.shape, lambda i: (0, 0, 0)), full((batch, S5_WIDTH)),
                  full((batch, S5_WIDTH)), pl.BlockSpec(wc.shape, lambda i: (0, 0, 0)),
                  full((1, W_GROUP)),
                  full((W_GROUP, W_GROUP)), full((1, W_GROUP))],
        out_specs=pl.BlockSpec((batch, tc, W_GROUP), lambda i: (0, i, 0)),
        out_shape=jax.ShapeDtypeStruct((batch, SEQ, W_GROUP), BF16),
        scratch_shapes=[pltpu.VMEM((rows, 2 * S5_WIDTH), F32),
                        pltpu.VMEM((batch, 2 * S5_WIDTH), F32),
                        pltpu.VMEM((W_GROUP // LANES, rows, LANES), F32)],
        compiler_params=_cparams(("arbitrary",)),
        name="s5",
    )(u3, wb, ar, ai, wc, d, gw, gb)


def _out_proj_kernel(ya_ref, yb_ref, yc_ref, yd_ref, mg_ref, wo_ref, x_ref, fg_ref,
                     wrh_ref, wrl_ref, br_ref, x1_ref, h2_ref, lg_ref, *, tm):
    acc = x_ref[...]
    for gi, y_ref in enumerate((ya_ref, yb_ref, yc_ref, yd_ref)):
        sl = slice(gi * W_GROUP, (gi + 1) * W_GROUP)
        n = (_rms(y_ref[...].astype(F32)) * mg_ref[:, sl]).astype(BF16)
        acc = acc + _dot(n, wo_ref[sl, :])
    x1_ref[...] = acc
    h2 = _rms(acc) * fg_ref[...]
    hi = h2.astype(BF16)
    lo = (h2 - hi.astype(F32)).astype(BF16)
    lg_ref[...] = (_dot(hi, wrh_ref[...]) + _dot(hi, wrl_ref[...]) + _dot(lo, wrh_ref[...])
                   + br_ref[...])
    _pack_rows(h2_ref, h2, tm)


def _out_proj(ya, yb, yc, yd, mg, wo_all, layer, x, fg, wr_hi, wr_lo, br):
    t = x.shape[0]
    tm = 512
    yspec = pl.BlockSpec((tm, W_GROUP), lambda i: (i, 0))
    full = lambda shape: pl.BlockSpec(shape, lambda i: (0, 0))
    return pl.pallas_call(
        functools.partial(_out_proj_kernel, tm=tm),
        grid=(t // tm,),
        in_specs=[yspec, yspec, yspec, yspec, full((1, D_MODEL)),
                  pl.BlockSpec((None, D_MODEL, D_MODEL), lambda i: (layer, 0, 0)),
                  pl.BlockSpec((tm, D_MODEL), lambda i: (i, 0)), full((1, D_MODEL)),
                  full((D_MODEL, LANES)), full((D_MODEL, LANES)), full((1, LANES))],
        out_specs=[pl.BlockSpec((tm, D_MODEL), lambda i: (i, 0)),
                   pl.BlockSpec((tm * ROW_PITCH, LANES), lambda i: (i, 0)),
                   pl.BlockSpec((tm, LANES), lambda i: (i, 0))],
        out_shape=[jax.ShapeDtypeStruct((t, D_MODEL), F32),
                   jax.ShapeDtypeStruct((t * ROW_PITCH, LANES), F32),
                   jax.ShapeDtypeStruct((t, LANES), F32)],
        compiler_params=_cparams(("parallel",)),
        name="out_proj",
    )(ya, yb, yc, yd, mg, wo_all, x, fg, wr_hi, wr_lo, br)


ROUTE_ROWS = 40


def _route(lg, meta_ref, metat_ref, cnt_ref, col_ref, row_ref, tm):
    @pl.when(pl.program_id(0) == 0)
    def _():
        col_ref[...] = jnp.zeros(col_ref.shape, F32)
        row_ref[...] = jnp.zeros(row_ref.shape, F32)

    nr = ROUTE_ROWS
    lt = jnp.transpose(lg)[0:nr, :]
    row = lax.broadcasted_iota(jnp.int32, (nr, tm), 0)
    big = jnp.int32(1 << 20)
    cmax = lambda a: jnp.max(a, axis=0, keepdims=True)
    cmin = lambda a: jnp.min(a, axis=0, keepdims=True)
    csum = lambda a: jnp.sum(a, axis=0, keepdims=True)

    is_g = row < MOE_GROUPS
    gl = jnp.where(is_g, lt, NEG_INF)
    gm = cmax(gl)
    p_top = 1.0 / csum(jnp.where(is_g, jnp.exp(gl - gm), 0.0))
    g_top = cmin(jnp.where(is_g & (gl == gm), row, big))

    is_e = (row >= MOE_GROUPS) & (row < MOE_GROUPS + MOE_EXPERTS) \
        & (((row - MOE_GROUPS) // MOE_EPG) == g_top)
    el = jnp.where(is_e, lt, NEG_INF)
    ee = jnp.where(is_e, jnp.exp(el - cmax(el)), 0.0)
    p = jnp.where(is_e, ee / csum(ee), -1.0)
    p1 = cmax(p)
    i1 = cmin(jnp.where(p == p1, row, big))
    p_rest = jnp.where(row == i1, -1.0, p)
    p2 = cmax(p_rest)
    i2 = cmin(jnp.where((p_rest == p2) & is_e & (row != i1), row, big))
    den = p1 + p2
    w1 = p_top * (p1 / den)
    w2 = p_top * (p2 / den)

    hit1, hit2 = row == i1, row == i2
    oh = (hit1 | hit2).astype(BF16)
    r = lax.broadcasted_iota(jnp.int32, (tm, tm), 0)
    c = lax.broadcasted_iota(jnp.int32, (tm, tm), 1)
    before = _dot(oh, (r < c).astype(BF16)) + col_ref[:, 0:1]
    r1 = csum(jnp.where(hit1, before, 0.0))
    r2 = csum(jnp.where(hit2, before, 0.0))
    col_ref[...] = col_ref[...] + jnp.sum(oh.astype(F32), axis=1, keepdims=True)
    oh_all = jnp.concatenate([oh, jnp.zeros((LANES - nr, tm), BF16)], axis=0)
    row_ref[...] = row_ref[...] + _dot_nt(jnp.ones((8, tm), BF16), oh_all)
    cnt_ref[...] = row_ref[...]

    e1 = (i1 - MOE_GROUPS).astype(F32)
    e2 = (i2 - MOE_GROUPS).astype(F32)
    row8 = lax.broadcasted_iota(jnp.int32, (8, tm), 0)
    meta = jnp.zeros((8, tm), F32)
    for k, v in enumerate((e1, e2, w1, w2, r1, r2)):
        meta = jnp.where(row8 == k, v, meta)
    metat_ref[...] = meta
    meta = jnp.concatenate([meta, jnp.zeros((LANES - 8, tm), F32)], axis=0)
    meta_ref[...] = jnp.transpose(meta)


def _router_kernel(lg_ref, meta_ref, metat_ref, cnt_ref, col_ref, row_ref, *, tm):
    _route(lg_ref[...], meta_ref, metat_ref, cnt_ref, col_ref, row_ref, tm)


def _router(logits):
    t = logits.shape[0]
    tm = 512
    return pl.pallas_call(
        functools.partial(_router_kernel, tm=tm),
        grid=(t // tm,),
        in_specs=[pl.BlockSpec((tm, LANES), lambda i: (i, 0))],
        out_specs=[pl.BlockSpec((tm, LANES), lambda i: (i, 0)),
                   pl.BlockSpec((8, tm), lambda i: (0, i)),
                   pl.BlockSpec((8, LANES), lambda i: (i, 0))],
        out_shape=[jax.ShapeDtypeStruct((t, LANES), F32),
                   jax.ShapeDtypeStruct((8, t), F32),
                   jax.ShapeDtypeStruct((t // tm * 8, LANES), F32)],
        scratch_shapes=[pltpu.VMEM((ROUTE_ROWS, LANES), F32), pltpu.VMEM((8, LANES), F32)],
        compiler_params=_cparams(("arbitrary",)),
        name="router",
    )(logits)


def _dest_kernel(metat_ref, cnt_ref, o_ref, *, tm):
    cnt = cnt_ref[...]
    start = (_lane_cumsum(cnt) - cnt)[0:1, :]
    nr = ROUTE_ROWS
    start_col = jnp.transpose(jnp.broadcast_to(start, (LANES, LANES)))[0:nr, 0:1]
    mt = metat_ref[...]
    row = lax.broadcasted_iota(jnp.int32, (nr, tm), 0)
    row8 = lax.broadcasted_iota(jnp.int32, (8, tm), 0)
    out = jnp.zeros((8, tm), F32)
    for k in range(2):
        e_row = mt[k:k + 1, :].astype(jnp.int32) + MOE_GROUPS
        d = jnp.sum(jnp.where(row == e_row, start_col, 0.0), axis=0, keepdims=True) + mt[4 + k:5 + k, :]
        out = jnp.where(row8 == k, d, out)
    o_ref[...] = out.astype(jnp.int32)


def _dest(meta_t, counts):
    t = meta_t.shape[1]
    tm = 512
    last = counts.shape[0] // 8 - 1
    return pl.pallas_call(
        functools.partial(_dest_kernel, tm=tm),
        grid=(t // tm,),
        in_specs=[pl.BlockSpec((8, tm), lambda i: (0, i)),
                  pl.BlockSpec((8, LANES), lambda i: (last, 0))],
        out_specs=pl.BlockSpec((8, tm), lambda i: (0, i)),
        out_shape=jax.ShapeDtypeStruct((8, t), jnp.int32),
        compiler_params=_cparams(("parallel",)),
        name="dest",
    )(meta_t, counts)


def _dispatch_kernel(d1_ref, d2_ref, h_ref, xs_ref, sem, *, td):
    base = pl.program_id(0) * td

    def start(rb, carry):
        for k in range(DMA_UNROLL):
            r = rb * DMA_UNROLL + k
            src = h_ref.at[pl.ds(r * ROW_PITCH, ROW_PITCH)]
            for prio, d_ref in enumerate((d1_ref, d2_ref)):
                dst = xs_ref.at[pl.ds(d_ref[base + r] * ROW_PITCH, ROW_PITCH)]
                pltpu.make_async_copy(src, dst, sem).start(priority=prio)
        return carry

    lax.fori_loop(0, td // DMA_UNROLL, start, 0)
    for _ in range(2):
        pltpu.make_async_copy(h_ref, xs_ref.at[pl.ds(0, td * ROW_PITCH)], sem).wait()


def _dispatch(dest1, dest2, h2):
    t = dest1.shape[0]
    td = 1024
    return pl.pallas_call(
        functools.partial(_dispatch_kernel, td=td),
        grid_spec=pltpu.PrefetchScalarGridSpec(
            num_scalar_prefetch=2,
            grid=(t // td,),
            in_specs=[pl.BlockSpec((td * ROW_PITCH, LANES), lambda i, *_: (i, 0))],
            out_specs=pl.BlockSpec(memory_space=pl.ANY),
            scratch_shapes=[pltpu.SemaphoreType.DMA(())]),
        out_shape=jax.ShapeDtypeStruct((2 * t * ROW_PITCH, LANES), F32),
        compiler_params=pltpu.CompilerParams(dimension_semantics=("arbitrary",),
                                             has_side_effects=True),
        name="dispatch",
    )(dest1, dest2, h2)


def _expert_kernel(vt_ref, ve_ref, vlo_ref, vhi_ref, vfirst_ref, vvalid_ref, vnew_ref, vnext_ref,
                   vslot_ref, xs_ref, wg_ref, wu_ref, wd_ref, ys_ref,
                   wgf_ref, wuf_ref, wdf_ref, wgb_ref, wub_ref, wdb_ref, sem, *, tmx, layer):
    v = pl.program_id(0)

    def weight_copies(e, slot):
        return [pltpu.make_async_copy(w_ref.at[layer, e], f_ref.at[slot], sem.at[slot])
                for w_ref, f_ref in ((wg_ref, wgf_ref), (wu_ref, wuf_ref), (wd_ref, wdf_ref))]

    @pl.when(vnew_ref[v] == 1)
    def _():
        slot = vslot_ref[v]

        @pl.when(v == 0)
        def _():
            for cp in weight_copies(ve_ref[v], slot):
                cp.start()

        for cp in weight_copies(ve_ref[v], slot):
            cp.wait()

        @pl.when(vnext_ref[v] >= 0)
        def _():
            for cp in weight_copies(vnext_ref[v], 1 - slot):
                cp.start()

        wgb_ref[...] = wgf_ref[slot].astype(BF16)
        wub_ref[...] = wuf_ref[slot].astype(BF16)
        wdb_ref[...] = wdf_ref[slot].astype(BF16)

    @pl.when(vvalid_ref[v] == 1)
    def _():
        x = jnp.concatenate(_unpack_rows(lambda rows: xs_ref[rows, :], tmx), axis=1).astype(BF16)
        a = _dot(x, wgb_ref[...])
        u = _dot(x, wub_ref[...])
        hid = (jax.nn.silu(a) * u).astype(BF16)
        y = _dot(hid, wdb_ref[...])
        rows = lax.broadcasted_iota(jnp.int32, (tmx, 1), 0)
        mine = (rows >= vlo_ref[v]) & (rows < vhi_ref[v])

        @pl.when(vfirst_ref[v] == 1)
        def _():
            _pack_rows(ys_ref, y, tmx, mask=mine)

        @pl.when(vfirst_ref[v] == 0)
        def _():
            _pack_rows(ys_ref, y, tmx, mask=mine, old=True)


def _experts(sched, xs, wg, wu, wd, layer, tmx):
    nvis = sched[0].shape[0]
    rows = xs.shape[0]
    xspec = pl.BlockSpec((tmx * ROW_PITCH, LANES), lambda v, vt, *_: (vt[v], 0))
    hbm = pl.BlockSpec(memory_space=pl.ANY)
    return pl.pallas_call(
        functools.partial(_expert_kernel, tmx=tmx, layer=layer),
        grid_spec=pltpu.PrefetchScalarGridSpec(
            num_scalar_prefetch=9,
            grid=(nvis,),
            in_specs=[xspec, hbm, hbm, hbm],
            out_specs=xspec,
            scratch_shapes=[pltpu.VMEM((2, D_MODEL, MOE_HIDDEN), F32),
                            pltpu.VMEM((2, D_MODEL, MOE_HIDDEN), F32),
                            pltpu.VMEM((2, MOE_HIDDEN, D_MODEL), F32),
                            pltpu.VMEM((D_MODEL, MOE_HIDDEN), BF16),
                            pltpu.VMEM((D_MODEL, MOE_HIDDEN), BF16),
                            pltpu.VMEM((MOE_HIDDEN, D_MODEL), BF16),
                            pltpu.SemaphoreType.DMA((2,))]),
        out_shape=jax.ShapeDtypeStruct((rows, LANES), F32),
        compiler_params=_cparams(("arbitrary",)),
        name="experts",
    )(*sched, xs, wg, wu, wd)


def _combine_kernel(d1_ref, d2_ref, ys_ref, meta_ref, x1_ref, fg_ref, o_ref,
                    b1_ref, b2_ref, sem, *, tc, final):
    i = pl.program_id(0)
    n = pl.num_programs(0)
    slot = i % 2

    def gather(tile, slot):
        def start(rb, carry):
            for k in range(DMA_UNROLL):
                r = rb * DMA_UNROLL + k
                dst = pl.ds(r * ROW_PITCH, ROW_CHUNKS)
                for prio, (d_ref, b_ref) in enumerate(((d1_ref, b1_ref), (d2_ref, b2_ref))):
                    src = ys_ref.at[pl.ds(d_ref[tile * tc + r] * ROW_PITCH, ROW_CHUNKS)]
                    pltpu.make_async_copy(src, b_ref.at[slot, dst], sem.at[slot]).start(priority=prio)
            return carry

        lax.fori_loop(0, tc // DMA_UNROLL, start, 0)

    @pl.when(i == 0)
    def _():
        gather(0, 0)

    @pl.when(i + 1 < n)
    def _():
        gather(i + 1, 1 - slot)

    for b_ref in (b1_ref, b2_ref):
        pltpu.make_async_copy(ys_ref.at[pl.ds(0, tc * ROW_CHUNKS)],
                              b_ref.at[slot, pl.ds(0, tc * ROW_CHUNKS)], sem.at[slot]).wait()
    w1 = meta_ref[:, 2:3]
    w2 = meta_ref[:, 3:4]
    y1 = _unpack_rows(lambda rows: b1_ref[slot, rows, :], tc)
    y2 = _unpack_rows(lambda rows: b2_ref[slot, rows, :], tc)
    x2 = jnp.concatenate([x1_ref[:, c * LANES:(c + 1) * LANES] + (w1 * y1[c] + w2 * y2[c])
                          for c in range(ROW_CHUNKS)], axis=1)
    if final:
        x2 = _rms(x2) * fg_ref[...]
    o_ref[...] = x2


def _combine(dest1, dest2, ys, meta, x1, fg, final):
    t = x1.shape[0]
    tc = 256
    return pl.pallas_call(
        functools.partial(_combine_kernel, tc=tc, final=final),
        grid_spec=pltpu.PrefetchScalarGridSpec(
            num_scalar_prefetch=2,
            grid=(t // tc,),
            in_specs=[pl.BlockSpec(memory_space=pl.ANY),
                      pl.BlockSpec((tc, LANES), lambda i, *_: (i, 0)),
                      pl.BlockSpec((tc, D_MODEL), lambda i, *_: (i, 0)),
                      pl.BlockSpec((1, D_MODEL), lambda i, *_: (0, 0))],
            out_specs=pl.BlockSpec((tc, D_MODEL), lambda i, *_: (i, 0)),
            scratch_shapes=[pltpu.VMEM((2, tc * ROW_PITCH, LANES), F32),
                            pltpu.VMEM((2, tc * ROW_PITCH, LANES), F32),
                            pltpu.SemaphoreType.DMA((2,))]),
        out_shape=jax.ShapeDtypeStruct((t, D_MODEL), F32),
        compiler_params=_cparams(("arbitrary",)),
        name="combine",
    )(dest1, dest2, ys, meta, x1, fg)


def _rope_tables(r0):
    inv = ROPE_THETA ** (-jnp.arange(0, 2 * ROPE_HALF, 2, dtype=F32) / (2 * ROPE_HALF))
    ang = jnp.arange(SEQ, dtype=F32)[:, None] * inv[None, :]
    cos, sin = jnp.cos(ang), jnp.sin(ang)
    c = jnp.ones((SEQ, LANES), F32).at[:, r0:r0 + ROPE_HALF].set(cos)
    c = c.at[:, r0 + ROPE_HALF:r0 + 2 * ROPE_HALF].set(cos)
    sa = jnp.zeros((SEQ, LANES), F32).at[:, r0:r0 + ROPE_HALF].set(-sin)
    sb = jnp.zeros((SEQ, LANES), F32).at[:, r0 + ROPE_HALF:r0 + 2 * ROPE_HALF].set(sin)
    return c, sa, sb


def _pad_cols(w, width):
    return jnp.pad(w, ((0, 0), (0, width - w.shape[1])))


def _s5_params(a_re, a_im, log_dt, b_re, b_im, c_re, c_im, batch):
    dt = jnp.exp(log_dt)[:, None]
    mag = jnp.exp(a_re * dt)
    abar_r, abar_i = mag * jnp.cos(a_im * dt), mag * jnp.sin(a_im * dt)
    den = a_re * a_re + a_im * a_im
    nr, ni = abar_r - 1.0, abar_i
    coef_r = (nr * a_re + ni * a_im) / den
    coef_i = (ni * a_re - nr * a_im) / den
    bbar_r = coef_r[..., None] * b_re - coef_i[..., None] * b_im
    bbar_i = coef_r[..., None] * b_im + coef_i[..., None] * b_re
    nslab, gps = W_GROUP // LANES, LANES // S5_CH
    eye = jnp.eye(gps, dtype=F32)
    slab = lambda m: m.reshape((nslab, gps) + m.shape[1:])
    blk_b = lambda m: jnp.einsum('jgpc,gh->jgchp', slab(m), eye).reshape(nslab, LANES, gps * S5_STATE)
    wb = jnp.concatenate([blk_b(bbar_r), blk_b(bbar_i)], axis=2).astype(BF16)
    blk_c = lambda m: jnp.einsum('jgcp,gh->jgphc', slab(m), eye).reshape(nslab, gps * S5_STATE, LANES)
    wc = jnp.concatenate([blk_c(c_re), blk_c(-c_im)], axis=1).astype(BF16)
    ar = jnp.broadcast_to(abar_r.reshape(1, S5_WIDTH), (batch, S5_WIDTH))
    ai = jnp.broadcast_to(abar_i.reshape(1, S5_WIDTH), (batch, S5_WIDTH))
    return wb, ar, ai, wc


def _nsa_consts(ck):
    c_start = np.arange(N_CMP_PAD) * CMP_STRIDE
    b_start = np.arange(N_BLK) * SEL_BLOCK
    ov = ((c_start[None, :] < b_start[:, None] + SEL_BLOCK)
          & (c_start[None, :] + CMP_BLOCK > b_start[:, None])).astype(np.float32)
    key_blk = np.arange(SEQ) // SEL_BLOCK
    e = (key_blk[None, :] == np.arange(N_BLK)[:, None]).astype(np.float32)
    e = e.reshape(N_BLK, SEQ // ck, ck).transpose(1, 0, 2)
    return jnp.asarray(ov), jnp.asarray(e, dtype=BF16)


def _lane_cumsum(v):
    r = lax.broadcasted_iota(jnp.int32, (LANES, LANES), 0)
    c = lax.broadcasted_iota(jnp.int32, (LANES, LANES), 1)
    incl = (r <= c).astype(BF16)
    hi = jnp.floor(v * (1.0 / 256.0))
    lo = v - 256.0 * hi
    return 256.0 * _dot(hi.astype(BF16), incl) + _dot(lo.astype(BF16), incl)


def _sched_kernel(cnt_ref, o_ref, *, tmx, nv):
    cnt = cnt_ref[...]
    ends = _lane_cumsum(cnt)
    offs = ends - cnt
    first = jnp.floor(offs * (1.0 / tmx))
    last = jnp.floor(jnp.maximum(ends - 1.0, 0.0) * (1.0 / tmx))
    nvis = jnp.where(cnt > 0.0, last - first + 1.0, 0.0)
    cumv = _lane_cumsum(nvis)
    row1 = lambda a: a[0:1, :]
    total = jnp.max(row1(cumv), axis=-1, keepdims=True)
    rsum = lambda a: jnp.sum(a, axis=-1, keepdims=True)
    v = lax.broadcasted_iota(jnp.int32, (nv, LANES), 0).astype(F32)
    lane = lax.broadcasted_iota(jnp.int32, (nv, LANES), 1).astype(F32)
    vc = jnp.minimum(v, total - 1.0)
    e_lane = rsum((row1(cumv) <= vc).astype(F32))
    hit = lane == e_lane
    pick = lambda a: rsum(jnp.where(hit, row1(a), 0.0))
    vt = pick(first) + vc[:, 0:1] - (pick(cumv) - pick(nvis))
    vlo = jnp.clip(pick(offs) - vt * tmx, 0.0, float(tmx))
    vhi = jnp.clip(pick(ends) - vt * tmx, 0.0, float(tmx))
    changed = lambda a: (v == 0.0) | (a != pltpu.roll(a, 1, 0))
    vfirst = changed(jnp.broadcast_to(vt, (nv, LANES))).astype(F32)
    vnew = changed(jnp.broadcast_to(e_lane, (nv, LANES))).astype(F32)
    vvalid = (v < total).astype(F32)
    nonempty = (cnt > 0.0).astype(F32)
    order = pick(_lane_cumsum(nonempty) - nonempty)
    vslot = order - 2.0 * jnp.floor(order * 0.5)
    far = float(1 << 20)
    nxt = jnp.min(jnp.where((row1(nonempty) > 0.0) & (lane > e_lane), lane, far), axis=-1, keepdims=True)
    vnext = jnp.where(nxt >= far, -1.0, nxt - MOE_GROUPS)
    out = jnp.zeros((nv, LANES), F32)
    cols = (vt, e_lane - MOE_GROUPS, vlo, vhi, vfirst, vvalid, vnew, vnext, vslot)
    for k, col in enumerate(cols):
        out = jnp.where(lane == k, col, out)
    o_ref[...] = out.astype(jnp.int32)


def _moe_schedule(counts, n_rows, tmx):
    nvis_max = n_rows // tmx + MOE_EXPERTS
    nv = 256
    last = counts.shape[0] // 8 - 1
    sched = pl.pallas_call(
        functools.partial(_sched_kernel, tmx=tmx, nv=nv),
        grid=(1,),
        in_specs=[pl.BlockSpec((8, LANES), lambda i: (last, 0))],
        out_specs=pl.BlockSpec((nv, LANES), lambda i: (0, 0)),
        out_shape=jax.ShapeDtypeStruct((nv, LANES), jnp.int32),
        name="sched",
    )(counts)
    return tuple(sched[:nvis_max, k] for k in range(9))


def kernel(x, attn_norm_g, w_in, mla_q_norm_g, mla_kv_norm_g, mla_w_uq, mla_w_ukv, nsa_cmp_pe, nsa_cmp_w1, nsa_cmp_w2, pool_w, pool_b, pool_scale, s5_a_re, s5_a_im, s5_log_dt, s5_b_re, s5_b_im, s5_c_re, s5_c_im, s5_d, s5_glu_w, s5_glu_b, mix_norm_g, w_out, ffn_norm_g, moe_w_group, moe_b_group, moe_w_expert, moe_b_expert, moe_w_gate, moe_w_up, moe_w_down, final_norm_g):
    batch, seq, d = x.shape
    depth = w_in.shape[0]
    t = batch * seq
    xf = x.reshape(t, d)
    qtabs = _rope_tables(MLA_NOPE)
    ktabs = _rope_tables(0)
    ovt, emat = _nsa_consts(512)
    row = lambda v: v.reshape(1, -1)
    tmx = 256
    w_in_all = _w_in_prep(w_in)
    w_out_all = w_out.astype(BF16)

    for l in range(depth):
        wuq = jnp.pad(mla_w_uq[l], ((0, 512 - MLA_Q_LORA), (0, 0))).astype(BF16)
        ukv = mla_w_ukv[l].reshape(HEAD_D, MLA_HEADS, MLA_NOPE + MLA_V)
        wk = jnp.pad(ukv[:, :, :MLA_NOPE], ((0, 0), (0, 0), (0, HEAD_D - MLA_NOPE)))
        wk = wk.reshape(HEAD_D, 512).astype(BF16)
        wv = ukv[:, :, MLA_NOPE:].reshape(HEAD_D, 512).astype(BF16)
        qg = jnp.pad(mla_q_norm_g[l], (0, 512 - MLA_Q_LORA)).reshape(1, 512)
        q_a, k_a, v_a, q_b, kc, vc, kv_b, gates, u_pool, u_s5 = _in_proj(
            xf, row(attn_norm_g[l]), w_in_all, l, qg, row(mla_kv_norm_g[l]), wuq, wk, wv, qtabs, ktabs)

        y_a = _mla_attn(q_a, k_a, v_a, batch)

        pe = jnp.broadcast_to(nsa_cmp_pe[l].reshape(2, 1, CMP_BLOCK * HEAD_D),
                              (2, 8, CMP_BLOCK * HEAD_D)).astype(BF16)
        kcmp, vcmp = _nsa_compress(kc, vc, nsa_cmp_w1[l].astype(BF16), nsa_cmp_w2[l].astype(BF16),
                                   pe, batch)
        y_b = _nsa_attn(q_b, kcmp, vcmp, kv_b, gates, ovt, emat, batch)

        y_c = _pool(u_pool, pool_w[l].astype(BF16), row(pool_b[l]), row(pool_scale[l]), batch)

        wb, ar, ai, wc = _s5_params(s5_a_re[l], s5_a_im[l], s5_log_dt[l], s5_b_re[l], s5_b_im[l],
                                    s5_c_re[l], s5_c_im[l], batch)
        y_d = _s5(u_s5.reshape(batch, seq, W_GROUP), wb, ar, ai, wc, row(s5_d[l]),
                  s5_glu_w[l].astype(BF16), row(s5_glu_b[l])).reshape(t, W_GROUP)

        wr = jnp.concatenate([moe_w_group[l], moe_w_expert[l]], axis=1)
        wr = _pad_cols(wr, LANES)
        wr_hi = wr.astype(BF16)
        wr_lo = (wr - wr_hi.astype(F32)).astype(BF16)
        br = jnp.pad(jnp.concatenate([moe_b_group[l], moe_b_expert[l]]), (0, LANES - 36)).reshape(1, LANES)
        x1, h2, logits = _out_proj(y_a, y_b, y_c, y_d, row(mix_norm_g[l]), w_out_all, l,
                                   xf, row(ffn_norm_g[l]), wr_hi, wr_lo, br)

        meta, meta_t, counts = _router(logits)
        dest = _dest(meta_t, counts)
        dest1, dest2 = dest[0], dest[1]
        sched = _moe_schedule(counts, 2 * t, tmx)
        xs = _dispatch(dest1, dest2, h2)
        ys = _experts(sched, xs, moe_w_gate, moe_w_up, moe_w_down, l, tmx)
        xf = _combine(dest1, dest2, ys, meta, x1, row(final_norm_g), final=(l == depth - 1))

    return xf.reshape(batch, seq, d)
```
